```python
import jax, jax.numpy as jnp
from jax import lax
import numpy as np

D_MODEL = 2048
BATCH = 2
SEQ = 4096
DEPTH = 1
DEC_BATCH = 128
DEC_SEQ = 1
PAST_LEN = 2048
PAGE_SIZE = 128

N_HEADS = 8
N_KV_HEADS = 2
HEAD_DIM = 128
ATTN_WIDTH = N_HEADS * HEAD_DIM
KV_WIDTH = N_KV_HEADS * HEAD_DIM
IDX_HEADS = 16
IDX_DIM = 64
TOPK_MAX = 256
Q_BLOCK = 128
ROPE_THETA = 10000.0
GLA_HEADS = 4
GLA_DK = 128
GLA_DV = 256
GLA_KEY_WIDTH = GLA_HEADS * GLA_DK
GLA_VAL_WIDTH = GLA_HEADS * GLA_DV
GLA_GATE_RANK = 16
GLA_GATE_TAU = 16.0
GLA_CHUNK = 64
D_FF = 5632
RMS_EPS = 1e-6
N_BRANCH = 2
D_IN = (ATTN_WIDTH + 2 * KV_WIDTH + IDX_HEADS * IDX_DIM + IDX_DIM + IDX_HEADS
        + 2 * GLA_KEY_WIDTH + 2 * GLA_VAL_WIDTH + GLA_GATE_RANK + N_BRANCH * D_MODEL)

kernel_name = "dsa_gla_macaron_sandwich_step"


def _split_points():
    sizes = (ATTN_WIDTH, KV_WIDTH, KV_WIDTH, IDX_HEADS * IDX_DIM, IDX_DIM, IDX_HEADS,
             GLA_KEY_WIDTH, GLA_KEY_WIDTH, GLA_VAL_WIDTH, GLA_GATE_RANK, GLA_VAL_WIDTH,
             D_MODEL, D_MODEL)
    pts, acc = [], 0
    for s in sizes[:-1]:
        acc += s
        pts.append(acc)
    return pts


def rmsnorm(x, w):
    xf = x.astype(jnp.float32)
    y = xf * lax.rsqrt(jnp.mean(xf * xf, axis=-1, keepdims=True) + RMS_EPS)
    return (y * w.astype(jnp.float32)).astype(x.dtype)


def rope(x, pos):
    d = x.shape[-1]
    inv = ROPE_THETA ** (-jnp.arange(0, d, 2, dtype=jnp.float32) / d)
    ang = pos.astype(jnp.float32)[:, None] * inv[None, :]
    cos = jnp.cos(ang)[:, None, :]
    sin = jnp.sin(ang)[:, None, :]
    x1, x2 = jnp.split(x.astype(jnp.float32), 2, axis=-1)
    out = jnp.concatenate([x1 * cos - x2 * sin, x2 * cos + x1 * sin], axis=-1)
    return out.astype(x.dtype)


def ffn_sublayer(h, pre_w, w_gate, w_up, w_down, post_w):
    z = rmsnorm(h, pre_w)
    f = (jax.nn.silu(z @ w_gate) * (z @ w_up)) @ w_down
    return h + 0.5 * rmsnorm(f, post_w)


def sparse_attend(q, qi, wi, qpos, k, v, ki, top):
    n, t = q.shape[:2]
    L = k.shape[1]
    kpos = jnp.arange(L, dtype=jnp.int32)
    s_idx = jnp.einsum('nthd,nld->nthl', qi.astype(jnp.float32), ki.astype(jnp.float32))
    w_sc = wi.astype(jnp.float32) * (IDX_HEADS ** -0.5 * IDX_DIM ** -0.5)
    score = jnp.einsum('nth,nthl->ntl', w_sc, jax.nn.relu(s_idx))
    causal = kpos[None, :] <= qpos[:, None]
    score = jnp.where(causal[None], score, -jnp.inf)
    _, sel = lax.top_k(score, top)
    valid = sel <= qpos[None, :, None]
    kg = jax.vmap(lambda kb, ib: kb[ib])(k, sel)
    vg = jax.vmap(lambda vb, ib: vb[ib])(v, sel)
    qg = q.reshape(n, t, N_KV_HEADS, N_HEADS // N_KV_HEADS, HEAD_DIM)
    logits = jnp.einsum('ntgrd,ntkgd->ntgrk', qg.astype(jnp.float32), kg.astype(jnp.float32))
    logits = logits * (HEAD_DIM ** -0.5)
    logits = jnp.where(valid[:, :, None, None, :], logits, -jnp.inf)
    p = jax.nn.softmax(logits, axis=-1)
    o = jnp.einsum('ntgrk,ntkgd->ntgrd', p.astype(v.dtype), vg)
    return o.reshape(n, t, ATTN_WIDTH).astype(q.dtype)


def gla_chunked(q, k, v, log_a, s0):
    n, t = q.shape[:2]
    c = GLA_CHUNK if t >= GLA_CHUNK else t
    pad = (-t) % c
    f32 = jnp.float32
    def prep(a):
        a = jnp.pad(a.astype(f32), ((0, 0), (0, pad), (0, 0), (0, 0)))
        nc = a.shape[1] // c
        return a.reshape(n, nc, c, *a.shape[2:]).swapaxes(0, 1)
    qs = prep(q * (GLA_DK ** -0.5))
    ks, vs, gs = prep(k), prep(v), prep(log_a)
    tri = jnp.tril(jnp.ones((c, c), dtype=bool))

    def step(S, inp):
        qc, kc, vc, gc = inp
        b = jnp.cumsum(gc, axis=1)
        o_inter = jnp.einsum('nchk,nhkv->nchv', qc * jnp.exp(b), S)
        diff = b[:, :, None] - b[:, None, :]
        decay = jnp.exp(jnp.where(tri[None, :, :, None, None], diff, -jnp.inf))
        A = jnp.einsum('nihk,njhk,nijhk->nhij', qc, kc, decay)
        o_intra = jnp.einsum('nhij,njhv->nihv', A, vc)
        b_last = b[:, -1]
        S_new = (jnp.exp(b_last)[..., None] * S
                 + jnp.einsum('nchk,nchv->nhkv', kc * jnp.exp(b_last[:, None] - b), vc))
        return S_new, o_inter + o_intra

    S_fin, o = lax.scan(step, s0.astype(f32), (qs, ks, vs, gs))
    o = o.swapaxes(0, 1).reshape(n, -1, GLA_HEADS, GLA_DV)[:, :t]
    return o.astype(q.dtype), S_fin.astype(q.dtype)


def decoder_layer(x, pos, attend_fn, s0, lw):
    (ffn1_pre_w, ffn1_w_gate, ffn1_w_up, ffn1_w_down, ffn1_post_w,
     mix_pre_w, w_in, gla_gate_w2, gla_gate_b, gla_norm_w,
     w_proj_attn, w_proj_gla, w_out, mix_post_w,
     ffn2_pre_w, ffn2_w_gate, ffn2_w_up, ffn2_w_down, ffn2_post_w) = lw
    n, t, _ = x.shape
    h = ffn_sublayer(x, ffn1_pre_w, ffn1_w_gate, ffn1_w_up, ffn1_w_down, ffn1_post_w)
    u = rmsnorm(h, mix_pre_w)
    z = u @ w_in
    (q, k, v, qi, ki, wi, gq, gk, gv, g_lr, g_r, g_attn, g_gla) = jnp.split(z, _split_points(), axis=-1)
    q = rope(q.reshape(n, t, N_HEADS, HEAD_DIM), pos)
    k = rope(k.reshape(n, t, N_KV_HEADS, HEAD_DIM), pos)
    v = v.reshape(n, t, N_KV_HEADS, HEAD_DIM)
    qi = rope(qi.reshape(n, t, IDX_HEADS, IDX_DIM), pos)
    ki = rope(ki.reshape(n, t, 1, IDX_DIM), pos)[:, :, 0]
    o_attn = attend_fn(q, qi, wi, k, v, ki)
    log_a = jax.nn.log_sigmoid((g_lr @ gla_gate_w2 + gla_gate_b).astype(jnp.float32)) / GLA_GATE_TAU
    o_g, s_fin = gla_chunked(gq.reshape(n, t, GLA_HEADS, GLA_DK),
                             gk.reshape(n, t, GLA_HEADS, GLA_DK),
                             gv.reshape(n, t, GLA_HEADS, GLA_DV),
                             log_a.reshape(n, t, GLA_HEADS, GLA_DK), s0)
    o_g = rmsnorm(o_g, gla_norm_w) * jax.nn.silu(g_r).reshape(n, t, GLA_HEADS, GLA_DV)
    o_g = o_g.reshape(n, t, GLA_VAL_WIDTH)
    merged = (jax.nn.sigmoid(g_attn) * (o_attn @ w_proj_attn)
              + jax.nn.sigmoid(g_gla) * (o_g @ w_proj_gla))
    h = h + rmsnorm(merged @ w_out, mix_post_w)
    h = ffn_sublayer(h, ffn2_pre_w, ffn2_w_gate, ffn2_w_up, ffn2_w_down, ffn2_post_w)
    return h, k, v, ki, s_fin


def setup_inputs(seed: int = 0) -> dict:
    key = jax.random.key(seed)
    ks = jax.random.split(key, 32)
    f32 = jnp.float32
    def nrm(kk, shape, scale=1.0):
        return jax.random.normal(kk, shape, f32) * scale
    def gain(kk, dim):
        return 1.0 + 0.01 * jax.random.normal(kk, (dim,), f32)
    n_pages = PAST_LEN // PAGE_SIZE
    n_used = DEC_BATCH * n_pages
    n_pool = n_used + max(1, n_used // 4)
    page_table = jax.random.permutation(ks[0], n_pool)[:n_used].reshape(DEC_BATCH, n_pages).astype(jnp.int32)
    return {
        "x_prompt": nrm(ks[1], (BATCH, SEQ, D_MODEL)),
        "x_sample": nrm(ks[2], (DEC_BATCH, DEC_SEQ, D_MODEL)),
        "cache_k": nrm(ks[3], (n_pool, PAGE_SIZE, N_KV_HEADS, HEAD_DIM)),
        "cache_v": nrm(ks[4], (n_pool, PAGE_SIZE, N_KV_HEADS, HEAD_DIM)),
        "cache_kidx": nrm(ks[5], (n_pool, PAGE_SIZE, IDX_DIM)),
        "page_table": page_table,
        "state_gla": nrm(ks[6], (DEC_BATCH, GLA_HEADS, GLA_DK, GLA_DV)),
        "ffn1_pre_w": gain(ks[7], D_MODEL),
        "ffn1_w_gate": nrm(ks[8], (D_MODEL, D_FF), D_MODEL ** -0.5),
        "ffn1_w_up": nrm(ks[9], (D_MODEL, D_FF), D_MODEL ** -0.5),
        "ffn1_w_down": nrm(ks[10], (D_FF, D_MODEL), D_FF ** -0.5),
        "ffn1_post_w": gain(ks[11], D_MODEL),
        "mix_pre_w": gain(ks[12], D_MODEL),
        "w_in": nrm(ks[13], (D_MODEL, D_IN), D_MODEL ** -0.5),
        "gla_gate_w2": nrm(ks[14], (GLA_GATE_RANK, GLA_KEY_WIDTH), GLA_GATE_RANK ** -0.5),
        "gla_gate_b": nrm(ks[15], (GLA_KEY_WIDTH,), 0.1),
        "gla_norm_w": gain(ks[16], GLA_DV),
        "w_proj_attn": nrm(ks[17], (ATTN_WIDTH, D_MODEL), ATTN_WIDTH ** -0.5),
        "w_proj_gla": nrm(ks[18], (GLA_VAL_WIDTH, D_MODEL), GLA_VAL_WIDTH ** -0.5),
        "w_out": nrm(ks[19], (D_MODEL, D_MODEL), D_MODEL ** -0.5),
        "mix_post_w": gain(ks[20], D_MODEL),
        "ffn2_pre_w": gain(ks[21], D_MODEL),
        "ffn2_w_gate": nrm(ks[22], (D_MODEL, D_FF), D_MODEL ** -0.5),
        "ffn2_w_up": nrm(ks[23], (D_MODEL, D_FF), D_MODEL ** -0.5),
        "ffn2_w_down": nrm(ks[24], (D_FF, D_MODEL), D_FF ** -0.5),
        "ffn2_post_w": gain(ks[25], D_MODEL),
    }


def reference(x_prompt, x_sample, cache_k, cache_v, cache_kidx, page_table, state_gla,
              ffn1_pre_w, ffn1_w_gate, ffn1_w_up, ffn1_w_down, ffn1_post_w,
              mix_pre_w, w_in, gla_gate_w2, gla_gate_b, gla_norm_w,
              w_proj_attn, w_proj_gla, w_out, mix_post_w,
              ffn2_pre_w, ffn2_w_gate, ffn2_w_up, ffn2_w_down, ffn2_post_w):
    lw = (ffn1_pre_w, ffn1_w_gate, ffn1_w_up, ffn1_w_down, ffn1_post_w,
          mix_pre_w, w_in, gla_gate_w2, gla_gate_b, gla_norm_w,
          w_proj_attn, w_proj_gla, w_out, mix_post_w,
          ffn2_pre_w, ffn2_w_gate, ffn2_w_up, ffn2_w_down, ffn2_post_w)

    b, s, _ = x_prompt.shape
    pos_p = jnp.arange(s, dtype=jnp.int32)
    top_p = min(TOPK_MAX, s // 4)
    nb = s // Q_BLOCK

    def prompt_attend(q, qi, wi, k, v, ki):
        def blk(a):
            return a.reshape(b, nb, Q_BLOCK, *a.shape[2:]).swapaxes(0, 1)
        pos_b = pos_p.reshape(nb, Q_BLOCK)
        out = lax.map(lambda xs: sparse_attend(xs[0], xs[1], xs[2], xs[3], k, v, ki, top_p),
                      (blk(q), blk(qi), blk(wi), pos_b))
        return out.swapaxes(0, 1).reshape(b, s, ATTN_WIDTH)

    s0_p = jnp.zeros((b, GLA_HEADS, GLA_DK, GLA_DV), jnp.float32)
    y_prompt, k_p, v_p, ki_p, gla_p = decoder_layer(x_prompt, pos_p, prompt_attend, s0_p, lw)

    nd, td, _ = x_sample.shape
    past = page_table.shape[1] * cache_k.shape[1]
    past_k = cache_k[page_table].reshape(nd, past, N_KV_HEADS, HEAD_DIM)
    past_v = cache_v[page_table].reshape(nd, past, N_KV_HEADS, HEAD_DIM)
    past_ki = cache_kidx[page_table].reshape(nd, past, IDX_DIM)
    pos_s = past + jnp.arange(td, dtype=jnp.int32)
    top_s = min(TOPK_MAX, (past + td) // 4)

    def sample_attend(q, qi, wi, k, v, ki):
        k_all = jnp.concatenate([past_k, k], axis=1)
        v_all = jnp.concatenate([past_v, v], axis=1)
        ki_all = jnp.concatenate([past_ki, ki], axis=1)
        return sparse_attend(q, qi, wi, pos_s, k_all, v_all, ki_all, top_s)

    y_sample, k_s, v_s, ki_s, gla_s = decoder_layer(x_sample, pos_s, sample_attend, state_gla, lw)

    return (y_prompt, y_sample, k_p, v_p, ki_p, gla_p, k_s, v_s, ki_s, gla_s)
```

```python
import functools

import jax
import jax.numpy as jnp
from jax import lax
from jax.experimental import pallas as pl
from jax.experimental.pallas import tpu as pltpu

F32, BF16, I32 = jnp.float32, jnp.bfloat16, jnp.int32
HIGHEST = lax.Precision.HIGHEST

N_HEADS = 8
N_KV_HEADS = 2
HEAD_DIM = 128
IDX_HEADS = 16
IDX_DIM = 64
TOPK_MAX = 256
Q_BLOCK = 128
ROPE_THETA = 10000.0
GLA_HEADS = 4
GLA_DK = 128
GLA_DV = 256
GLA_GATE_RANK = 16
GLA_GATE_TAU = 16.0
RMS_EPS = 1e-6

LANES = 128
ATTN_WIDTH = N_HEADS * HEAD_DIM
KV_WIDTH = N_KV_HEADS * HEAD_DIM
QI_WIDTH = IDX_HEADS * IDX_DIM
GLA_KEY_WIDTH = GLA_HEADS * GLA_DK
GLA_VAL_WIDTH = GLA_HEADS * GLA_DV
HEADS_PER_KV = N_HEADS // N_KV_HEADS

C_Q, C_K, C_V, C_QI, C_MISC = 0, 1024, 1280, 1536, 2560
C_GQ, C_GK, C_GV, C_GR, C_GA = 3072, 3584, 4096, 5120, 6144
MISC_WI, MISC_GLR = IDX_DIM, IDX_DIM + IDX_HEADS
MIX_TN = 512

NEG = -1e30
INT_MIN = -2 ** 31
NEG_INF_KEY = -2139095041
VMEM_LIMIT = 56 * 1024 * 1024


def _params(sem, vmem=VMEM_LIMIT):
    return pltpu.CompilerParams(dimension_semantics=sem, vmem_limit_bytes=vmem)


def _rms(x, w):
    return x * lax.rsqrt(jnp.mean(x * x, axis=-1, keepdims=True) + RMS_EPS) * w


def _dot_nt(a, b):
    return lax.dot_general(a, b, (((1,), (1,)), ((), ())), preferred_element_type=F32)


def _ffn_body(x_ref, prew_ref, wg_ref, wu_ref, wd_ref, postw_ref, o_ref, z_scr, acc_scr):
    j = pl.program_id(1)

    @pl.when(j == 0)
    def _():
        z_scr[...] = _rms(x_ref[...], prew_ref[...]).astype(BF16)
        acc_scr[...] = jnp.zeros_like(acc_scr)

    z = z_scr[...]
    g = jnp.dot(z, wg_ref[...], preferred_element_type=F32)
    u = jnp.dot(z, wu_ref[...], preferred_element_type=F32)
    a = (g * jax.nn.sigmoid(g) * u).astype(BF16)
    acc_scr[...] += jnp.dot(a, wd_ref[...], preferred_element_type=F32)

    @pl.when(j == pl.num_programs(1) - 1)
    def _():
        o_ref[...] = x_ref[...] + 0.5 * _rms(acc_scr[...], postw_ref[...])


def _ffn(x, pre_w, wg, wu, wd, post_w, tm):
    rows, d = x.shape
    dff = wg.shape[1]
    tf = 512 if dff % 512 == 0 else dff
    return pl.pallas_call(
        _ffn_body,
        out_shape=jax.ShapeDtypeStruct((rows, d), F32),
        grid=(rows // tm, dff // tf),
        in_specs=[
            pl.BlockSpec((tm, d), lambda i, j: (i, 0)),
            pl.BlockSpec((1, d), lambda i, j: (0, 0)),
            pl.BlockSpec((d, tf), lambda i, j: (0, j)),
            pl.BlockSpec((d, tf), lambda i, j: (0, j)),
            pl.BlockSpec((tf, d), lambda i, j: (j, 0)),
            pl.BlockSpec((1, d), lambda i, j: (0, 0)),
        ],
        out_specs=pl.BlockSpec((tm, d), lambda i, j: (i, 0)),
        scratch_shapes=[pltpu.VMEM((tm, d), BF16), pltpu.VMEM((tm, d), F32)],
        compiler_params=_params(("parallel", "arbitrary")),
        name="ffn",
    )(x, pre_w.reshape(1, d), wg, wu, wd, post_w.reshape(1, d))


def _mix_in_body(x_ref, prew_ref, w_ref, c128_ref, s128_ref, c64_ref, s64_ref, o_ref, u_scr):
    j = pl.program_id(1)

    @pl.when(j == 0)
    def _():
        u_scr[...] = _rms(x_ref[...], prew_ref[...]).astype(BF16)

    r = jnp.dot(u_scr[...], w_ref[...], preferred_element_type=F32)

    def rope128(x):
        return x * c128_ref[...] + pltpu.roll(x, HEAD_DIM // 2, 1) * s128_ref[...]

    def rope64(x):
        lane = lax.broadcasted_iota(I32, x.shape, 1)
        first = (lane % IDX_DIM) < (IDX_DIM // 2)
        rot = jnp.where(first, pltpu.roll(x, LANES - IDX_DIM // 2, 1), pltpu.roll(x, IDX_DIM // 2, 1))
        return x * c64_ref[...] + rot * s64_ref[...]

    def sl(t):
        return slice(t * LANES, (t + 1) * LANES)

    @pl.when(j < 2)
    def _():
        for t in range(4):
            o_ref[:, sl(t)] = rope128(r[:, sl(t)]) * (HEAD_DIM ** -0.5)

    @pl.when(j == 2)
    def _():
        for t in range(2):
            o_ref[:, sl(t)] = rope128(r[:, sl(t)])
        o_ref[:, 2 * LANES:] = r[:, 2 * LANES:]

    @pl.when((j == 3) | (j == 4))
    def _():
        for t in range(4):
            o_ref[:, sl(t)] = rope64(r[:, sl(t)])

    @pl.when(j == 5)
    def _():
        x = r[:, sl(0)]
        lane = lax.broadcasted_iota(I32, x.shape, 1)
        wi_scale = IDX_HEADS ** -0.5 * IDX_DIM ** -0.5
        y = jnp.where(lane < MISC_WI, rope64(x), jnp.where(lane < MISC_GLR, x * wi_scale, x))
        o_ref[:, sl(0)] = y
        o_ref[:, LANES:] = r[:, LANES:]

    @pl.when(j > 5)
    def _():
        o_ref[...] = r


def _mix_in(h, pre_w, w_perm, tabs, tm, tab_blocks):
    rows, d = h.shape
    zw = w_perm.shape[1]
    c128, s128, c64, s64 = tabs
    tab_spec = pl.BlockSpec((tm, LANES), lambda i, j: (i % tab_blocks, 0))
    return pl.pallas_call(
        _mix_in_body,
        out_shape=jax.ShapeDtypeStruct((rows, zw), F32),
        grid=(rows // tm, zw // MIX_TN),
        in_specs=[
            pl.BlockSpec((tm, d), lambda i, j: (i, 0)),
            pl.BlockSpec((1, d), lambda i, j: (0, 0)),
            pl.BlockSpec((d, MIX_TN), lambda i, j: (0, j)),
            tab_spec, tab_spec, tab_spec, tab_spec,
        ],
        out_specs=pl.BlockSpec((tm, MIX_TN), lambda i, j: (i, j)),
        scratch_shapes=[pltpu.VMEM((tm, d), BF16)],
        compiler_params=_params(("parallel", "arbitrary")),
        name="mix_in",
    )(h, pre_w.reshape(1, d), w_perm, c128, s128, c64, s64)


def _sortable(x):
    b = lax.bitcast_convert_type(x + 0.0, I32)
    return b ^ ((b >> 31) & jnp.int32(0x7FFFFFFF))


def _select_threshold(key_ref, j_ref, nk, tk, topk, rows, idx_bits):
    def count(pred):
        def body(c, acc):
            ks = pl.multiple_of(c * tk, tk)
            m = jnp.where(pred(key_ref[:, pl.ds(ks, tk)], ks), 1.0, 0.0)
            for t in range(tk // LANES):
                acc = acc + m[:, t * LANES:(t + 1) * LANES]
            return acc
        acc = lax.fori_loop(0, nk, body, jnp.zeros((rows, LANES), F32))
        return jnp.sum(acc, axis=1, keepdims=True)

    def bit_body(bi, t):
        cand = t ^ lax.shift_left(jnp.int32(1), 31 - bi)
        return jnp.where(count(lambda blk, ks: blk >= cand) >= topk, cand, t)

    thr = lax.fori_loop(0, 32, bit_body, jnp.full((rows, 1), INT_MIN, I32))

    cnt_ge = count(lambda blk, ks: blk >= thr)
    cnt_gt = count(lambda blk, ks: blk > thr)
    need = topk - cnt_gt
    tie = jnp.where((cnt_ge > topk) & (thr > NEG_INF_KEY), 1.0, 0.0)
    j_ref[...] = jnp.full((rows, 1), 1 << idx_bits, I32)

    @pl.when(jnp.max(tie) > 0.0)
    def _():
        def jbit(bi, jb):
            cand = jb + lax.shift_left(jnp.int32(1), idx_bits - 1 - bi)

            def pred(blk, ks):
                kpos = ks + lax.broadcasted_iota(I32, blk.shape, 1)
                return (blk == thr) & (kpos < cand)
            return jnp.where(count(pred) <= need, cand, jb)
        j_ref[...] = lax.fori_loop(0, idx_bits, jbit, jnp.zeros((rows, 1), I32))

    return thr


def _selected(key, kpos, thr, jb):
    return (key > thr) | ((key == thr) & (kpos < jb))


def _attn_prompt_body(q_ref, qia_ref, qib_ref, misc_ref, k_ref, v_ref, kim_ref, o_ref,
                      key_scr, j_scr, qs_scr, m_scr, l_scr, acc_scr, *, tk, topk, idx_bits):
    i = pl.program_id(1)
    tq = Q_BLOCK
    nk = (i * tq + tq + tk - 1) // tk
    qpos = i * tq + lax.broadcasted_iota(I32, (tq, 1), 0)

    def score_chunk(c, carry):
        ks = pl.multiple_of(c * tk, tk)
        ki = kim_ref[pl.ds(ks, tk), 0:IDX_DIM].astype(BF16)
        acc = jnp.zeros((tq, tk), F32)
        for h in range(IDX_HEADS):
            ref = qia_ref if h < IDX_HEADS // 2 else qib_ref
            hh = h % (IDX_HEADS // 2)
            qh = ref[:, hh * IDX_DIM:(hh + 1) * IDX_DIM].astype(BF16)
            w = misc_ref[:, MISC_WI + h:MISC_WI + h + 1]
            acc = acc + w * jnp.maximum(_dot_nt(qh, ki), 0.0)
        kpos = ks + lax.broadcasted_iota(I32, (tq, tk), 1)
        acc = jnp.where(kpos <= qpos, acc, -jnp.inf)
        key_scr[:, pl.ds(ks, tk)] = _sortable(acc)
        return carry

    lax.fori_loop(0, nk, score_chunk, 0)
    thr = _select_threshold(key_scr, j_scr, nk, tk, topk, tq, idx_bits)
    jb = j_scr[...]

    for g in range(N_KV_HEADS):
        for r in range(HEADS_PER_KV):
            hd = g * HEADS_PER_KV + r
            qs_scr[g, r * tq:(r + 1) * tq, :] = q_ref[:, hd * HEAD_DIM:(hd + 1) * HEAD_DIM].astype(BF16)
    m_scr[...] = jnp.full(m_scr.shape, NEG, F32)
    l_scr[...] = jnp.zeros(l_scr.shape, F32)
    acc_scr[...] = jnp.zeros(acc_scr.shape, F32)

    def attn_chunk(c, carry):
        ks = pl.multiple_of(c * tk, tk)
        kpos = ks + lax.broadcasted_iota(I32, (tq, tk), 1)
        sel = _selected(key_scr[:, pl.ds(ks, tk)], kpos, thr, jb) & (kpos <= qpos)
        bias = jnp.where(sel, 0.0, NEG)
        bias = jnp.concatenate([bias] * HEADS_PER_KV, axis=0)
        for g in range(N_KV_HEADS):
            kc = k_ref[pl.ds(ks, tk), g * HEAD_DIM:(g + 1) * HEAD_DIM].astype(BF16)
            vc = v_ref[pl.ds(ks, tk), g * HEAD_DIM:(g + 1) * HEAD_DIM].astype(BF16)
            s = _dot_nt(qs_scr[g], kc) + bias
            m_prev = m_scr[g]
            m_new = jnp.maximum(m_prev, jnp.max(s, axis=1, keepdims=True))
            alpha = jnp.exp(m_prev - m_new)
            p = jnp.exp(s - m_new)
            l_scr[g] = alpha * l_scr[g] + jnp.sum(p, axis=1, keepdims=True)
            acc_scr[g] = alpha * acc_scr[g] + jnp.dot(p.astype(BF16), vc, preferred_element_type=F32)
            m_scr[g] = m_new
        return carry

    lax.fori_loop(0, nk, attn_chunk, 0)

    for g in range(N_KV_HEADS):
        o = acc_scr[g] / l_scr[g]
        for r in range(HEADS_PER_KV):
            hd = g * HEADS_PER_KV + r
            o_ref[:, hd * HEAD_DIM:(hd + 1) * HEAD_DIM] = o[r * tq:(r + 1) * tq, :].astype(BF16)


def _attn_prompt(z, b, s, topk):
    nq = s // Q_BLOCK
    tk = min(512, s)
    idx_bits = max(1, (s - 1).bit_length()) + 1
    body = functools.partial(_attn_prompt_body, tk=tk, topk=topk, idx_bits=idx_bits)
    rows = HEADS_PER_KV * Q_BLOCK
    return pl.pallas_call(
        body,
        out_shape=jax.ShapeDtypeStruct((b * s, ATTN_WIDTH), BF16),
        grid=(b, nq),
        in_specs=[
            pl.BlockSpec((Q_BLOCK, ATTN_WIDTH), lambda bb, i: (bb * nq + i, 0)),
            pl.BlockSpec((Q_BLOCK, QI_WIDTH // 2), lambda bb, i: (bb * nq + i, C_QI // (QI_WIDTH // 2))),
            pl.BlockSpec((Q_BLOCK, QI_WIDTH // 2), lambda bb, i: (bb * nq + i, C_QI // (QI_WIDTH // 2) + 1)),
            pl.BlockSpec((Q_BLOCK, LANES), lambda bb, i: (bb * nq + i, C_MISC // LANES)),
            pl.BlockSpec((s, KV_WIDTH), lambda bb, i: (bb, C_K // KV_WIDTH)),
            pl.BlockSpec((s, KV_WIDTH), lambda bb, i: (bb, C_V // KV_WIDTH)),
            pl.BlockSpec((s, LANES), lambda bb, i: (bb, C_MISC // LANES)),
        ],
        out_specs=pl.BlockSpec((Q_BLOCK, ATTN_WIDTH), lambda bb, i: (bb * nq + i, 0)),
        scratch_shapes=[
            pltpu.VMEM((Q_BLOCK, s), I32),
            pltpu.VMEM((Q_BLOCK, 1), I32),
            pltpu.VMEM((N_KV_HEADS, rows, HEAD_DIM), BF16),
            pltpu.VMEM((N_KV_HEADS, rows, 1), F32),
            pltpu.VMEM((N_KV_HEADS, rows, 1), F32),
            pltpu.VMEM((N_KV_HEADS, rows, HEAD_DIM), F32),
        ],
        compiler_params=_params(("parallel", "arbitrary")),
        name="attn_prompt",
    )(z, z, z, z, z, z, z)


def _log_decay(misc, w2, gb):
    x = jnp.dot(misc, w2, precision=HIGHEST, preferred_element_type=F32) + gb
    return (jnp.minimum(x, 0.0) - jnp.log1p(jnp.exp(-jnp.abs(x)))) * (1.0 / GLA_GATE_TAU)


def _gla_out(o, nw, gr):
    return (_rms(o, nw) * (gr * jax.nn.sigmoid(gr))).astype(BF16)


GLA_C = 128
GLA_SUB = 16


def _gla_prompt_body(gq_ref, gk_ref, gv_ref, misc_ref, gr_ref, w2_ref, gb_ref, nw_ref,
                     og_ref, sfin_ref, st_scr, a_scr, b_scr, o_scr, *, nchunk):
    t = pl.program_id(2)
    c_ = GLA_C

    @pl.when(t == 0)
    def _():
        st_scr[...] = jnp.zeros_like(st_scr)

    a_scr[...] = jnp.zeros_like(a_scr)
    row = lax.broadcasted_iota(I32, (c_, c_), 0)
    col = lax.broadcasted_iota(I32, (c_, c_), 1)
    tri = jnp.where(col <= row, 1.0, 0.0)
    sub_row = lax.broadcasted_iota(I32, (GLA_SUB, LANES), 0)
    sub_col = lax.broadcasted_iota(I32, (GLA_SUB, GLA_SUB), 1)

    def chunk(ci, carry):
        r0 = pl.multiple_of(ci * c_, c_)
        q = gq_ref[pl.ds(r0, c_), :] * (GLA_DK ** -0.5)
        k = gk_ref[pl.ds(r0, c_), :]
        v = gv_ref[pl.ds(r0, c_), :].astype(BF16)
        g = _log_decay(misc_ref[pl.ds(r0, c_), :], w2_ref[...], gb_ref[...])
        b = jnp.dot(tri, g, precision=HIGHEST, preferred_element_type=F32)
        b_scr[...] = b
        st = st_scr[...]
        o = _dot_nt((q * jnp.exp(b)).astype(BF16), st.astype(BF16))

        n = c_ // 2
        while n >= GLA_SUB:
            for rb in range(n, c_, 2 * n):
                bref = b[rb:rb + 1, :]
                qs = q[rb:rb + n, :] * jnp.exp(b[rb:rb + n, :] - bref)
                ks = k[rb - n:rb, :] * jnp.exp(bref - b[rb - n:rb, :])
                a_scr[rb:rb + n, rb - n:rb] = _dot_nt(qs.astype(BF16), ks.astype(BF16))
            n //= 2
        for blk in range(c_ // GLA_SUB):
            lo = blk * GLA_SUB
            qb = q[lo:lo + GLA_SUB, :]
            bb = b[lo:lo + GLA_SUB, :]
            ad = jnp.zeros((GLA_SUB, GLA_SUB), F32)
            for jj in range(GLA_SUB):
                bj = b_scr[pl.ds(lo + jj, 1), :]
                kj = gk_ref[pl.ds(r0 + lo + jj, 1), :]
                w = jnp.exp(jnp.where(sub_row >= jj, bb - bj, NEG))
                colj = jnp.sum(qb * kj * w, axis=1, keepdims=True)
                ad = jnp.where(sub_col == jj, colj, ad)
            a_scr[lo:lo + GLA_SUB, lo:lo + GLA_SUB] = ad

        o = o + jnp.dot(a_scr[...].astype(BF16), v, preferred_element_type=F32)
        o_scr[pl.ds(r0, c_), :] = o
        b_last = b[c_ - 1:c_, :]
        kd = (k * jnp.exp(b_last - b)).astype(BF16)
        kv = lax.dot_general(v, kd, (((0,), (0,)), ((), ())), preferred_element_type=F32)
        st_scr[...] = st * jnp.exp(b_last) + kv
        return carry

    lax.fori_loop(0, nchunk, chunk, 0)
    og_ref[...] = _gla_out(o_scr[...], nw_ref[...], gr_ref[...])

    @pl.when(t == pl.num_programs(2) - 1)
    def _():
        sfin_ref[0, 0] = st_scr[...].T


def _gla_prompt(z, w2p, gbias, norm_w, b, s):
    tb = min(512, s)
    nt = s // tb
    body = functools.partial(_gla_prompt_body, nchunk=tb // GLA_C)

    def rowblk(bb, t):
        return bb * nt + t

    return pl.pallas_call(
        body,
        out_shape=(jax.ShapeDtypeStruct((b * s, GLA_VAL_WIDTH), BF16),
                   jax.ShapeDtypeStruct((b, GLA_HEADS, GLA_DK, GLA_DV), F32)),
        grid=(b, GLA_HEADS, nt),
        in_specs=[
            pl.BlockSpec((tb, GLA_DK), lambda bb, h, t: (rowblk(bb, t), C_GQ // GLA_DK + h)),
            pl.BlockSpec((tb, GLA_DK), lambda bb, h, t: (rowblk(bb, t), C_GK // GLA_DK + h)),
            pl.BlockSpec((tb, GLA_DV), lambda bb, h, t: (rowblk(bb, t), C_GV // GLA_DV + h)),
            pl.BlockSpec((tb, LANES), lambda bb, h, t: (rowblk(bb, t), C_MISC // LANES)),
            pl.BlockSpec((tb, GLA_DV), lambda bb, h, t: (rowblk(bb, t), C_GR // GLA_DV + h)),
            pl.BlockSpec((LANES, GLA_DK), lambda bb, h, t: (0, h)),
            pl.BlockSpec((1, GLA_DK), lambda bb, h, t: (0, h)),
            pl.BlockSpec((1, GLA_DV), lambda bb, h, t: (0, 0)),
        ],
        out_specs=(pl.BlockSpec((tb, GLA_DV), lambda bb, h, t: (rowblk(bb, t), h)),
                   pl.BlockSpec((1, 1, GLA_DK, GLA_DV), lambda bb, h, t: (bb, h, 0, 0))),
        scratch_shapes=[
            pltpu.VMEM((GLA_DV, GLA_DK), F32),
            pltpu.VMEM((GLA_C, GLA_C), F32),
            pltpu.VMEM((GLA_C, GLA_DK), F32),
            pltpu.VMEM((tb, GLA_DV), F32),
        ],
        compiler_params=_params(("parallel", "parallel", "arbitrary")),
        name="gla_prompt",
    )(z, z, z, z, z, w2p, gbias, norm_w.reshape(1, GLA_DV))


def _merge_body(oa_ref, og_ref, ga_ref, gg_ref, wa_ref, wg_ref, o_ref):
    pa = jnp.dot(oa_ref[...], wa_ref[...], preferred_element_type=F32)
    pg = jnp.dot(og_ref[...], wg_ref[...], preferred_element_type=F32)
    o_ref[...] = (jax.nn.sigmoid(ga_ref[...]) * pa + jax.nn.sigmoid(gg_ref[...]) * pg).astype(BF16)


def _merge(o_attn, o_gla, z, wa, wg, tm):
    rows = o_attn.shape[0]
    d = wa.shape[1]
    return pl.pallas_call(
        _merge_body,
        out_shape=jax.ShapeDtypeStruct((rows, d), BF16),
        grid=(rows // tm,),
        in_specs=[
            pl.BlockSpec((tm, ATTN_WIDTH), lambda i: (i, 0)),
            pl.BlockSpec((tm, GLA_VAL_WIDTH), lambda i: (i, 0)),
            pl.BlockSpec((tm, d), lambda i: (i, C_GA // d)),
            pl.BlockSpec((tm, d), lambda i: (i, C_GA // d + 1)),
            pl.BlockSpec((ATTN_WIDTH, d), lambda i: (0, 0)),
            pl.BlockSpec((GLA_VAL_WIDTH, d), lambda i: (0, 0)),
        ],
        out_specs=pl.BlockSpec((tm, d), lambda i: (i, 0)),
        compiler_params=_params(("parallel",)),
        name="merge",
    )(o_attn, o_gla, z, z, wa, wg)


def _out_proj_body(m_ref, h_ref, w_ref, pw_ref, o_ref):
    y = jnp.dot(m_ref[...], w_ref[...], preferred_element_type=F32)
    o_ref[...] = h_ref[...] + _rms(y, pw_ref[...])


def _out_proj(merged, h, w_out, post_w, tm):
    rows, d = h.shape
    return pl.pallas_call(
        _out_proj_body,
        out_shape=jax.ShapeDtypeStruct((rows, d), F32),
        grid=(rows // tm,),
        in_specs=[
            pl.BlockSpec((tm, d), lambda i: (i, 0)),
            pl.BlockSpec((tm, d), lambda i: (i, 0)),
            pl.BlockSpec((d, d), lambda i: (0, 0)),
            pl.BlockSpec((1, d), lambda i: (0, 0)),
        ],
        out_specs=pl.BlockSpec((tm, d), lambda i: (i, 0)),
        compiler_params=_params(("parallel",)),
        name="out_proj",
    )(merged, h, w_out, post_w.reshape(1, d))


def _sample_scores_body(pt_ref, qi_ref, wi_ref, kidx_ref, o_ref):
    s = _dot_nt(qi_ref[0].astype(BF16), kidx_ref[0].astype(BF16))
    o_ref[0] = jnp.sum(jnp.maximum(s, 0.0) * wi_ref[0], axis=0, keepdims=True)


def _sample_scores(page_table, qi3, wi3, cache_kidx):
    nd, n_pages = page_table.shape
    page = cache_kidx.shape[1]
    grid_spec = pltpu.PrefetchScalarGridSpec(
        num_scalar_prefetch=1,
        grid=(nd, n_pages),
        in_specs=[
            pl.BlockSpec((1, IDX_HEADS, IDX_DIM), lambda b, p, pt: (b, 0, 0)),
            pl.BlockSpec((1, IDX_HEADS, 1), lambda b, p, pt: (b, 0, 0)),
            pl.BlockSpec((1, page, IDX_DIM), lambda b, p, pt: (pt[b, p], 0, 0)),
        ],
        out_specs=pl.BlockSpec((1, 1, page), lambda b, p, pt: (b * n_pages + p, 0, 0)),
    )
    return pl.pallas_call(
        _sample_scores_body,
        out_shape=jax.ShapeDtypeStruct((nd * n_pages, 1, page), F32),
        grid_spec=grid_spec,
        compiler_params=_params(("arbitrary", "arbitrary")),
        name="sample_scores",
    )(page_table, qi3, wi3, cache_kidx)


def _sample_select_body(sp_ref, qi_ref, misc_ref, bp_ref, bs_ref, key_scr, j_scr, *, past, tk, topk, idx_bits):
    rows = sp_ref.shape[0]
    misc = misc_ref[...]
    d_in = lax.broadcasted_iota(I32, (LANES, QI_WIDTH), 0)
    c_out = lax.broadcasted_iota(I32, (LANES, QI_WIDTH), 1)
    rep = jnp.where((d_in < IDX_DIM) & (c_out % IDX_DIM == d_in), 1.0, 0.0)
    ki_t = jnp.dot(misc, rep, precision=HIGHEST, preferred_element_type=F32)
    c_in = lax.broadcasted_iota(I32, (QI_WIDTH, LANES), 0)
    l_out = lax.broadcasted_iota(I32, (QI_WIDTH, LANES), 1)
    seg = jnp.where(l_out == MISC_WI + c_in // IDX_DIM, 1.0, 0.0)
    hd = jnp.dot(qi_ref[...] * ki_t, seg, precision=HIGHEST, preferred_element_type=F32)
    lane = lax.broadcasted_iota(I32, (rows, LANES), 1)
    is_wi = (lane >= MISC_WI) & (lane < MISC_GLR)
    s_self = jnp.sum(jnp.where(is_wi, jnp.maximum(hd, 0.0) * misc, 0.0), axis=1, keepdims=True)

    key_scr[:, 0:past] = _sortable(sp_ref[...])
    key_scr[:, past:] = _sortable(jnp.where(lax.broadcasted_iota(I32, (rows, tk), 1) == 0, s_self, -jnp.inf))
    nk = key_scr.shape[1] // tk
    thr = _select_threshold(key_scr, j_scr, nk, tk, topk, rows, idx_bits)
    jb = j_scr[...]
    kpos = lax.broadcasted_iota(I32, key_scr.shape, 1)
    bias = jnp.where(_selected(key_scr[...], kpos, thr, jb), 0.0, NEG)
    bp_ref[...] = bias[:, 0:past]
    bs_ref[...] = bias[:, past:past + LANES]


def _sample_select(s_past, qi2, misc, topk):
    nd, past = s_past.shape
    tk = LANES
    width = past + tk
    idx_bits = max(1, (width - 1).bit_length()) + 1
    body = functools.partial(_sample_select_body, past=past, tk=tk, topk=topk, idx_bits=idx_bits)
    return pl.pallas_call(
        body,
        out_shape=(jax.ShapeDtypeStruct((nd, past), F32), jax.ShapeDtypeStruct((nd, LANES), F32)),
        grid=(1,),
        in_specs=[
            pl.BlockSpec((nd, past), lambda i: (0, 0)),
            pl.BlockSpec((nd, QI_WIDTH), lambda i: (0, 0)),
            pl.BlockSpec((nd, LANES), lambda i: (0, 0)),
        ],
        out_specs=(pl.BlockSpec((nd, past), lambda i: (0, 0)), pl.BlockSpec((nd, LANES), lambda i: (0, 0))),
        scratch_shapes=[pltpu.VMEM((nd, width), I32), pltpu.VMEM((nd, 1), I32)],
        compiler_params=_params(("arbitrary",)),
        name="sample_select",
    )(s_past, qi2, misc)


def _sample_attn_body(pt_ref, q_ref, ks_ref, vs_ref, bp_ref, bs_ref, ck_ref, cv_ref, o_ref, m_scr, l_scr, acc_scr):
    p_idx = pl.program_id(1)
    q = q_ref[0]
    qb = q.astype(BF16)
    grp0 = lax.broadcasted_iota(I32, (N_HEADS, HEAD_DIM), 0) < HEADS_PER_KV

    @pl.when(p_idx == 0)
    def _():
        m_scr[...] = jnp.full(m_scr.shape, NEG, F32)
        l_scr[...] = jnp.zeros(l_scr.shape, F32)
        acc_scr[...] = jnp.zeros(acc_scr.shape, F32)

    kp = ck_ref[0].astype(BF16)
    vp = cv_ref[0].astype(BF16)
    s = jnp.where(grp0, _dot_nt(qb, kp[:, 0:HEAD_DIM]), _dot_nt(qb, kp[:, HEAD_DIM:])) + bp_ref[0]
    m_prev = m_scr[...]
    m_new = jnp.maximum(m_prev, jnp.max(s, axis=1, keepdims=True))
    alpha = jnp.exp(m_prev - m_new)
    p = jnp.exp(s - m_new).astype(BF16)
    pv = jnp.where(grp0, jnp.dot(p, vp[:, 0:HEAD_DIM], preferred_element_type=F32),
                   jnp.dot(p, vp[:, HEAD_DIM:], preferred_element_type=F32))
    l_scr[...] = alpha * l_scr[...] + jnp.sum(p.astype(F32), axis=1, keepdims=True)
    acc_scr[...] = alpha * acc_scr[...] + pv
    m_scr[...] = m_new

    @pl.when(p_idx == pl.num_programs(1) - 1)
    def _():
        k_self = jnp.where(grp0, ks_ref[0][:, 0:HEAD_DIM], ks_ref[0][:, HEAD_DIM:])
        v_self = jnp.where(grp0, vs_ref[0][:, 0:HEAD_DIM], vs_ref[0][:, HEAD_DIM:])
        s_self = jnp.sum(q * k_self, axis=1, keepdims=True) + bs_ref[0][:, 0:1]
        m_prev = m_scr[...]
        m_new = jnp.maximum(m_prev, s_self)
        alpha = jnp.exp(m_prev - m_new)
        p_self = jnp.exp(s_self - m_new)
        l_fin = alpha * l_scr[...] + p_self
        acc = alpha * acc_scr[...] + p_self * v_self
        o_ref[0] = (acc / l_fin).astype(BF16)


def _sample_attn(page_table, q3, k_self, v_self, bias_past, bias_self, cache_k, cache_v):
    nd, n_pages = page_table.shape
    page = cache_k.shape[1]
    grid_spec = pltpu.PrefetchScalarGridSpec(
        num_scalar_prefetch=1,
        grid=(nd, n_pages),
        in_specs=[
            pl.BlockSpec((1, N_HEADS, HEAD_DIM), lambda b, p, pt: (b, 0, 0)),
            pl.BlockSpec((1, 1, KV_WIDTH), lambda b, p, pt: (b, 0, 0)),
            pl.BlockSpec((1, 1, KV_WIDTH), lambda b, p, pt: (b, 0, 0)),
            pl.BlockSpec((1, 1, page), lambda b, p, pt: (b * n_pages + p, 0, 0)),
            pl.BlockSpec((1, 1, LANES), lambda b, p, pt: (b, 0, 0)),
            pl.BlockSpec((1, page, KV_WIDTH), lambda b, p, pt: (pt[b, p], 0, 0)),
            pl.BlockSpec((1, page, KV_WIDTH), lambda b, p, pt: (pt[b, p], 0, 0)),
        ],
        out_specs=pl.BlockSpec((1, N_HEADS, HEAD_DIM), lambda b, p, pt: (b, 0, 0)),
        scratch_shapes=[
            pltpu.VMEM((N_HEADS, 1), F32),
            pltpu.VMEM((N_HEADS, 1), F32),
            pltpu.VMEM((N_HEADS, HEAD_DIM), F32),
        ],
    )
    return pl.pallas_call(
        _sample_attn_body,
        out_shape=jax.ShapeDtypeStruct((nd, N_HEADS, HEAD_DIM), BF16),
        grid_spec=grid_spec,
        compiler_params=_params(("arbitrary", "arbitrary")),
        name="sample_attn",
    )(page_table, q3, k_self, v_self, bias_past, bias_self, cache_k, cache_v)


def _gla_sample_body(gq_ref, gk_ref, gv_ref, misc_ref, gr_ref, s_ref, w2_ref, gb_ref, nw_ref, og_ref, so_ref):
    eye = jnp.where(lax.broadcasted_iota(I32, (GLA_DK, GLA_DK), 0)
                    == lax.broadcasted_iota(I32, (GLA_DK, GLA_DK), 1), 1.0, 0.0)

    def column(row):
        return jnp.sum(eye * row, axis=1, keepdims=True)

    misc = misc_ref[0]
    for h in range(GLA_HEADS):
        ksl = slice(h * GLA_DK, (h + 1) * GLA_DK)
        vsl = slice(h * GLA_DV, (h + 1) * GLA_DV)
        g = _log_decay(misc, w2_ref[:, ksl], gb_ref[:, ksl])
        s_new = column(jnp.exp(g)) * s_ref[0, h] + column(gk_ref[0][:, ksl]) * gv_ref[0][:, vsl]
        so_ref[0, h] = s_new
        o = jnp.sum(column(gq_ref[0][:, ksl] * (GLA_DK ** -0.5)) * s_new, axis=0, keepdims=True)
        og_ref[0, :, vsl] = _gla_out(o, nw_ref[...], gr_ref[0][:, vsl])


def _gla_sample(gq, gk, gv, misc, gr, state, w2p, gbias, norm_w):
    nd = state.shape[0]

    def row3(w):
        return pl.BlockSpec((1, 1, w), lambda b: (b, 0, 0))

    st_spec = pl.BlockSpec((1, GLA_HEADS, GLA_DK, GLA_DV), lambda b: (b, 0, 0, 0))
    return pl.pallas_call(
        _gla_sample_body,
        out_shape=(jax.ShapeDtypeStruct((nd, 1, GLA_VAL_WIDTH), BF16),
                   jax.ShapeDtypeStruct(state.shape, F32)),
        grid=(nd,),
        in_specs=[
            row3(GLA_KEY_WIDTH), row3(GLA_KEY_WIDTH), row3(GLA_VAL_WIDTH), row3(LANES), row3(GLA_VAL_WIDTH),
            st_spec,
            pl.BlockSpec((LANES, GLA_KEY_WIDTH), lambda b: (0, 0)),
            pl.BlockSpec((1, GLA_KEY_WIDTH), lambda b: (0, 0)),
            pl.BlockSpec((1, GLA_DV), lambda b: (0, 0)),
        ],
        out_specs=(row3(GLA_VAL_WIDTH), st_spec),
        compiler_params=_params(("parallel",)),
        name="gla_sample",
    )(gq, gk, gv, misc, gr, state, w2p, gbias, norm_w.reshape(1, GLA_DV))


def _rope_tables(pos, d):
    inv = ROPE_THETA ** (-jnp.arange(0, d, 2, dtype=F32) / d)
    ang = pos.astype(F32)[:, None] * inv[None, :]
    cos, sin = jnp.cos(ang), jnp.sin(ang)
    reps = LANES // d
    return (jnp.tile(jnp.concatenate([cos, cos], axis=-1), (1, reps)),
            jnp.tile(jnp.concatenate([-sin, sin], axis=-1), (1, reps)))


def _permute_w_in(w_in, d):
    sizes = (ATTN_WIDTH, KV_WIDTH, KV_WIDTH, QI_WIDTH, IDX_DIM, IDX_HEADS, GLA_KEY_WIDTH, GLA_KEY_WIDTH,
             GLA_VAL_WIDTH, GLA_GATE_RANK, GLA_VAL_WIDTH, d, d)
    pts, acc = [], 0
    for sz in sizes[:-1]:
        acc += sz
        pts.append(acc)
    q, k, v, qi, ki, wi, gq, gk, gv, glr, gr, ga, gg = jnp.split(w_in, pts, axis=1)
    pad = jnp.zeros((w_in.shape[0], C_GQ - (C_MISC + IDX_DIM + IDX_HEADS + GLA_GATE_RANK)), w_in.dtype)
    return jnp.concatenate([q, k, v, qi, ki, wi, glr, pad, gq, gk, gv, gr, ga, gg], axis=1).astype(BF16)


def kernel(x_prompt, x_sample, cache_k, cache_v, cache_kidx, page_table, state_gla,
           ffn1_pre_w, ffn1_w_gate, ffn1_w_up, ffn1_w_down, ffn1_post_w,
           mix_pre_w, w_in, gla_gate_w2, gla_gate_b, gla_norm_w,
           w_proj_attn, w_proj_gla, w_out, mix_post_w,
           ffn2_pre_w, ffn2_w_gate, ffn2_w_up, ffn2_w_down, ffn2_post_w):
    b, s, d = x_prompt.shape
    nd, td, _ = x_sample.shape
    n_pool, page = cache_k.shape[:2]
    n_pages = page_table.shape[1]
    past = n_pages * page
    assert td == 1 and s % Q_BLOCK == 0 and d % MIX_TN == 0 and C_GA % d == 0

    f1 = (ffn1_w_gate.astype(BF16), ffn1_w_up.astype(BF16), ffn1_w_down.astype(BF16))
    f2 = (ffn2_w_gate.astype(BF16), ffn2_w_up.astype(BF16), ffn2_w_down.astype(BF16))
    w_perm = _permute_w_in(w_in, d)
    wpa, wpg, wo = w_proj_attn.astype(BF16), w_proj_gla.astype(BF16), w_out.astype(BF16)
    w2p = jnp.zeros((LANES, GLA_KEY_WIDTH), F32).at[MISC_GLR:MISC_GLR + GLA_GATE_RANK].set(gla_gate_w2)
    gbias = gla_gate_b.reshape(1, GLA_KEY_WIDTH)

    def trunk(x, tm, tabs, tab_blocks, mixer):
        h = _ffn(x, ffn1_pre_w, *f1, ffn1_post_w, tm)
        z = _mix_in(h, mix_pre_w, w_perm, tabs, tm, tab_blocks)
        o_attn, o_gla, s_fin = mixer(z)
        tmm = min(tm, 256)
        merged = _merge(o_attn, o_gla, z, wpa, wpg, tmm)
        h = _out_proj(merged, h, wo, mix_post_w, tmm)
        y = _ffn(h, ffn2_pre_w, *f2, ffn2_post_w, tm)
        return y, z, s_fin

    pos_p = jnp.arange(s, dtype=I32)
    tm_p = 512 if s % 512 == 0 else Q_BLOCK
    tabs_p = _rope_tables(pos_p, HEAD_DIM) + _rope_tables(pos_p, IDX_DIM)
    top_p = min(TOPK_MAX, s // 4)

    def mixer_p(z):
        o_attn = _attn_prompt(z, b, s, top_p)
        o_gla, s_fin = _gla_prompt(z, w2p, gbias, gla_norm_w, b, s)
        return o_attn, o_gla, s_fin

    y_p, z_p, gla_p = trunk(x_prompt.reshape(b * s, d), tm_p, tabs_p, s // tm_p, mixer_p)

    pos_s = jnp.full((nd,), past, I32)
    tabs_s = _rope_tables(pos_s, HEAD_DIM) + _rope_tables(pos_s, IDX_DIM)
    top_s = min(TOPK_MAX, (past + td) // 4)
    ck = cache_k.reshape(n_pool, page, KV_WIDTH)
    cv = cache_v.reshape(n_pool, page, KV_WIDTH)

    def mixer_s(z):
        misc = z[:, C_MISC:C_MISC + LANES]
        qi2 = z[:, C_QI:C_QI + QI_WIDTH]
        s_past = _sample_scores(page_table, qi2.reshape(nd, IDX_HEADS, IDX_DIM),
                                misc[:, MISC_WI:MISC_GLR].reshape(nd, IDX_HEADS, 1), cache_kidx)
        bias_past, bias_self = _sample_select(s_past.reshape(nd, past), qi2, misc, top_s)
        o_attn = _sample_attn(page_table, z[:, C_Q:C_Q + ATTN_WIDTH].reshape(nd, N_HEADS, HEAD_DIM),
                              z[:, C_K:C_K + KV_WIDTH].reshape(nd, 1, KV_WIDTH),
                              z[:, C_V:C_V + KV_WIDTH].reshape(nd, 1, KV_WIDTH),
                              bias_past.reshape(nd * n_pages, 1, page), bias_self.reshape(nd, 1, LANES), ck, cv)
        o_gla, s_fin = _gla_sample(z[:, C_GQ:C_GQ + GLA_KEY_WIDTH].reshape(nd, 1, GLA_KEY_WIDTH),
                                   z[:, C_GK:C_GK + GLA_KEY_WIDTH].reshape(nd, 1, GLA_KEY_WIDTH),
                                   z[:, C_GV:C_GV + GLA_VAL_WIDTH].reshape(nd, 1, GLA_VAL_WIDTH),
                                   misc.reshape(nd, 1, LANES),
                                   z[:, C_GR:C_GR + GLA_VAL_WIDTH].reshape(nd, 1, GLA_VAL_WIDTH),
                                   state_gla, w2p, gbias, gla_norm_w)
        return o_attn.reshape(nd, ATTN_WIDTH), o_gla.reshape(nd, GLA_VAL_WIDTH), s_fin

    y_s, z_s, gla_s = trunk(x_sample.reshape(nd, d), nd, tabs_s, 1, mixer_s)

    def kv_out(z, n, t):
        return (z[:, C_K:C_K + KV_WIDTH].reshape(n, t, N_KV_HEADS, HEAD_DIM),
                z[:, C_V:C_V + KV_WIDTH].reshape(n, t, N_KV_HEADS, HEAD_DIM),
                z[:, C_MISC:C_MISC + IDX_DIM].reshape(n, t, IDX_DIM))

    k_p, v_p, ki_p = kv_out(z_p, b, s)
    k_s, v_s, ki_s = kv_out(z_s, nd, td)
    return (y_p.reshape(b, s, d), y_s.reshape(nd, td, d), k_p, v_p, ki_p, gla_p, k_s, v_s, ki_s, gla_s)
```

```python
import functools

import jax
import jax.numpy as jnp
from jax import lax
from jax.experimental import pallas as pl
from jax.experimental.pallas import tpu as pltpu

F32, BF16, I32 = jnp.float32, jnp.bfloat16, jnp.int32
HIGHEST = lax.Precision.HIGHEST

N_HEADS = 8
N_KV_HEADS = 2
HEAD_DIM = 128
IDX_HEADS = 16
IDX_DIM = 64
TOPK_MAX = 256
Q_BLOCK = 128
ROPE_THETA = 10000.0
GLA_HEADS = 4
GLA_DK = 128
GLA_DV = 256
GLA_GATE_RANK = 16
GLA_GATE_TAU = 16.0
RMS_EPS = 1e-6

LANES = 128
ATTN_WIDTH = N_HEADS * HEAD_DIM
KV_WIDTH = N_KV_HEADS * HEAD_DIM
QI_WIDTH = IDX_HEADS * IDX_DIM
GLA_KEY_WIDTH = GLA_HEADS * GLA_DK
GLA_VAL_WIDTH = GLA_HEADS * GLA_DV
HEADS_PER_KV = N_HEADS // N_KV_HEADS

C_Q, C_K, C_V, C_QI, C_MISC = 0, 1024, 1280, 1536, 2560
C_GQ, C_GK, C_GV, C_GR, C_GA = 3072, 3584, 4096, 5120, 6144
MISC_WI, MISC_GLR = IDX_DIM, IDX_DIM + IDX_HEADS
MIX_TN = 512

NEG = -1e30
INT_MIN = -2 ** 31
NEG_INF_KEY = -2139095041
VMEM_LIMIT = 56 * 1024 * 1024


def _params(sem, vmem=VMEM_LIMIT):
    return pltpu.CompilerParams(dimension_semantics=sem, vmem_limit_bytes=vmem)


def _rms(x, w):
    return x * lax.rsqrt(jnp.mean(x * x, axis=-1, keepdims=True) + RMS_EPS) * w


def _dot_nt(a, b):
    return lax.dot_general(a, b, (((1,), (1,)), ((), ())), preferred_element_type=F32)


def _ffn_body(x_ref, prew_ref, wg_ref, wu_ref, wd_ref, postw_ref, o_ref, z_scr, acc_scr):
    j = pl.program_id(1)

    @pl.when(j == 0)
    def _():
        z_scr[...] = _rms(x_ref[...], prew_ref[...]).astype(BF16)
        acc_scr[...] = jnp.zeros_like(acc_scr)

    z = z_scr[...]
    g = jnp.dot(z, wg_ref[...], preferred_element_type=F32)
    u = jnp.dot(z, wu_ref[...], preferred_element_type=F32)
    a = (g * jax.nn.sigmoid(g) * u).astype(BF16)
    acc_scr[...] += jnp.dot(a, wd_ref[...], preferred_element_type=F32)

    @pl.when(j == pl.num_programs(1) - 1)
    def _():
        o_ref[...] = x_ref[...] + 0.5 * _rms(acc_scr[...], postw_ref[...])


def _ffn(x, pre_w, wg, wu, wd, post_w, tm):
    rows, d = x.shape
    dff = wg.shape[1]
    tf = 512 if dff % 512 == 0 else dff
    return pl.pallas_call(
        _ffn_body,
        out_shape=jax.ShapeDtypeStruct((rows, d), F32),
        grid=(rows // tm, dff // tf),
        in_specs=[
            pl.BlockSpec((tm, d), lambda i, j: (i, 0)),
            pl.BlockSpec((1, d), lambda i, j: (0, 0)),
            pl.BlockSpec((d, tf), lambda i, j: (0, j)),
            pl.BlockSpec((d, tf), lambda i, j: (0, j)),
            pl.BlockSpec((tf, d), lambda i, j: (j, 0)),
            pl.BlockSpec((1, d), lambda i, j: (0, 0)),
        ],
        out_specs=pl.BlockSpec((tm, d), lambda i, j: (i, 0)),
        scratch_shapes=[pltpu.VMEM((tm, d), BF16), pltpu.VMEM((tm, d), F32)],
        compiler_params=_params(("parallel", "arbitrary")),
        name="ffn",
    )(x, pre_w.reshape(1, d), wg, wu, wd, post_w.reshape(1, d))


def _mix_in_body(x_ref, prew_ref, w_ref, c128_ref, s128_ref, c64_ref, s64_ref, o_ref, u_scr):
    j = pl.program_id(1)

    @pl.when(j == 0)
    def _():
        u_scr[...] = _rms(x_ref[...], prew_ref[...]).astype(BF16)

    r = jnp.dot(u_scr[...], w_ref[...], preferred_element_type=F32)

    def rope128(x):
        return x * c128_ref[...] + pltpu.roll(x, HEAD_DIM // 2, 1) * s128_ref[...]

    def rope64(x):
        lane = lax.broadcasted_iota(I32, x.shape, 1)
        first = (lane % IDX_DIM) < (IDX_DIM // 2)
        rot = jnp.where(first, pltpu.roll(x, LANES - IDX_DIM // 2, 1), pltpu.roll(x, IDX_DIM // 2, 1))
        return x * c64_ref[...] + rot * s64_ref[...]

    def sl(t):
        return slice(t * LANES, (t + 1) * LANES)

    @pl.when(j < 2)
    def _():
        for t in range(4):
            o_ref[:, sl(t)] = rope128(r[:, sl(t)]) * (HEAD_DIM ** -0.5)

    @pl.when(j == 2)
    def _():
        for t in range(2):
            o_ref[:, sl(t)] = rope128(r[:, sl(t)])
        o_ref[:, 2 * LANES:] = r[:, 2 * LANES:]

    @pl.when((j == 3) | (j == 4))
    def _():
        for t in range(4):
            o_ref[:, sl(t)] = rope64(r[:, sl(t)])

    @pl.when(j == 5)
    def _():
        x = r[:, sl(0)]
        lane = lax.broadcasted_iota(I32, x.shape, 1)
        wi_scale = IDX_HEADS ** -0.5 * IDX_DIM ** -0.5
        y = jnp.where(lane < MISC_WI, rope64(x), jnp.where(lane < MISC_GLR, x * wi_scale, x))
        o_ref[:, sl(0)] = y
        o_ref[:, LANES:] = r[:, LANES:]

    @pl.when(j > 5)
    def _():
        o_ref[...] = r


def _mix_in(h, pre_w, w_perm, tabs, tm, tab_blocks):
    rows, d = h.shape
    zw = w_perm.shape[1]
    c128, s128, c64, s64 = tabs
    tab_spec = pl.BlockSpec((tm, LANES), lambda i, j: (i % tab_blocks, 0))
    return pl.pallas_call(
        _mix_in_body,
        out_shape=jax.ShapeDtypeStruct((rows, zw), F32),
        grid=(rows // tm, zw // MIX_TN),
        in_specs=[
            pl.BlockSpec((tm, d), lambda i, j: (i, 0)),
            pl.BlockSpec((1, d), lambda i, j: (0, 0)),
            pl.BlockSpec((d, MIX_TN), lambda i, j: (0, j)),
            tab_spec, tab_spec, tab_spec, tab_spec,
        ],
        out_specs=pl.BlockSpec((tm, MIX_TN), lambda i, j: (i, j)),
        scratch_shapes=[pltpu.VMEM((tm, d), BF16)],
        compiler_params=_params(("parallel", "arbitrary")),
        name="mix_in",
    )(h, pre_w.reshape(1, d), w_perm, c128, s128, c64, s64)


def _key_to_float(key):
    key = jnp.maximum(key, NEG_INF_KEY)
    return lax.bitcast_convert_type(key ^ ((key >> 31) & jnp.int32(0x7FFFFFFF)), F32)


def _select_threshold(sc_ref, j_ref, nk, tk, topk, rows, idx_bits):
    def count(pred):
        def body(c, acc):
            ks = pl.multiple_of(c * tk, tk)
            m = jnp.where(pred(sc_ref[:, pl.ds(ks, tk)], ks), 1.0, 0.0)
            for t in range(tk // LANES):
                acc = acc + m[:, t * LANES:(t + 1) * LANES]
            return acc
        acc = lax.fori_loop(0, nk, body, jnp.zeros((rows, LANES), F32))
        return jnp.sum(acc, axis=1, keepdims=True)

    assert (sc_ref.shape[1] // LANES) < 64

    def bits_body(it, t):
        lo = lax.shift_left(jnp.int32(1), 30 - 2 * it)
        c1, c2 = t ^ lo, t ^ lax.shift_left(lo, 1)
        c3 = c2 ^ lo
        f1, f2, f3 = _key_to_float(c1), _key_to_float(c2), _key_to_float(c3)

        def body(c, acc):
            blk = sc_ref[:, pl.ds(pl.multiple_of(c * tk, tk), tk)]
            e = jnp.where(blk >= f3, 4161.0, jnp.where(blk >= f2, 65.0, jnp.where(blk >= f1, 1.0, 0.0)))
            for tt in range(tk // LANES):
                acc = acc + e[:, tt * LANES:(tt + 1) * LANES]
            return acc
        acc = lax.fori_loop(0, nk, body, jnp.zeros((rows, LANES), F32))
        e3 = jnp.floor(acc * (1.0 / 4096.0))
        rem = acc - 4096.0 * e3
        e2 = jnp.floor(rem * (1.0 / 64.0))
        n1 = jnp.sum(rem - 64.0 * e2, axis=1, keepdims=True)
        n2 = jnp.sum(e2, axis=1, keepdims=True)
        n3 = jnp.sum(e3, axis=1, keepdims=True)
        return jnp.where(n3 >= topk, c3, jnp.where(n2 >= topk, c2, jnp.where(n1 >= topk, c1, t)))

    thr_key = lax.fori_loop(0, 16, bits_body, jnp.full((rows, 1), INT_MIN, I32))
    thr = _key_to_float(thr_key)

    cnt_ge = count(lambda blk, ks: blk >= thr)
    cnt_gt = count(lambda blk, ks: blk > thr)
    need = topk - cnt_gt
    tie = jnp.where((cnt_ge > topk) & (thr_key > NEG_INF_KEY), 1.0, 0.0)
    j_ref[...] = jnp.full((rows, 1), 1 << idx_bits, I32)

    @pl.when(jnp.max(tie) > 0.0)
    def _():
        def jbit(bi, jb):
            cand = jb + lax.shift_left(jnp.int32(1), idx_bits - 1 - bi)

            def pred(blk, ks):
                kpos = ks + lax.broadcasted_iota(I32, blk.shape, 1)
                return (blk == thr) & (kpos < cand)
            return jnp.where(count(pred) <= need, cand, jb)
        j_ref[...] = lax.fori_loop(0, idx_bits, jbit, jnp.zeros((rows, 1), I32))

    return thr


def _selected(score, kpos, thr, jb):
    return (score > thr) | ((score == thr) & (kpos < jb))


def _attn_prompt_body(q_ref, qia_ref, qib_ref, misc_ref, k_ref, v_ref, kim_ref, o_ref,
                      sc_scr, j_scr, bias_scr, kb_scr, vb_scr, kib_scr, qs_scr, mx_scr, l_scr, acc_scr,
                      *, tk, topk, idx_bits):
    i = pl.program_id(1)
    tq = Q_BLOCK
    nk = (i * tq + tq + tk - 1) // tk
    qpos = i * tq + lax.broadcasted_iota(I32, (tq, 1), 0)
    n_tiles = tk // LANES

    @pl.when(i == 0)
    def _():
        def cast_chunk(c, carry):
            rs = pl.ds(pl.multiple_of(c * tk, tk), tk)
            kb_scr[rs, :] = k_ref[rs, :].astype(BF16)
            vb_scr[rs, :] = v_ref[rs, :].astype(BF16)
            kib_scr[rs, :] = kim_ref[rs, 0:IDX_DIM].astype(BF16)
            return carry
        lax.fori_loop(0, k_ref.shape[0] // tk, cast_chunk, 0)

    def score_chunk(c, carry):
        ks = pl.multiple_of(c * tk, tk)
        ki = kib_scr[pl.ds(ks, tk), :]
        acc = jnp.zeros((tq, tk), F32)
        for h in range(IDX_HEADS):
            ref = qia_ref if h < IDX_HEADS // 2 else qib_ref
            hh = h % (IDX_HEADS // 2)
            qh = ref[:, hh * IDX_DIM:(hh + 1) * IDX_DIM].astype(BF16)
            w = misc_ref[:, MISC_WI + h:MISC_WI + h + 1]
            acc = acc + w * jnp.maximum(_dot_nt(qh, ki), 0.0)
        kpos = ks + lax.broadcasted_iota(I32, (tq, tk), 1)
        sc_scr[:, pl.ds(ks, tk)] = jnp.where(kpos <= qpos, acc, -jnp.inf)
        return carry

    lax.fori_loop(0, nk, score_chunk, 0)
    thr = _select_threshold(sc_scr, j_scr, nk, tk, topk, tq, idx_bits)
    jb = j_scr[...]

    for g in range(N_KV_HEADS):
        for r in range(HEADS_PER_KV):
            hd = g * HEADS_PER_KV + r
            qs_scr[g, r * tq:(r + 1) * tq, :] = q_ref[:, hd * HEAD_DIM:(hd + 1) * HEAD_DIM].astype(BF16)
    mx_scr[...] = jnp.full(mx_scr.shape, NEG, F32)
    l_scr[...] = jnp.zeros(l_scr.shape, F32)
    acc_scr[...] = jnp.zeros(acc_scr.shape, F32)

    def logits(g, ks, bias4):
        kc = kb_scr[pl.ds(ks, tk), g * HEAD_DIM:(g + 1) * HEAD_DIM]
        return _dot_nt(qs_scr[g], kc) + bias4

    def max_pass(c, carry):
        ks = pl.multiple_of(c * tk, tk)
        kpos = ks + lax.broadcasted_iota(I32, (tq, tk), 1)
        sel = _selected(sc_scr[:, pl.ds(ks, tk)], kpos, thr, jb) & (kpos <= qpos)
        bias = jnp.where(sel, 0.0, NEG)
        bias_scr[:, pl.ds(ks, tk)] = bias
        bias4 = jnp.concatenate([bias] * HEADS_PER_KV, axis=0)
        for g in range(N_KV_HEADS):
            s = logits(g, ks, bias4)
            m = mx_scr[g]
            for t in range(n_tiles):
                m = jnp.maximum(m, s[:, t * LANES:(t + 1) * LANES])
            mx_scr[g] = m
        return carry

    lax.fori_loop(0, nk, max_pass, 0)
    for g in range(N_KV_HEADS):
        mx_scr[g] = jnp.broadcast_to(jnp.max(mx_scr[g], axis=1, keepdims=True), mx_scr.shape[1:])

    def exp_pass(c, carry):
        ks = pl.multiple_of(c * tk, tk)
        bias4 = jnp.concatenate([bias_scr[:, pl.ds(ks, tk)]] * HEADS_PER_KV, axis=0)
        for g in range(N_KV_HEADS):
            s = logits(g, ks, bias4)
            m = mx_scr[g]
            p = [jnp.exp(s[:, t * LANES:(t + 1) * LANES] - m) for t in range(n_tiles)]
            l_scr[g] += functools.reduce(lambda x, y: x + y, p)
            pb = jnp.concatenate(p, axis=1).astype(BF16)
            vc = vb_scr[pl.ds(ks, tk), g * HEAD_DIM:(g + 1) * HEAD_DIM]
            acc_scr[g] += jnp.dot(pb, vc, preferred_element_type=F32)
        return carry

    lax.fori_loop(0, nk, exp_pass, 0)

    for g in range(N_KV_HEADS):
        o = acc_scr[g] / jnp.sum(l_scr[g], axis=1, keepdims=True)
        for r in range(HEADS_PER_KV):
            hd = g * HEADS_PER_KV + r
            o_ref[:, hd * HEAD_DIM:(hd + 1) * HEAD_DIM] = o[r * tq:(r + 1) * tq, :].astype(BF16)


def _attn_prompt(z, b, s, topk):
    nq = s // Q_BLOCK
    tk = min(512, s)
    idx_bits = max(1, (s - 1).bit_length()) + 1
    body = functools.partial(_attn_prompt_body, tk=tk, topk=topk, idx_bits=idx_bits)
    rows = HEADS_PER_KV * Q_BLOCK
    return pl.pallas_call(
        body,
        out_shape=jax.ShapeDtypeStruct((b * s, ATTN_WIDTH), BF16),
        grid=(b, nq),
        in_specs=[
            pl.BlockSpec((Q_BLOCK, ATTN_WIDTH), lambda bb, i: (bb * nq + i, 0)),
            pl.BlockSpec((Q_BLOCK, QI_WIDTH // 2), lambda bb, i: (bb * nq + i, C_QI // (QI_WIDTH // 2))),
            pl.BlockSpec((Q_BLOCK, QI_WIDTH // 2), lambda bb, i: (bb * nq + i, C_QI // (QI_WIDTH // 2) + 1)),
            pl.BlockSpec((Q_BLOCK, LANES), lambda bb, i: (bb * nq + i, C_MISC // LANES)),
            pl.BlockSpec((s, KV_WIDTH), lambda bb, i: (bb, C_K // KV_WIDTH)),
            pl.BlockSpec((s, KV_WIDTH), lambda bb, i: (bb, C_V // KV_WIDTH)),
            pl.BlockSpec((s, LANES), lambda bb, i: (bb, C_MISC // LANES)),
        ],
        out_specs=pl.BlockSpec((Q_BLOCK, ATTN_WIDTH), lambda bb, i: (bb * nq + i, 0)),
        scratch_shapes=[
            pltpu.VMEM((Q_BLOCK, s), F32),
            pltpu.VMEM((Q_BLOCK, 1), I32),
            pltpu.VMEM((Q_BLOCK, s), F32),
            pltpu.VMEM((s, KV_WIDTH), BF16),
            pltpu.VMEM((s, KV_WIDTH), BF16),
            pltpu.VMEM((s, IDX_DIM), BF16),
            pltpu.VMEM((N_KV_HEADS, rows, HEAD_DIM), BF16),
            pltpu.VMEM((N_KV_HEADS, rows, LANES), F32),
            pltpu.VMEM((N_KV_HEADS, rows, LANES), F32),
            pltpu.VMEM((N_KV_HEADS, rows, HEAD_DIM), F32),
        ],
        compiler_params=_params(("parallel", "arbitrary")),
        name="attn_prompt",
    )(z, z, z, z, z, z, z)


def _log_decay(misc, w2, gb):
    x = jnp.dot(misc, w2, precision=HIGHEST, preferred_element_type=F32) + gb
    return (jnp.minimum(x, 0.0) - jnp.log1p(jnp.exp(-jnp.abs(x)))) * (1.0 / GLA_GATE_TAU)


def _gla_out(o, nw, gr):
    return (_rms(o, nw) * (gr * jax.nn.sigmoid(gr))).astype(BF16)


GLA_C = 128
GLA_SUB = 16


def _gla_prompt_body(gq_ref, gk_ref, gv_ref, misc_ref, gr_ref, w2_ref, gb_ref, nw_ref,
                     og_ref, sfin_ref, st_scr, a_scr, b_scr, o_scr, *, nchunk):
    t = pl.program_id(2)
    c_ = GLA_C

    @pl.when(t == 0)
    def _():
        st_scr[...] = jnp.zeros_like(st_scr)

    a_scr[...] = jnp.zeros_like(a_scr)
    row = lax.broadcasted_iota(I32, (c_, c_), 0)
    col = lax.broadcasted_iota(I32, (c_, c_), 1)
    tri = jnp.where(col <= row, 1.0, 0.0)
    sub_row = lax.broadcasted_iota(I32, (GLA_SUB, LANES), 0)
    sub_col = lax.broadcasted_iota(I32, (GLA_SUB, GLA_SUB), 1)

    def chunk(ci, carry):
        r0 = pl.multiple_of(ci * c_, c_)
        q = gq_ref[pl.ds(r0, c_), :] * (GLA_DK ** -0.5)
        k = gk_ref[pl.ds(r0, c_), :]
        v = gv_ref[pl.ds(r0, c_), :].astype(BF16)
        g = _log_decay(misc_ref[pl.ds(r0, c_), :], w2_ref[...], gb_ref[...])
        b = jnp.dot(tri, g, precision=HIGHEST, preferred_element_type=F32)
        b_scr[...] = b
        st = st_scr[...]
        o = _dot_nt((q * jnp.exp(b)).astype(BF16), st.astype(BF16))

        n = c_ // 2
        while n >= GLA_SUB:
            for rb in range(n, c_, 2 * n):
                bref = b[rb:rb + 1, :]
                qs = q[rb:rb + n, :] * jnp.exp(b[rb:rb + n, :] - bref)
                ks = k[rb - n:rb, :] * jnp.exp(bref - b[rb - n:rb, :])
                a_scr[rb:rb + n, rb - n:rb] = _dot_nt(qs.astype(BF16), ks.astype(BF16))
            n //= 2
        for blk in range(c_ // GLA_SUB):
            lo = blk * GLA_SUB
            qb = q[lo:lo + GLA_SUB, :]
            bb = b[lo:lo + GLA_SUB, :]
            ad = jnp.zeros((GLA_SUB, GLA_SUB), F32)
            for jj in range(GLA_SUB):
                bj = b_scr[pl.ds(lo + jj, 1), :]
                kj = gk_ref[pl.ds(r0 + lo + jj, 1), :]
                w = jnp.exp(jnp.where(sub_row >= jj, bb - bj, NEG))
                colj = jnp.sum(qb * kj * w, axis=1, keepdims=True)
                ad = jnp.where(sub_col == jj, colj, ad)
            a_scr[lo:lo + GLA_SUB, lo:lo + GLA_SUB] = ad

        o = o + jnp.dot(a_scr[...].astype(BF16), v, preferred_element_type=F32)
        o_scr[pl.ds(r0, c_), :] = o
        b_last = b[c_ - 1:c_, :]
        kd = (k * jnp.exp(b_last - b)).astype(BF16)
        kv = lax.dot_general(v, kd, (((0,), (0,)), ((), ())), preferred_element_type=F32)
        st_scr[...] = st * jnp.exp(b_last) + kv
        return carry

    lax.fori_loop(0, nchunk, chunk, 0)
    og_ref[...] = _gla_out(o_scr[...], nw_ref[...], gr_ref[...])

    @pl.when(t == pl.num_programs(2) - 1)
    def _():
        sfin_ref[0, 0] = st_scr[...].T


def _gla_prompt(z, w2p, gbias, norm_w, b, s):
    tb = min(512, s)
    nt = s // tb
    body = functools.partial(_gla_prompt_body, nchunk=tb // GLA_C)

    def rowblk(bb, t):
        return bb * nt + t

    return pl.pallas_call(
        body,
        out_shape=(jax.ShapeDtypeStruct((b * s, GLA_VAL_WIDTH), BF16),
                   jax.ShapeDtypeStruct((b, GLA_HEADS, GLA_DK, GLA_DV), F32)),
        grid=(b, GLA_HEADS, nt),
        in_specs=[
            pl.BlockSpec((tb, GLA_DK), lambda bb, h, t: (rowblk(bb, t), C_GQ // GLA_DK + h)),
            pl.BlockSpec((tb, GLA_DK), lambda bb, h, t: (rowblk(bb, t), C_GK // GLA_DK + h)),
            pl.BlockSpec((tb, GLA_DV), lambda bb, h, t: (rowblk(bb, t), C_GV // GLA_DV + h)),
            pl.BlockSpec((tb, LANES), lambda bb, h, t: (rowblk(bb, t), C_MISC // LANES)),
            pl.BlockSpec((tb, GLA_DV), lambda bb, h, t: (rowblk(bb, t), C_GR // GLA_DV + h)),
            pl.BlockSpec((LANES, GLA_DK), lambda bb, h, t: (0, h)),
            pl.BlockSpec((1, GLA_DK), lambda bb, h, t: (0, h)),
            pl.BlockSpec((1, GLA_DV), lambda bb, h, t: (0, 0)),
        ],
        out_specs=(pl.BlockSpec((tb, GLA_DV), lambda bb, h, t: (rowblk(bb, t), h)),
                   pl.BlockSpec((1, 1, GLA_DK, GLA_DV), lambda bb, h, t: (bb, h, 0, 0))),
        scratch_shapes=[
            pltpu.VMEM((GLA_DV, GLA_DK), F32),
            pltpu.VMEM((GLA_C, GLA_C), F32),
            pltpu.VMEM((GLA_C, GLA_DK), F32),
            pltpu.VMEM((tb, GLA_DV), F32),
        ],
        compiler_params=_params(("parallel", "parallel", "arbitrary")),
        name="gla_prompt",
    )(z, z, z, z, z, w2p, gbias, norm_w.reshape(1, GLA_DV))


def _merge_body(oa_ref, og_ref, ga_ref, gg_ref, wa_ref, wg_ref, o_ref):
    pa = jnp.dot(oa_ref[...], wa_ref[...], preferred_element_type=F32)
    pg = jnp.dot(og_ref[...], wg_ref[...], preferred_element_type=F32)
    o_ref[...] = (jax.nn.sigmoid(ga_ref[...]) * pa + jax.nn.sigmoid(gg_ref[...]) * pg).astype(BF16)


def _merge(o_attn, o_gla, z, wa, wg, tm):
    rows = o_attn.shape[0]
    d = wa.shape[1]
    return pl.pallas_call(
        _merge_body,
        out_shape=jax.ShapeDtypeStruct((rows, d), BF16),
        grid=(rows // tm,),
        in_specs=[
            pl.BlockSpec((tm, ATTN_WIDTH), lambda i: (i, 0)),
            pl.BlockSpec((tm, GLA_VAL_WIDTH), lambda i: (i, 0)),
            pl.BlockSpec((tm, d), lambda i: (i, C_GA // d)),
            pl.BlockSpec((tm, d), lambda i: (i, C_GA // d + 1)),
            pl.BlockSpec((ATTN_WIDTH, d), lambda i: (0, 0)),
            pl.BlockSpec((GLA_VAL_WIDTH, d), lambda i: (0, 0)),
        ],
        out_specs=pl.BlockSpec((tm, d), lambda i: (i, 0)),
        compiler_params=_params(("parallel",)),
        name="merge",
    )(o_attn, o_gla, z, z, wa, wg)


def _out_proj_body(m_ref, h_ref, w_ref, pw_ref, o_ref):
    y = jnp.dot(m_ref[...], w_ref[...], preferred_element_type=F32)
    o_ref[...] = h_ref[...] + _rms(y, pw_ref[...])


def _out_proj(merged, h, w_out, post_w, tm):
    rows, d = h.shape
    return pl.pallas_call(
        _out_proj_body,
        out_shape=jax.ShapeDtypeStruct((rows, d), F32),
        grid=(rows // tm,),
        in_specs=[
            pl.BlockSpec((tm, d), lambda i: (i, 0)),
            pl.BlockSpec((tm, d), lambda i: (i, 0)),
            pl.BlockSpec((d, d), lambda i: (0, 0)),
            pl.BlockSpec((1, d), lambda i: (0, 0)),
        ],
        out_specs=pl.BlockSpec((tm, d), lambda i: (i, 0)),
        compiler_params=_params(("parallel",)),
        name="out_proj",
    )(merged, h, w_out, post_w.reshape(1, d))


def _page_copies(pt_ref, hbm, buf, sem, seq, slot, n_pages, dst):
    return [pltpu.make_async_copy(hbm.at[pt_ref[seq, p]], buf.at[slot].at[dst(p)], sem)
            for p in range(n_pages)]


def _gather_step(fetch):
    b = pl.program_id(0)
    slot = b % 2

    @pl.when(b == 0)
    def _():
        for cp in fetch(b, slot):
            cp.start()

    @pl.when(b + 1 < pl.num_programs(0))
    def _():
        for cp in fetch(b + 1, 1 - slot):
            cp.start()

    for cp in fetch(b, slot):
        cp.wait()
    return slot


def _sample_scores_body(pt_ref, qi_ref, wi_ref, kidx_hbm, o_ref, kbuf, sem, *, n_pages, page):
    def fetch(seq, slot):
        return _page_copies(pt_ref, kidx_hbm, kbuf, sem.at[slot], seq, slot, n_pages,
                            lambda p: (slice(None), pl.ds(p * page, page)))

    slot = _gather_step(fetch)
    s = jnp.dot(qi_ref[0].astype(BF16), kbuf[slot].astype(BF16), preferred_element_type=F32)
    o_ref[0] = jnp.sum(jnp.maximum(s, 0.0) * wi_ref[0], axis=0, keepdims=True)


def _sample_scores(page_table, qi3, wi3, kidx_t):
    nd, n_pages = page_table.shape
    page = kidx_t.shape[2]
    past = n_pages * page
    body = functools.partial(_sample_scores_body, n_pages=n_pages, page=page)
    grid_spec = pltpu.PrefetchScalarGridSpec(
        num_scalar_prefetch=1,
        grid=(nd,),
        in_specs=[
            pl.BlockSpec((1, IDX_HEADS, IDX_DIM), lambda b, pt: (b, 0, 0)),
            pl.BlockSpec((1, IDX_HEADS, 1), lambda b, pt: (b, 0, 0)),
            pl.BlockSpec(memory_space=pl.ANY),
        ],
        out_specs=pl.BlockSpec((1, 1, past), lambda b, pt: (b, 0, 0)),
        scratch_shapes=[pltpu.VMEM((2, IDX_DIM, past), F32), pltpu.SemaphoreType.DMA((2,))],
    )
    return pl.pallas_call(
        body,
        out_shape=jax.ShapeDtypeStruct((nd, 1, past), F32),
        grid_spec=grid_spec,
        compiler_params=_params(("arbitrary",)),
        name="sample_scores",
    )(page_table, qi3, wi3, kidx_t)


def _sample_select_body(sp_ref, qi_ref, misc_ref, bp_ref, bs_ref, sc_scr, j_scr, *, past, tk, topk, idx_bits):
    rows = sp_ref.shape[0]
    misc = misc_ref[...]
    d_in = lax.broadcasted_iota(I32, (LANES, QI_WIDTH), 0)
    c_out = lax.broadcasted_iota(I32, (LANES, QI_WIDTH), 1)
    rep = jnp.where((d_in < IDX_DIM) & (c_out % IDX_DIM == d_in), 1.0, 0.0)
    ki_t = jnp.dot(misc, rep, precision=HIGHEST, preferred_element_type=F32)
    c_in = lax.broadcasted_iota(I32, (QI_WIDTH, LANES), 0)
    l_out = lax.broadcasted_iota(I32, (QI_WIDTH, LANES), 1)
    seg = jnp.where(l_out == MISC_WI + c_in // IDX_DIM, 1.0, 0.0)
    hd = jnp.dot(qi_ref[...] * ki_t, seg, precision=HIGHEST, preferred_element_type=F32)
    lane = lax.broadcasted_iota(I32, (rows, LANES), 1)
    is_wi = (lane >= MISC_WI) & (lane < MISC_GLR)
    s_self = jnp.sum(jnp.where(is_wi, jnp.maximum(hd, 0.0) * misc, 0.0), axis=1, keepdims=True)

    sc_scr[:, 0:past] = sp_ref[...]
    sc_scr[:, past:] = jnp.where(lax.broadcasted_iota(I32, (rows, tk), 1) == 0, s_self, -jnp.inf)
    nk = sc_scr.shape[1] // tk
    thr = _select_threshold(sc_scr, j_scr, nk, tk, topk, rows, idx_bits)
    jb = j_scr[...]
    kpos = lax.broadcasted_iota(I32, sc_scr.shape, 1)
    bias = jnp.where(_selected(sc_scr[...], kpos, thr, jb), 0.0, NEG)
    bp_ref[...] = bias[:, 0:past]
    bs_ref[...] = bias[:, past:past + LANES]


def _sample_select(s_past, qi2, misc, topk):
    nd, past = s_past.shape
    tk = LANES
    width = past + tk
    idx_bits = max(1, (width - 1).bit_length()) + 1
    body = functools.partial(_sample_select_body, past=past, tk=tk, topk=topk, idx_bits=idx_bits)
    return pl.pallas_call(
        body,
        out_shape=(jax.ShapeDtypeStruct((nd, past), F32), jax.ShapeDtypeStruct((nd, LANES), F32)),
        grid=(1,),
        in_specs=[
            pl.BlockSpec((nd, past), lambda i: (0, 0)),
            pl.BlockSpec((nd, QI_WIDTH), lambda i: (0, 0)),
            pl.BlockSpec((nd, LANES), lambda i: (0, 0)),
        ],
        out_specs=(pl.BlockSpec((nd, past), lambda i: (0, 0)), pl.BlockSpec((nd, LANES), lambda i: (0, 0))),
        scratch_shapes=[pltpu.VMEM((nd, width), F32), pltpu.VMEM((nd, 1), I32)],
        compiler_params=_params(("arbitrary",)),
        name="sample_select",
    )(s_past, qi2, misc)


SAMPLE_KCHUNK = 1024


def _sample_attn_body(pt_ref, q_ref, ks_ref, vs_ref, b2_ref, bs_ref, ck_hbm, cv_hbm, o_ref, kbuf, vbuf, sem,
                      *, n_pages, prows):
    def fetch(seq, slot):
        dst = lambda p: (pl.ds(p * prows, prows), slice(None))
        return (_page_copies(pt_ref, ck_hbm, kbuf, sem.at[0, slot], seq, slot, n_pages, dst)
                + _page_copies(pt_ref, cv_hbm, vbuf, sem.at[1, slot], seq, slot, n_pages, dst))

    slot = _gather_step(fetch)
    q = q_ref[0]
    qb = q.astype(BF16)
    total = n_pages * prows
    ch = min(SAMPLE_KCHUNK, total)
    head_grp = lax.broadcasted_iota(I32, (N_HEADS, ch), 0) // HEADS_PER_KV
    row_grp = lax.broadcasted_iota(I32, (N_HEADS, ch), 1) % N_KV_HEADS
    own = head_grp == row_grp
    s_chunks = []
    for c in range(total // ch):
        kc = kbuf[slot, c * ch:(c + 1) * ch, :].astype(BF16)
        s_chunks.append(jnp.where(own, _dot_nt(qb, kc) + b2_ref[0][:, c * ch:(c + 1) * ch], NEG))

    grp0 = lax.broadcasted_iota(I32, (N_HEADS, HEAD_DIM), 0) < HEADS_PER_KV
    k_self = jnp.where(grp0, ks_ref[0][:, 0:HEAD_DIM], ks_ref[0][:, HEAD_DIM:])
    v_self = jnp.where(grp0, vs_ref[0][:, 0:HEAD_DIM], vs_ref[0][:, HEAD_DIM:])
    s_self = jnp.sum(q * k_self, axis=1, keepdims=True) + bs_ref[0][:, 0:1]

    m = s_self
    for s in s_chunks:
        m = jnp.maximum(m, jnp.max(s, axis=1, keepdims=True))
    p_self = jnp.exp(s_self - m)
    l = p_self
    acc = p_self * v_self
    for c, s in enumerate(s_chunks):
        p = jnp.exp(s - m)
        l = l + jnp.sum(p, axis=1, keepdims=True)
        vc = vbuf[slot, c * ch:(c + 1) * ch, :].astype(BF16)
        acc = acc + jnp.dot(p.astype(BF16), vc, preferred_element_type=F32)
    o_ref[0] = (acc / l).astype(BF16)


def _sample_attn(page_table, q3, k_self, v_self, bias2, bias_self, ck2, cv2):
    nd, n_pages = page_table.shape
    prows = ck2.shape[1]
    total = n_pages * prows
    body = functools.partial(_sample_attn_body, n_pages=n_pages, prows=prows)
    grid_spec = pltpu.PrefetchScalarGridSpec(
        num_scalar_prefetch=1,
        grid=(nd,),
        in_specs=[
            pl.BlockSpec((1, N_HEADS, HEAD_DIM), lambda b, pt: (b, 0, 0)),
            pl.BlockSpec((1, 1, KV_WIDTH), lambda b, pt: (b, 0, 0)),
            pl.BlockSpec((1, 1, KV_WIDTH), lambda b, pt: (b, 0, 0)),
            pl.BlockSpec((1, 1, total), lambda b, pt: (b, 0, 0)),
            pl.BlockSpec((1, 1, LANES), lambda b, pt: (b, 0, 0)),
            pl.BlockSpec(memory_space=pl.ANY),
            pl.BlockSpec(memory_space=pl.ANY),
        ],
        out_specs=pl.BlockSpec((1, N_HEADS, HEAD_DIM), lambda b, pt: (b, 0, 0)),
        scratch_shapes=[
            pltpu.VMEM((2, total, HEAD_DIM), F32),
            pltpu.VMEM((2, total, HEAD_DIM), F32),
            pltpu.SemaphoreType.DMA((2, 2)),
        ],
    )
    return pl.pallas_call(
        body,
        out_shape=jax.ShapeDtypeStruct((nd, N_HEADS, HEAD_DIM), BF16),
        grid_spec=grid_spec,
        compiler_params=_params(("arbitrary",)),
        name="sample_attn",
    )(page_table, q3, k_self, v_self, bias2, bias_self, ck2, cv2)


def _gla_sample_body(gq_ref, gk_ref, gv_ref, misc_ref, gr_ref, s_ref, w2_ref, gb_ref, nw_ref, og_ref, so_ref):
    eye = jnp.where(lax.broadcasted_iota(I32, (GLA_DK, GLA_DK), 0)
                    == lax.broadcasted_iota(I32, (GLA_DK, GLA_DK), 1), 1.0, 0.0)

    def column(row):
        return jnp.sum(eye * row, axis=1, keepdims=True)

    misc = misc_ref[0]
    for h in range(GLA_HEADS):
        ksl = slice(h * GLA_DK, (h + 1) * GLA_DK)
        vsl = slice(h * GLA_DV, (h + 1) * GLA_DV)
        g = _log_decay(misc, w2_ref[:, ksl], gb_ref[:, ksl])
        s_new = column(jnp.exp(g)) * s_ref[0, h] + column(gk_ref[0][:, ksl]) * gv_ref[0][:, vsl]
        so_ref[0, h] = s_new
        o = jnp.sum(column(gq_ref[0][:, ksl] * (GLA_DK ** -0.5)) * s_new, axis=0, keepdims=True)
        og_ref[0, :, vsl] = _gla_out(o, nw_ref[...], gr_ref[0][:, vsl])


def _gla_sample(gq, gk, gv, misc, gr, state, w2p, gbias, norm_w):
    nd = state.shape[0]

    def row3(w):
        return pl.BlockSpec((1, 1, w), lambda b: (b, 0, 0))

    st_spec = pl.BlockSpec((1, GLA_HEADS, GLA_DK, GLA_DV), lambda b: (b, 0, 0, 0))
    return pl.pallas_call(
        _gla_sample_body,
        out_shape=(jax.ShapeDtypeStruct((nd, 1, GLA_VAL_WIDTH), BF16),
                   jax.ShapeDtypeStruct(state.shape, F32)),
        grid=(nd,),
        in_specs=[
            row3(GLA_KEY_WIDTH), row3(GLA_KEY_WIDTH), row3(GLA_VAL_WIDTH), row3(LANES), row3(GLA_VAL_WIDTH),
            st_spec,
            pl.BlockSpec((LANES, GLA_KEY_WIDTH), lambda b: (0, 0)),
            pl.BlockSpec((1, GLA_KEY_WIDTH), lambda b: (0, 0)),
            pl.BlockSpec((1, GLA_DV), lambda b: (0, 0)),
        ],
        out_specs=(row3(GLA_VAL_WIDTH), st_spec),
        compiler_params=_params(("parallel",)),
        name="gla_sample",
    )(gq, gk, gv, misc, gr, state, w2p, gbias, norm_w.reshape(1, GLA_DV))


def _rope_tables(pos, d):
    inv = ROPE_THETA ** (-jnp.arange(0, d, 2, dtype=F32) / d)
    ang = pos.astype(F32)[:, None] * inv[None, :]
    cos, sin = jnp.cos(ang), jnp.sin(ang)
    reps = LANES // d
    return (jnp.tile(jnp.concatenate([cos, cos], axis=-1), (1, reps)),
            jnp.tile(jnp.concatenate([-sin, sin], axis=-1), (1, reps)))


def _permute_w_in(w_in, d):
    sizes = (ATTN_WIDTH, KV_WIDTH, KV_WIDTH, QI_WIDTH, IDX_DIM, IDX_HEADS, GLA_KEY_WIDTH, GLA_KEY_WIDTH,
             GLA_VAL_WIDTH, GLA_GATE_RANK, GLA_VAL_WIDTH, d, d)
    pts, acc = [], 0
    for sz in sizes[:-1]:
        acc += sz
        pts.append(acc)
    q, k, v, qi, ki, wi, gq, gk, gv, glr, gr, ga, gg = jnp.split(w_in, pts, axis=1)
    pad = jnp.zeros((w_in.shape[0], C_GQ - (C_MISC + IDX_DIM + IDX_HEADS + GLA_GATE_RANK)), w_in.dtype)
    return jnp.concatenate([q, k, v, qi, ki, wi, glr, pad, gq, gk, gv, gr, ga, gg], axis=1).astype(BF16)


def kernel(x_prompt, x_sample, cache_k, cache_v, cache_kidx, page_table, state_gla,
           ffn1_pre_w, ffn1_w_gate, ffn1_w_up, ffn1_w_down, ffn1_post_w,
           mix_pre_w, w_in, gla_gate_w2, gla_gate_b, gla_norm_w,
           w_proj_attn, w_proj_gla, w_out, mix_post_w,
           ffn2_pre_w, ffn2_w_gate, ffn2_w_up, ffn2_w_down, ffn2_post_w):
    b, s, d = x_prompt.shape
    nd, td, _ = x_sample.shape
    n_pool, page = cache_k.shape[:2]
    n_pages = page_table.shape[1]
    past = n_pages * page
    assert td == 1 and s % Q_BLOCK == 0 and d % MIX_TN == 0 and C_GA % d == 0

    f1 = (ffn1_w_gate.astype(BF16), ffn1_w_up.astype(BF16), ffn1_w_down.astype(BF16))
    f2 = (ffn2_w_gate.astype(BF16), ffn2_w_up.astype(BF16), ffn2_w_down.astype(BF16))
    w_perm = _permute_w_in(w_in, d)
    wpa, wpg, wo = w_proj_attn.astype(BF16), w_proj_gla.astype(BF16), w_out.astype(BF16)
    w2p = jnp.zeros((LANES, GLA_KEY_WIDTH), F32).at[MISC_GLR:MISC_GLR + GLA_GATE_RANK].set(gla_gate_w2)
    gbias = gla_gate_b.reshape(1, GLA_KEY_WIDTH)

    def trunk(x, tm, tm_mix, tabs, tab_rows, mixer):
        h = _ffn(x, ffn1_pre_w, *f1, ffn1_post_w, tm)
        z = _mix_in(h, mix_pre_w, w_perm, tabs, tm_mix, tab_rows // tm_mix)
        o_attn, o_gla, s_fin = mixer(z)
        tmm = min(tm, 256)
        merged = _merge(o_attn, o_gla, z, wpa, wpg, tmm)
        h = _out_proj(merged, h, wo, mix_post_w, tmm)
        y = _ffn(h, ffn2_pre_w, *f2, ffn2_post_w, tm)
        return y, z, s_fin

    pos_p = jnp.arange(s, dtype=I32)
    tm_p = 512 if s % 512 == 0 else Q_BLOCK
    tabs_p = _rope_tables(pos_p, HEAD_DIM) + _rope_tables(pos_p, IDX_DIM)
    top_p = min(TOPK_MAX, s // 4)

    def mixer_p(z):
        o_attn = _attn_prompt(z, b, s, top_p)
        o_gla, s_fin = _gla_prompt(z, w2p, gbias, gla_norm_w, b, s)
        return o_attn, o_gla, s_fin

    tm_mix = 1024 if s % 1024 == 0 else tm_p
    y_p, z_p, gla_p = trunk(x_prompt.reshape(b * s, d), tm_p, tm_mix, tabs_p, s, mixer_p)

    pos_s = jnp.full((nd,), past, I32)
    tabs_s = _rope_tables(pos_s, HEAD_DIM) + _rope_tables(pos_s, IDX_DIM)
    top_s = min(TOPK_MAX, (past + td) // 4)
    ck2 = cache_k.reshape(n_pool, page * N_KV_HEADS, HEAD_DIM)
    cv2 = cache_v.reshape(n_pool, page * N_KV_HEADS, HEAD_DIM)
    kidx_t = jnp.swapaxes(cache_kidx, 1, 2)

    def mixer_s(z):
        misc = z[:, C_MISC:C_MISC + LANES]
        qi2 = z[:, C_QI:C_QI + QI_WIDTH]
        s_past = _sample_scores(page_table, qi2.reshape(nd, IDX_HEADS, IDX_DIM),
                                misc[:, MISC_WI:MISC_GLR].reshape(nd, IDX_HEADS, 1), kidx_t)
        bias_past, bias_self = _sample_select(s_past.reshape(nd, past), qi2, misc, top_s)
        bias2 = jnp.repeat(bias_past, N_KV_HEADS, axis=1).reshape(nd, 1, past * N_KV_HEADS)
        o_attn = _sample_attn(page_table, z[:, C_Q:C_Q + ATTN_WIDTH].reshape(nd, N_HEADS, HEAD_DIM),
                              z[:, C_K:C_K + KV_WIDTH].reshape(nd, 1, KV_WIDTH),
                              z[:, C_V:C_V + KV_WIDTH].reshape(nd, 1, KV_WIDTH),
                              bias2, bias_self.reshape(nd, 1, LANES), ck2, cv2)
        o_gla, s_fin = _gla_sample(z[:, C_GQ:C_GQ + GLA_KEY_WIDTH].reshape(nd, 1, GLA_KEY_WIDTH),
                                   z[:, C_GK:C_GK + GLA_KEY_WIDTH].reshape(nd, 1, GLA_KEY_WIDTH),
                                   z[:, C_GV:C_GV + GLA_VAL_WIDTH].reshape(nd, 1, GLA_VAL_WIDTH),
                                   misc.reshape(nd, 1, LANES),
                                   z[:, C_GR:C_GR + GLA_VAL_WIDTH].reshape(nd, 1, GLA_VAL_WIDTH),
                                   state_gla, w2p, gbias, gla_norm_w)
        return o_attn.reshape(nd, ATTN_WIDTH), o_gla.reshape(nd, GLA_VAL_WIDTH), s_fin

    y_s, z_s, gla_s = trunk(x_sample.reshape(nd, d), nd, nd, tabs_s, nd, mixer_s)

    def kv_out(z, n, t):
        return (z[:, C_K:C_K + KV_WIDTH].reshape(n, t, N_KV_HEADS, HEAD_DIM),
                z[:, C_V:C_V + KV_WIDTH].reshape(n, t, N_KV_HEADS, HEAD_DIM),
                z[:, C_MISC:C_MISC + IDX_DIM].reshape(n, t, IDX_DIM))

    k_p, v_p, ki_p = kv_out(z_p, b, s)
    k_s, v_s, ki_s = kv_out(z_s, nd, td)
    return (y_p.reshape(b, s, d), y_s.reshape(nd, td, d), k_p, v_p, ki_p, gla_p, k_s, v_s, ki_s, gla_s)
```

```python
import functools

import jax
import jax.numpy as jnp
from jax import lax
from jax.experimental import pallas as pl
from jax.experimental.pallas import tpu as pltpu

F32, BF16, I32 = jnp.float32, jnp.bfloat16, jnp.int32
HIGHEST = lax.Precision.HIGHEST

N_HEADS = 8
N_KV_HEADS = 2
HEAD_DIM = 128
IDX_HEADS = 16
IDX_DIM = 64
TOPK_MAX = 256
Q_BLOCK = 128
ROPE_THETA = 10000.0
GLA_HEADS = 4
GLA_DK = 128
GLA_DV = 256
GLA_GATE_RANK = 16
GLA_GATE_TAU = 16.0
RMS_EPS = 1e-6

LANES = 128
ATTN_WIDTH = N_HEADS * HEAD_DIM
KV_WIDTH = N_KV_HEADS * HEAD_DIM
QI_WIDTH = IDX_HEADS * IDX_DIM
GLA_KEY_WIDTH = GLA_HEADS * GLA_DK
GLA_VAL_WIDTH = GLA_HEADS * GLA_DV
HEADS_PER_KV = N_HEADS // N_KV_HEADS

C_Q, C_K, C_V, C_QI, C_MISC = 0, 1024, 1280, 1536, 2560
C_GQ, C_GK, C_GV, C_GR, C_GA = 3072, 3584, 4096, 5120, 6144
MISC_WI, MISC_GLR = IDX_DIM, IDX_DIM + IDX_HEADS
MIX_TN = 512

NEG = -1e30
INT_MIN = -2 ** 31
NEG_INF_KEY = -2139095041
VMEM_LIMIT = 56 * 1024 * 1024


def _params(sem, vmem=VMEM_LIMIT):
    return pltpu.CompilerParams(dimension_semantics=sem, vmem_limit_bytes=vmem)


def _rms(x, w):
    return x * lax.rsqrt(jnp.mean(x * x, axis=-1, keepdims=True) + RMS_EPS) * w


def _dot_nt(a, b):
    return lax.dot_general(a, b, (((1,), (1,)), ((), ())), preferred_element_type=F32)


def _ffn_body(x_ref, prew_ref, wg_ref, wu_ref, wd_ref, postw_ref, o_ref, z_scr, acc_scr):
    j = pl.program_id(1)

    @pl.when(j == 0)
    def _():
        z_scr[...] = _rms(x_ref[...], prew_ref[...]).astype(BF16)
        acc_scr[...] = jnp.zeros_like(acc_scr)

    z = z_scr[...]
    g = jnp.dot(z, wg_ref[...], preferred_element_type=F32)
    u = jnp.dot(z, wu_ref[...], preferred_element_type=F32)
    a = (g * jax.nn.sigmoid(g) * u).astype(BF16)
    acc_scr[...] += jnp.dot(a, wd_ref[...], preferred_element_type=F32)

    @pl.when(j == pl.num_programs(1) - 1)
    def _():
        o_ref[...] = x_ref[...] + 0.5 * _rms(acc_scr[...], postw_ref[...])


def _ffn(x, pre_w, wg, wu, wd, post_w, tm):
    rows, d = x.shape
    dff = wg.shape[1]
    tf = 512 if dff % 512 == 0 else dff
    return pl.pallas_call(
        _ffn_body,
        out_shape=jax.ShapeDtypeStruct((rows, d), F32),
        grid=(rows // tm, dff // tf),
        in_specs=[
            pl.BlockSpec((tm, d), lambda i, j: (i, 0)),
            pl.BlockSpec((1, d), lambda i, j: (0, 0)),
            pl.BlockSpec((d, tf), lambda i, j: (0, j)),
            pl.BlockSpec((d, tf), lambda i, j: (0, j)),
            pl.BlockSpec((tf, d), lambda i, j: (j, 0)),
            pl.BlockSpec((1, d), lambda i, j: (0, 0)),
        ],
        out_specs=pl.BlockSpec((tm, d), lambda i, j: (i, 0)),
        scratch_shapes=[pltpu.VMEM((tm, d), BF16), pltpu.VMEM((tm, d), F32)],
        compiler_params=_params(("parallel", "arbitrary")),
        name="ffn",
    )(x, pre_w.reshape(1, d), wg, wu, wd, post_w.reshape(1, d))


def _mix_in_body(x_ref, prew_ref, w_ref, c128_ref, s128_ref, c64_ref, s64_ref, o_ref, u_scr):
    j = pl.program_id(1)

    @pl.when(j == 0)
    def _():
        u_scr[...] = _rms(x_ref[...], prew_ref[...]).astype(BF16)

    r = jnp.dot(u_scr[...], w_ref[...], preferred_element_type=F32)

    def rope128(x):
        return x * c128_ref[...] + pltpu.roll(x, HEAD_DIM // 2, 1) * s128_ref[...]

    def rope64(x):
        lane = lax.broadcasted_iota(I32, x.shape, 1)
        first = (lane % IDX_DIM) < (IDX_DIM // 2)
        rot = jnp.where(first, pltpu.roll(x, LANES - IDX_DIM // 2, 1), pltpu.roll(x, IDX_DIM // 2, 1))
        return x * c64_ref[...] + rot * s64_ref[...]

    def sl(t):
        return slice(t * LANES, (t + 1) * LANES)

    @pl.when(j < 2)
    def _():
        for t in range(4):
            o_ref[:, sl(t)] = rope128(r[:, sl(t)]) * (HEAD_DIM ** -0.5)

    @pl.when(j == 2)
    def _():
        for t in range(2):
            o_ref[:, sl(t)] = rope128(r[:, sl(t)])
        o_ref[:, 2 * LANES:] = r[:, 2 * LANES:]

    @pl.when((j == 3) | (j == 4))
    def _():
        for t in range(4):
            o_ref[:, sl(t)] = rope64(r[:, sl(t)])

    @pl.when(j == 5)
    def _():
        x = r[:, sl(0)]
        lane = lax.broadcasted_iota(I32, x.shape, 1)
        wi_scale = IDX_HEADS ** -0.5 * IDX_DIM ** -0.5
        y = jnp.where(lane < MISC_WI, rope64(x), jnp.where(lane < MISC_GLR, x * wi_scale, x))
        o_ref[:, sl(0)] = y
        o_ref[:, LANES:] = r[:, LANES:]

    @pl.when(j > 5)
    def _():
        o_ref[...] = r


def _mix_in(h, pre_w, w_perm, tabs, tm, tab_blocks):
    rows, d = h.shape
    zw = w_perm.shape[1]
    c128, s128, c64, s64 = tabs
    tab_spec = pl.BlockSpec((tm, LANES), lambda i, j: (i % tab_blocks, 0))
    return pl.pallas_call(
        _mix_in_body,
        out_shape=jax.ShapeDtypeStruct((rows, zw), F32),
        grid=(rows // tm, zw // MIX_TN),
        in_specs=[
            pl.BlockSpec((tm, d), lambda i, j: (i, 0)),
            pl.BlockSpec((1, d), lambda i, j: (0, 0)),
            pl.BlockSpec((d, MIX_TN), lambda i, j: (0, j)),
            tab_spec, tab_spec, tab_spec, tab_spec,
        ],
        out_specs=pl.BlockSpec((tm, MIX_TN), lambda i, j: (i, j)),
        scratch_shapes=[pltpu.VMEM((tm, d), BF16)],
        compiler_params=_params(("parallel", "arbitrary")),
        name="mix_in",
    )(h, pre_w.reshape(1, d), w_perm, c128, s128, c64, s64)


def _key_to_float(key):
    key = jnp.maximum(key, NEG_INF_KEY)
    return lax.bitcast_convert_type(key ^ ((key >> 31) & jnp.int32(0x7FFFFFFF)), F32)


def _select_threshold(sc_ref, j_ref, nk, tk, topk, idx_bits, key_axis):
    n_other = sc_ref.shape[1 - key_axis]
    vec = (n_other, 1) if key_axis == 1 else (1, n_other)
    step = LANES if key_axis == 1 else 8

    def count(pred):
        def body(c, acc):
            ks = pl.multiple_of(c * tk, tk)
            blk = sc_ref[:, pl.ds(ks, tk)] if key_axis == 1 else sc_ref[pl.ds(ks, tk), :]
            m = jnp.where(pred(blk, ks), 1.0, 0.0)
            parts = [lax.slice_in_dim(m, a, a + step, axis=key_axis) for a in range(0, tk, step)]
            while len(parts) > 1:
                parts = [parts[a] + parts[a + 1] for a in range(0, len(parts), 2)]
            return acc + parts[0]
        acc0 = jnp.zeros((n_other, step) if key_axis == 1 else (step, n_other), F32)
        return jnp.sum(lax.fori_loop(0, nk, body, acc0), axis=key_axis, keepdims=True)

    def bit_body(bi, t):
        cand = t ^ lax.shift_left(jnp.int32(1), 31 - bi)
        cand_f = _key_to_float(cand)
        return jnp.where(count(lambda blk, ks: blk >= cand_f) >= topk, cand, t)

    thr_key = lax.fori_loop(0, 32, bit_body, jnp.full(vec, INT_MIN, I32))
    thr = _key_to_float(thr_key)

    cnt_ge = count(lambda blk, ks: blk >= thr)
    cnt_gt = count(lambda blk, ks: blk > thr)
    need = topk - cnt_gt
    tie = jnp.where((cnt_ge > topk) & (thr_key > NEG_INF_KEY), 1.0, 0.0)
    j_ref[...] = jnp.full(vec, 1 << idx_bits, I32)

    @pl.when(jnp.max(tie) > 0.0)
    def _():
        def jbit(bi, jb):
            cand = jb + lax.shift_left(jnp.int32(1), idx_bits - 1 - bi)

            def pred(blk, ks):
                kpos = ks + lax.broadcasted_iota(I32, blk.shape, key_axis)
                return (blk == thr) & (kpos < cand)
            return jnp.where(count(pred) <= need, cand, jb)
        j_ref[...] = lax.fori_loop(0, idx_bits, jbit, jnp.zeros(vec, I32))

    return thr


def _selected(score, kpos, thr, jb):
    return (score > thr) | ((score == thr) & (kpos < jb))


def _attn_prompt_body(q_ref, qia_ref, qib_ref, misc_ref, k_ref, v_ref, kim_ref, o_ref,
                      sc_scr, j_scr, bias_scr, kb_scr, vb_scr, kib_scr, qi_scr, wt_scr, qs_scr, mx_scr, l_scr,
                      acc_scr, *, tk, topk, idx_bits):
    i = pl.program_id(1)
    tq = Q_BLOCK
    nk = (i * tq + tq + tk - 1) // tk
    n_tiles = tk // LANES
    sub = 256
    qpos_t = i * tq + lax.broadcasted_iota(I32, (1, tq), 1)

    @pl.when(i == 0)
    def _():
        def cast_chunk(c, carry):
            rs = pl.ds(pl.multiple_of(c * tk, tk), tk)
            kb_scr[rs, :] = k_ref[rs, :].astype(BF16)
            vb_scr[rs, :] = v_ref[rs, :].astype(BF16)
            kib_scr[rs, :] = kim_ref[rs, 0:IDX_DIM].astype(BF16)
            return carry
        lax.fori_loop(0, k_ref.shape[0] // tk, cast_chunk, 0)

    for h in range(IDX_HEADS):
        ref = qia_ref if h < IDX_HEADS // 2 else qib_ref
        hh = h % (IDX_HEADS // 2)
        qi_scr[h // 2, (h % 2) * tq:(h % 2 + 1) * tq, :] = ref[:, hh * IDX_DIM:(hh + 1) * IDX_DIM].astype(BF16)
    wt_scr[...] = misc_ref[...].T

    def score_chunk(c, carry):
        for s0 in range(0, tk, sub):
            ks = pl.multiple_of(c * tk, tk) + s0
            ki = kib_scr[pl.ds(ks, sub), :]
            acc = jnp.zeros((sub, tq), F32)
            for hp in range(IDX_HEADS // 2):
                r = jnp.maximum(_dot_nt(ki, qi_scr[hp]), 0.0)
                acc = (acc + r[:, 0:tq] * wt_scr[pl.ds(MISC_WI + 2 * hp, 1), :]
                       + r[:, tq:] * wt_scr[pl.ds(MISC_WI + 2 * hp + 1, 1), :])
            kpos = ks + lax.broadcasted_iota(I32, (sub, tq), 0)
            sc_scr[pl.ds(ks, sub), :] = jnp.where(kpos <= qpos_t, acc, -jnp.inf)
        return carry

    lax.fori_loop(0, nk, score_chunk, 0)
    thr = _select_threshold(sc_scr, j_scr, nk, tk, topk, idx_bits, 0)
    jb = j_scr[...]

    for g in range(N_KV_HEADS):
        for r in range(HEADS_PER_KV):
            hd = g * HEADS_PER_KV + r
            qs_scr[g, r * tq:(r + 1) * tq, :] = q_ref[:, hd * HEAD_DIM:(hd + 1) * HEAD_DIM].astype(BF16)
    mx_scr[...] = jnp.full(mx_scr.shape, NEG, F32)
    l_scr[...] = jnp.zeros(l_scr.shape, F32)
    acc_scr[...] = jnp.zeros(acc_scr.shape, F32)

    def logits(g, ks, bias4):
        kc = kb_scr[pl.ds(ks, tk), g * HEAD_DIM:(g + 1) * HEAD_DIM]
        return _dot_nt(qs_scr[g], kc) + bias4

    def max_pass(c, carry):
        ks = pl.multiple_of(c * tk, tk)
        kpos = ks + lax.broadcasted_iota(I32, (tk, tq), 0)
        sel = _selected(sc_scr[pl.ds(ks, tk), :], kpos, thr, jb) & (kpos <= qpos_t)
        bias = jnp.where(sel, 0.0, NEG).T
        bias_scr[:, pl.ds(ks, tk)] = bias
        bias4 = jnp.concatenate([bias] * HEADS_PER_KV, axis=0)
        for g in range(N_KV_HEADS):
            s = logits(g, ks, bias4)
            m = mx_scr[g]
            for t in range(n_tiles):
                m = jnp.maximum(m, s[:, t * LANES:(t + 1) * LANES])
            mx_scr[g] = m
        return carry

    lax.fori_loop(0, nk, max_pass, 0)
    for g in range(N_KV_HEADS):
        mx_scr[g] = jnp.broadcast_to(jnp.max(mx_scr[g], axis=1, keepdims=True), mx_scr.shape[1:])

    def exp_pass(c, carry):
        ks = pl.multiple_of(c * tk, tk)
        bias4 = jnp.concatenate([bias_scr[:, pl.ds(ks, tk)]] * HEADS_PER_KV, axis=0)
        for g in range(N_KV_HEADS):
            s = logits(g, ks, bias4)
            m = mx_scr[g]
            p = [jnp.exp(s[:, t * LANES:(t + 1) * LANES] - m) for t in range(n_tiles)]
            l_scr[g] += functools.reduce(lambda x, y: x + y, p)
            pb = jnp.concatenate(p, axis=1).astype(BF16)
            vc = vb_scr[pl.ds(ks, tk), g * HEAD_DIM:(g + 1) * HEAD_DIM]
            acc_scr[g] += jnp.dot(pb, vc, preferred_element_type=F32)
        return carry

    lax.fori_loop(0, nk, exp_pass, 0)

    for g in range(N_KV_HEADS):
        o = acc_scr[g] / jnp.sum(l_scr[g], axis=1, keepdims=True)
        for r in range(HEADS_PER_KV):
            hd = g * HEADS_PER_KV + r
            o_ref[:, hd * HEAD_DIM:(hd + 1) * HEAD_DIM] = o[r * tq:(r + 1) * tq, :].astype(BF16)


def _attn_prompt(z, b, s, topk):
    nq = s // Q_BLOCK
    tk = min(512, s)
    idx_bits = max(1, (s - 1).bit_length()) + 1
    body = functools.partial(_attn_prompt_body, tk=tk, topk=topk, idx_bits=idx_bits)
    rows = HEADS_PER_KV * Q_BLOCK
    return pl.pallas_call(
        body,
        out_shape=jax.ShapeDtypeStruct((b * s, ATTN_WIDTH), BF16),
        grid=(b, nq),
        in_specs=[
            pl.BlockSpec((Q_BLOCK, ATTN_WIDTH), lambda bb, i: (bb * nq + i, 0)),
            pl.BlockSpec((Q_BLOCK, QI_WIDTH // 2), lambda bb, i: (bb * nq + i, C_QI // (QI_WIDTH // 2))),
            pl.BlockSpec((Q_BLOCK, QI_WIDTH // 2), lambda bb, i: (bb * nq + i, C_QI // (QI_WIDTH // 2) + 1)),
            pl.BlockSpec((Q_BLOCK, LANES), lambda bb, i: (bb * nq + i, C_MISC // LANES)),
            pl.BlockSpec((s, KV_WIDTH), lambda bb, i: (bb, C_K // KV_WIDTH)),
            pl.BlockSpec((s, KV_WIDTH), lambda bb, i: (bb, C_V // KV_WIDTH)),
            pl.BlockSpec((s, LANES), lambda bb, i: (bb, C_MISC // LANES)),
        ],
        out_specs=pl.BlockSpec((Q_BLOCK, ATTN_WIDTH), lambda bb, i: (bb * nq + i, 0)),
        scratch_shapes=[
            pltpu.VMEM((s, Q_BLOCK), F32),
            pltpu.VMEM((1, Q_BLOCK), I32),
            pltpu.VMEM((Q_BLOCK, s), F32),
            pltpu.VMEM((s, KV_WIDTH), BF16),
            pltpu.VMEM((s, KV_WIDTH), BF16),
            pltpu.VMEM((s, IDX_DIM), BF16),
            pltpu.VMEM((IDX_HEADS // 2, 2 * Q_BLOCK, IDX_DIM), BF16),
            pltpu.VMEM((LANES, Q_BLOCK), F32),
            pltpu.VMEM((N_KV_HEADS, rows, HEAD_DIM), BF16),
            pltpu.VMEM((N_KV_HEADS, rows, LANES), F32),
            pltpu.VMEM((N_KV_HEADS, rows, LANES), F32),
            pltpu.VMEM((N_KV_HEADS, rows, HEAD_DIM), F32),
        ],
        compiler_params=_params(("parallel", "arbitrary")),
        name="attn_prompt",
    )(z, z, z, z, z, z, z)


def _log_decay(misc, w2, gb):
    x = jnp.dot(misc, w2, precision=HIGHEST, preferred_element_type=F32) + gb
    return (jnp.minimum(x, 0.0) - jnp.log1p(jnp.exp(-jnp.abs(x)))) * (1.0 / GLA_GATE_TAU)


def _gla_out(o, nw, gr):
    return (_rms(o, nw) * (gr * jax.nn.sigmoid(gr))).astype(BF16)


GLA_C = 128
GLA_SUB = 16
GLA_HPS = 2


def _gla_prompt_body(gq_ref, gk_ref, gv_ref, misc_ref, gr_ref, w2_ref, gb_ref, nw_ref,
                     og_ref, sfin_ref, st_scr, a_scr, b_scr, k_scr, o_scr, *, nchunk):
    t = pl.program_id(2)
    c_ = GLA_C

    @pl.when(t == 0)
    def _():
        st_scr[...] = jnp.zeros_like(st_scr)

    a_scr[...] = jnp.zeros_like(a_scr)
    row = lax.broadcasted_iota(I32, (c_, c_), 0)
    col = lax.broadcasted_iota(I32, (c_, c_), 1)
    tri = jnp.where(col <= row, 1.0, 0.0)
    sub_row = lax.broadcasted_iota(I32, (GLA_SUB, LANES), 0)
    sub_col = lax.broadcasted_iota(I32, (GLA_SUB, GLA_SUB), 1)

    def head_chunk(hh, r0):
        ksl = slice(hh * GLA_DK, (hh + 1) * GLA_DK)
        vsl = slice(hh * GLA_DV, (hh + 1) * GLA_DV)
        q = gq_ref[pl.ds(r0, c_), ksl] * (GLA_DK ** -0.5)
        k = gk_ref[pl.ds(r0, c_), ksl]
        v = gv_ref[pl.ds(r0, c_), vsl].astype(BF16)
        g = _log_decay(misc_ref[pl.ds(r0, c_), :], w2_ref[:, ksl], gb_ref[:, ksl])
        b = jnp.dot(tri, g, precision=HIGHEST, preferred_element_type=F32)
        b_scr[hh] = b
        k_scr[hh] = k
        st = st_scr[hh]
        o = _dot_nt((q * jnp.exp(b)).astype(BF16), st.astype(BF16))

        n = c_ // 2
        while n >= GLA_SUB:
            for rb in range(n, c_, 2 * n):
                bref = b[rb:rb + 1, :]
                qs = q[rb:rb + n, :] * jnp.exp(b[rb:rb + n, :] - bref)
                ks = k[rb - n:rb, :] * jnp.exp(bref - b[rb - n:rb, :])
                a_scr[hh, rb:rb + n, rb - n:rb] = _dot_nt(qs.astype(BF16), ks.astype(BF16))
            n //= 2
        for blk in range(c_ // GLA_SUB):
            lo = blk * GLA_SUB
            qb = q[lo:lo + GLA_SUB, :]
            bb = b[lo:lo + GLA_SUB, :]
            ad = jnp.zeros((GLA_SUB, GLA_SUB), F32)
            for jj in range(GLA_SUB):
                bj = b_scr[hh, pl.ds(lo + jj, 1), :]
                kj = k_scr[hh, pl.ds(lo + jj, 1), :]
                w = jnp.exp(jnp.where(sub_row >= jj, bb - bj, NEG))
                colj = jnp.sum(qb * kj * w, axis=1, keepdims=True)
                ad = jnp.where(sub_col == jj, colj, ad)
            a_scr[hh, lo:lo + GLA_SUB, lo:lo + GLA_SUB] = ad

        o = o + jnp.dot(a_scr[hh].astype(BF16), v, preferred_element_type=F32)
        o_scr[pl.ds(r0, c_), vsl] = o
        b_last = b[c_ - 1:c_, :]
        kd = (k * jnp.exp(b_last - b)).astype(BF16)
        kv = lax.dot_general(v, kd, (((0,), (0,)), ((), ())), preferred_element_type=F32)
        st_scr[hh] = st * jnp.exp(b_last) + kv

    def chunk(ci, carry):
        r0 = pl.multiple_of(ci * c_, c_)
        for hh in range(GLA_HPS):
            head_chunk(hh, r0)
        return carry

    lax.fori_loop(0, nchunk, chunk, 0)
    for hh in range(GLA_HPS):
        vsl = slice(hh * GLA_DV, (hh + 1) * GLA_DV)
        og_ref[:, vsl] = _gla_out(o_scr[:, vsl], nw_ref[...], gr_ref[:, vsl])

    @pl.when(t == pl.num_programs(2) - 1)
    def _():
        for hh in range(GLA_HPS):
            sfin_ref[0, hh] = st_scr[hh].T


def _gla_prompt(z, w2p, gbias, norm_w, b, s):
    tb = min(512, s)
    nt = s // tb
    body = functools.partial(_gla_prompt_body, nchunk=tb // GLA_C)

    def rowblk(bb, t):
        return bb * nt + t

    kw, vw = GLA_HPS * GLA_DK, GLA_HPS * GLA_DV
    return pl.pallas_call(
        body,
        out_shape=(jax.ShapeDtypeStruct((b * s, GLA_VAL_WIDTH), BF16),
                   jax.ShapeDtypeStruct((b, GLA_HEADS, GLA_DK, GLA_DV), F32)),
        grid=(b, GLA_HEADS // GLA_HPS, nt),
        in_specs=[
            pl.BlockSpec((tb, kw), lambda bb, h, t: (rowblk(bb, t), C_GQ // kw + h)),
            pl.BlockSpec((tb, kw), lambda bb, h, t: (rowblk(bb, t), C_GK // kw + h)),
            pl.BlockSpec((tb, vw), lambda bb, h, t: (rowblk(bb, t), C_GV // vw + h)),
            pl.BlockSpec((tb, LANES), lambda bb, h, t: (rowblk(bb, t), C_MISC // LANES)),
            pl.BlockSpec((tb, vw), lambda bb, h, t: (rowblk(bb, t), C_GR // vw + h)),
            pl.BlockSpec((LANES, kw), lambda bb, h, t: (0, h)),
            pl.BlockSpec((1, kw), lambda bb, h, t: (0, h)),
            pl.BlockSpec((1, GLA_DV), lambda bb, h, t: (0, 0)),
        ],
        out_specs=(pl.BlockSpec((tb, vw), lambda bb, h, t: (rowblk(bb, t), h)),
                   pl.BlockSpec((1, GLA_HPS, GLA_DK, GLA_DV), lambda bb, h, t: (bb, h, 0, 0))),
        scratch_shapes=[
            pltpu.VMEM((GLA_HPS, GLA_DV, GLA_DK), F32),
            pltpu.VMEM((GLA_HPS, GLA_C, GLA_C), F32),
            pltpu.VMEM((GLA_HPS, GLA_C, GLA_DK), F32),
            pltpu.VMEM((GLA_HPS, GLA_C, GLA_DK), F32),
            pltpu.VMEM((tb, vw), F32),
        ],
        compiler_params=_params(("parallel", "parallel", "arbitrary")),
        name="gla_prompt",
    )(z, z, z, z, z, w2p, gbias, norm_w.reshape(1, GLA_DV))


def _merge_body(oa_ref, og_ref, ga_ref, gg_ref, wa_ref, wg_ref, o_ref):
    pa = jnp.dot(oa_ref[...], wa_ref[...], preferred_element_type=F32)
    pg = jnp.dot(og_ref[...], wg_ref[...], preferred_element_type=F32)
    o_ref[...] = (jax.nn.sigmoid(ga_ref[...]) * pa + jax.nn.sigmoid(gg_ref[...]) * pg).astype(BF16)


def _merge(o_attn, o_gla, z, wa, wg, tm):
    rows = o_attn.shape[0]
    d = wa.shape[1]
    return pl.pallas_call(
        _merge_body,
        out_shape=jax.ShapeDtypeStruct((rows, d), BF16),
        grid=(rows // tm,),
        in_specs=[
            pl.BlockSpec((tm, ATTN_WIDTH), lambda i: (i, 0)),
            pl.BlockSpec((tm, GLA_VAL_WIDTH), lambda i: (i, 0)),
            pl.BlockSpec((tm, d), lambda i: (i, C_GA // d)),
            pl.BlockSpec((tm, d), lambda i: (i, C_GA // d + 1)),
            pl.BlockSpec((ATTN_WIDTH, d), lambda i: (0, 0)),
            pl.BlockSpec((GLA_VAL_WIDTH, d), lambda i: (0, 0)),
        ],
        out_specs=pl.BlockSpec((tm, d), lambda i: (i, 0)),
        compiler_params=_params(("parallel",)),
        name="merge",
    )(o_attn, o_gla, z, z, wa, wg)


def _out_proj_body(m_ref, h_ref, w_ref, pw_ref, o_ref):
    y = jnp.dot(m_ref[...], w_ref[...], preferred_element_type=F32)
    o_ref[...] = h_ref[...] + _rms(y, pw_ref[...])


def _out_proj(merged, h, w_out, post_w, tm):
    rows, d = h.shape
    return pl.pallas_call(
        _out_proj_body,
        out_shape=jax.ShapeDtypeStruct((rows, d), F32),
        grid=(rows // tm,),
        in_specs=[
            pl.BlockSpec((tm, d), lambda i: (i, 0)),
            pl.BlockSpec((tm, d), lambda i: (i, 0)),
            pl.BlockSpec((d, d), lambda i: (0, 0)),
            pl.BlockSpec((1, d), lambda i: (0, 0)),
        ],
        out_specs=pl.BlockSpec((tm, d), lambda i: (i, 0)),
        compiler_params=_params(("parallel",)),
        name="out_proj",
    )(merged, h, w_out, post_w.reshape(1, d))


def _page_copies(pt_ref, hbm, buf, sem, seq, slot, n_pages, dst):
    return [pltpu.make_async_copy(hbm.at[pt_ref[seq, p]], buf.at[slot].at[dst(p)], sem)
            for p in range(n_pages)]


def _gather_step(fetch):
    b = pl.program_id(0)
    slot = b % 2

    @pl.when(b == 0)
    def _():
        for cp in fetch(b, slot):
            cp.start()

    @pl.when(b + 1 < pl.num_programs(0))
    def _():
        for cp in fetch(b + 1, 1 - slot):
            cp.start()

    for cp in fetch(b, slot):
        cp.wait()
    return slot


def _sample_scores_body(pt_ref, qi_ref, wi_ref, kidx_hbm, o_ref, kbuf, sem, *, n_pages, page):
    def fetch(seq, slot):
        return _page_copies(pt_ref, kidx_hbm, kbuf, sem.at[slot], seq, slot, n_pages,
                            lambda p: (slice(None), pl.ds(p * page, page)))

    slot = _gather_step(fetch)
    s = jnp.dot(qi_ref[0].astype(BF16), kbuf[slot].astype(BF16), preferred_element_type=F32)
    o_ref[0] = jnp.sum(jnp.maximum(s, 0.0) * wi_ref[0], axis=0, keepdims=True)


def _sample_scores(page_table, qi3, wi3, kidx_t):
    nd, n_pages = page_table.shape
    page = kidx_t.shape[2]
    past = n_pages * page
    body = functools.partial(_sample_scores_body, n_pages=n_pages, page=page)
    grid_spec = pltpu.PrefetchScalarGridSpec(
        num_scalar_prefetch=1,
        grid=(nd,),
        in_specs=[
            pl.BlockSpec((1, IDX_HEADS, IDX_DIM), lambda b, pt: (b, 0, 0)),
            pl.BlockSpec((1, IDX_HEADS, 1), lambda b, pt: (b, 0, 0)),
            pl.BlockSpec(memory_space=pl.ANY),
        ],
        out_specs=pl.BlockSpec((1, 1, past), lambda b, pt: (b, 0, 0)),
        scratch_shapes=[pltpu.VMEM((2, IDX_DIM, past), F32), pltpu.SemaphoreType.DMA((2,))],
    )
    return pl.pallas_call(
        body,
        out_shape=jax.ShapeDtypeStruct((nd, 1, past), F32),
        grid_spec=grid_spec,
        compiler_params=_params(("arbitrary",)),
        name="sample_scores",
    )(page_table, qi3, wi3, kidx_t)


def _sample_select_body(sp_ref, qi_ref, misc_ref, bp_ref, bs_ref, sc_scr, j_scr, *, past, tk, topk, idx_bits):
    rows = sp_ref.shape[0]
    misc = misc_ref[...]
    d_in = lax.broadcasted_iota(I32, (LANES, QI_WIDTH), 0)
    c_out = lax.broadcasted_iota(I32, (LANES, QI_WIDTH), 1)
    rep = jnp.where((d_in < IDX_DIM) & (c_out % IDX_DIM == d_in), 1.0, 0.0)
    ki_t = jnp.dot(misc, rep, precision=HIGHEST, preferred_element_type=F32)
    c_in = lax.broadcasted_iota(I32, (QI_WIDTH, LANES), 0)
    l_out = lax.broadcasted_iota(I32, (QI_WIDTH, LANES), 1)
    seg = jnp.where(l_out == MISC_WI + c_in // IDX_DIM, 1.0, 0.0)
    hd = jnp.dot(qi_ref[...] * ki_t, seg, precision=HIGHEST, preferred_element_type=F32)
    lane = lax.broadcasted_iota(I32, (rows, LANES), 1)
    is_wi = (lane >= MISC_WI) & (lane < MISC_GLR)
    s_self = jnp.sum(jnp.where(is_wi, jnp.maximum(hd, 0.0) * misc, 0.0), axis=1, keepdims=True)

    sc_scr[:, 0:past] = sp_ref[...]
    sc_scr[:, past:] = jnp.where(lax.broadcasted_iota(I32, (rows, tk), 1) == 0, s_self, -jnp.inf)
    nk = sc_scr.shape[1] // tk
    thr = _select_threshold(sc_scr, j_scr, nk, tk, topk, idx_bits, 1)
    jb = j_scr[...]
    kpos = lax.broadcasted_iota(I32, sc_scr.shape, 1)
    bias = jnp.where(_selected(sc_scr[...], kpos, thr, jb), 0.0, NEG)
    bp_ref[...] = bias[:, 0:past]
    bs_ref[...] = bias[:, past:past + LANES]


def _sample_select(s_past, qi2, misc, topk):
    nd, past = s_past.shape
    tk = LANES
    width = past + tk
    idx_bits = max(1, (width - 1).bit_length()) + 1
    body = functools.partial(_sample_select_body, past=past, tk=tk, topk=topk, idx_bits=idx_bits)
    return pl.pallas_call(
        body,
        out_shape=(jax.ShapeDtypeStruct((nd, past), F32), jax.ShapeDtypeStruct((nd, LANES), F32)),
        grid=(1,),
        in_specs=[
            pl.BlockSpec((nd, past), lambda i: (0, 0)),
            pl.BlockSpec((nd, QI_WIDTH), lambda i: (0, 0)),
            pl.BlockSpec((nd, LANES), lambda i: (0, 0)),
        ],
        out_specs=(pl.BlockSpec((nd, past), lambda i: (0, 0)), pl.BlockSpec((nd, LANES), lambda i: (0, 0))),
        scratch_shapes=[pltpu.VMEM((nd, width), F32), pltpu.VMEM((nd, 1), I32)],
        compiler_params=_params(("arbitrary",)),
        name="sample_select",
    )(s_past, qi2, misc)


SAMPLE_KCHUNK = 1024


def _sample_attn_body(pt_ref, q_ref, ks_ref, vs_ref, b2_ref, bs_ref, ck_hbm, cv_hbm, o_ref, kbuf, vbuf, sem,
                      *, n_pages, prows):
    def fetch(seq, slot):
        dst = lambda p: (pl.ds(p * prows, prows), slice(None))
        return (_page_copies(pt_ref, ck_hbm, kbuf, sem.at[0, slot], seq, slot, n_pages, dst)
                + _page_copies(pt_ref, cv_hbm, vbuf, sem.at[1, slot], seq, slot, n_pages, dst))

    slot = _gather_step(fetch)
    q = q_ref[0]
    qb = q.astype(BF16)
    total = n_pages * prows
    ch = min(SAMPLE_KCHUNK, total)
    head_grp = lax.broadcasted_iota(I32, (N_HEADS, ch), 0) // HEADS_PER_KV
    row_grp = lax.broadcasted_iota(I32, (N_HEADS, ch), 1) % N_KV_HEADS
    own = head_grp == row_grp
    s_chunks = []
    for c in range(total // ch):
        kc = kbuf[slot, c * ch:(c + 1) * ch, :].astype(BF16)
        s_chunks.append(jnp.where(own, _dot_nt(qb, kc) + b2_ref[0][:, c * ch:(c + 1) * ch], NEG))

    grp0 = lax.broadcasted_iota(I32, (N_HEADS, HEAD_DIM), 0) < HEADS_PER_KV
    k_self = jnp.where(grp0, ks_ref[0][:, 0:HEAD_DIM], ks_ref[0][:, HEAD_DIM:])
    v_self = jnp.where(grp0, vs_ref[0][:, 0:HEAD_DIM], vs_ref[0][:, HEAD_DIM:])
    s_self = jnp.sum(q * k_self, axis=1, keepdims=True) + bs_ref[0][:, 0:1]

    m = s_self
    for s in s_chunks:
        m = jnp.maximum(m, jnp.max(s, axis=1, keepdims=True))
    p_self = jnp.exp(s_self - m)
    l = p_self
    acc = p_self * v_self
    for c, s in enumerate(s_chunks):
        p = jnp.exp(s - m)
        l = l + jnp.sum(p, axis=1, keepdims=True)
        vc = vbuf[slot, c * ch:(c + 1) * ch, :].astype(BF16)
        acc = acc + jnp.dot(p.astype(BF16), vc, preferred_element_type=F32)
    o_ref[0] = (acc / l).astype(BF16)


def _sample_attn(page_table, q3, k_self, v_self, bias2, bias_self, ck2, cv2):
    nd, n_pages = page_table.shape
    prows = ck2.shape[1]
    total = n_pages * prows
    body = functools.partial(_sample_attn_body, n_pages=n_pages, prows=prows)
    grid_spec = pltpu.PrefetchScalarGridSpec(
        num_scalar_prefetch=1,
        grid=(nd,),
        in_specs=[
            pl.BlockSpec((1, N_HEADS, HEAD_DIM), lambda b, pt: (b, 0, 0)),
            pl.BlockSpec((1, 1, KV_WIDTH), lambda b, pt: (b, 0, 0)),
            pl.BlockSpec((1, 1, KV_WIDTH), lambda b, pt: (b, 0, 0)),
            pl.BlockSpec((1, 1, total), lambda b, pt: (b, 0, 0)),
            pl.BlockSpec((1, 1, LANES), lambda b, pt: (b, 0, 0)),
            pl.BlockSpec(memory_space=pl.ANY),
            pl.BlockSpec(memory_space=pl.ANY),
        ],
        out_specs=pl.BlockSpec((1, N_HEADS, HEAD_DIM), lambda b, pt: (b, 0, 0)),
        scratch_shapes=[
            pltpu.VMEM((2, total, HEAD_DIM), F32),
            pltpu.VMEM((2, total, HEAD_DIM), F32),
            pltpu.SemaphoreType.DMA((2, 2)),
        ],
    )
    return pl.pallas_call(
        body,
        out_shape=jax.ShapeDtypeStruct((nd, N_HEADS, HEAD_DIM), BF16),
        grid_spec=grid_spec,
        compiler_params=_params(("arbitrary",)),
        name="sample_attn",
    )(page_table, q3, k_self, v_self, bias2, bias_self, ck2, cv2)


def _gla_sample_body(gq_ref, gk_ref, gv_ref, misc_ref, gr_ref, s_ref, w2_ref, gb_ref, nw_ref, og_ref, so_ref):
    eye = jnp.where(lax.broadcasted_iota(I32, (GLA_DK, GLA_DK), 0)
                    == lax.broadcasted_iota(I32, (GLA_DK, GLA_DK), 1), 1.0, 0.0)

    def column(row):
        return jnp.sum(eye * row, axis=1, keepdims=True)

    for sq in range(s_ref.shape[0]):
        misc = misc_ref[sq]
        for h in range(GLA_HEADS):
            ksl = slice(h * GLA_DK, (h + 1) * GLA_DK)
            vsl = slice(h * GLA_DV, (h + 1) * GLA_DV)
            g = _log_decay(misc, w2_ref[:, ksl], gb_ref[:, ksl])
            s_new = column(jnp.exp(g)) * s_ref[sq, h] + column(gk_ref[sq][:, ksl]) * gv_ref[sq][:, vsl]
            so_ref[sq, h] = s_new
            o = jnp.sum(column(gq_ref[sq][:, ksl] * (GLA_DK ** -0.5)) * s_new, axis=0, keepdims=True)
            og_ref[sq, :, vsl] = _gla_out(o, nw_ref[...], gr_ref[sq][:, vsl])


def _gla_sample(gq, gk, gv, misc, gr, state, w2p, gbias, norm_w):
    nd = state.shape[0]
    sb = 4 if nd % 4 == 0 else 1

    def row3(w):
        return pl.BlockSpec((sb, 1, w), lambda b: (b, 0, 0))

    st_spec = pl.BlockSpec((sb, GLA_HEADS, GLA_DK, GLA_DV), lambda b: (b, 0, 0, 0))
    return pl.pallas_call(
        _gla_sample_body,
        out_shape=(jax.ShapeDtypeStruct((nd, 1, GLA_VAL_WIDTH), BF16),
                   jax.ShapeDtypeStruct(state.shape, F32)),
        grid=(nd // sb,),
        in_specs=[
            row3(GLA_KEY_WIDTH), row3(GLA_KEY_WIDTH), row3(GLA_VAL_WIDTH), row3(LANES), row3(GLA_VAL_WIDTH),
            st_spec,
            pl.BlockSpec((LANES, GLA_KEY_WIDTH), lambda b: (0, 0)),
            pl.BlockSpec((1, GLA_KEY_WIDTH), lambda b: (0, 0)),
            pl.BlockSpec((1, GLA_DV), lambda b: (0, 0)),
        ],
        out_specs=(row3(GLA_VAL_WIDTH), st_spec),
        compiler_params=_params(("parallel",)),
        name="gla_sample",
    )(gq, gk, gv, misc, gr, state, w2p, gbias, norm_w.reshape(1, GLA_DV))


def _rope_tables(pos, d):
    inv = ROPE_THETA ** (-jnp.arange(0, d, 2, dtype=F32) / d)
    ang = pos.astype(F32)[:, None] * inv[None, :]
    cos, sin = jnp.cos(ang), jnp.sin(ang)
    reps = LANES // d
    return (jnp.tile(jnp.concatenate([cos, cos], axis=-1), (1, reps)),
            jnp.tile(jnp.concatenate([-sin, sin], axis=-1), (1, reps)))


def _permute_w_in(w_in, d):
    sizes = (ATTN_WIDTH, KV_WIDTH, KV_WIDTH, QI_WIDTH, IDX_DIM, IDX_HEADS, GLA_KEY_WIDTH, GLA_KEY_WIDTH,
             GLA_VAL_WIDTH, GLA_GATE_RANK, GLA_VAL_WIDTH, d, d)
    pts, acc = [], 0
    for sz in sizes[:-1]:
        acc += sz
        pts.append(acc)
    q, k, v, qi, ki, wi, gq, gk, gv, glr, gr, ga, gg = jnp.split(w_in, pts, axis=1)
    pad = jnp.zeros((w_in.shape[0], C_GQ - (C_MISC + IDX_DIM + IDX_HEADS + GLA_GATE_RANK)), w_in.dtype)
    return jnp.concatenate([q, k, v, qi, ki, wi, glr, pad, gq, gk, gv, gr, ga, gg], axis=1).astype(BF16)


def kernel(x_prompt, x_sample, cache_k, cache_v, cache_kidx, page_table, state_gla,
           ffn1_pre_w, ffn1_w_gate, ffn1_w_up, ffn1_w_down, ffn1_post_w,
           mix_pre_w, w_in, gla_gate_w2, gla_gate_b, gla_norm_w,
           w_proj_attn, w_proj_gla, w_out, mix_post_w,
           ffn2_pre_w, ffn2_w_gate, ffn2_w_up, ffn2_w_down, ffn2_post_w):
    b, s, d = x_prompt.shape
    nd, td, _ = x_sample.shape
    n_pool, page = cache_k.shape[:2]
    n_pages = page_table.shape[1]
    past = n_pages * page
    assert td == 1 and s % Q_BLOCK == 0 and d % MIX_TN == 0 and C_GA % d == 0

    f1 = (ffn1_w_gate.astype(BF16), ffn1_w_up.astype(BF16), ffn1_w_down.astype(BF16))
    f2 = (ffn2_w_gate.astype(BF16), ffn2_w_up.astype(BF16), ffn2_w_down.astype(BF16))
    w_perm = _permute_w_in(w_in, d)
    wpa, wpg, wo = w_proj_attn.astype(BF16), w_proj_gla.astype(BF16), w_out.astype(BF16)
    w2p = jnp.zeros((LANES, GLA_KEY_WIDTH), F32).at[MISC_GLR:MISC_GLR + GLA_GATE_RANK].set(gla_gate_w2)
    gbias = gla_gate_b.reshape(1, GLA_KEY_WIDTH)

    def trunk(x, tm, tm_mix, tabs, tab_rows, mixer):
        h = _ffn(x, ffn1_pre_w, *f1, ffn1_post_w, tm)
        z = _mix_in(h, mix_pre_w, w_perm, tabs, tm_mix, tab_rows // tm_mix)
        o_attn, o_gla, s_fin = mixer(z)
        tmm = min(tm, 256)
        merged = _merge(o_attn, o_gla, z, wpa, wpg, tmm)
        h = _out_proj(merged, h, wo, mix_post_w, tmm)
        y = _ffn(h, ffn2_pre_w, *f2, ffn2_post_w, tm)
        return y, z, s_fin

    pos_p = jnp.arange(s, dtype=I32)
    tm_p = 512 if s % 512 == 0 else Q_BLOCK
    tabs_p = _rope_tables(pos_p, HEAD_DIM) + _rope_tables(pos_p, IDX_DIM)
    top_p = min(TOPK_MAX, s // 4)

    def mixer_p(z):
        o_attn = _attn_prompt(z, b, s, top_p)
        o_gla, s_fin = _gla_prompt(z, w2p, gbias, gla_norm_w, b, s)
        return o_attn, o_gla, s_fin

    tm_mix = 1024 if s % 1024 == 0 else tm_p
    y_p, z_p, gla_p = trunk(x_prompt.reshape(b * s, d), tm_p, tm_mix, tabs_p, s, mixer_p)

    pos_s = jnp.full((nd,), past, I32)
    tabs_s = _rope_tables(pos_s, HEAD_DIM) + _rope_tables(pos_s, IDX_DIM)
    top_s = min(TOPK_MAX, (past + td) // 4)
    ck2 = cache_k.reshape(n_pool, page * N_KV_HEADS, HEAD_DIM)
    cv2 = cache_v.reshape(n_pool, page * N_KV_HEADS, HEAD_DIM)
    kidx_t = jnp.swapaxes(cache_kidx, 1, 2)

    def mixer_s(z):
        misc = z[:, C_MISC:C_MISC + LANES]
        qi2 = z[:, C_QI:C_QI + QI_WIDTH]
        s_past = _sample_scores(page_table, qi2.reshape(nd, IDX_HEADS, IDX_DIM),
                                misc[:, MISC_WI:MISC_GLR].reshape(nd, IDX_HEADS, 1), kidx_t)
        bias_past, bias_self = _sample_select(s_past.reshape(nd, past), qi2, misc, top_s)
        bias2 = jnp.repeat(bias_past, N_KV_HEADS, axis=1).reshape(nd, 1, past * N_KV_HEADS)
        o_attn = _sample_attn(page_table, z[:, C_Q:C_Q + ATTN_WIDTH].reshape(nd, N_HEADS, HEAD_DIM),
                              z[:, C_K:C_K + KV_WIDTH].reshape(nd, 1, KV_WIDTH),
                              z[:, C_V:C_V + KV_WIDTH].reshape(nd, 1, KV_WIDTH),
                              bias2, bias_self.reshape(nd, 1, LANES), ck2, cv2)
        o_gla, s_fin = _gla_sample(z[:, C_GQ:C_GQ + GLA_KEY_WIDTH].reshape(nd, 1, GLA_KEY_WIDTH),
                                   z[:, C_GK:C_GK + GLA_KEY_WIDTH].reshape(nd, 1, GLA_KEY_WIDTH),
                                   z[:, C_GV:C_GV + GLA_VAL_WIDTH].reshape(nd, 1, GLA_VAL_WIDTH),
                                   misc.reshape(nd, 1, LANES),
                                   z[:, C_GR:C_GR + GLA_VAL_WIDTH].reshape(nd, 1, GLA_VAL_WIDTH),
                                   state_gla, w2p, gbias, gla_norm_w)
        return o_attn.reshape(nd, ATTN_WIDTH), o_gla.reshape(nd, GLA_VAL_WIDTH), s_fin

    y_s, z_s, gla_s = trunk(x_sample.reshape(nd, d), nd, nd, tabs_s, nd, mixer_s)

    def kv_out(z, n, t):
        return (z[:, C_K:C_K + KV_WIDTH].reshape(n, t, N_KV_HEADS, HEAD_DIM),
                z[:, C_V:C_V + KV_WIDTH].reshape(n, t, N_KV_HEADS, HEAD_DIM),
                z[:, C_MISC:C_MISC + IDX_DIM].reshape(n, t, IDX_DIM))

    k_p, v_p, ki_p = kv_out(z_p, b, s)
    k_s, v_s, ki_s = kv_out(z_s, nd, td)
    return (y_p.reshape(b, s, d), y_s.reshape(nd, td, d), k_p, v_p, ki_p, gla_p, k_s, v_s, ki_s, gla_s)
```

```python
import functools

import jax
import jax.numpy as jnp
from jax import lax
from jax.experimental import pallas as pl
from jax.experimental.pallas import tpu as pltpu

F32, BF16, I32 = jnp.float32, jnp.bfloat16, jnp.int32
HIGHEST = lax.Precision.HIGHEST

N_HEADS = 8
N_KV_HEADS = 2
HEAD_DIM = 128
IDX_HEADS = 16
IDX_DIM = 64
TOPK_MAX = 256
Q_BLOCK = 128
ROPE_THETA = 10000.0
GLA_HEADS = 4
GLA_DK = 128
GLA_DV = 256
GLA_GATE_RANK = 16
GLA_GATE_TAU = 16.0
RMS_EPS = 1e-6

LANES = 128
ATTN_WIDTH = N_HEADS * HEAD_DIM
KV_WIDTH = N_KV_HEADS * HEAD_DIM
QI_WIDTH = IDX_HEADS * IDX_DIM
GLA_KEY_WIDTH = GLA_HEADS * GLA_DK
GLA_VAL_WIDTH = GLA_HEADS * GLA_DV
HEADS_PER_KV = N_HEADS // N_KV_HEADS

C_Q, C_K, C_V, C_QI, C_MISC = 0, 1024, 1280, 1536, 2560
C_GQ, C_GK, C_GV, C_GR, C_GA = 3072, 3584, 4096, 5120, 6144
MISC_WI, MISC_GLR = IDX_DIM, IDX_DIM + IDX_HEADS
MIX_TN = 512

NEG = -1e30
INT_MIN = -2 ** 31
NEG_INF_KEY = -2139095041
VMEM_LIMIT = 56 * 1024 * 1024


def _params(sem, vmem=VMEM_LIMIT):
    return pltpu.CompilerParams(dimension_semantics=sem, vmem_limit_bytes=vmem)


def _rms(x, w):
    return x * lax.rsqrt(jnp.mean(x * x, axis=-1, keepdims=True) + RMS_EPS) * w


def _dot_nt(a, b):
    return lax.dot_general(a, b, (((1,), (1,)), ((), ())), preferred_element_type=F32)


def _ffn_body(x_ref, prew_ref, wg_ref, wu_ref, wd_ref, postw_ref, o_ref, *rest):
    *wb_refs, z_scr, acc_scr = rest
    j = pl.program_id(1)

    @pl.when(j == 0)
    def _():
        z_scr[...] = _rms(x_ref[...], prew_ref[...]).astype(BF16)
        acc_scr[...] = jnp.zeros_like(acc_scr)

    wg, wu, wd = wg_ref[...].astype(BF16), wu_ref[...].astype(BF16), wd_ref[...].astype(BF16)
    for ref, w in zip(wb_refs, (wg, wu, wd)):
        ref[...] = w
    z = z_scr[...]
    g = jnp.dot(z, wg, preferred_element_type=F32)
    u = jnp.dot(z, wu, preferred_element_type=F32)
    a = (g * jax.nn.sigmoid(g) * u).astype(BF16)
    acc_scr[...] += jnp.dot(a, wd, preferred_element_type=F32)

    @pl.when(j == pl.num_programs(1) - 1)
    def _():
        o_ref[...] = x_ref[...] + 0.5 * _rms(acc_scr[...], postw_ref[...])


def _ffn(x, pre_w, wg, wu, wd, post_w, tm):
    rows, d = x.shape
    dff = wg.shape[1]
    tf = 512 if dff % 512 == 0 else dff
    emit = wg.dtype == F32
    assert not emit or rows == tm
    w_specs = [
        pl.BlockSpec((d, tf), lambda i, j: (0, j)),
        pl.BlockSpec((d, tf), lambda i, j: (0, j)),
        pl.BlockSpec((tf, d), lambda i, j: (j, 0)),
    ]
    y_shape = jax.ShapeDtypeStruct((rows, d), F32)
    y_spec = pl.BlockSpec((tm, d), lambda i, j: (i, 0))
    out = pl.pallas_call(
        _ffn_body,
        out_shape=(y_shape, *(jax.ShapeDtypeStruct(w.shape, BF16) for w in (wg, wu, wd))) if emit else y_shape,
        grid=(rows // tm, dff // tf),
        in_specs=[
            pl.BlockSpec((tm, d), lambda i, j: (i, 0)),
            pl.BlockSpec((1, d), lambda i, j: (0, 0)),
            *w_specs,
            pl.BlockSpec((1, d), lambda i, j: (0, 0)),
        ],
        out_specs=(y_spec, *w_specs) if emit else y_spec,
        scratch_shapes=[pltpu.VMEM((tm, d), BF16), pltpu.VMEM((tm, d), F32)],
        compiler_params=_params(("parallel", "arbitrary")),
        name="ffn",
    )(x, pre_w.reshape(1, d), wg, wu, wd, post_w.reshape(1, d))
    return (out[0], tuple(out[1:])) if emit else out


def _mix_in_body(x_ref, prew_ref, w_ref, c128_ref, s128_ref, c64_ref, s64_ref, o_ref, u_scr):
    j = pl.program_id(1)

    @pl.when(j == 0)
    def _():
        u_scr[...] = _rms(x_ref[...], prew_ref[...]).astype(BF16)

    r = jnp.dot(u_scr[...], w_ref[...], preferred_element_type=F32)

    def rope128(x):
        return x * c128_ref[...] + pltpu.roll(x, HEAD_DIM // 2, 1) * s128_ref[...]

    def rope64(x):
        lane = lax.broadcasted_iota(I32, x.shape, 1)
        first = (lane % IDX_DIM) < (IDX_DIM // 2)
        rot = jnp.where(first, pltpu.roll(x, LANES - IDX_DIM // 2, 1), pltpu.roll(x, IDX_DIM // 2, 1))
        return x * c64_ref[...] + rot * s64_ref[...]

    def sl(t):
        return slice(t * LANES, (t + 1) * LANES)

    @pl.when(j < 2)
    def _():
        for t in range(4):
            o_ref[:, sl(t)] = rope128(r[:, sl(t)]) * (HEAD_DIM ** -0.5)

    @pl.when(j == 2)
    def _():
        for t in range(2):
            o_ref[:, sl(t)] = rope128(r[:, sl(t)])
        o_ref[:, 2 * LANES:] = r[:, 2 * LANES:]

    @pl.when((j == 3) | (j == 4))
    def _():
        for t in range(4):
            o_ref[:, sl(t)] = rope64(r[:, sl(t)])

    @pl.when(j == 5)
    def _():
        x = r[:, sl(0)]
        lane = lax.broadcasted_iota(I32, x.shape, 1)
        wi_scale = IDX_HEADS ** -0.5 * IDX_DIM ** -0.5
        y = jnp.where(lane < MISC_WI, rope64(x), jnp.where(lane < MISC_GLR, x * wi_scale, x))
        o_ref[:, sl(0)] = y
        o_ref[:, LANES:] = r[:, LANES:]

    @pl.when(j > 5)
    def _():
        o_ref[...] = r


def _mix_in(h, pre_w, w_perm, tabs, tm, tab_blocks):
    rows, d = h.shape
    zw = w_perm.shape[1]
    c128, s128, c64, s64 = tabs
    tab_spec = pl.BlockSpec((tm, LANES), lambda i, j: (i % tab_blocks, 0))
    return pl.pallas_call(
        _mix_in_body,
        out_shape=jax.ShapeDtypeStruct((rows, zw), F32),
        grid=(rows // tm, zw // MIX_TN),
        in_specs=[
            pl.BlockSpec((tm, d), lambda i, j: (i, 0)),
            pl.BlockSpec((1, d), lambda i, j: (0, 0)),
            pl.BlockSpec((d, MIX_TN), lambda i, j: (0, j)),
            tab_spec, tab_spec, tab_spec, tab_spec,
        ],
        out_specs=pl.BlockSpec((tm, MIX_TN), lambda i, j: (i, j)),
        scratch_shapes=[pltpu.VMEM((tm, d), BF16)],
        compiler_params=_params(("parallel", "arbitrary")),
        name="mix_in",
    )(h, pre_w.reshape(1, d), w_perm, c128, s128, c64, s64)


def _key_to_float(key):
    key = jnp.maximum(key, NEG_INF_KEY)
    return lax.bitcast_convert_type(key ^ ((key >> 31) & jnp.int32(0x7FFFFFFF)), F32)


def _select_threshold(sc_ref, j_ref, nk, tk, topk, idx_bits, key_axis):
    n_other = sc_ref.shape[1 - key_axis]
    vec = (n_other, 1) if key_axis == 1 else (1, n_other)
    step = LANES if key_axis == 1 else 8

    def count(pred):
        def body(c, acc):
            ks = pl.multiple_of(c * tk, tk)
            blk = sc_ref[:, pl.ds(ks, tk)] if key_axis == 1 else sc_ref[pl.ds(ks, tk), :]
            m = jnp.where(pred(blk, ks), 1.0, 0.0)
            parts = [lax.slice_in_dim(m, a, a + step, axis=key_axis) for a in range(0, tk, step)]
            while len(parts) > 1:
                parts = [parts[a] + parts[a + 1] for a in range(0, len(parts), 2)]
            return acc + parts[0]
        acc0 = jnp.zeros((n_other, step) if key_axis == 1 else (step, n_other), F32)
        return jnp.sum(lax.fori_loop(0, nk, body, acc0), axis=key_axis, keepdims=True)

    def bit_body(bi, t):
        cand = t ^ lax.shift_left(jnp.int32(1), 31 - bi)
        cand_f = _key_to_float(cand)
        return jnp.where(count(lambda blk, ks: blk >= cand_f) >= topk, cand, t)

    thr_key = lax.fori_loop(0, 32, bit_body, jnp.full(vec, INT_MIN, I32))
    thr = _key_to_float(thr_key)

    cnt_ge = count(lambda blk, ks: blk >= thr)
    cnt_gt = count(lambda blk, ks: blk > thr)
    need = topk - cnt_gt
    tie = jnp.where((cnt_ge > topk) & (thr_key > NEG_INF_KEY), 1.0, 0.0)
    j_ref[...] = jnp.full(vec, 1 << idx_bits, I32)

    @pl.when(jnp.max(tie) > 0.0)
    def _():
        def jbit(bi, jb):
            cand = jb + lax.shift_left(jnp.int32(1), idx_bits - 1 - bi)

            def pred(blk, ks):
                kpos = ks + lax.broadcasted_iota(I32, blk.shape, key_axis)
                return (blk == thr) & (kpos < cand)
            return jnp.where(count(pred) <= need, cand, jb)
        j_ref[...] = lax.fori_loop(0, idx_bits, jbit, jnp.zeros(vec, I32))

    return thr


def _selected(score, kpos, thr, jb):
    return (score > thr) | ((score == thr) & (kpos < jb))


def _attn_prompt_body(q_ref, qia_ref, qib_ref, misc_ref, k_ref, v_ref, kim_ref, o_ref,
                      sc_scr, j_scr, bias_scr, kb_scr, vb_scr, kib_scr, qi_scr, wt_scr, qs_scr, mx_scr, l_scr,
                      acc_scr, *, tk, topk, idx_bits):
    i = pl.program_id(1)
    tq = Q_BLOCK
    nk = (i * tq + tq + tk - 1) // tk
    n_tiles = tk // LANES
    sub = 256
    qpos_t = i * tq + lax.broadcasted_iota(I32, (1, tq), 1)

    @pl.when(i == 0)
    def _():
        def cast_chunk(c, carry):
            rs = pl.ds(pl.multiple_of(c * tk, tk), tk)
            kb_scr[rs, :] = k_ref[rs, :].astype(BF16)
            vb_scr[rs, :] = v_ref[rs, :].astype(BF16)
            kib_scr[rs, :] = kim_ref[rs, 0:IDX_DIM].astype(BF16)
            return carry
        lax.fori_loop(0, k_ref.shape[0] // tk, cast_chunk, 0)

    for h in range(IDX_HEADS):
        ref = qia_ref if h < IDX_HEADS // 2 else qib_ref
        hh = h % (IDX_HEADS // 2)
        qi_scr[h // 2, (h % 2) * tq:(h % 2 + 1) * tq, :] = ref[:, hh * IDX_DIM:(hh + 1) * IDX_DIM].astype(BF16)
    wt_scr[...] = misc_ref[...].T

    def score_chunk(c, carry):
        for s0 in range(0, tk, sub):
            ks = pl.multiple_of(c * tk, tk) + s0
            ki = kib_scr[pl.ds(ks, sub), :]
            acc = jnp.zeros((sub, tq), F32)
            for hp in range(IDX_HEADS // 2):
                r = jnp.maximum(_dot_nt(ki, qi_scr[hp]), 0.0)
                acc = (acc + r[:, 0:tq] * wt_scr[pl.ds(MISC_WI + 2 * hp, 1), :]
                       + r[:, tq:] * wt_scr[pl.ds(MISC_WI + 2 * hp + 1, 1), :])
            kpos = ks + lax.broadcasted_iota(I32, (sub, tq), 0)
            sc_scr[pl.ds(ks, sub), :] = jnp.where(kpos <= qpos_t, acc, -jnp.inf)
        return carry

    lax.fori_loop(0, nk, score_chunk, 0)
    thr = _select_threshold(sc_scr, j_scr, nk, tk, topk, idx_bits, 0)
    jb = j_scr[...]

    for g in range(N_KV_HEADS):
        for r in range(HEADS_PER_KV):
            hd = g * HEADS_PER_KV + r
            qs_scr[g, r * tq:(r + 1) * tq, :] = q_ref[:, hd * HEAD_DIM:(hd + 1) * HEAD_DIM].astype(BF16)
    mx_scr[...] = jnp.full(mx_scr.shape, NEG, F32)
    l_scr[...] = jnp.zeros(l_scr.shape, F32)
    acc_scr[...] = jnp.zeros(acc_scr.shape, F32)

    def logits(g, ks, bias4):
        kc = kb_scr[pl.ds(ks, tk), g * HEAD_DIM:(g + 1) * HEAD_DIM]
        return _dot_nt(qs_scr[g], kc) + bias4

    def max_pass(c, carry):
        ks = pl.multiple_of(c * tk, tk)
        kpos = ks + lax.broadcasted_iota(I32, (tk, tq), 0)
        sel = _selected(sc_scr[pl.ds(ks, tk), :], kpos, thr, jb) & (kpos <= qpos_t)
        bias = jnp.where(sel, 0.0, NEG).T
        bias_scr[:, pl.ds(ks, tk)] = bias
        bias4 = jnp.concatenate([bias] * HEADS_PER_KV, axis=0)
        for g in range(N_KV_HEADS):
            s = logits(g, ks, bias4)
            m = mx_scr[g]
            for t in range(n_tiles):
                m = jnp.maximum(m, s[:, t * LANES:(t + 1) * LANES])
            mx_scr[g] = m
        return carry

    lax.fori_loop(0, nk, max_pass, 0)
    for g in range(N_KV_HEADS):
        mx_scr[g] = jnp.broadcast_to(jnp.max(mx_scr[g], axis=1, keepdims=True), mx_scr.shape[1:])

    def exp_pass(c, carry):
        ks = pl.multiple_of(c * tk, tk)
        bias4 = jnp.concatenate([bias_scr[:, pl.ds(ks, tk)]] * HEADS_PER_KV, axis=0)
        for g in range(N_KV_HEADS):
            s = logits(g, ks, bias4)
            m = mx_scr[g]
            p = [jnp.exp(s[:, t * LANES:(t + 1) * LANES] - m) for t in range(n_tiles)]
            l_scr[g] += functools.reduce(lambda x, y: x + y, p)
            pb = jnp.concatenate(p, axis=1).astype(BF16)
            vc = vb_scr[pl.ds(ks, tk), g * HEAD_DIM:(g + 1) * HEAD_DIM]
            acc_scr[g] += jnp.dot(pb, vc, preferred_element_type=F32)
        return carry

    lax.fori_loop(0, nk, exp_pass, 0)

    for g in range(N_KV_HEADS):
        o = acc_scr[g] / jnp.sum(l_scr[g], axis=1, keepdims=True)
        for r in range(HEADS_PER_KV):
            hd = g * HEADS_PER_KV + r
            o_ref[:, hd * HEAD_DIM:(hd + 1) * HEAD_DIM] = o[r * tq:(r + 1) * tq, :].astype(BF16)


def _attn_prompt(z, b, s, topk):
    nq = s // Q_BLOCK
    tk = min(512, s)
    idx_bits = max(1, (s - 1).bit_length()) + 1
    body = functools.partial(_attn_prompt_body, tk=tk, topk=topk, idx_bits=idx_bits)
    rows = HEADS_PER_KV * Q_BLOCK
    return pl.pallas_call(
        body,
        out_shape=jax.ShapeDtypeStruct((b * s, ATTN_WIDTH), BF16),
        grid=(b, nq),
        in_specs=[
            pl.BlockSpec((Q_BLOCK, ATTN_WIDTH), lambda bb, i: (bb * nq + i, 0)),
            pl.BlockSpec((Q_BLOCK, QI_WIDTH // 2), lambda bb, i: (bb * nq + i, C_QI // (QI_WIDTH // 2))),
            pl.BlockSpec((Q_BLOCK, QI_WIDTH // 2), lambda bb, i: (bb * nq + i, C_QI // (QI_WIDTH // 2) + 1)),
            pl.BlockSpec((Q_BLOCK, LANES), lambda bb, i: (bb * nq + i, C_MISC // LANES)),
            pl.BlockSpec((s, KV_WIDTH), lambda bb, i: (bb, C_K // KV_WIDTH)),
            pl.BlockSpec((s, KV_WIDTH), lambda bb, i: (bb, C_V // KV_WIDTH)),
            pl.BlockSpec((s, LANES), lambda bb, i: (bb, C_MISC // LANES)),
        ],
        out_specs=pl.BlockSpec((Q_BLOCK, ATTN_WIDTH), lambda bb, i: (bb * nq + i, 0)),
        scratch_shapes=[
            pltpu.VMEM((s, Q_BLOCK), F32),
            pltpu.VMEM((1, Q_BLOCK), I32),
            pltpu.VMEM((Q_BLOCK, s), F32),
            pltpu.VMEM((s, KV_WIDTH), BF16),
            pltpu.VMEM((s, KV_WIDTH), BF16),
            pltpu.VMEM((s, IDX_DIM), BF16),
            pltpu.VMEM((IDX_HEADS // 2, 2 * Q_BLOCK, IDX_DIM), BF16),
            pltpu.VMEM((LANES, Q_BLOCK), F32),
            pltpu.VMEM((N_KV_HEADS, rows, HEAD_DIM), BF16),
            pltpu.VMEM((N_KV_HEADS, rows, LANES), F32),
            pltpu.VMEM((N_KV_HEADS, rows, LANES), F32),
            pltpu.VMEM((N_KV_HEADS, rows, HEAD_DIM), F32),
        ],
        compiler_params=_params(("parallel", "arbitrary")),
        name="attn_prompt",
    )(z, z, z, z, z, z, z)


def _log_decay(misc, w2, gb):
    x = jnp.dot(misc, w2, precision=HIGHEST, preferred_element_type=F32) + gb
    return (jnp.minimum(x, 0.0) - jnp.log1p(jnp.exp(-jnp.abs(x)))) * (1.0 / GLA_GATE_TAU)


def _gla_out(o, nw, gr):
    return (_rms(o, nw) * (gr * jax.nn.sigmoid(gr))).astype(BF16)


GLA_C = 128
GLA_SUB = 16
GLA_HPS = 4


def _gla_prompt_body(gq_ref, gk_ref, gv_ref, misc_ref, gr_ref, w2_ref, gb_ref, nw_ref,
                     og_ref, sfin_ref, st_scr, a_scr, b_scr, k_scr, o_scr, *, nchunk):
    t = pl.program_id(2)
    c_ = GLA_C

    @pl.when(t == 0)
    def _():
        st_scr[...] = jnp.zeros_like(st_scr)

    a_scr[...] = jnp.zeros_like(a_scr)
    row = lax.broadcasted_iota(I32, (c_, c_), 0)
    col = lax.broadcasted_iota(I32, (c_, c_), 1)
    tri = jnp.where(col <= row, 1.0, 0.0)
    sub_row = lax.broadcasted_iota(I32, (GLA_SUB, LANES), 0)
    sub_col = lax.broadcasted_iota(I32, (GLA_SUB, GLA_SUB), 1)

    def head_chunk(hh, r0):
        ksl = slice(hh * GLA_DK, (hh + 1) * GLA_DK)
        vsl = slice(hh * GLA_DV, (hh + 1) * GLA_DV)
        q = gq_ref[pl.ds(r0, c_), ksl] * (GLA_DK ** -0.5)
        k = gk_ref[pl.ds(r0, c_), ksl]
        v = gv_ref[pl.ds(r0, c_), vsl].astype(BF16)
        g = _log_decay(misc_ref[pl.ds(r0, c_), :], w2_ref[:, ksl], gb_ref[:, ksl])
        b = jnp.dot(tri, g, precision=HIGHEST, preferred_element_type=F32)
        b_scr[hh] = b
        k_scr[hh] = k
        st = st_scr[hh]
        o = _dot_nt((q * jnp.exp(b)).astype(BF16), st.astype(BF16))

        n = c_ // 2
        while n >= GLA_SUB:
            for rb in range(n, c_, 2 * n):
                bref = b[rb:rb + 1, :]
                qs = q[rb:rb + n, :] * jnp.exp(b[rb:rb + n, :] - bref)
                ks = k[rb - n:rb, :] * jnp.exp(bref - b[rb - n:rb, :])
                a_scr[hh, rb:rb + n, rb - n:rb] = _dot_nt(qs.astype(BF16), ks.astype(BF16))
            n //= 2
        for blk in range(c_ // GLA_SUB):
            lo = blk * GLA_SUB
            qb = q[lo:lo + GLA_SUB, :]
            bb = b[lo:lo + GLA_SUB, :]
            ad = jnp.zeros((GLA_SUB, GLA_SUB), F32)
            for jj in range(GLA_SUB):
                bj = b_scr[hh, pl.ds(lo + jj, 1), :]
                kj = k_scr[hh, pl.ds(lo + jj, 1), :]
                w = jnp.exp(jnp.where(sub_row >= jj, bb - bj, NEG))
                colj = jnp.sum(qb * kj * w, axis=1, keepdims=True)
                ad = jnp.where(sub_col == jj, colj, ad)
            a_scr[hh, lo:lo + GLA_SUB, lo:lo + GLA_SUB] = ad

        o = o + jnp.dot(a_scr[hh].astype(BF16), v, preferred_element_type=F32)
        o_scr[pl.ds(r0, c_), vsl] = o
        b_last = b[c_ - 1:c_, :]
        kd = (k * jnp.exp(b_last - b)).astype(BF16)
        kv = lax.dot_general(v, kd, (((0,), (0,)), ((), ())), preferred_element_type=F32)
        st_scr[hh] = st * jnp.exp(b_last) + kv

    def chunk(ci, carry):
        r0 = pl.multiple_of(ci * c_, c_)
        for hh in range(GLA_HPS):
            head_chunk(hh, r0)
        return carry

    lax.fori_loop(0, nchunk, chunk, 0)
    for hh in range(GLA_HPS):
        vsl = slice(hh * GLA_DV, (hh + 1) * GLA_DV)
        og_ref[:, vsl] = _gla_out(o_scr[:, vsl], nw_ref[...], gr_ref[:, vsl])

    @pl.when(t == pl.num_programs(2) - 1)
    def _():
        for hh in range(GLA_HPS):
            sfin_ref[0, hh] = st_scr[hh].T


def _gla_prompt(z, w2p, gbias, norm_w, b, s):
    tb = min(512, s)
    nt = s // tb
    body = functools.partial(_gla_prompt_body, nchunk=tb // GLA_C)

    def rowblk(bb, t):
        return bb * nt + t

    kw, vw = GLA_HPS * GLA_DK, GLA_HPS * GLA_DV
    return pl.pallas_call(
        body,
        out_shape=(jax.ShapeDtypeStruct((b * s, GLA_VAL_WIDTH), BF16),
                   jax.ShapeDtypeStruct((b, GLA_HEADS, GLA_DK, GLA_DV), F32)),
        grid=(b, GLA_HEADS // GLA_HPS, nt),
        in_specs=[
            pl.BlockSpec((tb, kw), lambda bb, h, t: (rowblk(bb, t), C_GQ // kw + h)),
            pl.BlockSpec((tb, kw), lambda bb, h, t: (rowblk(bb, t), C_GK // kw + h)),
            pl.BlockSpec((tb, vw), lambda bb, h, t: (rowblk(bb, t), C_GV // vw + h)),
            pl.BlockSpec((tb, LANES), lambda bb, h, t: (rowblk(bb, t), C_MISC // LANES)),
            pl.BlockSpec((tb, vw), lambda bb, h, t: (rowblk(bb, t), C_GR // vw + h)),
            pl.BlockSpec((LANES, kw), lambda bb, h, t: (0, h)),
            pl.BlockSpec((1, kw), lambda bb, h, t: (0, h)),
            pl.BlockSpec((1, GLA_DV), lambda bb, h, t: (0, 0)),
        ],
        out_specs=(pl.BlockSpec((tb, vw), lambda bb, h, t: (rowblk(bb, t), h)),
                   pl.BlockSpec((1, GLA_HPS, GLA_DK, GLA_DV), lambda bb, h, t: (bb, h, 0, 0))),
        scratch_shapes=[
            pltpu.VMEM((GLA_HPS, GLA_DV, GLA_DK), F32),
            pltpu.VMEM((GLA_HPS, GLA_C, GLA_C), F32),
            pltpu.VMEM((GLA_HPS, GLA_C, GLA_DK), F32),
            pltpu.VMEM((GLA_HPS, GLA_C, GLA_DK), F32),
            pltpu.VMEM((tb, vw), F32),
        ],
        compiler_params=_params(("parallel", "parallel", "arbitrary")),
        name="gla_prompt",
    )(z, z, z, z, z, w2p, gbias, norm_w.reshape(1, GLA_DV))


def _merge_body(oa_ref, og_ref, ga_ref, gg_ref, wa_ref, wg_ref, o_ref):
    pa = jnp.dot(oa_ref[...], wa_ref[...], preferred_element_type=F32)
    pg = jnp.dot(og_ref[...], wg_ref[...], preferred_element_type=F32)
    o_ref[...] = (jax.nn.sigmoid(ga_ref[...]) * pa + jax.nn.sigmoid(gg_ref[...]) * pg).astype(BF16)


def _merge(o_attn, o_gla, z, wa, wg, tm):
    rows = o_attn.shape[0]
    d = wa.shape[1]
    return pl.pallas_call(
        _merge_body,
        out_shape=jax.ShapeDtypeStruct((rows, d), BF16),
        grid=(rows // tm,),
        in_specs=[
            pl.BlockSpec((tm, ATTN_WIDTH), lambda i: (i, 0)),
            pl.BlockSpec((tm, GLA_VAL_WIDTH), lambda i: (i, 0)),
            pl.BlockSpec((tm, d), lambda i: (i, C_GA // d)),
            pl.BlockSpec((tm, d), lambda i: (i, C_GA // d + 1)),
            pl.BlockSpec((ATTN_WIDTH, d), lambda i: (0, 0)),
            pl.BlockSpec((GLA_VAL_WIDTH, d), lambda i: (0, 0)),
        ],
        out_specs=pl.BlockSpec((tm, d), lambda i: (i, 0)),
        compiler_params=_params(("parallel",)),
        name="merge",
    )(o_attn, o_gla, z, z, wa, wg)


def _out_proj_body(m_ref, h_ref, w_ref, pw_ref, o_ref):
    y = jnp.dot(m_ref[...], w_ref[...], preferred_element_type=F32)
    o_ref[...] = h_ref[...] + _rms(y, pw_ref[...])


def _out_proj(merged, h, w_out, post_w, tm):
    rows, d = h.shape
    return pl.pallas_call(
        _out_proj_body,
        out_shape=jax.ShapeDtypeStruct((rows, d), F32),
        grid=(rows // tm,),
        in_specs=[
            pl.BlockSpec((tm, d), lambda i: (i, 0)),
            pl.BlockSpec((tm, d), lambda i: (i, 0)),
            pl.BlockSpec((d, d), lambda i: (0, 0)),
            pl.BlockSpec((1, d), lambda i: (0, 0)),
        ],
        out_specs=pl.BlockSpec((tm, d), lambda i: (i, 0)),
        compiler_params=_params(("parallel",)),
        name="out_proj",
    )(merged, h, w_out, post_w.reshape(1, d))


def _page_copies(pt_ref, hbm, buf, sem, seq, slot, n_pages, dst):
    return [pltpu.make_async_copy(hbm.at[pt_ref[seq, p]], buf.at[slot].at[dst(p)], sem)
            for p in range(n_pages)]


def _gather_step(fetch):
    b = pl.program_id(0)
    slot = b % 2

    @pl.when(b == 0)
    def _():
        for cp in fetch(b, slot):
            cp.start()

    @pl.when(b + 1 < pl.num_programs(0))
    def _():
        for cp in fetch(b + 1, 1 - slot):
            cp.start()

    for cp in fetch(b, slot):
        cp.wait()
    return slot


def _sample_scores_body(pt_ref, qi_ref, wi_ref, kidx_hbm, o_ref, kbuf, sem, *, n_pages, page):
    def fetch(seq, slot):
        return _page_copies(pt_ref, kidx_hbm, kbuf, sem.at[slot], seq, slot, n_pages,
                            lambda p: (slice(None), pl.ds(p * page, page)))

    slot = _gather_step(fetch)
    s = jnp.dot(qi_ref[0].astype(BF16), kbuf[slot].astype(BF16), preferred_element_type=F32)
    o_ref[0] = jnp.sum(jnp.maximum(s, 0.0) * wi_ref[0], axis=0, keepdims=True)


def _sample_scores(page_table, qi3, wi3, kidx_t):
    nd, n_pages = page_table.shape
    page = kidx_t.shape[2]
    past = n_pages * page
    body = functools.partial(_sample_scores_body, n_pages=n_pages, page=page)
    grid_spec = pltpu.PrefetchScalarGridSpec(
        num_scalar_prefetch=1,
        grid=(nd,),
        in_specs=[
            pl.BlockSpec((1, IDX_HEADS, IDX_DIM), lambda b, pt: (b, 0, 0)),
            pl.BlockSpec((1, IDX_HEADS, 1), lambda b, pt: (b, 0, 0)),
            pl.BlockSpec(memory_space=pl.ANY),
        ],
        out_specs=pl.BlockSpec((1, 1, past), lambda b, pt: (b, 0, 0)),
        scratch_shapes=[pltpu.VMEM((2, IDX_DIM, past), F32), pltpu.SemaphoreType.DMA((2,))],
    )
    return pl.pallas_call(
        body,
        out_shape=jax.ShapeDtypeStruct((nd, 1, past), F32),
        grid_spec=grid_spec,
        compiler_params=_params(("arbitrary",)),
        name="sample_scores",
    )(page_table, qi3, wi3, kidx_t)


def _sample_select_body(sp_ref, qi_ref, misc_ref, bp_ref, bs_ref, sc_scr, j_scr, *, past, tk, topk, idx_bits):
    rows = sp_ref.shape[0]
    misc = misc_ref[...]
    d_in = lax.broadcasted_iota(I32, (LANES, QI_WIDTH), 0)
    c_out = lax.broadcasted_iota(I32, (LANES, QI_WIDTH), 1)
    rep = jnp.where((d_in < IDX_DIM) & (c_out % IDX_DIM == d_in), 1.0, 0.0)
    ki_t = jnp.dot(misc, rep, precision=HIGHEST, preferred_element_type=F32)
    c_in = lax.broadcasted_iota(I32, (QI_WIDTH, LANES), 0)
    l_out = lax.broadcasted_iota(I32, (QI_WIDTH, LANES), 1)
    seg = jnp.where(l_out == MISC_WI + c_in // IDX_DIM, 1.0, 0.0)
    hd = jnp.dot(qi_ref[...] * ki_t, seg, precision=HIGHEST, preferred_element_type=F32)
    lane = lax.broadcasted_iota(I32, (rows, LANES), 1)
    is_wi = (lane >= MISC_WI) & (lane < MISC_GLR)
    s_self = jnp.sum(jnp.where(is_wi, jnp.maximum(hd, 0.0) * misc, 0.0), axis=1, keepdims=True)

    sc_scr[:, 0:past] = sp_ref[...]
    sc_scr[:, past:] = jnp.where(lax.broadcasted_iota(I32, (rows, tk), 1) == 0, s_self, -jnp.inf)
    nk = sc_scr.shape[1] // tk
    thr = _select_threshold(sc_scr, j_scr, nk, tk, topk, idx_bits, 1)
    jb = j_scr[...]
    kpos = lax.broadcasted_iota(I32, sc_scr.shape, 1)
    bias = jnp.where(_selected(sc_scr[...], kpos, thr, jb), 0.0, NEG)
    bp_ref[...] = bias[:, 0:past]
    bs_ref[...] = bias[:, past:past + LANES]


def _sample_select(s_past, qi2, misc, topk):
    nd, past = s_past.shape
    tk = LANES
    width = past + tk
    idx_bits = max(1, (width - 1).bit_length()) + 1
    body = functools.partial(_sample_select_body, past=past, tk=tk, topk=topk, idx_bits=idx_bits)
    return pl.pallas_call(
        body,
        out_shape=(jax.ShapeDtypeStruct((nd, past), F32), jax.ShapeDtypeStruct((nd, LANES), F32)),
        grid=(1,),
        in_specs=[
            pl.BlockSpec((nd, past), lambda i: (0, 0)),
            pl.BlockSpec((nd, QI_WIDTH), lambda i: (0, 0)),
            pl.BlockSpec((nd, LANES), lambda i: (0, 0)),
        ],
        out_specs=(pl.BlockSpec((nd, past), lambda i: (0, 0)), pl.BlockSpec((nd, LANES), lambda i: (0, 0))),
        scratch_shapes=[pltpu.VMEM((nd, width), F32), pltpu.VMEM((nd, 1), I32)],
        compiler_params=_params(("arbitrary",)),
        name="sample_select",
    )(s_past, qi2, misc)


SAMPLE_KCHUNK = 1024


def _sample_attn_body(pt_ref, q_ref, ks_ref, vs_ref, b2_ref, bs_ref, ck_hbm, cv_hbm, o_ref, kbuf, vbuf, sem,
                      *, n_pages, prows):
    def fetch(seq, slot):
        dst = lambda p: (pl.ds(p * prows, prows), slice(None))
        return (_page_copies(pt_ref, ck_hbm, kbuf, sem.at[0, slot], seq, slot, n_pages, dst)
                + _page_copies(pt_ref, cv_hbm, vbuf, sem.at[1, slot], seq, slot, n_pages, dst))

    slot = _gather_step(fetch)
    q = q_ref[0]
    qb = q.astype(BF16)
    total = n_pages * prows
    ch = min(SAMPLE_KCHUNK, total)
    head_grp = lax.broadcasted_iota(I32, (N_HEADS, ch), 0) // HEADS_PER_KV
    row_grp = lax.broadcasted_iota(I32, (N_HEADS, ch), 1) % N_KV_HEADS
    own = head_grp == row_grp
    s_chunks = []
    for c in range(total // ch):
        kc = kbuf[slot, c * ch:(c + 1) * ch, :].astype(BF16)
        s_chunks.append(jnp.where(own, _dot_nt(qb, kc) + b2_ref[0][:, c * ch:(c + 1) * ch], NEG))

    grp0 = lax.broadcasted_iota(I32, (N_HEADS, HEAD_DIM), 0) < HEADS_PER_KV
    k_self = jnp.where(grp0, ks_ref[0][:, 0:HEAD_DIM], ks_ref[0][:, HEAD_DIM:])
    v_self = jnp.where(grp0, vs_ref[0][:, 0:HEAD_DIM], vs_ref[0][:, HEAD_DIM:])
    s_self = jnp.sum(q * k_self, axis=1, keepdims=True) + bs_ref[0][:, 0:1]

    m = s_self
    for s in s_chunks:
        m = jnp.maximum(m, jnp.max(s, axis=1, keepdims=True))
    p_self = jnp.exp(s_self - m)
    l = p_self
    acc = p_self * v_self
    for c, s in enumerate(s_chunks):
        p = jnp.exp(s - m)
        l = l + jnp.sum(p, axis=1, keepdims=True)
        vc = vbuf[slot, c * ch:(c + 1) * ch, :].astype(BF16)
        acc = acc + jnp.dot(p.astype(BF16), vc, preferred_element_type=F32)
    o_ref[0] = (acc / l).astype(BF16)


def _sample_attn(page_table, q3, k_self, v_self, bias2, bias_self, ck2, cv2):
    nd, n_pages = page_table.shape
    prows = ck2.shape[1]
    total = n_pages * prows
    body = functools.partial(_sample_attn_body, n_pages=n_pages, prows=prows)
    grid_spec = pltpu.PrefetchScalarGridSpec(
        num_scalar_prefetch=1,
        grid=(nd,),
        in_specs=[
            pl.BlockSpec((1, N_HEADS, HEAD_DIM), lambda b, pt: (b, 0, 0)),
            pl.BlockSpec((1, 1, KV_WIDTH), lambda b, pt: (b, 0, 0)),
            pl.BlockSpec((1, 1, KV_WIDTH), lambda b, pt: (b, 0, 0)),
            pl.BlockSpec((1, 1, total), lambda b, pt: (b, 0, 0)),
            pl.BlockSpec((1, 1, LANES), lambda b, pt: (b, 0, 0)),
            pl.BlockSpec(memory_space=pl.ANY),
            pl.BlockSpec(memory_space=pl.ANY),
        ],
        out_specs=pl.BlockSpec((1, N_HEADS, HEAD_DIM), lambda b, pt: (b, 0, 0)),
        scratch_shapes=[
            pltpu.VMEM((2, total, HEAD_DIM), F32),
            pltpu.VMEM((2, total, HEAD_DIM), F32),
            pltpu.SemaphoreType.DMA((2, 2)),
        ],
    )
    return pl.pallas_call(
        body,
        out_shape=jax.ShapeDtypeStruct((nd, N_HEADS, HEAD_DIM), BF16),
        grid_spec=grid_spec,
        compiler_params=_params(("arbitrary",)),
        name="sample_attn",
    )(page_table, q3, k_self, v_self, bias2, bias_self, ck2, cv2)


def _gla_sample_body(gq_ref, gk_ref, gv_ref, misc_ref, gr_ref, s_ref, w2_ref, gb_ref, nw_ref, og_ref, so_ref):
    eye = jnp.where(lax.broadcasted_iota(I32, (GLA_DK, GLA_DK), 0)
                    == lax.broadcasted_iota(I32, (GLA_DK, GLA_DK), 1), 1.0, 0.0)

    def column(row):
        return jnp.sum(eye * row, axis=1, keepdims=True)

    for sq in range(s_ref.shape[0]):
        misc = misc_ref[sq]
        for h in range(GLA_HEADS):
            ksl = slice(h * GLA_DK, (h + 1) * GLA_DK)
            vsl = slice(h * GLA_DV, (h + 1) * GLA_DV)
            g = _log_decay(misc, w2_ref[:, ksl], gb_ref[:, ksl])
            s_new = column(jnp.exp(g)) * s_ref[sq, h] + column(gk_ref[sq][:, ksl]) * gv_ref[sq][:, vsl]
            so_ref[sq, h] = s_new
            o = jnp.sum(column(gq_ref[sq][:, ksl] * (GLA_DK ** -0.5)) * s_new, axis=0, keepdims=True)
            og_ref[sq, :, vsl] = _gla_out(o, nw_ref[...], gr_ref[sq][:, vsl])


def _gla_sample(gq, gk, gv, misc, gr, state, w2p, gbias, norm_w):
    nd = state.shape[0]
    sb = 4 if nd % 4 == 0 else 1

    def row3(w):
        return pl.BlockSpec((sb, 1, w), lambda b: (b, 0, 0))

    st_spec = pl.BlockSpec((sb, GLA_HEADS, GLA_DK, GLA_DV), lambda b: (b, 0, 0, 0))
    return pl.pallas_call(
        _gla_sample_body,
        out_shape=(jax.ShapeDtypeStruct((nd, 1, GLA_VAL_WIDTH), BF16),
                   jax.ShapeDtypeStruct(state.shape, F32)),
        grid=(nd // sb,),
        in_specs=[
            row3(GLA_KEY_WIDTH), row3(GLA_KEY_WIDTH), row3(GLA_VAL_WIDTH), row3(LANES), row3(GLA_VAL_WIDTH),
            st_spec,
            pl.BlockSpec((LANES, GLA_KEY_WIDTH), lambda b: (0, 0)),
            pl.BlockSpec((1, GLA_KEY_WIDTH), lambda b: (0, 0)),
            pl.BlockSpec((1, GLA_DV), lambda b: (0, 0)),
        ],
        out_specs=(row3(GLA_VAL_WIDTH), st_spec),
        compiler_params=_params(("parallel",)),
        name="gla_sample",
    )(gq, gk, gv, misc, gr, state, w2p, gbias, norm_w.reshape(1, GLA_DV))


def _rope_tables(pos, d):
    inv = ROPE_THETA ** (-jnp.arange(0, d, 2, dtype=F32) / d)
    ang = pos.astype(F32)[:, None] * inv[None, :]
    cos, sin = jnp.cos(ang), jnp.sin(ang)
    reps = LANES // d
    return (jnp.tile(jnp.concatenate([cos, cos], axis=-1), (1, reps)),
            jnp.tile(jnp.concatenate([-sin, sin], axis=-1), (1, reps)))


def _permute_w_in(w_in, d):
    sizes = (ATTN_WIDTH, KV_WIDTH, KV_WIDTH, QI_WIDTH, IDX_DIM, IDX_HEADS, GLA_KEY_WIDTH, GLA_KEY_WIDTH,
             GLA_VAL_WIDTH, GLA_GATE_RANK, GLA_VAL_WIDTH, d, d)
    pts, acc = [], 0
    for sz in sizes[:-1]:
        acc += sz
        pts.append(acc)
    q, k, v, qi, ki, wi, gq, gk, gv, glr, gr, ga, gg = jnp.split(w_in, pts, axis=1)
    pad = jnp.zeros((w_in.shape[0], C_GQ - (C_MISC + IDX_DIM + IDX_HEADS + GLA_GATE_RANK)), w_in.dtype)
    return jnp.concatenate([q, k, v, qi, ki, wi, glr, pad, gq, gk, gv, gr, ga, gg], axis=1).astype(BF16)


def kernel(x_prompt, x_sample, cache_k, cache_v, cache_kidx, page_table, state_gla,
           ffn1_pre_w, ffn1_w_gate, ffn1_w_up, ffn1_w_down, ffn1_post_w,
           mix_pre_w, w_in, gla_gate_w2, gla_gate_b, gla_norm_w,
           w_proj_attn, w_proj_gla, w_out, mix_post_w,
           ffn2_pre_w, ffn2_w_gate, ffn2_w_up, ffn2_w_down, ffn2_post_w):
    b, s, d = x_prompt.shape
    nd, td, _ = x_sample.shape
    n_pool, page = cache_k.shape[:2]
    n_pages = page_table.shape[1]
    past = n_pages * page
    assert td == 1 and s % Q_BLOCK == 0 and d % MIX_TN == 0 and C_GA % d == 0

    w_perm = _permute_w_in(w_in, d)
    wpa, wpg, wo = w_proj_attn.astype(BF16), w_proj_gla.astype(BF16), w_out.astype(BF16)
    w2p = jnp.zeros((LANES, GLA_KEY_WIDTH), F32).at[MISC_GLR:MISC_GLR + GLA_GATE_RANK].set(gla_gate_w2)
    gbias = gla_gate_b.reshape(1, GLA_KEY_WIDTH)

    def trunk(x, tm, tm_mix, tabs, tab_rows, mixer, f1, f2):
        h = _ffn(x, ffn1_pre_w, *f1, ffn1_post_w, tm)
        h, f1b = h if isinstance(h, tuple) else (h, f1)
        z = _mix_in(h, mix_pre_w, w_perm, tabs, tm_mix, tab_rows // tm_mix)
        o_attn, o_gla, s_fin = mixer(z)
        tmm = min(tm, 256)
        merged = _merge(o_attn, o_gla, z, wpa, wpg, tmm)
        h = _out_proj(merged, h, wo, mix_post_w, tmm)
        y = _ffn(h, ffn2_pre_w, *f2, ffn2_post_w, tm)
        y, f2b = y if isinstance(y, tuple) else (y, f2)
        return y, z, s_fin, f1b, f2b

    pos_p = jnp.arange(s, dtype=I32)
    tm_p = 512 if s % 512 == 0 else Q_BLOCK
    tabs_p = _rope_tables(pos_p, HEAD_DIM) + _rope_tables(pos_p, IDX_DIM)
    top_p = min(TOPK_MAX, s // 4)

    def mixer_p(z):
        o_attn = _attn_prompt(z, b, s, top_p)
        o_gla, s_fin = _gla_prompt(z, w2p, gbias, gla_norm_w, b, s)
        return o_attn, o_gla, s_fin

    tm_mix = 1024 if s % 1024 == 0 else tm_p

    pos_s = jnp.full((nd,), past, I32)
    tabs_s = _rope_tables(pos_s, HEAD_DIM) + _rope_tables(pos_s, IDX_DIM)
    top_s = min(TOPK_MAX, (past + td) // 4)
    ck2 = cache_k.reshape(n_pool, page * N_KV_HEADS, HEAD_DIM)
    cv2 = cache_v.reshape(n_pool, page * N_KV_HEADS, HEAD_DIM)
    kidx_t = jnp.swapaxes(cache_kidx, 1, 2)

    def mixer_s(z):
        misc = z[:, C_MISC:C_MISC + LANES]
        qi2 = z[:, C_QI:C_QI + QI_WIDTH]
        s_past = _sample_scores(page_table, qi2.reshape(nd, IDX_HEADS, IDX_DIM),
                                misc[:, MISC_WI:MISC_GLR].reshape(nd, IDX_HEADS, 1), kidx_t)
        bias_past, bias_self = _sample_select(s_past.reshape(nd, past), qi2, misc, top_s)
        bias2 = jnp.repeat(bias_past, N_KV_HEADS, axis=1).reshape(nd, 1, past * N_KV_HEADS)
        o_attn = _sample_attn(page_table, z[:, C_Q:C_Q + ATTN_WIDTH].reshape(nd, N_HEADS, HEAD_DIM),
                              z[:, C_K:C_K + KV_WIDTH].reshape(nd, 1, KV_WIDTH),
                              z[:, C_V:C_V + KV_WIDTH].reshape(nd, 1, KV_WIDTH),
                              bias2, bias_self.reshape(nd, 1, LANES), ck2, cv2)
        o_gla, s_fin = _gla_sample(z[:, C_GQ:C_GQ + GLA_KEY_WIDTH].reshape(nd, 1, GLA_KEY_WIDTH),
                                   z[:, C_GK:C_GK + GLA_KEY_WIDTH].reshape(nd, 1, GLA_KEY_WIDTH),
                                   z[:, C_GV:C_GV + GLA_VAL_WIDTH].reshape(nd, 1, GLA_VAL_WIDTH),
                                   misc.reshape(nd, 1, LANES),
                                   z[:, C_GR:C_GR + GLA_VAL_WIDTH].reshape(nd, 1, GLA_VAL_WIDTH),
                                   state_gla, w2p, gbias, gla_norm_w)
        return o_attn.reshape(nd, ATTN_WIDTH), o_gla.reshape(nd, GLA_VAL_WIDTH), s_fin

    y_s, z_s, gla_s, f1b, f2b = trunk(x_sample.reshape(nd, d), nd, nd, tabs_s, nd, mixer_s,
                                      (ffn1_w_gate, ffn1_w_up, ffn1_w_down), (ffn2_w_gate, ffn2_w_up, ffn2_w_down))
    y_p, z_p, gla_p, _, _ = trunk(x_prompt.reshape(b * s, d), tm_p, tm_mix, tabs_p, s, mixer_p, f1b, f2b)

    def kv_out(z, n, t):
        return (z[:, C_K:C_K + KV_WIDTH].reshape(n, t, N_KV_HEADS, HEAD_DIM),
                z[:, C_V:C_V + KV_WIDTH].reshape(n, t, N_KV_HEADS, HEAD_DIM),
                z[:, C_MISC:C_MISC + IDX_DIM].reshape(n, t, IDX_DIM))

    k_p, v_p, ki_p = kv_out(z_p, b, s)
    k_s, v_s, ki_s = kv_out(z_s, nd, td)
    return (y_p.reshape(b, s, d), y_s.reshape(nd, td, d), k_p, v_p, ki_p, gla_p, k_s, v_s, ki_s, gla_s)
```

```python
import functools

import jax
import jax.numpy as jnp
from jax import lax
from jax.experimental import pallas as pl
from jax.experimental.pallas import tpu as pltpu

F32, BF16, I32 = jnp.float32, jnp.bfloat16, jnp.int32
HIGHEST = lax.Precision.HIGHEST

N_HEADS = 8
N_KV_HEADS = 2
HEAD_DIM = 128
IDX_HEADS = 16
IDX_DIM = 64
TOPK_MAX = 256
Q_BLOCK = 128
ROPE_THETA = 10000.0
GLA_HEADS = 4
GLA_DK = 128
GLA_DV = 256
GLA_GATE_RANK = 16
GLA_GATE_TAU = 16.0
RMS_EPS = 1e-6

LANES = 128
ATTN_WIDTH = N_HEADS * HEAD_DIM
KV_WIDTH = N_KV_HEADS * HEAD_DIM
QI_WIDTH = IDX_HEADS * IDX_DIM
GLA_KEY_WIDTH = GLA_HEADS * GLA_DK
GLA_VAL_WIDTH = GLA_HEADS * GLA_DV
HEADS_PER_KV = N_HEADS // N_KV_HEADS

C_Q, C_K, C_V, C_QI, C_MISC = 0, 1024, 1280, 1536, 2560
C_GQ, C_GK, C_GV, C_GR, C_GA = 3072, 3584, 4096, 5120, 6144
MISC_WI, MISC_GLR = IDX_DIM, IDX_DIM + IDX_HEADS
MIX_TN = 512

NEG = -1e30
SOFTMAX_MIN_DENOM = 2.0 ** -60
INT_MIN = -2 ** 31
NEG_INF_KEY = -2139095041
VMEM_LIMIT = 56 * 1024 * 1024


def _params(sem, vmem=VMEM_LIMIT):
    return pltpu.CompilerParams(dimension_semantics=sem, vmem_limit_bytes=vmem)


def _rms(x, w):
    return x * lax.rsqrt(jnp.mean(x * x, axis=-1, keepdims=True) + RMS_EPS) * w


def _dot_nt(a, b):
    return lax.dot_general(a, b, (((1,), (1,)), ((), ())), preferred_element_type=F32)


def _ffn_body(x_ref, prew_ref, wg_ref, wu_ref, wd_ref, postw_ref, o_ref, *rest):
    *wb_refs, z_scr, acc_scr = rest
    j = pl.program_id(1)

    @pl.when(j == 0)
    def _():
        z_scr[...] = _rms(x_ref[...], prew_ref[...]).astype(BF16)
        acc_scr[...] = jnp.zeros_like(acc_scr)

    wg, wu, wd = wg_ref[...].astype(BF16), wu_ref[...].astype(BF16), wd_ref[...].astype(BF16)
    for ref, w in zip(wb_refs, (wg, wu, wd)):
        ref[...] = w
    z = z_scr[...]
    g = jnp.dot(z, wg, preferred_element_type=F32)
    u = jnp.dot(z, wu, preferred_element_type=F32)
    a = (g * jax.nn.sigmoid(g) * u).astype(BF16)
    acc_scr[...] += jnp.dot(a, wd, preferred_element_type=F32)

    @pl.when(j == pl.num_programs(1) - 1)
    def _():
        o_ref[...] = x_ref[...] + 0.5 * _rms(acc_scr[...], postw_ref[...])


def _ffn(x, pre_w, wg, wu, wd, post_w, tm):
    rows, d = x.shape
    dff = wg.shape[1]
    tf = 512 if dff % 512 == 0 else dff
    emit = wg.dtype == F32
    assert not emit or rows == tm
    w_specs = [
        pl.BlockSpec((d, tf), lambda i, j: (0, j)),
        pl.BlockSpec((d, tf), lambda i, j: (0, j)),
        pl.BlockSpec((tf, d), lambda i, j: (j, 0)),
    ]
    y_shape = jax.ShapeDtypeStruct((rows, d), F32)
    y_spec = pl.BlockSpec((tm, d), lambda i, j: (i, 0))
    out = pl.pallas_call(
        _ffn_body,
        out_shape=(y_shape, *(jax.ShapeDtypeStruct(w.shape, BF16) for w in (wg, wu, wd))) if emit else y_shape,
        grid=(rows // tm, dff // tf),
        in_specs=[
            pl.BlockSpec((tm, d), lambda i, j: (i, 0)),
            pl.BlockSpec((1, d), lambda i, j: (0, 0)),
            *w_specs,
            pl.BlockSpec((1, d), lambda i, j: (0, 0)),
        ],
        out_specs=(y_spec, *w_specs) if emit else y_spec,
        scratch_shapes=[pltpu.VMEM((tm, d), BF16), pltpu.VMEM((tm, d), F32)],
        compiler_params=_params(("parallel", "arbitrary")),
        name="ffn",
    )(x, pre_w.reshape(1, d), wg, wu, wd, post_w.reshape(1, d))
    return (out[0], tuple(out[1:])) if emit else out


def _mix_in_body(x_ref, prew_ref, w_ref, c128_ref, s128_ref, c64_ref, s64_ref, o_ref, u_scr):
    j = pl.program_id(1)

    @pl.when(j == 0)
    def _():
        u_scr[...] = _rms(x_ref[...], prew_ref[...]).astype(BF16)

    r = jnp.dot(u_scr[...], w_ref[...], preferred_element_type=F32)

    def rope128(x):
        return x * c128_ref[...] + pltpu.roll(x, HEAD_DIM // 2, 1) * s128_ref[...]

    def rope64(x):
        lane = lax.broadcasted_iota(I32, x.shape, 1)
        first = (lane % IDX_DIM) < (IDX_DIM // 2)
        rot = jnp.where(first, pltpu.roll(x, LANES - IDX_DIM // 2, 1), pltpu.roll(x, IDX_DIM // 2, 1))
        return x * c64_ref[...] + rot * s64_ref[...]

    def sl(t):
        return slice(t * LANES, (t + 1) * LANES)

    @pl.when(j < 2)
    def _():
        for t in range(4):
            o_ref[:, sl(t)] = rope128(r[:, sl(t)]) * (HEAD_DIM ** -0.5)

    @pl.when(j == 2)
    def _():
        for t in range(2):
            o_ref[:, sl(t)] = rope128(r[:, sl(t)])
        o_ref[:, 2 * LANES:] = r[:, 2 * LANES:]

    @pl.when((j == 3) | (j == 4))
    def _():
        for t in range(4):
            o_ref[:, sl(t)] = rope64(r[:, sl(t)])

    @pl.when(j == 5)
    def _():
        x = r[:, sl(0)]
        lane = lax.broadcasted_iota(I32, x.shape, 1)
        wi_scale = IDX_HEADS ** -0.5 * IDX_DIM ** -0.5
        y = jnp.where(lane < MISC_WI, rope64(x), jnp.where(lane < MISC_GLR, x * wi_scale, x))
        o_ref[:, sl(0)] = y
        o_ref[:, LANES:] = r[:, LANES:]

    @pl.when(j > 5)
    def _():
        o_ref[...] = r


def _mix_in(h, pre_w, w_perm, tabs, tm, tab_blocks):
    rows, d = h.shape
    zw = w_perm.shape[1]
    c128, s128, c64, s64 = tabs
    tab_spec = pl.BlockSpec((tm, LANES), lambda i, j: (i % tab_blocks, 0))
    return pl.pallas_call(
        _mix_in_body,
        out_shape=jax.ShapeDtypeStruct((rows, zw), F32),
        grid=(rows // tm, zw // MIX_TN),
        in_specs=[
            pl.BlockSpec((tm, d), lambda i, j: (i, 0)),
            pl.BlockSpec((1, d), lambda i, j: (0, 0)),
            pl.BlockSpec((d, MIX_TN), lambda i, j: (0, j)),
            tab_spec, tab_spec, tab_spec, tab_spec,
        ],
        out_specs=pl.BlockSpec((tm, MIX_TN), lambda i, j: (i, j)),
        scratch_shapes=[pltpu.VMEM((tm, d), BF16)],
        compiler_params=_params(("parallel", "arbitrary")),
        name="mix_in",
    )(h, pre_w.reshape(1, d), w_perm, c128, s128, c64, s64)


def _key_to_float(key):
    key = jnp.maximum(key, NEG_INF_KEY)
    return lax.bitcast_convert_type(key ^ ((key >> 31) & jnp.int32(0x7FFFFFFF)), F32)


SORT_N = 16


def _sort_pairs(n):
    pairs, p = [], 1
    while p < n:
        k = p
        while k >= 1:
            for j in range(k % p, n - k, 2 * k):
                for i in range(min(k, n - j - k)):
                    if (i + j) // (2 * p) == (i + j + k) // (2 * p):
                        pairs.append((i + j, i + j + k))
            k //= 2
        p *= 2
    return pairs


def _sort_groups_desc(x):
    v = [x[j * 8:(j + 1) * 8, :] for j in range(SORT_N)]
    for i, j in _sort_pairs(SORT_N):
        v[i], v[j] = jnp.maximum(v[i], v[j]), jnp.minimum(v[i], v[j])
    return jnp.concatenate(v, axis=0)


def _count_ge_sorted(v, t):
    c8 = v[7] >= t
    c4 = jnp.where(c8, v[11], v[3]) >= t
    c2 = jnp.where(c8, jnp.where(c4, v[13], v[9]), jnp.where(c4, v[5], v[1])) >= t
    e = [jnp.where(c2, v[4 * a + 2], v[4 * a]) for a in range(4)]
    c1 = jnp.where(c8, jnp.where(c4, e[3], e[2]), jnp.where(c4, e[1], e[0])) >= t
    low = (jnp.where(c8, 8.0, 0.0) + jnp.where(c4, 4.0, 0.0)) + (jnp.where(c2, 2.0, 0.0) + jnp.where(c1, 1.0, 0.0))
    return jnp.where(v[15] >= t, 16.0, low)


def _select_threshold(sc_ref, j_ref, nk, tk, topk, idx_bits, key_axis, srt_ref=None):
    n_other = sc_ref.shape[1 - key_axis]
    vec = (n_other, 1) if key_axis == 1 else (1, n_other)
    step = LANES if key_axis == 1 else 8

    def count_ge(t):
        if srt_ref is None:
            return count(lambda blk, ks: blk >= t)
        tb = jnp.broadcast_to(t, (8, n_other))

        def body(c, acc):
            ks = pl.multiple_of(c * tk, tk)
            for g0 in range(0, tk, SORT_N * 8):
                v = [srt_ref[pl.ds(ks + g0 + j * 8, 8), :] for j in range(SORT_N)]
                acc = acc + _count_ge_sorted(v, tb)
            return acc
        return jnp.sum(lax.fori_loop(0, nk, body, jnp.zeros((8, n_other), F32)), axis=0, keepdims=True)

    def count(pred):
        def body(c, acc):
            ks = pl.multiple_of(c * tk, tk)
            blk = sc_ref[:, pl.ds(ks, tk)] if key_axis == 1 else sc_ref[pl.ds(ks, tk), :]
            m = jnp.where(pred(blk, ks), 1.0, 0.0)
            parts = [lax.slice_in_dim(m, a, a + step, axis=key_axis) for a in range(0, tk, step)]
            while len(parts) > 1:
                parts = [parts[a] + parts[a + 1] for a in range(0, len(parts), 2)]
            return acc + parts[0]
        acc0 = jnp.zeros((n_other, step) if key_axis == 1 else (step, n_other), F32)
        return jnp.sum(lax.fori_loop(0, nk, body, acc0), axis=key_axis, keepdims=True)

    def bit_body(bi, t):
        cand = t ^ lax.shift_left(jnp.int32(1), 31 - bi)
        return jnp.where(count_ge(_key_to_float(cand)) >= topk, cand, t)

    thr_key = lax.fori_loop(0, 32, bit_body, jnp.full(vec, INT_MIN, I32))
    thr = _key_to_float(thr_key)

    cnt_ge = count(lambda blk, ks: blk >= thr)
    cnt_gt = count(lambda blk, ks: blk > thr)
    need = topk - cnt_gt
    tie = jnp.where((cnt_ge > topk) & (thr_key > NEG_INF_KEY), 1.0, 0.0)
    j_ref[...] = jnp.full(vec, 1 << idx_bits, I32)

    @pl.when(jnp.max(tie) > 0.0)
    def _():
        def jbit(bi, jb):
            cand = jb + lax.shift_left(jnp.int32(1), idx_bits - 1 - bi)

            def pred(blk, ks):
                kpos = ks + lax.broadcasted_iota(I32, blk.shape, key_axis)
                return (blk == thr) & (kpos < cand)
            return jnp.where(count(pred) <= need, cand, jb)
        j_ref[...] = lax.fori_loop(0, idx_bits, jbit, jnp.zeros(vec, I32))

    return thr


def _selected(score, kpos, thr, jb):
    return (score > thr) | ((score == thr) & (kpos < jb))


def _attn_prompt_body(q_ref, qia_ref, qib_ref, misc_ref, k_ref, v_ref, kim_ref, o_ref,
                      sc_scr, srt_scr, j_scr, bias_scr, kb_scr, vb_scr, kib_scr, kn_scr, qi_scr, wt_scr, qs_scr,
                      mx_scr, l_scr, acc_scr, *, tk, topk, idx_bits):
    i = pl.program_id(1)
    tq = Q_BLOCK
    nk = (i * tq + tq + tk - 1) // tk
    n_tiles = tk // LANES
    sub = 256
    qpos_t = i * tq + lax.broadcasted_iota(I32, (1, tq), 1)

    @pl.when(i == 0)
    def _():
        def cast_chunk(c, kn):
            rs = pl.ds(pl.multiple_of(c * tk, tk), tk)
            kf = k_ref[rs, :]
            kb_scr[rs, :] = kf.astype(BF16)
            vb_scr[rs, :] = v_ref[rs, :].astype(BF16)
            kib_scr[rs, :] = kim_ref[rs, 0:IDX_DIM].astype(BF16)
            sq = kf * kf
            return tuple(
                jnp.maximum(kn[g], jnp.max(jnp.sum(sq[:, g * HEAD_DIM:(g + 1) * HEAD_DIM], axis=1, keepdims=True),
                                           axis=0, keepdims=True))
                for g in range(N_KV_HEADS))
        kn = lax.fori_loop(0, k_ref.shape[0] // tk, cast_chunk, (jnp.zeros((1, 1), F32),) * N_KV_HEADS)
        for g in range(N_KV_HEADS):
            kn_scr[g] = jnp.broadcast_to(kn[g], kn_scr.shape[1:])

    for h in range(IDX_HEADS):
        ref = qia_ref if h < IDX_HEADS // 2 else qib_ref
        hh = h % (IDX_HEADS // 2)
        qi_scr[h // 2, (h % 2) * tq:(h % 2 + 1) * tq, :] = ref[:, hh * IDX_DIM:(hh + 1) * IDX_DIM].astype(BF16)
    wt_scr[...] = misc_ref[...].T

    def score_chunk(c, carry):
        for s0 in range(0, tk, sub):
            ks = pl.multiple_of(c * tk, tk) + s0
            ki = kib_scr[pl.ds(ks, sub), :]
            acc = jnp.zeros((sub, tq), F32)
            for hp in range(IDX_HEADS // 2):
                r = jnp.maximum(_dot_nt(ki, qi_scr[hp]), 0.0)
                acc = (acc + r[:, 0:tq] * wt_scr[pl.ds(MISC_WI + 2 * hp, 1), :]
                       + r[:, tq:] * wt_scr[pl.ds(MISC_WI + 2 * hp + 1, 1), :])
            kpos = ks + lax.broadcasted_iota(I32, (sub, tq), 0)
            masked = jnp.where(kpos <= qpos_t, acc, -jnp.inf)
            sc_scr[pl.ds(ks, sub), :] = masked
            for g0 in range(0, sub, SORT_N * 8):
                srt_scr[pl.ds(ks + g0, SORT_N * 8), :] = _sort_groups_desc(masked[g0:g0 + SORT_N * 8, :])
        return carry

    lax.fori_loop(0, nk, score_chunk, 0)
    thr = _select_threshold(sc_scr, j_scr, nk, tk, topk, idx_bits, 0, srt_scr)
    jb = j_scr[...]

    for g in range(N_KV_HEADS):
        kn = jnp.sqrt(kn_scr[g][0:1, :])
        for r in range(HEADS_PER_KV):
            hd = g * HEADS_PER_KV + r
            qh = q_ref[:, hd * HEAD_DIM:(hd + 1) * HEAD_DIM]
            qs_scr[g, r * tq:(r + 1) * tq, :] = qh.astype(BF16)
            mx_scr[g, r * tq:(r + 1) * tq, :] = jnp.sqrt(jnp.sum(qh * qh, axis=1, keepdims=True)) * kn * 1.02
    l_scr[...] = jnp.zeros(l_scr.shape, F32)
    acc_scr[...] = jnp.zeros(acc_scr.shape, F32)

    def logits(g, ks, bias4):
        kc = kb_scr[pl.ds(ks, tk), g * HEAD_DIM:(g + 1) * HEAD_DIM]
        return _dot_nt(qs_scr[g], kc) + bias4

    def exp_chunk(ks, bias):
        bias4 = jnp.concatenate([bias] * HEADS_PER_KV, axis=0)
        for g in range(N_KV_HEADS):
            s = logits(g, ks, bias4)
            m = mx_scr[g]
            p = [jnp.exp(s[:, t * LANES:(t + 1) * LANES] - m) for t in range(n_tiles)]
            l_scr[g] += functools.reduce(lambda x, y: x + y, p)
            pb = jnp.concatenate(p, axis=1).astype(BF16)
            vc = vb_scr[pl.ds(ks, tk), g * HEAD_DIM:(g + 1) * HEAD_DIM]
            acc_scr[g] += jnp.dot(pb, vc, preferred_element_type=F32)

    def bound_pass(c, carry):
        ks = pl.multiple_of(c * tk, tk)
        kpos = ks + lax.broadcasted_iota(I32, (tk, tq), 0)
        sel = _selected(sc_scr[pl.ds(ks, tk), :], kpos, thr, jb) & (kpos <= qpos_t)
        bias = jnp.where(sel, 0.0, NEG).T
        bias_scr[:, pl.ds(ks, tk)] = bias
        exp_chunk(ks, bias)
        return carry

    lax.fori_loop(0, nk, bound_pass, 0)
    l_min = functools.reduce(jnp.minimum, [jnp.min(jnp.sum(l_scr[g], axis=1, keepdims=True))
                                           for g in range(N_KV_HEADS)])

    @pl.when(l_min < SOFTMAX_MIN_DENOM)
    def _():
        mx_scr[...] = jnp.full(mx_scr.shape, NEG, F32)
        l_scr[...] = jnp.zeros(l_scr.shape, F32)
        acc_scr[...] = jnp.zeros(acc_scr.shape, F32)

        def max_pass(c, carry):
            ks = pl.multiple_of(c * tk, tk)
            bias4 = jnp.concatenate([bias_scr[:, pl.ds(ks, tk)]] * HEADS_PER_KV, axis=0)
            for g in range(N_KV_HEADS):
                s = logits(g, ks, bias4)
                m = mx_scr[g]
                for t in range(n_tiles):
                    m = jnp.maximum(m, s[:, t * LANES:(t + 1) * LANES])
                mx_scr[g] = m
            return carry

        lax.fori_loop(0, nk, max_pass, 0)
        for g in range(N_KV_HEADS):
            mx_scr[g] = jnp.broadcast_to(jnp.max(mx_scr[g], axis=1, keepdims=True), mx_scr.shape[1:])

        def exp_pass(c, carry):
            ks = pl.multiple_of(c * tk, tk)
            exp_chunk(ks, bias_scr[:, pl.ds(ks, tk)])
            return carry

        lax.fori_loop(0, nk, exp_pass, 0)

    for g in range(N_KV_HEADS):
        o = acc_scr[g] / jnp.sum(l_scr[g], axis=1, keepdims=True)
        for r in range(HEADS_PER_KV):
            hd = g * HEADS_PER_KV + r
            o_ref[:, hd * HEAD_DIM:(hd + 1) * HEAD_DIM] = o[r * tq:(r + 1) * tq, :].astype(BF16)


def _attn_prompt(z, b, s, topk):
    nq = s // Q_BLOCK
    tk = min(512, s)
    idx_bits = max(1, (s - 1).bit_length()) + 1
    body = functools.partial(_attn_prompt_body, tk=tk, topk=topk, idx_bits=idx_bits)
    rows = HEADS_PER_KV * Q_BLOCK
    return pl.pallas_call(
        body,
        out_shape=jax.ShapeDtypeStruct((b * s, ATTN_WIDTH), BF16),
        grid=(b, nq),
        in_specs=[
            pl.BlockSpec((Q_BLOCK, ATTN_WIDTH), lambda bb, i: (bb * nq + i, 0)),
            pl.BlockSpec((Q_BLOCK, QI_WIDTH // 2), lambda bb, i: (bb * nq + i, C_QI // (QI_WIDTH // 2))),
            pl.BlockSpec((Q_BLOCK, QI_WIDTH // 2), lambda bb, i: (bb * nq + i, C_QI // (QI_WIDTH // 2) + 1)),
            pl.BlockSpec((Q_BLOCK, LANES), lambda bb, i: (bb * nq + i, C_MISC // LANES)),
            pl.BlockSpec((s, KV_WIDTH), lambda bb, i: (bb, C_K // KV_WIDTH)),
            pl.BlockSpec((s, KV_WIDTH), lambda bb, i: (bb, C_V // KV_WIDTH)),
            pl.BlockSpec((s, LANES), lambda bb, i: (bb, C_MISC // LANES)),
        ],
        out_specs=pl.BlockSpec((Q_BLOCK, ATTN_WIDTH), lambda bb, i: (bb * nq + i, 0)),
        scratch_shapes=[
            pltpu.VMEM((s, Q_BLOCK), F32),
            pltpu.VMEM((s, Q_BLOCK), F32),
            pltpu.VMEM((1, Q_BLOCK), I32),
            pltpu.VMEM((Q_BLOCK, s), F32),
            pltpu.VMEM((s, KV_WIDTH), BF16),
            pltpu.VMEM((s, KV_WIDTH), BF16),
            pltpu.VMEM((s, IDX_DIM), BF16),
            pltpu.VMEM((N_KV_HEADS, 8, LANES), F32),
            pltpu.VMEM((IDX_HEADS // 2, 2 * Q_BLOCK, IDX_DIM), BF16),
            pltpu.VMEM((LANES, Q_BLOCK), F32),
            pltpu.VMEM((N_KV_HEADS, rows, HEAD_DIM), BF16),
            pltpu.VMEM((N_KV_HEADS, rows, LANES), F32),
            pltpu.VMEM((N_KV_HEADS, rows, LANES), F32),
            pltpu.VMEM((N_KV_HEADS, rows, HEAD_DIM), F32),
        ],
        compiler_params=_params(("parallel", "arbitrary")),
        name="attn_prompt",
    )(z, z, z, z, z, z, z)


def _log_decay(misc, w2, gb):
    x = jnp.dot(misc, w2, precision=HIGHEST, preferred_element_type=F32) + gb
    return (jnp.minimum(x, 0.0) - jnp.log1p(jnp.exp(-jnp.abs(x)))) * (1.0 / GLA_GATE_TAU)


def _gla_out(o, nw, gr):
    return (_rms(o, nw) * (gr * jax.nn.sigmoid(gr))).astype(BF16)


GLA_C = 128
GLA_SUB = 16
GLA_HPS = 4


def _gla_prompt_body(gq_ref, gk_ref, gv_ref, misc_ref, gr_ref, w2_ref, gb_ref, nw_ref,
                     og_ref, sfin_ref, st_scr, a_scr, b_scr, k_scr, o_scr, *, nchunk):
    t = pl.program_id(2)
    c_ = GLA_C

    @pl.when(t == 0)
    def _():
        st_scr[...] = jnp.zeros_like(st_scr)

    a_scr[...] = jnp.zeros_like(a_scr)
    row = lax.broadcasted_iota(I32, (c_, c_), 0)
    col = lax.broadcasted_iota(I32, (c_, c_), 1)
    tri = jnp.where(col <= row, 1.0, 0.0)
    sub_row = lax.broadcasted_iota(I32, (GLA_SUB, LANES), 0)
    sub_col = lax.broadcasted_iota(I32, (GLA_SUB, GLA_SUB), 1)

    def head_chunk(hh, r0):
        ksl = slice(hh * GLA_DK, (hh + 1) * GLA_DK)
        vsl = slice(hh * GLA_DV, (hh + 1) * GLA_DV)
        q = gq_ref[pl.ds(r0, c_), ksl] * (GLA_DK ** -0.5)
        k = gk_ref[pl.ds(r0, c_), ksl]
        v = gv_ref[pl.ds(r0, c_), vsl].astype(BF16)
        g = _log_decay(misc_ref[pl.ds(r0, c_), :], w2_ref[:, ksl], gb_ref[:, ksl])
        b = jnp.dot(tri, g, precision=HIGHEST, preferred_element_type=F32)
        b_scr[hh] = b
        k_scr[hh] = k
        st = st_scr[hh]
        o = _dot_nt((q * jnp.exp(b)).astype(BF16), st.astype(BF16))

        n = c_ // 2
        while n >= GLA_SUB:
            for rb in range(n, c_, 2 * n):
                bref = b[rb:rb + 1, :]
                qs = q[rb:rb + n, :] * jnp.exp(b[rb:rb + n, :] - bref)
                ks = k[rb - n:rb, :] * jnp.exp(bref - b[rb - n:rb, :])
                a_scr[hh, rb:rb + n, rb - n:rb] = _dot_nt(qs.astype(BF16), ks.astype(BF16))
            n //= 2
        for blk in range(c_ // GLA_SUB):
            lo = blk * GLA_SUB
            qb = q[lo:lo + GLA_SUB, :]
            bb = b[lo:lo + GLA_SUB, :]
            ad = jnp.zeros((GLA_SUB, GLA_SUB), F32)
            for jj in range(GLA_SUB):
                bj = b_scr[hh, pl.ds(lo + jj, 1), :]
                kj = k_scr[hh, pl.ds(lo + jj, 1), :]
                w = jnp.exp(jnp.where(sub_row >= jj, bb - bj, NEG))
                colj = jnp.sum(qb * kj * w, axis=1, keepdims=True)
                ad = jnp.where(sub_col == jj, colj, ad)
            a_scr[hh, lo:lo + GLA_SUB, lo:lo + GLA_SUB] = ad

        o = o + jnp.dot(a_scr[hh].astype(BF16), v, preferred_element_type=F32)
        o_scr[pl.ds(r0, c_), vsl] = o
        b_last = b[c_ - 1:c_, :]
        kd = (k * jnp.exp(b_last - b)).astype(BF16)
        kv = lax.dot_general(v, kd, (((0,), (0,)), ((), ())), preferred_element_type=F32)
        st_scr[hh] = st * jnp.exp(b_last) + kv

    def chunk(ci, carry):
        r0 = pl.multiple_of(ci * c_, c_)
        for hh in range(GLA_HPS):
            head_chunk(hh, r0)
        return carry

    lax.fori_loop(0, nchunk, chunk, 0)
    for hh in range(GLA_HPS):
        vsl = slice(hh * GLA_DV, (hh + 1) * GLA_DV)
        og_ref[:, vsl] = _gla_out(o_scr[:, vsl], nw_ref[...], gr_ref[:, vsl])

    @pl.when(t == pl.num_programs(2) - 1)
    def _():
        for hh in range(GLA_HPS):
            sfin_ref[0, hh] = st_scr[hh].T


def _gla_prompt(z, w2p, gbias, norm_w, b, s):
    tb = min(512, s)
    nt = s // tb
    body = functools.partial(_gla_prompt_body, nchunk=tb // GLA_C)

    def rowblk(bb, t):
        return bb * nt + t

    kw, vw = GLA_HPS * GLA_DK, GLA_HPS * GLA_DV
    return pl.pallas_call(
        body,
        out_shape=(jax.ShapeDtypeStruct((b * s, GLA_VAL_WIDTH), BF16),
                   jax.ShapeDtypeStruct((b, GLA_HEADS, GLA_DK, GLA_DV), F32)),
        grid=(b, GLA_HEADS // GLA_HPS, nt),
        in_specs=[
            pl.BlockSpec((tb, kw), lambda bb, h, t: (rowblk(bb, t), C_GQ // kw + h)),
            pl.BlockSpec((tb, kw), lambda bb, h, t: (rowblk(bb, t), C_GK // kw + h)),
            pl.BlockSpec((tb, vw), lambda bb, h, t: (rowblk(bb, t), C_GV // vw + h)),
            pl.BlockSpec((tb, LANES), lambda bb, h, t: (rowblk(bb, t), C_MISC // LANES)),
            pl.BlockSpec((tb, vw), lambda bb, h, t: (rowblk(bb, t), C_GR // vw + h)),
            pl.BlockSpec((LANES, kw), lambda bb, h, t: (0, h)),
            pl.BlockSpec((1, kw), lambda bb, h, t: (0, h)),
            pl.BlockSpec((1, GLA_DV), lambda bb, h, t: (0, 0)),
        ],
        out_specs=(pl.BlockSpec((tb, vw), lambda bb, h, t: (rowblk(bb, t), h)),
                   pl.BlockSpec((1, GLA_HPS, GLA_DK, GLA_DV), lambda bb, h, t: (bb, h, 0, 0))),
        scratch_shapes=[
            pltpu.VMEM((GLA_HPS, GLA_DV, GLA_DK), F32),
            pltpu.VMEM((GLA_HPS, GLA_C, GLA_C), F32),
            pltpu.VMEM((GLA_HPS, GLA_C, GLA_DK), F32),
            pltpu.VMEM((GLA_HPS, GLA_C, GLA_DK), F32),
            pltpu.VMEM((tb, vw), F32),
        ],
        compiler_params=_params(("parallel", "parallel", "arbitrary")),
        name="gla_prompt",
    )(z, z, z, z, z, w2p, gbias, norm_w.reshape(1, GLA_DV))


def _merge_body(oa_ref, og_ref, ga_ref, gg_ref, wa_ref, wg_ref, o_ref):
    pa = jnp.dot(oa_ref[...], wa_ref[...], preferred_element_type=F32)
    pg = jnp.dot(og_ref[...], wg_ref[...], preferred_element_type=F32)
    o_ref[...] = (jax.nn.sigmoid(ga_ref[...]) * pa + jax.nn.sigmoid(gg_ref[...]) * pg).astype(BF16)


def _merge(o_attn, o_gla, z, wa, wg, tm):
    rows = o_attn.shape[0]
    d = wa.shape[1]
    return pl.pallas_call(
        _merge_body,
        out_shape=jax.ShapeDtypeStruct((rows, d), BF16),
        grid=(rows // tm,),
        in_specs=[
            pl.BlockSpec((tm, ATTN_WIDTH), lambda i: (i, 0)),
            pl.BlockSpec((tm, GLA_VAL_WIDTH), lambda i: (i, 0)),
            pl.BlockSpec((tm, d), lambda i: (i, C_GA // d)),
            pl.BlockSpec((tm, d), lambda i: (i, C_GA // d + 1)),
            pl.BlockSpec((ATTN_WIDTH, d), lambda i: (0, 0)),
            pl.BlockSpec((GLA_VAL_WIDTH, d), lambda i: (0, 0)),
        ],
        out_specs=pl.BlockSpec((tm, d), lambda i: (i, 0)),
        compiler_params=_params(("parallel",)),
        name="merge",
    )(o_attn, o_gla, z, z, wa, wg)


def _out_proj_body(m_ref, h_ref, w_ref, pw_ref, o_ref):
    y = jnp.dot(m_ref[...], w_ref[...], preferred_element_type=F32)
    o_ref[...] = h_ref[...] + _rms(y, pw_ref[...])


def _out_proj(merged, h, w_out, post_w, tm):
    rows, d = h.shape
    return pl.pallas_call(
        _out_proj_body,
        out_shape=jax.ShapeDtypeStruct((rows, d), F32),
        grid=(rows // tm,),
        in_specs=[
            pl.BlockSpec((tm, d), lambda i: (i, 0)),
            pl.BlockSpec((tm, d), lambda i: (i, 0)),
            pl.BlockSpec((d, d), lambda i: (0, 0)),
            pl.BlockSpec((1, d), lambda i: (0, 0)),
        ],
        out_specs=pl.BlockSpec((tm, d), lambda i: (i, 0)),
        compiler_params=_params(("parallel",)),
        name="out_proj",
    )(merged, h, w_out, post_w.reshape(1, d))


def _page_copies(pt_ref, hbm, buf, sem, seq, slot, n_pages, dst):
    return [pltpu.make_async_copy(hbm.at[pt_ref[seq, p]], buf.at[slot].at[dst(p)], sem)
            for p in range(n_pages)]


def _gather_step(fetch):
    b = pl.program_id(0)
    slot = b % 2

    @pl.when(b == 0)
    def _():
        for cp in fetch(b, slot):
            cp.start()

    @pl.when(b + 1 < pl.num_programs(0))
    def _():
        for cp in fetch(b + 1, 1 - slot):
            cp.start()

    for cp in fetch(b, slot):
        cp.wait()
    return slot


def _sample_scores_body(pt_ref, qi_ref, wi_ref, kidx_hbm, o_ref, kbuf, sem, *, n_pages, page):
    def fetch(seq, slot):
        return _page_copies(pt_ref, kidx_hbm, kbuf, sem.at[slot], seq, slot, n_pages,
                            lambda p: (slice(None), pl.ds(p * page, page)))

    slot = _gather_step(fetch)
    s = jnp.dot(qi_ref[0].astype(BF16), kbuf[slot].astype(BF16), preferred_element_type=F32)
    o_ref[0] = jnp.sum(jnp.maximum(s, 0.0) * wi_ref[0], axis=0, keepdims=True)


def _sample_scores(page_table, qi3, wi3, kidx_t):
    nd, n_pages = page_table.shape
    page = kidx_t.shape[2]
    past = n_pages * page
    body = functools.partial(_sample_scores_body, n_pages=n_pages, page=page)
    grid_spec = pltpu.PrefetchScalarGridSpec(
        num_scalar_prefetch=1,
        grid=(nd,),
        in_specs=[
            pl.BlockSpec((1, IDX_HEADS, IDX_DIM), lambda b, pt: (b, 0, 0)),
            pl.BlockSpec((1, IDX_HEADS, 1), lambda b, pt: (b, 0, 0)),
            pl.BlockSpec(memory_space=pl.ANY),
        ],
        out_specs=pl.BlockSpec((1, 1, past), lambda b, pt: (b, 0, 0)),
        scratch_shapes=[pltpu.VMEM((2, IDX_DIM, past), F32), pltpu.SemaphoreType.DMA((2,))],
    )
    return pl.pallas_call(
        body,
        out_shape=jax.ShapeDtypeStruct((nd, 1, past), F32),
        grid_spec=grid_spec,
        compiler_params=_params(("arbitrary",)),
        name="sample_scores",
    )(page_table, qi3, wi3, kidx_t)


def _sample_select_body(sp_ref, qi_ref, misc_ref, bp_ref, bs_ref, sc_scr, j_scr, *, past, tk, topk, idx_bits):
    rows = sp_ref.shape[0]
    misc = misc_ref[...]
    d_in = lax.broadcasted_iota(I32, (LANES, QI_WIDTH), 0)
    c_out = lax.broadcasted_iota(I32, (LANES, QI_WIDTH), 1)
    rep = jnp.where((d_in < IDX_DIM) & (c_out % IDX_DIM == d_in), 1.0, 0.0)
    ki_t = jnp.dot(misc, rep, precision=HIGHEST, preferred_element_type=F32)
    c_in = lax.broadcasted_iota(I32, (QI_WIDTH, LANES), 0)
    l_out = lax.broadcasted_iota(I32, (QI_WIDTH, LANES), 1)
    seg = jnp.where(l_out == MISC_WI + c_in // IDX_DIM, 1.0, 0.0)
    hd = jnp.dot(qi_ref[...] * ki_t, seg, precision=HIGHEST, preferred_element_type=F32)
    lane = lax.broadcasted_iota(I32, (rows, LANES), 1)
    is_wi = (lane >= MISC_WI) & (lane < MISC_GLR)
    s_self = jnp.sum(jnp.where(is_wi, jnp.maximum(hd, 0.0) * misc, 0.0), axis=1, keepdims=True)

    sc_scr[:, 0:past] = sp_ref[...]
    sc_scr[:, past:] = jnp.where(lax.broadcasted_iota(I32, (rows, tk), 1) == 0, s_self, -jnp.inf)
    nk = sc_scr.shape[1] // tk
    thr = _select_threshold(sc_scr, j_scr, nk, tk, topk, idx_bits, 1)
    jb = j_scr[...]
    kpos = lax.broadcasted_iota(I32, sc_scr.shape, 1)
    bias = jnp.where(_selected(sc_scr[...], kpos, thr, jb), 0.0, NEG)
    bp_ref[...] = bias[:, 0:past]
    bs_ref[...] = bias[:, past:past + LANES]


def _sample_select(s_past, qi2, misc, topk):
    nd, past = s_past.shape
    tk = LANES
    width = past + tk
    idx_bits = max(1, (width - 1).bit_length()) + 1
    body = functools.partial(_sample_select_body, past=past, tk=tk, topk=topk, idx_bits=idx_bits)
    return pl.pallas_call(
        body,
        out_shape=(jax.ShapeDtypeStruct((nd, past), F32), jax.ShapeDtypeStruct((nd, LANES), F32)),
        grid=(1,),
        in_specs=[
            pl.BlockSpec((nd, past), lambda i: (0, 0)),
            pl.BlockSpec((nd, QI_WIDTH), lambda i: (0, 0)),
            pl.BlockSpec((nd, LANES), lambda i: (0, 0)),
        ],
        out_specs=(pl.BlockSpec((nd, past), lambda i: (0, 0)), pl.BlockSpec((nd, LANES), lambda i: (0, 0))),
        scratch_shapes=[pltpu.VMEM((nd, width), F32), pltpu.VMEM((nd, 1), I32)],
        compiler_params=_params(("arbitrary",)),
        name="sample_select",
    )(s_past, qi2, misc)


SAMPLE_KCHUNK = 1024


def _sample_attn_body(pt_ref, q_ref, ks_ref, vs_ref, b2_ref, bs_ref, ck_hbm, cv_hbm, o_ref, kbuf, vbuf, sem,
                      *, n_pages, prows):
    def fetch(seq, slot):
        dst = lambda p: (pl.ds(p * prows, prows), slice(None))
        return (_page_copies(pt_ref, ck_hbm, kbuf, sem.at[0, slot], seq, slot, n_pages, dst)
                + _page_copies(pt_ref, cv_hbm, vbuf, sem.at[1, slot], seq, slot, n_pages, dst))

    slot = _gather_step(fetch)
    q = q_ref[0]
    qb = q.astype(BF16)
    total = n_pages * prows
    ch = min(SAMPLE_KCHUNK, total)
    head_grp = lax.broadcasted_iota(I32, (N_HEADS, ch), 0) // HEADS_PER_KV
    row_grp = lax.broadcasted_iota(I32, (N_HEADS, ch), 1) % N_KV_HEADS
    own = head_grp == row_grp
    s_chunks = []
    for c in range(total // ch):
        kc = kbuf[slot, c * ch:(c + 1) * ch, :].astype(BF16)
        s_chunks.append(jnp.where(own, _dot_nt(qb, kc) + b2_ref[0][:, c * ch:(c + 1) * ch], NEG))

    grp0 = lax.broadcasted_iota(I32, (N_HEADS, HEAD_DIM), 0) < HEADS_PER_KV
    k_self = jnp.where(grp0, ks_ref[0][:, 0:HEAD_DIM], ks_ref[0][:, HEAD_DIM:])
    v_self = jnp.where(grp0, vs_ref[0][:, 0:HEAD_DIM], vs_ref[0][:, HEAD_DIM:])
    s_self = jnp.sum(q * k_self, axis=1, keepdims=True) + bs_ref[0][:, 0:1]

    m = s_self
    for s in s_chunks:
        m = jnp.maximum(m, jnp.max(s, axis=1, keepdims=True))
    p_self = jnp.exp(s_self - m)
    l = p_self
    acc = p_self * v_self
    for c, s in enumerate(s_chunks):
        p = jnp.exp(s - m)
        l = l + jnp.sum(p, axis=1, keepdims=True)
        vc = vbuf[slot, c * ch:(c + 1) * ch, :].astype(BF16)
        acc = acc + jnp.dot(p.astype(BF16), vc, preferred_element_type=F32)
    o_ref[0] = (acc / l).astype(BF16)


def _sample_attn(page_table, q3, k_self, v_self, bias2, bias_self, ck2, cv2):
    nd, n_pages = page_table.shape
    prows = ck2.shape[1]
    total = n_pages * prows
    body = functools.partial(_sample_attn_body, n_pages=n_pages, prows=prows)
    grid_spec = pltpu.PrefetchScalarGridSpec(
        num_scalar_prefetch=1,
        grid=(nd,),
        in_specs=[
            pl.BlockSpec((1, N_HEADS, HEAD_DIM), lambda b, pt: (b, 0, 0)),
            pl.BlockSpec((1, 1, KV_WIDTH), lambda b, pt: (b, 0, 0)),
            pl.BlockSpec((1, 1, KV_WIDTH), lambda b, pt: (b, 0, 0)),
            pl.BlockSpec((1, 1, total), lambda b, pt: (b, 0, 0)),
            pl.BlockSpec((1, 1, LANES), lambda b, pt: (b, 0, 0)),
            pl.BlockSpec(memory_space=pl.ANY),
            pl.BlockSpec(memory_space=pl.ANY),
        ],
        out_specs=pl.BlockSpec((1, N_HEADS, HEAD_DIM), lambda b, pt: (b, 0, 0)),
        scratch_shapes=[
            pltpu.VMEM((2, total, HEAD_DIM), F32),
            pltpu.VMEM((2, total, HEAD_DIM), F32),
            pltpu.SemaphoreType.DMA((2, 2)),
        ],
    )
    return pl.pallas_call(
        body,
        out_shape=jax.ShapeDtypeStruct((nd, N_HEADS, HEAD_DIM), BF16),
        grid_spec=grid_spec,
        compiler_params=_params(("arbitrary",)),
        name="sample_attn",
    )(page_table, q3, k_self, v_self, bias2, bias_self, ck2, cv2)


def _gla_sample_body(gq_ref, gk_ref, gv_ref, misc_ref, gr_ref, s_ref, w2_ref, gb_ref, nw_ref, og_ref, so_ref):
    eye = jnp.where(lax.broadcasted_iota(I32, (GLA_DK, GLA_DK), 0)
                    == lax.broadcasted_iota(I32, (GLA_DK, GLA_DK), 1), 1.0, 0.0)

    def column(row):
        return jnp.sum(eye * row, axis=1, keepdims=True)

    for sq in range(s_ref.shape[0]):
        misc = misc_ref[sq]
        for h in range(GLA_HEADS):
            ksl = slice(h * GLA_DK, (h + 1) * GLA_DK)
            vsl = slice(h * GLA_DV, (h + 1) * GLA_DV)
            g = _log_decay(misc, w2_ref[:, ksl], gb_ref[:, ksl])
            s_new = column(jnp.exp(g)) * s_ref[sq, h] + column(gk_ref[sq][:, ksl]) * gv_ref[sq][:, vsl]
            so_ref[sq, h] = s_new
            o = jnp.sum(column(gq_ref[sq][:, ksl] * (GLA_DK ** -0.5)) * s_new, axis=0, keepdims=True)
            og_ref[sq, :, vsl] = _gla_out(o, nw_ref[...], gr_ref[sq][:, vsl])


def _gla_sample(gq, gk, gv, misc, gr, state, w2p, gbias, norm_w):
    nd = state.shape[0]
    sb = 4 if nd % 4 == 0 else 1

    def row3(w):
        return pl.BlockSpec((sb, 1, w), lambda b: (b, 0, 0))

    st_spec = pl.BlockSpec((sb, GLA_HEADS, GLA_DK, GLA_DV), lambda b: (b, 0, 0, 0))
    return pl.pallas_call(
        _gla_sample_body,
        out_shape=(jax.ShapeDtypeStruct((nd, 1, GLA_VAL_WIDTH), BF16),
                   jax.ShapeDtypeStruct(state.shape, F32)),
        grid=(nd // sb,),
        in_specs=[
            row3(GLA_KEY_WIDTH), row3(GLA_KEY_WIDTH), row3(GLA_VAL_WIDTH), row3(LANES), row3(GLA_VAL_WIDTH),
            st_spec,
            pl.BlockSpec((LANES, GLA_KEY_WIDTH), lambda b: (0, 0)),
            pl.BlockSpec((1, GLA_KEY_WIDTH), lambda b: (0, 0)),
            pl.BlockSpec((1, GLA_DV), lambda b: (0, 0)),
        ],
        out_specs=(row3(GLA_VAL_WIDTH), st_spec),
        compiler_params=_params(("parallel",)),
        name="gla_sample",
    )(gq, gk, gv, misc, gr, state, w2p, gbias, norm_w.reshape(1, GLA_DV))


def _rope_tables(pos, d):
    inv = ROPE_THETA ** (-jnp.arange(0, d, 2, dtype=F32) / d)
    ang = pos.astype(F32)[:, None] * inv[None, :]
    cos, sin = jnp.cos(ang), jnp.sin(ang)
    reps = LANES // d
    return (jnp.tile(jnp.concatenate([cos, cos], axis=-1), (1, reps)),
            jnp.tile(jnp.concatenate([-sin, sin], axis=-1), (1, reps)))


def _permute_w_in(w_in, d):
    sizes = (ATTN_WIDTH, KV_WIDTH, KV_WIDTH, QI_WIDTH, IDX_DIM, IDX_HEADS, GLA_KEY_WIDTH, GLA_KEY_WIDTH,
             GLA_VAL_WIDTH, GLA_GATE_RANK, GLA_VAL_WIDTH, d, d)
    pts, acc = [], 0
    for sz in sizes[:-1]:
        acc += sz
        pts.append(acc)
    q, k, v, qi, ki, wi, gq, gk, gv, glr, gr, ga, gg = jnp.split(w_in, pts, axis=1)
    pad = jnp.zeros((w_in.shape[0], C_GQ - (C_MISC + IDX_DIM + IDX_HEADS + GLA_GATE_RANK)), w_in.dtype)
    return jnp.concatenate([q, k, v, qi, ki, wi, glr, pad, gq, gk, gv, gr, ga, gg], axis=1).astype(BF16)


def kernel(x_prompt, x_sample, cache_k, cache_v, cache_kidx, page_table, state_gla,
           ffn1_pre_w, ffn1_w_gate, ffn1_w_up, ffn1_w_down, ffn1_post_w,
           mix_pre_w, w_in, gla_gate_w2, gla_gate_b, gla_norm_w,
           w_proj_attn, w_proj_gla, w_out, mix_post_w,
           ffn2_pre_w, ffn2_w_gate, ffn2_w_up, ffn2_w_down, ffn2_post_w):
    b, s, d = x_prompt.shape
    nd, td, _ = x_sample.shape
    n_pool, page = cache_k.shape[:2]
    n_pages = page_table.shape[1]
    past = n_pages * page
    assert td == 1 and s % Q_BLOCK == 0 and d % MIX_TN == 0 and C_GA % d == 0

    w_perm = _permute_w_in(w_in, d)
    wpa, wpg, wo = w_proj_attn.astype(BF16), w_proj_gla.astype(BF16), w_out.astype(BF16)
    w2p = jnp.zeros((LANES, GLA_KEY_WIDTH), F32).at[MISC_GLR:MISC_GLR + GLA_GATE_RANK].set(gla_gate_w2)
    gbias = gla_gate_b.reshape(1, GLA_KEY_WIDTH)

    def trunk(x, tm, tm_mix, tabs, tab_rows, mixer, f1, f2):
        h = _ffn(x, ffn1_pre_w, *f1, ffn1_post_w, tm)
        h, f1b = h if isinstance(h, tuple) else (h, f1)
        z = _mix_in(h, mix_pre_w, w_perm, tabs, tm_mix, tab_rows // tm_mix)
        o_attn, o_gla, s_fin = mixer(z)
        tmm = min(tm, 256)
        merged = _merge(o_attn, o_gla, z, wpa, wpg, tmm)
        h = _out_proj(merged, h, wo, mix_post_w, tmm)
        y = _ffn(h, ffn2_pre_w, *f2, ffn2_post_w, tm)
        y, f2b = y if isinstance(y, tuple) else (y, f2)
        return y, z, s_fin, f1b, f2b

    pos_p = jnp.arange(s, dtype=I32)
    tm_p = 512 if s % 512 == 0 else Q_BLOCK
    tabs_p = _rope_tables(pos_p, HEAD_DIM) + _rope_tables(pos_p, IDX_DIM)
    top_p = min(TOPK_MAX, s // 4)

    def mixer_p(z):
        o_attn = _attn_prompt(z, b, s, top_p)
        o_gla, s_fin = _gla_prompt(z, w2p, gbias, gla_norm_w, b, s)
        return o_attn, o_gla, s_fin

    tm_mix = 1024 if s % 1024 == 0 else tm_p

    pos_s = jnp.full((nd,), past, I32)
    tabs_s = _rope_tables(pos_s, HEAD_DIM) + _rope_tables(pos_s, IDX_DIM)
    top_s = min(TOPK_MAX, (past + td) // 4)
    ck2 = cache_k.reshape(n_pool, page * N_KV_HEADS, HEAD_DIM)
    cv2 = cache_v.reshape(n_pool, page * N_KV_HEADS, HEAD_DIM)
    kidx_t = jnp.swapaxes(cache_kidx, 1, 2)

    def mixer_s(z):
        misc = z[:, C_MISC:C_MISC + LANES]
        qi2 = z[:, C_QI:C_QI + QI_WIDTH]
        s_past = _sample_scores(page_table, qi2.reshape(nd, IDX_HEADS, IDX_DIM),
                                misc[:, MISC_WI:MISC_GLR].reshape(nd, IDX_HEADS, 1), kidx_t)
        bias_past, bias_self = _sample_select(s_past.reshape(nd, past), qi2, misc, top_s)
        bias2 = jnp.repeat(bias_past, N_KV_HEADS, axis=1).reshape(nd, 1, past * N_KV_HEADS)
        o_attn = _sample_attn(page_table, z[:, C_Q:C_Q + ATTN_WIDTH].reshape(nd, N_HEADS, HEAD_DIM),
                              z[:, C_K:C_K + KV_WIDTH].reshape(nd, 1, KV_WIDTH),
                              z[:, C_V:C_V + KV_WIDTH].reshape(nd, 1, KV_WIDTH),
                              bias2, bias_self.reshape(nd, 1, LANES), ck2, cv2)
        o_gla, s_fin = _gla_sample(z[:, C_GQ:C_GQ + GLA_KEY_WIDTH].reshape(nd, 1, GLA_KEY_WIDTH),
                                   z[:, C_GK:C_GK + GLA_KEY_WIDTH].reshape(nd, 1, GLA_KEY_WIDTH),
                                   z[:, C_GV:C_GV + GLA_VAL_WIDTH].reshape(nd, 1, GLA_VAL_WIDTH),
                                   misc.reshape(nd, 1, LANES),
                                   z[:, C_GR:C_GR + GLA_VAL_WIDTH].reshape(nd, 1, GLA_VAL_WIDTH),
                                   state_gla, w2p, gbias, gla_norm_w)
        return o_attn.reshape(nd, ATTN_WIDTH), o_gla.reshape(nd, GLA_VAL_WIDTH), s_fin

    y_s, z_s, gla_s, f1b, f2b = trunk(x_sample.reshape(nd, d), nd, nd, tabs_s, nd, mixer_s,
                                      (ffn1_w_gate, ffn1_w_up, ffn1_w_down), (ffn2_w_gate, ffn2_w_up, ffn2_w_down))
    y_p, z_p, gla_p, _, _ = trunk(x_prompt.reshape(b * s, d), tm_p, tm_mix, tabs_p, s, mixer_p, f1b, f2b)

    def kv_out(z, n, t):
        return (z[:, C_K:C_K + KV_WIDTH].reshape(n, t, N_KV_HEADS, HEAD_DIM),
                z[:, C_V:C_V + KV_WIDTH].reshape(n, t, N_KV_HEADS, HEAD_DIM),
                z[:, C_MISC:C_MISC + IDX_DIM].reshape(n, t, IDX_DIM))

    k_p, v_p, ki_p = kv_out(z_p, b, s)
    k_s, v_s, ki_s = kv_out(z_s, nd, td)
    return (y_p.reshape(b, s, d), y_s.reshape(nd, td, d), k_p, v_p, ki_p, gla_p, k_s, v_s, ki_s, gla_s)
```

```python
import functools

import jax
import jax.numpy as jnp
from jax import lax
from jax.experimental import pallas as pl
from jax.experimental.pallas import tpu as pltpu

F32, BF16, I32 = jnp.float32, jnp.bfloat16, jnp.int32
HIGHEST = lax.Precision.HIGHEST

N_HEADS = 8
N_KV_HEADS = 2
HEAD_DIM = 128
IDX_HEADS = 16
IDX_DIM = 64
TOPK_MAX = 256
Q_BLOCK = 128
ROPE_THETA = 10000.0
GLA_HEADS = 4
GLA_DK = 128
GLA_DV = 256
GLA_GATE_RANK = 16
GLA_GATE_TAU = 16.0
RMS_EPS = 1e-6

LANES = 128
ATTN_WIDTH = N_HEADS * HEAD_DIM
KV_WIDTH = N_KV_HEADS * HEAD_DIM
QI_WIDTH = IDX_HEADS * IDX_DIM
GLA_KEY_WIDTH = GLA_HEADS * GLA_DK
GLA_VAL_WIDTH = GLA_HEADS * GLA_DV
HEADS_PER_KV = N_HEADS // N_KV_HEADS

C_Q, C_K, C_V, C_QI, C_MISC = 0, 1024, 1280, 1536, 2560
C_GQ, C_GK, C_GV, C_GR, C_GA = 3072, 3584, 4096, 5120, 6144
MISC_WI, MISC_GLR = IDX_DIM, IDX_DIM + IDX_HEADS
MIX_TN = 512

NEG = -1e30
SOFTMAX_MIN_DENOM = 2.0 ** -60
INT_MIN = -2 ** 31
NEG_INF_KEY = -2139095041
VMEM_LIMIT = 56 * 1024 * 1024


def _params(sem, vmem=VMEM_LIMIT):
    return pltpu.CompilerParams(dimension_semantics=sem, vmem_limit_bytes=vmem)


def _rms(x, w):
    return x * lax.rsqrt(jnp.mean(x * x, axis=-1, keepdims=True) + RMS_EPS) * w


def _dot_nt(a, b):
    return lax.dot_general(a, b, (((1,), (1,)), ((), ())), preferred_element_type=F32)


def _ffn_body(x_ref, prew_ref, wg_ref, wu_ref, wd_ref, postw_ref, o_ref, *rest):
    *wb_refs, z_scr, acc_scr = rest
    j = pl.program_id(1)

    @pl.when(j == 0)
    def _():
        z_scr[...] = _rms(x_ref[...], prew_ref[...]).astype(BF16)
        acc_scr[...] = jnp.zeros_like(acc_scr)

    wg, wu, wd = wg_ref[...].astype(BF16), wu_ref[...].astype(BF16), wd_ref[...].astype(BF16)
    for ref, w in zip(wb_refs, (wg, wu, wd)):
        ref[...] = w
    z = z_scr[...]
    g = jnp.dot(z, wg, preferred_element_type=F32)
    u = jnp.dot(z, wu, preferred_element_type=F32)
    a = (g * jax.nn.sigmoid(g) * u).astype(BF16)
    acc_scr[...] += jnp.dot(a, wd, preferred_element_type=F32)

    @pl.when(j == pl.num_programs(1) - 1)
    def _():
        o_ref[...] = x_ref[...] + 0.5 * _rms(acc_scr[...], postw_ref[...])


def _ffn(x, pre_w, wg, wu, wd, post_w, tm):
    rows, d = x.shape
    dff = wg.shape[1]
    tf = 512 if dff % 512 == 0 else dff
    emit = wg.dtype == F32
    assert not emit or rows == tm
    w_specs = [
        pl.BlockSpec((d, tf), lambda i, j: (0, j)),
        pl.BlockSpec((d, tf), lambda i, j: (0, j)),
        pl.BlockSpec((tf, d), lambda i, j: (j, 0)),
    ]
    y_shape = jax.ShapeDtypeStruct((rows, d), F32)
    y_spec = pl.BlockSpec((tm, d), lambda i, j: (i, 0))
    out = pl.pallas_call(
        _ffn_body,
        out_shape=(y_shape, *(jax.ShapeDtypeStruct(w.shape, BF16) for w in (wg, wu, wd))) if emit else y_shape,
        grid=(rows // tm, dff // tf),
        in_specs=[
            pl.BlockSpec((tm, d), lambda i, j: (i, 0)),
            pl.BlockSpec((1, d), lambda i, j: (0, 0)),
            *w_specs,
            pl.BlockSpec((1, d), lambda i, j: (0, 0)),
        ],
        out_specs=(y_spec, *w_specs) if emit else y_spec,
        scratch_shapes=[pltpu.VMEM((tm, d), BF16), pltpu.VMEM((tm, d), F32)],
        compiler_params=_params(("parallel", "arbitrary")),
        name="ffn",
    )(x, pre_w.reshape(1, d), wg, wu, wd, post_w.reshape(1, d))
    return (out[0], tuple(out[1:])) if emit else out


def _mix_in_body(x_ref, prew_ref, w_ref, c128_ref, s128_ref, c64_ref, s64_ref, o_ref, u_scr):
    j = pl.program_id(1)

    @pl.when(j == 0)
    def _():
        u_scr[...] = _rms(x_ref[...], prew_ref[...]).astype(BF16)

    o_ref[...] = jnp.dot(u_scr[...], w_ref[...], preferred_element_type=F32)

    def rope128(x):
        return x * c128_ref[...] + pltpu.roll(x, HEAD_DIM // 2, 1) * s128_ref[...]

    def rope64(x):
        lane = lax.broadcasted_iota(I32, x.shape, 1)
        first = (lane % IDX_DIM) < (IDX_DIM // 2)
        rot = jnp.where(first, pltpu.roll(x, LANES - IDX_DIM // 2, 1), pltpu.roll(x, IDX_DIM // 2, 1))
        return x * c64_ref[...] + rot * s64_ref[...]

    def sl(t):
        return slice(t * LANES, (t + 1) * LANES)

    @pl.when(j < 2)
    def _():
        for t in range(4):
            o_ref[:, sl(t)] = rope128(o_ref[:, sl(t)]) * (HEAD_DIM ** -0.5)

    @pl.when(j == 2)
    def _():
        for t in range(2):
            o_ref[:, sl(t)] = rope128(o_ref[:, sl(t)])

    @pl.when((j == 3) | (j == 4))
    def _():
        for t in range(4):
            o_ref[:, sl(t)] = rope64(o_ref[:, sl(t)])

    @pl.when(j == 5)
    def _():
        x = o_ref[:, sl(0)]
        lane = lax.broadcasted_iota(I32, x.shape, 1)
        wi_scale = IDX_HEADS ** -0.5 * IDX_DIM ** -0.5
        o_ref[:, sl(0)] = jnp.where(lane < MISC_WI, rope64(x), jnp.where(lane < MISC_GLR, x * wi_scale, x))


def _mix_in(h, pre_w, w_perm, tabs, tm, tab_blocks):
    rows, d = h.shape
    zw = w_perm.shape[1]
    c128, s128, c64, s64 = tabs
    tab_spec = pl.BlockSpec((tm, LANES), lambda i, j: (i % tab_blocks, 0))
    return pl.pallas_call(
        _mix_in_body,
        out_shape=jax.ShapeDtypeStruct((rows, zw), F32),
        grid=(rows // tm, zw // MIX_TN),
        in_specs=[
            pl.BlockSpec((tm, d), lambda i, j: (i, 0)),
            pl.BlockSpec((1, d), lambda i, j: (0, 0)),
            pl.BlockSpec((d, MIX_TN), lambda i, j: (0, j)),
            tab_spec, tab_spec, tab_spec, tab_spec,
        ],
        out_specs=pl.BlockSpec((tm, MIX_TN), lambda i, j: (i, j)),
        scratch_shapes=[pltpu.VMEM((tm, d), BF16)],
        compiler_params=_params(("parallel", "arbitrary")),
        name="mix_in",
    )(h, pre_w.reshape(1, d), w_perm, c128, s128, c64, s64)


def _key_to_float(key):
    key = jnp.maximum(key, NEG_INF_KEY)
    return lax.bitcast_convert_type(key ^ ((key >> 31) & jnp.int32(0x7FFFFFFF)), F32)


SORT_N = 16


def _sort_pairs(n):
    pairs, p = [], 1
    while p < n:
        k = p
        while k >= 1:
            for j in range(k % p, n - k, 2 * k):
                for i in range(min(k, n - j - k)):
                    if (i + j) // (2 * p) == (i + j + k) // (2 * p):
                        pairs.append((i + j, i + j + k))
            k //= 2
        p *= 2
    return pairs


def _sort_groups_desc(x):
    v = [x[j * 8:(j + 1) * 8, :] for j in range(SORT_N)]
    for i, j in _sort_pairs(SORT_N):
        v[i], v[j] = jnp.maximum(v[i], v[j]), jnp.minimum(v[i], v[j])
    return jnp.concatenate(v, axis=0)


def _count_ge_sorted(v, t):
    c8 = v[7] >= t
    c4 = jnp.where(c8, v[11], v[3]) >= t
    c2 = jnp.where(c8, jnp.where(c4, v[13], v[9]), jnp.where(c4, v[5], v[1])) >= t
    e = [jnp.where(c2, v[4 * a + 2], v[4 * a]) for a in range(4)]
    c1 = jnp.where(c8, jnp.where(c4, e[3], e[2]), jnp.where(c4, e[1], e[0])) >= t
    low = (jnp.where(c8, 8.0, 0.0) + jnp.where(c4, 4.0, 0.0)) + (jnp.where(c2, 2.0, 0.0) + jnp.where(c1, 1.0, 0.0))
    return jnp.where(v[15] >= t, 16.0, low)


def _select_threshold(sc_ref, j_ref, nk, tk, topk, idx_bits, key_axis, srt_ref=None):
    n_other = sc_ref.shape[1 - key_axis]
    vec = (n_other, 1) if key_axis == 1 else (1, n_other)
    step = LANES if key_axis == 1 else 8

    def count_ge(t):
        if srt_ref is None:
            return count(lambda blk, ks: blk >= t)
        tb = jnp.broadcast_to(t, (8, n_other))

        def body(c, acc):
            ks = pl.multiple_of(c * tk, tk)
            for g0 in range(0, tk, SORT_N * 8):
                v = [srt_ref[pl.ds(ks + g0 + j * 8, 8), :] for j in range(SORT_N)]
                acc = acc + _count_ge_sorted(v, tb)
            return acc
        return jnp.sum(lax.fori_loop(0, nk, body, jnp.zeros((8, n_other), F32)), axis=0, keepdims=True)

    def count(pred):
        def body(c, acc):
            ks = pl.multiple_of(c * tk, tk)
            blk = sc_ref[:, pl.ds(ks, tk)] if key_axis == 1 else sc_ref[pl.ds(ks, tk), :]
            m = jnp.where(pred(blk, ks), 1.0, 0.0)
            parts = [lax.slice_in_dim(m, a, a + step, axis=key_axis) for a in range(0, tk, step)]
            while len(parts) > 1:
                parts = [parts[a] + parts[a + 1] for a in range(0, len(parts), 2)]
            return acc + parts[0]
        acc0 = jnp.zeros((n_other, step) if key_axis == 1 else (step, n_other), F32)
        return jnp.sum(lax.fori_loop(0, nk, body, acc0), axis=key_axis, keepdims=True)

    def bit_body(bi, t):
        cand = t ^ lax.shift_left(jnp.int32(1), 31 - bi)
        return jnp.where(count_ge(_key_to_float(cand)) >= topk, cand, t)

    thr_key = lax.fori_loop(0, 32, bit_body, jnp.full(vec, INT_MIN, I32))
    thr = _key_to_float(thr_key)

    cnt_ge = count(lambda blk, ks: blk >= thr)
    cnt_gt = count(lambda blk, ks: blk > thr)
    need = topk - cnt_gt
    tie = jnp.where((cnt_ge > topk) & (thr_key > NEG_INF_KEY), 1.0, 0.0)
    j_ref[...] = jnp.full(vec, 1 << idx_bits, I32)

    @pl.when(jnp.max(tie) > 0.0)
    def _():
        def jbit(bi, jb):
            cand = jb + lax.shift_left(jnp.int32(1), idx_bits - 1 - bi)

            def pred(blk, ks):
                kpos = ks + lax.broadcasted_iota(I32, blk.shape, key_axis)
                return (blk == thr) & (kpos < cand)
            return jnp.where(count(pred) <= need, cand, jb)
        j_ref[...] = lax.fori_loop(0, idx_bits, jbit, jnp.zeros(vec, I32))

    return thr


def _selected(score, kpos, thr, jb):
    return (score > thr) | ((score == thr) & (kpos < jb))


def _attn_prompt_body(q_ref, qia_ref, qib_ref, misc_ref, k_ref, v_ref, kim_ref, o_ref,
                      sc_scr, srt_scr, j_scr, bias_scr, kb_scr, vb_scr, kib_scr, kn_scr, qi_scr, wt_scr, qs_scr,
                      mx_scr, l_scr, acc_scr, *, tk, topk, idx_bits):
    i = pl.program_id(1)
    tq = Q_BLOCK
    nk = (i * tq + tq + tk - 1) // tk
    n_tiles = tk // LANES
    sub = 256
    qpos_t = i * tq + lax.broadcasted_iota(I32, (1, tq), 1)

    @pl.when(i == 0)
    def _():
        def cast_chunk(c, kn):
            rs = pl.ds(pl.multiple_of(c * tk, tk), tk)
            kf = k_ref[rs, :]
            kb_scr[rs, :] = kf.astype(BF16)
            vb_scr[rs, :] = v_ref[rs, :].astype(BF16)
            kib_scr[rs, :] = kim_ref[rs, 0:IDX_DIM].astype(BF16)
            sq = kf * kf
            return tuple(
                jnp.maximum(kn[g], jnp.max(jnp.sum(sq[:, g * HEAD_DIM:(g + 1) * HEAD_DIM], axis=1, keepdims=True),
                                           axis=0, keepdims=True))
                for g in range(N_KV_HEADS))
        kn = lax.fori_loop(0, k_ref.shape[0] // tk, cast_chunk, (jnp.zeros((1, 1), F32),) * N_KV_HEADS)
        for g in range(N_KV_HEADS):
            kn_scr[g] = jnp.broadcast_to(kn[g], kn_scr.shape[1:])

    for h in range(IDX_HEADS):
        ref = qia_ref if h < IDX_HEADS // 2 else qib_ref
        hh = h % (IDX_HEADS // 2)
        qi_scr[h // 2, (h % 2) * tq:(h % 2 + 1) * tq, :] = ref[:, hh * IDX_DIM:(hh + 1) * IDX_DIM].astype(BF16)
    wt_scr[...] = misc_ref[...].T

    def score_chunk(c, carry):
        for s0 in range(0, tk, sub):
            ks = pl.multiple_of(c * tk, tk) + s0
            ki = kib_scr[pl.ds(ks, sub), :]
            acc = jnp.zeros((sub, tq), F32)
            for hp in range(IDX_HEADS // 2):
                r = jnp.maximum(_dot_nt(ki, qi_scr[hp]), 0.0)
                acc = (acc + r[:, 0:tq] * wt_scr[pl.ds(MISC_WI + 2 * hp, 1), :]
                       + r[:, tq:] * wt_scr[pl.ds(MISC_WI + 2 * hp + 1, 1), :])
            kpos = ks + lax.broadcasted_iota(I32, (sub, tq), 0)
            masked = jnp.where(kpos <= qpos_t, acc, -jnp.inf)
            sc_scr[pl.ds(ks, sub), :] = masked
            for g0 in range(0, sub, SORT_N * 8):
                srt_scr[pl.ds(ks + g0, SORT_N * 8), :] = _sort_groups_desc(masked[g0:g0 + SORT_N * 8, :])
        return carry

    lax.fori_loop(0, nk, score_chunk, 0)
    thr = _select_threshold(sc_scr, j_scr, nk, tk, topk, idx_bits, 0, srt_scr)
    jb = j_scr[...]

    for g in range(N_KV_HEADS):
        kn = jnp.sqrt(kn_scr[g][0:1, :])
        for r in range(HEADS_PER_KV):
            hd = g * HEADS_PER_KV + r
            qh = q_ref[:, hd * HEAD_DIM:(hd + 1) * HEAD_DIM]
            qs_scr[g, r * tq:(r + 1) * tq, :] = qh.astype(BF16)
            mx_scr[g, r * tq:(r + 1) * tq, :] = jnp.sqrt(jnp.sum(qh * qh, axis=1, keepdims=True)) * kn * 1.02
    l_scr[...] = jnp.zeros(l_scr.shape, F32)
    acc_scr[...] = jnp.zeros(acc_scr.shape, F32)

    def logits(g, ks, bias4):
        kc = kb_scr[pl.ds(ks, tk), g * HEAD_DIM:(g + 1) * HEAD_DIM]
        return _dot_nt(qs_scr[g], kc) + bias4

    def exp_chunk(ks, bias):
        bias4 = jnp.concatenate([bias] * HEADS_PER_KV, axis=0)
        for g in range(N_KV_HEADS):
            s = logits(g, ks, bias4)
            m = mx_scr[g]
            p = [jnp.exp(s[:, t * LANES:(t + 1) * LANES] - m) for t in range(n_tiles)]
            l_scr[g] += functools.reduce(lambda x, y: x + y, p)
            pb = jnp.concatenate(p, axis=1).astype(BF16)
            vc = vb_scr[pl.ds(ks, tk), g * HEAD_DIM:(g + 1) * HEAD_DIM]
            acc_scr[g] += jnp.dot(pb, vc, preferred_element_type=F32)

    def bound_pass(c, carry):
        ks = pl.multiple_of(c * tk, tk)
        kpos = ks + lax.broadcasted_iota(I32, (tk, tq), 0)
        sel = _selected(sc_scr[pl.ds(ks, tk), :], kpos, thr, jb) & (kpos <= qpos_t)
        bias = jnp.where(sel, 0.0, NEG).T
        bias_scr[:, pl.ds(ks, tk)] = bias
        exp_chunk(ks, bias)
        return carry

    lax.fori_loop(0, nk, bound_pass, 0)
    l_min = functools.reduce(jnp.minimum, [jnp.min(jnp.sum(l_scr[g], axis=1, keepdims=True))
                                           for g in range(N_KV_HEADS)])

    @pl.when(l_min < SOFTMAX_MIN_DENOM)
    def _():
        mx_scr[...] = jnp.full(mx_scr.shape, NEG, F32)
        l_scr[...] = jnp.zeros(l_scr.shape, F32)
        acc_scr[...] = jnp.zeros(acc_scr.shape, F32)

        def max_pass(c, carry):
            ks = pl.multiple_of(c * tk, tk)
            bias4 = jnp.concatenate([bias_scr[:, pl.ds(ks, tk)]] * HEADS_PER_KV, axis=0)
            for g in range(N_KV_HEADS):
                s = logits(g, ks, bias4)
                m = mx_scr[g]
                for t in range(n_tiles):
                    m = jnp.maximum(m, s[:, t * LANES:(t + 1) * LANES])
                mx_scr[g] = m
            return carry

        lax.fori_loop(0, nk, max_pass, 0)
        for g in range(N_KV_HEADS):
            mx_scr[g] = jnp.broadcast_to(jnp.max(mx_scr[g], axis=1, keepdims=True), mx_scr.shape[1:])

        def exp_pass(c, carry):
            ks = pl.multiple_of(c * tk, tk)
            exp_chunk(ks, bias_scr[:, pl.ds(ks, tk)])
            return carry

        lax.fori_loop(0, nk, exp_pass, 0)

    for g in range(N_KV_HEADS):
        o = acc_scr[g] / jnp.sum(l_scr[g], axis=1, keepdims=True)
        for r in range(HEADS_PER_KV):
            hd = g * HEADS_PER_KV + r
            o_ref[:, hd * HEAD_DIM:(hd + 1) * HEAD_DIM] = o[r * tq:(r + 1) * tq, :].astype(BF16)


def _attn_prompt(z, b, s, topk):
    nq = s // Q_BLOCK
    tk = min(512, s)
    idx_bits = max(1, (s - 1).bit_length()) + 1
    body = functools.partial(_attn_prompt_body, tk=tk, topk=topk, idx_bits=idx_bits)
    rows = HEADS_PER_KV * Q_BLOCK
    return pl.pallas_call(
        body,
        out_shape=jax.ShapeDtypeStruct((b * s, ATTN_WIDTH), BF16),
        grid=(b, nq),
        in_specs=[
            pl.BlockSpec((Q_BLOCK, ATTN_WIDTH), lambda bb, i: (bb * nq + i, 0)),
            pl.BlockSpec((Q_BLOCK, QI_WIDTH // 2), lambda bb, i: (bb * nq + i, C_QI // (QI_WIDTH // 2))),
            pl.BlockSpec((Q_BLOCK, QI_WIDTH // 2), lambda bb, i: (bb * nq + i, C_QI // (QI_WIDTH // 2) + 1)),
            pl.BlockSpec((Q_BLOCK, LANES), lambda bb, i: (bb * nq + i, C_MISC // LANES)),
            pl.BlockSpec((s, KV_WIDTH), lambda bb, i: (bb, C_K // KV_WIDTH)),
            pl.BlockSpec((s, KV_WIDTH), lambda bb, i: (bb, C_V // KV_WIDTH)),
            pl.BlockSpec((s, LANES), lambda bb, i: (bb, C_MISC // LANES)),
        ],
        out_specs=pl.BlockSpec((Q_BLOCK, ATTN_WIDTH), lambda bb, i: (bb * nq + i, 0)),
        scratch_shapes=[
            pltpu.VMEM((s, Q_BLOCK), F32),
            pltpu.VMEM((s, Q_BLOCK), F32),
            pltpu.VMEM((1, Q_BLOCK), I32),
            pltpu.VMEM((Q_BLOCK, s), F32),
            pltpu.VMEM((s, KV_WIDTH), BF16),
            pltpu.VMEM((s, KV_WIDTH), BF16),
            pltpu.VMEM((s, IDX_DIM), BF16),
            pltpu.VMEM((N_KV_HEADS, 8, LANES), F32),
            pltpu.VMEM((IDX_HEADS // 2, 2 * Q_BLOCK, IDX_DIM), BF16),
            pltpu.VMEM((LANES, Q_BLOCK), F32),
            pltpu.VMEM((N_KV_HEADS, rows, HEAD_DIM), BF16),
            pltpu.VMEM((N_KV_HEADS, rows, LANES), F32),
            pltpu.VMEM((N_KV_HEADS, rows, LANES), F32),
            pltpu.VMEM((N_KV_HEADS, rows, HEAD_DIM), F32),
        ],
        compiler_params=_params(("parallel", "arbitrary")),
        name="attn_prompt",
    )(z, z, z, z, z, z, z)


def _log_decay(misc, w2, gb):
    x = jnp.dot(misc, w2, precision=HIGHEST, preferred_element_type=F32) + gb
    return (jnp.minimum(x, 0.0) - jnp.log1p(jnp.exp(-jnp.abs(x)))) * (1.0 / GLA_GATE_TAU)


def _gla_out(o, nw, gr):
    return (_rms(o, nw) * (gr * jax.nn.sigmoid(gr))).astype(BF16)


GLA_C = 128
GLA_SUB = 8
GLA_HPS = 4


def _gla_prompt_body(gq_ref, gk_ref, gv_ref, misc_ref, gr_ref, w2_ref, gb_ref, nw_ref,
                     og_ref, sfin_ref, st_scr, a_scr, b_scr, k_scr, o_scr, *, nchunk):
    t = pl.program_id(2)
    c_ = GLA_C

    @pl.when(t == 0)
    def _():
        st_scr[...] = jnp.zeros_like(st_scr)

    a_scr[...] = jnp.zeros_like(a_scr)
    row = lax.broadcasted_iota(I32, (c_, c_), 0)
    col = lax.broadcasted_iota(I32, (c_, c_), 1)
    tri = jnp.where(col <= row, 1.0, 0.0)
    sub_row = lax.broadcasted_iota(I32, (GLA_SUB, LANES), 0)
    sub_col = lax.broadcasted_iota(I32, (GLA_SUB, GLA_SUB), 1)

    def head_chunk(hh, r0):
        ksl = slice(hh * GLA_DK, (hh + 1) * GLA_DK)
        vsl = slice(hh * GLA_DV, (hh + 1) * GLA_DV)
        q = gq_ref[pl.ds(r0, c_), ksl] * (GLA_DK ** -0.5)
        k = gk_ref[pl.ds(r0, c_), ksl]
        v = gv_ref[pl.ds(r0, c_), vsl].astype(BF16)
        g = _log_decay(misc_ref[pl.ds(r0, c_), :], w2_ref[:, ksl], gb_ref[:, ksl])
        b = jnp.dot(tri, g, precision=HIGHEST, preferred_element_type=F32)
        b_scr[hh] = b
        k_scr[hh] = k
        st = st_scr[hh]
        o = _dot_nt((q * jnp.exp(b)).astype(BF16), st.astype(BF16))

        n = c_ // 2
        while n >= GLA_SUB:
            for rb in range(n, c_, 2 * n):
                bref = b[rb:rb + 1, :]
                qs = q[rb:rb + n, :] * jnp.exp(b[rb:rb + n, :] - bref)
                ks = k[rb - n:rb, :] * jnp.exp(bref - b[rb - n:rb, :])
                a_scr[hh, rb:rb + n, rb - n:rb] = _dot_nt(qs.astype(BF16), ks.astype(BF16))
            n //= 2
        for blk in range(c_ // GLA_SUB):
            lo = blk * GLA_SUB
            qb = q[lo:lo + GLA_SUB, :]
            bb = b[lo:lo + GLA_SUB, :]
            ad = jnp.zeros((GLA_SUB, GLA_SUB), F32)
            for jj in range(GLA_SUB):
                bj = b_scr[hh, pl.ds(lo + jj, 1), :]
                kj = k_scr[hh, pl.ds(lo + jj, 1), :]
                w = jnp.exp(jnp.where(sub_row >= jj, bb - bj, NEG))
                colj = jnp.sum(qb * kj * w, axis=1, keepdims=True)
                ad = jnp.where(sub_col == jj, colj, ad)
            a_scr[hh, lo:lo + GLA_SUB, lo:lo + GLA_SUB] = ad

        o = o + jnp.dot(a_scr[hh].astype(BF16), v, preferred_element_type=F32)
        o_scr[pl.ds(r0, c_), vsl] = o
        b_last = b[c_ - 1:c_, :]
        kd = (k * jnp.exp(b_last - b)).astype(BF16)
        kv = lax.dot_general(v, kd, (((0,), (0,)), ((), ())), preferred_element_type=F32)
        st_scr[hh] = st * jnp.exp(b_last) + kv

    def chunk(ci, carry):
        r0 = pl.multiple_of(ci * c_, c_)
        for hh in range(GLA_HPS):
            head_chunk(hh, r0)
        return carry

    lax.fori_loop(0, nchunk, chunk, 0)
    for hh in range(GLA_HPS):
        vsl = slice(hh * GLA_DV, (hh + 1) * GLA_DV)
        og_ref[:, vsl] = _gla_out(o_scr[:, vsl], nw_ref[...], gr_ref[:, vsl])

    @pl.when(t == pl.num_programs(2) - 1)
    def _():
        for hh in range(GLA_HPS):
            sfin_ref[0, hh] = st_scr[hh].T


def _gla_prompt(z, w2p, gbias, norm_w, b, s):
    tb = min(512, s)
    nt = s // tb
    body = functools.partial(_gla_prompt_body, nchunk=tb // GLA_C)

    def rowblk(bb, t):
        return bb * nt + t

    kw, vw = GLA_HPS * GLA_DK, GLA_HPS * GLA_DV
    return pl.pallas_call(
        body,
        out_shape=(jax.ShapeDtypeStruct((b * s, GLA_VAL_WIDTH), BF16),
                   jax.ShapeDtypeStruct((b, GLA_HEADS, GLA_DK, GLA_DV), F32)),
        grid=(b, GLA_HEADS // GLA_HPS, nt),
        in_specs=[
            pl.BlockSpec((tb, kw), lambda bb, h, t: (rowblk(bb, t), C_GQ // kw + h)),
            pl.BlockSpec((tb, kw), lambda bb, h, t: (rowblk(bb, t), C_GK // kw + h)),
            pl.BlockSpec((tb, vw), lambda bb, h, t: (rowblk(bb, t), C_GV // vw + h)),
            pl.BlockSpec((tb, LANES), lambda bb, h, t: (rowblk(bb, t), C_MISC // LANES)),
            pl.BlockSpec((tb, vw), lambda bb, h, t: (rowblk(bb, t), C_GR // vw + h)),
            pl.BlockSpec((LANES, kw), lambda bb, h, t: (0, h)),
            pl.BlockSpec((1, kw), lambda bb, h, t: (0, h)),
            pl.BlockSpec((1, GLA_DV), lambda bb, h, t: (0, 0)),
        ],
        out_specs=(pl.BlockSpec((tb, vw), lambda bb, h, t: (rowblk(bb, t), h)),
                   pl.BlockSpec((1, GLA_HPS, GLA_DK, GLA_DV), lambda bb, h, t: (bb, h, 0, 0))),
        scratch_shapes=[
            pltpu.VMEM((GLA_HPS, GLA_DV, GLA_DK), F32),
            pltpu.VMEM((GLA_HPS, GLA_C, GLA_C), F32),
            pltpu.VMEM((GLA_HPS, GLA_C, GLA_DK), F32),
            pltpu.VMEM((GLA_HPS, GLA_C, GLA_DK), F32),
            pltpu.VMEM((tb, vw), F32),
        ],
        compiler_params=_params(("parallel", "parallel", "arbitrary")),
        name="gla_prompt",
    )(z, z, z, z, z, w2p, gbias, norm_w.reshape(1, GLA_DV))


def _merge_body(oa_ref, og_ref, ga_ref, gg_ref, wa_ref, wg_ref, o_ref):
    pa = jnp.dot(oa_ref[...], wa_ref[...], preferred_element_type=F32)
    pg = jnp.dot(og_ref[...], wg_ref[...], preferred_element_type=F32)
    o_ref[...] = (jax.nn.sigmoid(ga_ref[...]) * pa + jax.nn.sigmoid(gg_ref[...]) * pg).astype(BF16)


def _merge(o_attn, o_gla, z, wa, wg, tm):
    rows = o_attn.shape[0]
    d = wa.shape[1]
    return pl.pallas_call(
        _merge_body,
        out_shape=jax.ShapeDtypeStruct((rows, d), BF16),
        grid=(rows // tm,),
        in_specs=[
            pl.BlockSpec((tm, ATTN_WIDTH), lambda i: (i, 0)),
            pl.BlockSpec((tm, GLA_VAL_WIDTH), lambda i: (i, 0)),
            pl.BlockSpec((tm, d), lambda i: (i, C_GA // d)),
            pl.BlockSpec((tm, d), lambda i: (i, C_GA // d + 1)),
            pl.BlockSpec((ATTN_WIDTH, d), lambda i: (0, 0)),
            pl.BlockSpec((GLA_VAL_WIDTH, d), lambda i: (0, 0)),
        ],
        out_specs=pl.BlockSpec((tm, d), lambda i: (i, 0)),
        compiler_params=_params(("parallel",)),
        name="merge",
    )(o_attn, o_gla, z, z, wa, wg)


def _out_proj_body(m_ref, h_ref, w_ref, pw_ref, o_ref):
    y = jnp.dot(m_ref[...], w_ref[...], preferred_element_type=F32)
    o_ref[...] = h_ref[...] + _rms(y, pw_ref[...])


def _out_proj(merged, h, w_out, post_w, tm):
    rows, d = h.shape
    return pl.pallas_call(
        _out_proj_body,
        out_shape=jax.ShapeDtypeStruct((rows, d), F32),
        grid=(rows // tm,),
        in_specs=[
            pl.BlockSpec((tm, d), lambda i: (i, 0)),
            pl.BlockSpec((tm, d), lambda i: (i, 0)),
            pl.BlockSpec((d, d), lambda i: (0, 0)),
            pl.BlockSpec((1, d), lambda i: (0, 0)),
        ],
        out_specs=pl.BlockSpec((tm, d), lambda i: (i, 0)),
        compiler_params=_params(("parallel",)),
        name="out_proj",
    )(merged, h, w_out, post_w.reshape(1, d))


def _page_copies(pt_ref, hbm, buf, sem, seq, slot, n_pages, dst):
    return [pltpu.make_async_copy(hbm.at[pt_ref[seq, p]], buf.at[slot].at[dst(p)], sem)
            for p in range(n_pages)]


def _gather_step(fetch):
    b = pl.program_id(0)
    slot = b % 2

    @pl.when(b == 0)
    def _():
        for cp in fetch(b, slot):
            cp.start()

    @pl.when(b + 1 < pl.num_programs(0))
    def _():
        for cp in fetch(b + 1, 1 - slot):
            cp.start()

    for cp in fetch(b, slot):
        cp.wait()
    return slot


def _sample_scores_body(pt_ref, qi_ref, wi_ref, kidx_hbm, o_ref, kbuf, sem, *, n_pages, page):
    def fetch(seq, slot):
        return _page_copies(pt_ref, kidx_hbm, kbuf, sem.at[slot], seq, slot, n_pages,
                            lambda p: (slice(None), pl.ds(p * page, page)))

    slot = _gather_step(fetch)
    s = jnp.dot(qi_ref[0].astype(BF16), kbuf[slot].astype(BF16), preferred_element_type=F32)
    o_ref[0] = jnp.sum(jnp.maximum(s, 0.0) * wi_ref[0], axis=0, keepdims=True)


def _sample_scores(page_table, qi3, wi3, kidx_t):
    nd, n_pages = page_table.shape
    page = kidx_t.shape[2]
    past = n_pages * page
    body = functools.partial(_sample_scores_body, n_pages=n_pages, page=page)
    grid_spec = pltpu.PrefetchScalarGridSpec(
        num_scalar_prefetch=1,
        grid=(nd,),
        in_specs=[
            pl.BlockSpec((1, IDX_HEADS, IDX_DIM), lambda b, pt: (b, 0, 0)),
            pl.BlockSpec((1, IDX_HEADS, 1), lambda b, pt: (b, 0, 0)),
            pl.BlockSpec(memory_space=pl.ANY),
        ],
        out_specs=pl.BlockSpec((1, 1, past), lambda b, pt: (b, 0, 0)),
        scratch_shapes=[pltpu.VMEM((2, IDX_DIM, past), F32), pltpu.SemaphoreType.DMA((2,))],
    )
    return pl.pallas_call(
        body,
        out_shape=jax.ShapeDtypeStruct((nd, 1, past), F32),
        grid_spec=grid_spec,
        compiler_params=_params(("arbitrary",)),
        name="sample_scores",
    )(page_table, qi3, wi3, kidx_t)


def _sample_select_body(sp_ref, qi_ref, misc_ref, bp_ref, bs_ref, sc_scr, j_scr, *, past, tk, topk, idx_bits):
    rows = sp_ref.shape[0]
    misc = misc_ref[...]
    d_in = lax.broadcasted_iota(I32, (LANES, QI_WIDTH), 0)
    c_out = lax.broadcasted_iota(I32, (LANES, QI_WIDTH), 1)
    rep = jnp.where((d_in < IDX_DIM) & (c_out % IDX_DIM == d_in), 1.0, 0.0)
    ki_t = jnp.dot(misc, rep, precision=HIGHEST, preferred_element_type=F32)
    c_in = lax.broadcasted_iota(I32, (QI_WIDTH, LANES), 0)
    l_out = lax.broadcasted_iota(I32, (QI_WIDTH, LANES), 1)
    seg = jnp.where(l_out == MISC_WI + c_in // IDX_DIM, 1.0, 0.0)
    hd = jnp.dot(qi_ref[...] * ki_t, seg, precision=HIGHEST, preferred_element_type=F32)
    lane = lax.broadcasted_iota(I32, (rows, LANES), 1)
    is_wi = (lane >= MISC_WI) & (lane < MISC_GLR)
    s_self = jnp.sum(jnp.where(is_wi, jnp.maximum(hd, 0.0) * misc, 0.0), axis=1, keepdims=True)

    sc_scr[:, 0:past] = sp_ref[...]
    sc_scr[:, past:] = jnp.where(lax.broadcasted_iota(I32, (rows, tk), 1) == 0, s_self, -jnp.inf)
    nk = sc_scr.shape[1] // tk
    thr = _select_threshold(sc_scr, j_scr, nk, tk, topk, idx_bits, 1)
    jb = j_scr[...]
    kpos = lax.broadcasted_iota(I32, sc_scr.shape, 1)
    bias = jnp.where(_selected(sc_scr[...], kpos, thr, jb), 0.0, NEG)
    bp_ref[...] = bias[:, 0:past]
    bs_ref[...] = bias[:, past:past + LANES]


def _sample_select(s_past, qi2, misc, topk):
    nd, past = s_past.shape
    tk = LANES
    width = past + tk
    idx_bits = max(1, (width - 1).bit_length()) + 1
    body = functools.partial(_sample_select_body, past=past, tk=tk, topk=topk, idx_bits=idx_bits)
    return pl.pallas_call(
        body,
        out_shape=(jax.ShapeDtypeStruct((nd, past), F32), jax.ShapeDtypeStruct((nd, LANES), F32)),
        grid=(1,),
        in_specs=[
            pl.BlockSpec((nd, past), lambda i: (0, 0)),
            pl.BlockSpec((nd, QI_WIDTH), lambda i: (0, 0)),
            pl.BlockSpec((nd, LANES), lambda i: (0, 0)),
        ],
        out_specs=(pl.BlockSpec((nd, past), lambda i: (0, 0)), pl.BlockSpec((nd, LANES), lambda i: (0, 0))),
        scratch_shapes=[pltpu.VMEM((nd, width), F32), pltpu.VMEM((nd, 1), I32)],
        compiler_params=_params(("arbitrary",)),
        name="sample_select",
    )(s_past, qi2, misc)


SAMPLE_KCHUNK = 1024


def _sample_attn_body(pt_ref, q_ref, ks_ref, vs_ref, b2_ref, bs_ref, ck_hbm, cv_hbm, o_ref, kbuf, vbuf, sem,
                      *, n_pages, prows):
    def fetch(seq, slot):
        dst = lambda p: (pl.ds(p * prows, prows), slice(None))
        return (_page_copies(pt_ref, ck_hbm, kbuf, sem.at[0, slot], seq, slot, n_pages, dst)
                + _page_copies(pt_ref, cv_hbm, vbuf, sem.at[1, slot], seq, slot, n_pages, dst))

    slot = _gather_step(fetch)
    q = q_ref[0]
    qb = q.astype(BF16)
    total = n_pages * prows
    ch = min(SAMPLE_KCHUNK, total)
    head_grp = lax.broadcasted_iota(I32, (N_HEADS, ch), 0) // HEADS_PER_KV
    row_grp = lax.broadcasted_iota(I32, (N_HEADS, ch), 1) % N_KV_HEADS
    own = head_grp == row_grp
    s_chunks = []
    for c in range(total // ch):
        kc = kbuf[slot, c * ch:(c + 1) * ch, :].astype(BF16)
        s_chunks.append(jnp.where(own, _dot_nt(qb, kc) + b2_ref[0][:, c * ch:(c + 1) * ch], NEG))

    grp0 = lax.broadcasted_iota(I32, (N_HEADS, HEAD_DIM), 0) < HEADS_PER_KV
    k_self = jnp.where(grp0, ks_ref[0][:, 0:HEAD_DIM], ks_ref[0][:, HEAD_DIM:])
    v_self = jnp.where(grp0, vs_ref[0][:, 0:HEAD_DIM], vs_ref[0][:, HEAD_DIM:])
    s_self = jnp.sum(q * k_self, axis=1, keepdims=True) + bs_ref[0][:, 0:1]

    m = s_self
    for s in s_chunks:
        m = jnp.maximum(m, jnp.max(s, axis=1, keepdims=True))
    p_self = jnp.exp(s_self - m)
    l = p_self
    acc = p_self * v_self
    for c, s in enumerate(s_chunks):
        p = jnp.exp(s - m)
        l = l + jnp.sum(p, axis=1, keepdims=True)
        vc = vbuf[slot, c * ch:(c + 1) * ch, :].astype(BF16)
        acc = acc + jnp.dot(p.astype(BF16), vc, preferred_element_type=F32)
    o_ref[0] = (acc / l).astype(BF16)


def _sample_attn(page_table, q3, k_self, v_self, bias2, bias_self, ck2, cv2):
    nd, n_pages = page_table.shape
    prows = ck2.shape[1]
    total = n_pages * prows
    body = functools.partial(_sample_attn_body, n_pages=n_pages, prows=prows)
    grid_spec = pltpu.PrefetchScalarGridSpec(
        num_scalar_prefetch=1,
        grid=(nd,),
        in_specs=[
            pl.BlockSpec((1, N_HEADS, HEAD_DIM), lambda b, pt: (b, 0, 0)),
            pl.BlockSpec((1, 1, KV_WIDTH), lambda b, pt: (b, 0, 0)),
            pl.BlockSpec((1, 1, KV_WIDTH), lambda b, pt: (b, 0, 0)),
            pl.BlockSpec((1, 1, total), lambda b, pt: (b, 0, 0)),
            pl.BlockSpec((1, 1, LANES), lambda b, pt: (b, 0, 0)),
            pl.BlockSpec(memory_space=pl.ANY),
            pl.BlockSpec(memory_space=pl.ANY),
        ],
        out_specs=pl.BlockSpec((1, N_HEADS, HEAD_DIM), lambda b, pt: (b, 0, 0)),
        scratch_shapes=[
            pltpu.VMEM((2, total, HEAD_DIM), F32),
            pltpu.VMEM((2, total, HEAD_DIM), F32),
            pltpu.SemaphoreType.DMA((2, 2)),
        ],
    )
    return pl.pallas_call(
        body,
        out_shape=jax.ShapeDtypeStruct((nd, N_HEADS, HEAD_DIM), BF16),
        grid_spec=grid_spec,
        compiler_params=_params(("arbitrary",)),
        name="sample_attn",
    )(page_table, q3, k_self, v_self, bias2, bias_self, ck2, cv2)


def _gla_sample_body(gq_ref, gk_ref, gv_ref, misc_ref, gr_ref, s_ref, w2_ref, gb_ref, nw_ref, og_ref, so_ref):
    eye = jnp.where(lax.broadcasted_iota(I32, (GLA_DK, GLA_DK), 0)
                    == lax.broadcasted_iota(I32, (GLA_DK, GLA_DK), 1), 1.0, 0.0)

    def column(row):
        return jnp.sum(eye * row, axis=1, keepdims=True)

    for sq in range(s_ref.shape[0]):
        misc = misc_ref[sq]
        for h in range(GLA_HEADS):
            ksl = slice(h * GLA_DK, (h + 1) * GLA_DK)
            vsl = slice(h * GLA_DV, (h + 1) * GLA_DV)
            g = _log_decay(misc, w2_ref[:, ksl], gb_ref[:, ksl])
            s_new = column(jnp.exp(g)) * s_ref[sq, h] + column(gk_ref[sq][:, ksl]) * gv_ref[sq][:, vsl]
            so_ref[sq, h] = s_new
            o = jnp.sum(column(gq_ref[sq][:, ksl] * (GLA_DK ** -0.5)) * s_new, axis=0, keepdims=True)
            og_ref[sq, :, vsl] = _gla_out(o, nw_ref[...], gr_ref[sq][:, vsl])


def _gla_sample(gq, gk, gv, misc, gr, state, w2p, gbias, norm_w):
    nd = state.shape[0]
    sb = 4 if nd % 4 == 0 else 1

    def row3(w):
        return pl.BlockSpec((sb, 1, w), lambda b: (b, 0, 0))

    st_spec = pl.BlockSpec((sb, GLA_HEADS, GLA_DK, GLA_DV), lambda b: (b, 0, 0, 0))
    return pl.pallas_call(
        _gla_sample_body,
        out_shape=(jax.ShapeDtypeStruct((nd, 1, GLA_VAL_WIDTH), BF16),
                   jax.ShapeDtypeStruct(state.shape, F32)),
        grid=(nd // sb,),
        in_specs=[
            row3(GLA_KEY_WIDTH), row3(GLA_KEY_WIDTH), row3(GLA_VAL_WIDTH), row3(LANES), row3(GLA_VAL_WIDTH),
            st_spec,
            pl.BlockSpec((LANES, GLA_KEY_WIDTH), lambda b: (0, 0)),
            pl.BlockSpec((1, GLA_KEY_WIDTH), lambda b: (0, 0)),
            pl.BlockSpec((1, GLA_DV), lambda b: (0, 0)),
        ],
        out_specs=(row3(GLA_VAL_WIDTH), st_spec),
        compiler_params=_params(("parallel",)),
        name="gla_sample",
    )(gq, gk, gv, misc, gr, state, w2p, gbias, norm_w.reshape(1, GLA_DV))


def _rope_tables(pos, d):
    inv = ROPE_THETA ** (-jnp.arange(0, d, 2, dtype=F32) / d)
    ang = pos.astype(F32)[:, None] * inv[None, :]
    cos, sin = jnp.cos(ang), jnp.sin(ang)
    reps = LANES // d
    return (jnp.tile(jnp.concatenate([cos, cos], axis=-1), (1, reps)),
            jnp.tile(jnp.concatenate([-sin, sin], axis=-1), (1, reps)))


def _permute_w_in(w_in, d):
    sizes = (ATTN_WIDTH, KV_WIDTH, KV_WIDTH, QI_WIDTH, IDX_DIM, IDX_HEADS, GLA_KEY_WIDTH, GLA_KEY_WIDTH,
             GLA_VAL_WIDTH, GLA_GATE_RANK, GLA_VAL_WIDTH, d, d)
    pts, acc = [], 0
    for sz in sizes[:-1]:
        acc += sz
        pts.append(acc)
    q, k, v, qi, ki, wi, gq, gk, gv, glr, gr, ga, gg = jnp.split(w_in, pts, axis=1)
    pad = jnp.zeros((w_in.shape[0], C_GQ - (C_MISC + IDX_DIM + IDX_HEADS + GLA_GATE_RANK)), w_in.dtype)
    return jnp.concatenate([q, k, v, qi, ki, wi, glr, pad, gq, gk, gv, gr, ga, gg], axis=1).astype(BF16)


def kernel(x_prompt, x_sample, cache_k, cache_v, cache_kidx, page_table, state_gla,
           ffn1_pre_w, ffn1_w_gate, ffn1_w_up, ffn1_w_down, ffn1_post_w,
           mix_pre_w, w_in, gla_gate_w2, gla_gate_b, gla_norm_w,
           w_proj_attn, w_proj_gla, w_out, mix_post_w,
           ffn2_pre_w, ffn2_w_gate, ffn2_w_up, ffn2_w_down, ffn2_post_w):
    b, s, d = x_prompt.shape
    nd, td, _ = x_sample.shape
    n_pool, page = cache_k.shape[:2]
    n_pages = page_table.shape[1]
    past = n_pages * page
    assert td == 1 and s % Q_BLOCK == 0 and d % MIX_TN == 0 and C_GA % d == 0

    w_perm = _permute_w_in(w_in, d)
    wpa, wpg, wo = w_proj_attn.astype(BF16), w_proj_gla.astype(BF16), w_out.astype(BF16)
    w2p = jnp.zeros((LANES, GLA_KEY_WIDTH), F32).at[MISC_GLR:MISC_GLR + GLA_GATE_RANK].set(gla_gate_w2)
    gbias = gla_gate_b.reshape(1, GLA_KEY_WIDTH)

    def trunk(x, tm, tm_mix, tabs, tab_rows, mixer, f1, f2):
        h = _ffn(x, ffn1_pre_w, *f1, ffn1_post_w, tm)
        h, f1b = h if isinstance(h, tuple) else (h, f1)
        z = _mix_in(h, mix_pre_w, w_perm, tabs, tm_mix, tab_rows // tm_mix)
        o_attn, o_gla, s_fin = mixer(z)
        tmm = min(tm, 512)
        merged = _merge(o_attn, o_gla, z, wpa, wpg, tmm)
        h = _out_proj(merged, h, wo, mix_post_w, tmm)
        y = _ffn(h, ffn2_pre_w, *f2, ffn2_post_w, tm)
        y, f2b = y if isinstance(y, tuple) else (y, f2)
        return y, z, s_fin, f1b, f2b

    pos_p = jnp.arange(s, dtype=I32)
    tm_p = 512 if s % 512 == 0 else Q_BLOCK
    tabs_p = _rope_tables(pos_p, HEAD_DIM) + _rope_tables(pos_p, IDX_DIM)
    top_p = min(TOPK_MAX, s // 4)

    def mixer_p(z):
        o_attn = _attn_prompt(z, b, s, top_p)
        o_gla, s_fin = _gla_prompt(z, w2p, gbias, gla_norm_w, b, s)
        return o_attn, o_gla, s_fin

    tm_mix = 1024 if s % 1024 == 0 else tm_p

    pos_s = jnp.full((nd,), past, I32)
    tabs_s = _rope_tables(pos_s, HEAD_DIM) + _rope_tables(pos_s, IDX_DIM)
    top_s = min(TOPK_MAX, (past + td) // 4)
    ck2 = cache_k.reshape(n_pool, page * N_KV_HEADS, HEAD_DIM)
    cv2 = cache_v.reshape(n_pool, page * N_KV_HEADS, HEAD_DIM)
    kidx_t = jnp.swapaxes(cache_kidx, 1, 2)

    def mixer_s(z):
        misc = z[:, C_MISC:C_MISC + LANES]
        qi2 = z[:, C_QI:C_QI + QI_WIDTH]
        s_past = _sample_scores(page_table, qi2.reshape(nd, IDX_HEADS, IDX_DIM),
                                misc[:, MISC_WI:MISC_GLR].reshape(nd, IDX_HEADS, 1), kidx_t)
        bias_past, bias_self = _sample_select(s_past.reshape(nd, past), qi2, misc, top_s)
        bias2 = jnp.repeat(bias_past, N_KV_HEADS, axis=1).reshape(nd, 1, past * N_KV_HEADS)
        o_attn = _sample_attn(page_table, z[:, C_Q:C_Q + ATTN_WIDTH].reshape(nd, N_HEADS, HEAD_DIM),
                              z[:, C_K:C_K + KV_WIDTH].reshape(nd, 1, KV_WIDTH),
                              z[:, C_V:C_V + KV_WIDTH].reshape(nd, 1, KV_WIDTH),
                              bias2, bias_self.reshape(nd, 1, LANES), ck2, cv2)
        o_gla, s_fin = _gla_sample(z[:, C_GQ:C_GQ + GLA_KEY_WIDTH].reshape(nd, 1, GLA_KEY_WIDTH),
                                   z[:, C_GK:C_GK + GLA_KEY_WIDTH].reshape(nd, 1, GLA_KEY_WIDTH),
                                   z[:, C_GV:C_GV + GLA_VAL_WIDTH].reshape(nd, 1, GLA_VAL_WIDTH),
                                   misc.reshape(nd, 1, LANES),
                                   z[:, C_GR:C_GR + GLA_VAL_WIDTH].reshape(nd, 1, GLA_VAL_WIDTH),
                                   state_gla, w2p, gbias, gla_norm_w)
        return o_attn.reshape(nd, ATTN_WIDTH), o_gla.reshape(nd, GLA_VAL_WIDTH), s_fin

    y_s, z_s, gla_s, f1b, f2b = trunk(x_sample.reshape(nd, d), nd, nd, tabs_s, nd, mixer_s,
                                      (ffn1_w_gate, ffn1_w_up, ffn1_w_down), (ffn2_w_gate, ffn2_w_up, ffn2_w_down))
    y_p, z_p, gla_p, _, _ = trunk(x_prompt.reshape(b * s, d), tm_p, tm_mix, tabs_p, s, mixer_p, f1b, f2b)

    def kv_out(z, n, t):
        return (z[:, C_K:C_K + KV_WIDTH].reshape(n, t, N_KV_HEADS, HEAD_DIM),
                z[:, C_V:C_V + KV_WIDTH].reshape(n, t, N_KV_HEADS, HEAD_DIM),
                z[:, C_MISC:C_MISC + IDX_DIM].reshape(n, t, IDX_DIM))

    k_p, v_p, ki_p = kv_out(z_p, b, s)
    k_s, v_s, ki_s = kv_out(z_s, nd, td)
    return (y_p.reshape(b, s, d), y_s.reshape(nd, td, d), k_p, v_p, ki_p, gla_p, k_s, v_s, ki_s, gla_s)
```

```python
import functools

import jax
import jax.numpy as jnp
from jax import lax
from jax.experimental import pallas as pl
from jax.experimental.pallas import tpu as pltpu

F32, BF16, I32 = jnp.float32, jnp.bfloat16, jnp.int32
HIGHEST = lax.Precision.HIGHEST

N_HEADS = 8
N_KV_HEADS = 2
HEAD_DIM = 128
IDX_HEADS = 16
IDX_DIM = 64
TOPK_MAX = 256
Q_BLOCK = 128
ROPE_THETA = 10000.0
GLA_HEADS = 4
GLA_DK = 128
GLA_DV = 256
GLA_GATE_RANK = 16
GLA_GATE_TAU = 16.0
RMS_EPS = 1e-6

LANES = 128
ATTN_WIDTH = N_HEADS * HEAD_DIM
KV_WIDTH = N_KV_HEADS * HEAD_DIM
QI_WIDTH = IDX_HEADS * IDX_DIM
GLA_KEY_WIDTH = GLA_HEADS * GLA_DK
GLA_VAL_WIDTH = GLA_HEADS * GLA_DV
HEADS_PER_KV = N_HEADS // N_KV_HEADS

C_Q, C_K, C_V, C_QI, C_MISC = 0, 1024, 1280, 1536, 2560
C_GQ, C_GK, C_GV, C_GR, C_GA = 3072, 3584, 4096, 5120, 6144
MISC_WI, MISC_GLR = IDX_DIM, IDX_DIM + IDX_HEADS
MIX_TN = 512

NEG = -1e30
SOFTMAX_MIN_DENOM = 2.0 ** -60
INT_MIN = -2 ** 31
NEG_INF_KEY = -2139095041
VMEM_LIMIT = 56 * 1024 * 1024


def _params(sem, vmem=VMEM_LIMIT):
    return pltpu.CompilerParams(dimension_semantics=sem, vmem_limit_bytes=vmem)


def _rms(x, w):
    return x * lax.rsqrt(jnp.mean(x * x, axis=-1, keepdims=True) + RMS_EPS) * w


def _dot_nt(a, b):
    return lax.dot_general(a, b, (((1,), (1,)), ((), ())), preferred_element_type=F32)


def _ffn_body(x_ref, prew_ref, wg_ref, wu_ref, wd_ref, postw_ref, o_ref, *rest):
    *wb_refs, z_scr, acc_scr = rest
    j = pl.program_id(1)

    @pl.when(j == 0)
    def _():
        z_scr[...] = _rms(x_ref[...], prew_ref[...]).astype(BF16)
        acc_scr[...] = jnp.zeros_like(acc_scr)

    wg, wu, wd = wg_ref[...].astype(BF16), wu_ref[...].astype(BF16), wd_ref[...].astype(BF16)
    for ref, w in zip(wb_refs, (wg, wu, wd)):
        ref[...] = w
    z = z_scr[...]
    g = jnp.dot(z, wg, preferred_element_type=F32)
    u = jnp.dot(z, wu, preferred_element_type=F32)
    a = (g * jax.nn.sigmoid(g) * u).astype(BF16)
    acc_scr[...] += jnp.dot(a, wd, preferred_element_type=F32)

    @pl.when(j == pl.num_programs(1) - 1)
    def _():
        o_ref[...] = x_ref[...] + 0.5 * _rms(acc_scr[...], postw_ref[...])


def _ffn(x, pre_w, wg, wu, wd, post_w, tm):
    rows, d = x.shape
    dff = wg.shape[1]
    tf = 512 if dff % 512 == 0 else dff
    emit = wg.dtype == F32
    assert not emit or rows == tm
    w_specs = [
        pl.BlockSpec((d, tf), lambda i, j: (0, j)),
        pl.BlockSpec((d, tf), lambda i, j: (0, j)),
        pl.BlockSpec((tf, d), lambda i, j: (j, 0)),
    ]
    y_shape = jax.ShapeDtypeStruct((rows, d), F32)
    y_spec = pl.BlockSpec((tm, d), lambda i, j: (i, 0))
    out = pl.pallas_call(
        _ffn_body,
        out_shape=(y_shape, *(jax.ShapeDtypeStruct(w.shape, BF16) for w in (wg, wu, wd))) if emit else y_shape,
        grid=(rows // tm, dff // tf),
        in_specs=[
            pl.BlockSpec((tm, d), lambda i, j: (i, 0)),
            pl.BlockSpec((1, d), lambda i, j: (0, 0)),
            *w_specs,
            pl.BlockSpec((1, d), lambda i, j: (0, 0)),
        ],
        out_specs=(y_spec, *w_specs) if emit else y_spec,
        scratch_shapes=[pltpu.VMEM((tm, d), BF16), pltpu.VMEM((tm, d), F32)],
        compiler_params=_params(("parallel", "arbitrary")),
        name="ffn",
    )(x, pre_w.reshape(1, d), wg, wu, wd, post_w.reshape(1, d))
    return (out[0], tuple(out[1:])) if emit else out


def _mix_in_body(x_ref, prew_ref, w_ref, c128_ref, s128_ref, c64_ref, s64_ref, o_ref, kf_ref, vf_ref, u_scr):
    j = pl.program_id(1)

    @pl.when(j == 0)
    def _():
        u_scr[...] = _rms(x_ref[...], prew_ref[...]).astype(BF16)

    o_ref[...] = jnp.dot(u_scr[...], w_ref[...], preferred_element_type=F32)

    def rope128(x):
        return x * c128_ref[...] + pltpu.roll(x, HEAD_DIM // 2, 1) * s128_ref[...]

    def rope64(x):
        lane = lax.broadcasted_iota(I32, x.shape, 1)
        first = (lane % IDX_DIM) < (IDX_DIM // 2)
        rot = jnp.where(first, pltpu.roll(x, LANES - IDX_DIM // 2, 1), pltpu.roll(x, IDX_DIM // 2, 1))
        return x * c64_ref[...] + rot * s64_ref[...]

    def sl(t):
        return slice(t * LANES, (t + 1) * LANES)

    @pl.when(j < 2)
    def _():
        for t in range(4):
            o_ref[:, sl(t)] = rope128(o_ref[:, sl(t)]) * (HEAD_DIM ** -0.5)

    @pl.when(j == 2)
    def _():
        tm = o_ref.shape[0]
        for t in range(N_KV_HEADS):
            k = rope128(o_ref[:, sl(t)])
            o_ref[:, sl(t)] = k
            kf_ref[pl.ds(t, tm, stride=N_KV_HEADS), :] = k
            vf_ref[pl.ds(t, tm, stride=N_KV_HEADS), :] = o_ref[:, sl(N_KV_HEADS + t)]

    @pl.when((j == 3) | (j == 4))
    def _():
        for t in range(4):
            o_ref[:, sl(t)] = rope64(o_ref[:, sl(t)])

    @pl.when(j == 5)
    def _():
        x = o_ref[:, sl(0)]
        lane = lax.broadcasted_iota(I32, x.shape, 1)
        wi_scale = IDX_HEADS ** -0.5 * IDX_DIM ** -0.5
        o_ref[:, sl(0)] = jnp.where(lane < MISC_WI, rope64(x), jnp.where(lane < MISC_GLR, x * wi_scale, x))


def _mix_in(h, pre_w, w_perm, tabs, tm, tab_blocks):
    rows, d = h.shape
    zw = w_perm.shape[1]
    c128, s128, c64, s64 = tabs
    tab_spec = pl.BlockSpec((tm, LANES), lambda i, j: (i % tab_blocks, 0))
    kv_shape = jax.ShapeDtypeStruct((rows * N_KV_HEADS, HEAD_DIM), F32)
    kv_spec = pl.BlockSpec((tm * N_KV_HEADS, HEAD_DIM), lambda i, j: (i, 0))
    return pl.pallas_call(
        _mix_in_body,
        out_shape=(jax.ShapeDtypeStruct((rows, zw), F32), kv_shape, kv_shape),
        grid=(rows // tm, zw // MIX_TN),
        in_specs=[
            pl.BlockSpec((tm, d), lambda i, j: (i, 0)),
            pl.BlockSpec((1, d), lambda i, j: (0, 0)),
            pl.BlockSpec((d, MIX_TN), lambda i, j: (0, j)),
            tab_spec, tab_spec, tab_spec, tab_spec,
        ],
        out_specs=(pl.BlockSpec((tm, MIX_TN), lambda i, j: (i, j)), kv_spec, kv_spec),
        scratch_shapes=[pltpu.VMEM((tm, d), BF16)],
        compiler_params=_params(("parallel", "arbitrary")),
        name="mix_in",
    )(h, pre_w.reshape(1, d), w_perm, c128, s128, c64, s64)


def _key_to_float(key):
    key = jnp.maximum(key, NEG_INF_KEY)
    return lax.bitcast_convert_type(key ^ ((key >> 31) & jnp.int32(0x7FFFFFFF)), F32)


SORT_N = 16


def _sort_pairs(n):
    pairs, p = [], 1
    while p < n:
        k = p
        while k >= 1:
            for j in range(k % p, n - k, 2 * k):
                for i in range(min(k, n - j - k)):
                    if (i + j) // (2 * p) == (i + j + k) // (2 * p):
                        pairs.append((i + j, i + j + k))
            k //= 2
        p *= 2
    return pairs


def _sort_groups_desc(x):
    v = [x[j * 8:(j + 1) * 8, :] for j in range(SORT_N)]
    for i, j in _sort_pairs(SORT_N):
        v[i], v[j] = jnp.maximum(v[i], v[j]), jnp.minimum(v[i], v[j])
    return jnp.concatenate(v, axis=0)


def _count_ge_sorted(v, t):
    c8 = v[7] >= t
    c4 = jnp.where(c8, v[11], v[3]) >= t
    c2 = jnp.where(c8, jnp.where(c4, v[13], v[9]), jnp.where(c4, v[5], v[1])) >= t
    e = [jnp.where(c2, v[4 * a + 2], v[4 * a]) for a in range(4)]
    c1 = jnp.where(c8, jnp.where(c4, e[3], e[2]), jnp.where(c4, e[1], e[0])) >= t
    low = (jnp.where(c8, 8.0, 0.0) + jnp.where(c4, 4.0, 0.0)) + (jnp.where(c2, 2.0, 0.0) + jnp.where(c1, 1.0, 0.0))
    return jnp.where(v[15] >= t, 16.0, low)


def _select_threshold(sc_ref, j_ref, nk, tk, topk, idx_bits, key_axis, srt_ref=None):
    n_other = sc_ref.shape[1 - key_axis]
    vec = (n_other, 1) if key_axis == 1 else (1, n_other)
    step = LANES if key_axis == 1 else 8

    def count_ge(t):
        if srt_ref is None:
            return count(lambda blk, ks: blk >= t)
        tb = jnp.broadcast_to(t, (8, n_other))

        def body(c, acc):
            ks = pl.multiple_of(c * tk, tk)
            for g0 in range(0, tk, SORT_N * 8):
                v = [srt_ref[pl.ds(ks + g0 + j * 8, 8), :] for j in range(SORT_N)]
                acc = acc + _count_ge_sorted(v, tb)
            return acc
        return jnp.sum(lax.fori_loop(0, nk, body, jnp.zeros((8, n_other), F32)), axis=0, keepdims=True)

    def count(pred):
        def body(c, acc):
            ks = pl.multiple_of(c * tk, tk)
            blk = sc_ref[:, pl.ds(ks, tk)] if key_axis == 1 else sc_ref[pl.ds(ks, tk), :]
            m = jnp.where(pred(blk, ks), 1.0, 0.0)
            parts = [lax.slice_in_dim(m, a, a + step, axis=key_axis) for a in range(0, tk, step)]
            while len(parts) > 1:
                parts = [parts[a] + parts[a + 1] for a in range(0, len(parts), 2)]
            return acc + parts[0]
        acc0 = jnp.zeros((n_other, step) if key_axis == 1 else (step, n_other), F32)
        return jnp.sum(lax.fori_loop(0, nk, body, acc0), axis=key_axis, keepdims=True)

    def bit_body(bi, t):
        cand = t ^ lax.shift_left(jnp.int32(1), 31 - bi)
        return jnp.where(count_ge(_key_to_float(cand)) >= topk, cand, t)

    thr_key = lax.fori_loop(0, 32, bit_body, jnp.full(vec, INT_MIN, I32))
    thr = _key_to_float(thr_key)

    cnt_ge = count(lambda blk, ks: blk >= thr)
    cnt_gt = count(lambda blk, ks: blk > thr)
    need = topk - cnt_gt
    tie = jnp.where((cnt_ge > topk) & (thr_key > NEG_INF_KEY), 1.0, 0.0)
    j_ref[...] = jnp.full(vec, 1 << idx_bits, I32)

    @pl.when(jnp.max(tie) > 0.0)
    def _():
        def jbit(bi, jb):
            cand = jb + lax.shift_left(jnp.int32(1), idx_bits - 1 - bi)

            def pred(blk, ks):
                kpos = ks + lax.broadcasted_iota(I32, blk.shape, key_axis)
                return (blk == thr) & (kpos < cand)
            return jnp.where(count(pred) <= need, cand, jb)
        j_ref[...] = lax.fori_loop(0, idx_bits, jbit, jnp.zeros(vec, I32))

    return thr


def _selected(score, kpos, thr, jb):
    return (score > thr) | ((score == thr) & (kpos < jb))


def _attn_prompt_body(q_ref, qia_ref, qib_ref, misc_ref, k_ref, v_ref, kim_ref, o_ref,
                      sc_scr, srt_scr, j_scr, bias_scr, kb_scr, vb_scr, kib_scr, kn_scr, qi_scr, wt_scr, qs_scr,
                      mx_scr, l_scr, acc_scr, *, tk, topk, idx_bits):
    i = pl.program_id(1)
    tq = Q_BLOCK
    nk = (i * tq + tq + tk - 1) // tk
    n_tiles = tk // LANES
    sub = 256
    qpos_t = i * tq + lax.broadcasted_iota(I32, (1, tq), 1)

    @pl.when(i == 0)
    def _():
        def cast_chunk(c, kn):
            rs = pl.ds(pl.multiple_of(c * tk, tk), tk)
            kf = k_ref[rs, :]
            kb_scr[rs, :] = kf.astype(BF16)
            vb_scr[rs, :] = v_ref[rs, :].astype(BF16)
            kib_scr[rs, :] = kim_ref[rs, 0:IDX_DIM].astype(BF16)
            sq = kf * kf
            return tuple(
                jnp.maximum(kn[g], jnp.max(jnp.sum(sq[:, g * HEAD_DIM:(g + 1) * HEAD_DIM], axis=1, keepdims=True),
                                           axis=0, keepdims=True))
                for g in range(N_KV_HEADS))
        kn = lax.fori_loop(0, k_ref.shape[0] // tk, cast_chunk, (jnp.zeros((1, 1), F32),) * N_KV_HEADS)
        for g in range(N_KV_HEADS):
            kn_scr[g] = jnp.broadcast_to(kn[g], kn_scr.shape[1:])

    for h in range(IDX_HEADS):
        ref = qia_ref if h < IDX_HEADS // 2 else qib_ref
        hh = h % (IDX_HEADS // 2)
        qi_scr[h // 2, (h % 2) * tq:(h % 2 + 1) * tq, :] = ref[:, hh * IDX_DIM:(hh + 1) * IDX_DIM].astype(BF16)
    wt_scr[...] = misc_ref[...].T

    def score_chunk(c, carry):
        for s0 in range(0, tk, sub):
            ks = pl.multiple_of(c * tk, tk) + s0
            ki = kib_scr[pl.ds(ks, sub), :]
            acc = jnp.zeros((sub, tq), F32)
            for hp in range(IDX_HEADS // 2):
                r = jnp.maximum(_dot_nt(ki, qi_scr[hp]), 0.0)
                acc = (acc + r[:, 0:tq] * wt_scr[pl.ds(MISC_WI + 2 * hp, 1), :]
                       + r[:, tq:] * wt_scr[pl.ds(MISC_WI + 2 * hp + 1, 1), :])
            kpos = ks + lax.broadcasted_iota(I32, (sub, tq), 0)
            masked = jnp.where(kpos <= qpos_t, acc, -jnp.inf)
            sc_scr[pl.ds(ks, sub), :] = masked
            for g0 in range(0, sub, SORT_N * 8):
                srt_scr[pl.ds(ks + g0, SORT_N * 8), :] = _sort_groups_desc(masked[g0:g0 + SORT_N * 8, :])
        return carry

    lax.fori_loop(0, nk, score_chunk, 0)
    thr = _select_threshold(sc_scr, j_scr, nk, tk, topk, idx_bits, 0, srt_scr)
    jb = j_scr[...]

    for g in range(N_KV_HEADS):
        kn = jnp.sqrt(kn_scr[g][0:1, :])
        for r in range(HEADS_PER_KV):
            hd = g * HEADS_PER_KV + r
            qh = q_ref[:, hd * HEAD_DIM:(hd + 1) * HEAD_DIM]
            qs_scr[g, r * tq:(r + 1) * tq, :] = qh.astype(BF16)
            mx_scr[g, r * tq:(r + 1) * tq, :] = jnp.sqrt(jnp.sum(qh * qh, axis=1, keepdims=True)) * kn * 1.02
    l_scr[...] = jnp.zeros(l_scr.shape, F32)
    acc_scr[...] = jnp.zeros(acc_scr.shape, F32)

    def logits(g, ks, bias4):
        kc = kb_scr[pl.ds(ks, tk), g * HEAD_DIM:(g + 1) * HEAD_DIM]
        return _dot_nt(qs_scr[g], kc) + bias4

    def exp_chunk(ks, bias):
        bias4 = jnp.concatenate([bias] * HEADS_PER_KV, axis=0)
        for g in range(N_KV_HEADS):
            s = logits(g, ks, bias4)
            m = mx_scr[g]
            p = [jnp.exp(s[:, t * LANES:(t + 1) * LANES] - m) for t in range(n_tiles)]
            l_scr[g] += functools.reduce(lambda x, y: x + y, p)
            pb = jnp.concatenate(p, axis=1).astype(BF16)
            vc = vb_scr[pl.ds(ks, tk), g * HEAD_DIM:(g + 1) * HEAD_DIM]
            acc_scr[g] += jnp.dot(pb, vc, preferred_element_type=F32)

    def bound_pass(c, carry):
        ks = pl.multiple_of(c * tk, tk)
        kpos = ks + lax.broadcasted_iota(I32, (tk, tq), 0)
        sel = _selected(sc_scr[pl.ds(ks, tk), :], kpos, thr, jb) & (kpos <= qpos_t)
        bias = jnp.where(sel, 0.0, NEG).T
        bias_scr[:, pl.ds(ks, tk)] = bias
        exp_chunk(ks, bias)
        return carry

    lax.fori_loop(0, nk, bound_pass, 0)

    def denominators():
        for g in range(N_KV_HEADS):
            mx_scr[g] = jnp.broadcast_to(jnp.sum(l_scr[g], axis=1, keepdims=True), mx_scr.shape[1:])

    denominators()
    l_min = jnp.min(functools.reduce(jnp.minimum, [mx_scr[g] for g in range(N_KV_HEADS)]))

    @pl.when(l_min < SOFTMAX_MIN_DENOM)
    def _():
        mx_scr[...] = jnp.full(mx_scr.shape, NEG, F32)
        l_scr[...] = jnp.zeros(l_scr.shape, F32)
        acc_scr[...] = jnp.zeros(acc_scr.shape, F32)

        def max_pass(c, carry):
            ks = pl.multiple_of(c * tk, tk)
            bias4 = jnp.concatenate([bias_scr[:, pl.ds(ks, tk)]] * HEADS_PER_KV, axis=0)
            for g in range(N_KV_HEADS):
                s = logits(g, ks, bias4)
                m = mx_scr[g]
                for t in range(n_tiles):
                    m = jnp.maximum(m, s[:, t * LANES:(t + 1) * LANES])
                mx_scr[g] = m
            return carry

        lax.fori_loop(0, nk, max_pass, 0)
        for g in range(N_KV_HEADS):
            mx_scr[g] = jnp.broadcast_to(jnp.max(mx_scr[g], axis=1, keepdims=True), mx_scr.shape[1:])

        def exp_pass(c, carry):
            ks = pl.multiple_of(c * tk, tk)
            exp_chunk(ks, bias_scr[:, pl.ds(ks, tk)])
            return carry

        lax.fori_loop(0, nk, exp_pass, 0)
        denominators()

    for g in range(N_KV_HEADS):
        o = acc_scr[g] / mx_scr[g]
        for r in range(HEADS_PER_KV):
            hd = g * HEADS_PER_KV + r
            o_ref[:, hd * HEAD_DIM:(hd + 1) * HEAD_DIM] = o[r * tq:(r + 1) * tq, :].astype(BF16)


def _attn_prompt(z, b, s, topk):
    nq = s // Q_BLOCK
    tk = min(512, s)
    idx_bits = max(1, (s - 1).bit_length()) + 1
    body = functools.partial(_attn_prompt_body, tk=tk, topk=topk, idx_bits=idx_bits)
    rows = HEADS_PER_KV * Q_BLOCK
    return pl.pallas_call(
        body,
        out_shape=jax.ShapeDtypeStruct((b * s, ATTN_WIDTH), BF16),
        grid=(b, nq),
        in_specs=[
            pl.BlockSpec((Q_BLOCK, ATTN_WIDTH), lambda bb, i: (bb * nq + i, 0)),
            pl.BlockSpec((Q_BLOCK, QI_WIDTH // 2), lambda bb, i: (bb * nq + i, C_QI // (QI_WIDTH // 2))),
            pl.BlockSpec((Q_BLOCK, QI_WIDTH // 2), lambda bb, i: (bb * nq + i, C_QI // (QI_WIDTH // 2) + 1)),
            pl.BlockSpec((Q_BLOCK, LANES), lambda bb, i: (bb * nq + i, C_MISC // LANES)),
            pl.BlockSpec((s, KV_WIDTH), lambda bb, i: (bb, C_K // KV_WIDTH)),
            pl.BlockSpec((s, KV_WIDTH), lambda bb, i: (bb, C_V // KV_WIDTH)),
            pl.BlockSpec((s, LANES), lambda bb, i: (bb, C_MISC // LANES)),
        ],
        out_specs=pl.BlockSpec((Q_BLOCK, ATTN_WIDTH), lambda bb, i: (bb * nq + i, 0)),
        scratch_shapes=[
            pltpu.VMEM((s, Q_BLOCK), F32),
            pltpu.VMEM((s, Q_BLOCK), F32),
            pltpu.VMEM((1, Q_BLOCK), I32),
            pltpu.VMEM((Q_BLOCK, s), F32),
            pltpu.VMEM((s, KV_WIDTH), BF16),
            pltpu.VMEM((s, KV_WIDTH), BF16),
            pltpu.VMEM((s, IDX_DIM), BF16),
            pltpu.VMEM((N_KV_HEADS, 8, LANES), F32),
            pltpu.VMEM((IDX_HEADS // 2, 2 * Q_BLOCK, IDX_DIM), BF16),
            pltpu.VMEM((LANES, Q_BLOCK), F32),
            pltpu.VMEM((N_KV_HEADS, rows, HEAD_DIM), BF16),
            pltpu.VMEM((N_KV_HEADS, rows, LANES), F32),
            pltpu.VMEM((N_KV_HEADS, rows, LANES), F32),
            pltpu.VMEM((N_KV_HEADS, rows, HEAD_DIM), F32),
        ],
        compiler_params=_params(("parallel", "arbitrary")),
        name="attn_prompt",
    )(z, z, z, z, z, z, z)


def _log_decay(misc, w2, gb):
    x = jnp.dot(misc, w2, precision=HIGHEST, preferred_element_type=F32) + gb
    return (jnp.minimum(x, 0.0) - jnp.log1p(jnp.exp(-jnp.abs(x)))) * (1.0 / GLA_GATE_TAU)


def _gla_out(o, nw, gr):
    return (_rms(o, nw) * (gr * jax.nn.sigmoid(gr))).astype(BF16)


GLA_C = 128
GLA_SUB = 8
GLA_HPS = 4


def _gla_prompt_body(gq_ref, gk_ref, gv_ref, misc_ref, gr_ref, w2_ref, gb_ref, nw_ref,
                     og_ref, sfin_ref, st_scr, a_scr, b_scr, k_scr, o_scr, *, nchunk):
    t = pl.program_id(2)
    c_ = GLA_C

    @pl.when(t == 0)
    def _():
        st_scr[...] = jnp.zeros_like(st_scr)

    a_scr[...] = jnp.zeros_like(a_scr)
    row = lax.broadcasted_iota(I32, (c_, c_), 0)
    col = lax.broadcasted_iota(I32, (c_, c_), 1)
    tri = jnp.where(col <= row, 1.0, 0.0)
    sub_row = lax.broadcasted_iota(I32, (GLA_SUB, LANES), 0)
    sub_col = lax.broadcasted_iota(I32, (GLA_SUB, GLA_SUB), 1)

    def head_chunk(hh, r0):
        ksl = slice(hh * GLA_DK, (hh + 1) * GLA_DK)
        vsl = slice(hh * GLA_DV, (hh + 1) * GLA_DV)
        q = gq_ref[pl.ds(r0, c_), ksl] * (GLA_DK ** -0.5)
        k = gk_ref[pl.ds(r0, c_), ksl]
        v = gv_ref[pl.ds(r0, c_), vsl].astype(BF16)
        g = _log_decay(misc_ref[pl.ds(r0, c_), :], w2_ref[:, ksl], gb_ref[:, ksl])
        b = jnp.dot(tri, g, precision=HIGHEST, preferred_element_type=F32)
        b_scr[hh] = b
        k_scr[hh] = k
        st = st_scr[hh]
        o = _dot_nt((q * jnp.exp(b)).astype(BF16), st.astype(BF16))

        n = c_ // 2
        while n >= GLA_SUB:
            for rb in range(n, c_, 2 * n):
                bref = b[rb:rb + 1, :]
                qs = q[rb:rb + n, :] * jnp.exp(b[rb:rb + n, :] - bref)
                ks = k[rb - n:rb, :] * jnp.exp(bref - b[rb - n:rb, :])
                a_scr[hh, rb:rb + n, rb - n:rb] = _dot_nt(qs.astype(BF16), ks.astype(BF16))
            n //= 2
        for blk in range(c_ // GLA_SUB):
            lo = blk * GLA_SUB
            qb = q[lo:lo + GLA_SUB, :]
            bb = b[lo:lo + GLA_SUB, :]
            ad = jnp.zeros((GLA_SUB, GLA_SUB), F32)
            for jj in range(GLA_SUB):
                bj = b_scr[hh, pl.ds(lo + jj, 1), :]
                kj = k_scr[hh, pl.ds(lo + jj, 1), :]
                w = jnp.exp(jnp.where(sub_row >= jj, bb - bj, NEG))
                colj = jnp.sum(qb * kj * w, axis=1, keepdims=True)
                ad = jnp.where(sub_col == jj, colj, ad)
            a_scr[hh, lo:lo + GLA_SUB, lo:lo + GLA_SUB] = ad

        o = o + jnp.dot(a_scr[hh].astype(BF16), v, preferred_element_type=F32)
        o_scr[pl.ds(r0, c_), vsl] = o
        b_last = b[c_ - 1:c_, :]
        kd = (k * jnp.exp(b_last - b)).astype(BF16)
        kv = lax.dot_general(v, kd, (((0,), (0,)), ((), ())), preferred_element_type=F32)
        st_scr[hh] = st * jnp.exp(b_last) + kv

    def chunk(ci, carry):
        r0 = pl.multiple_of(ci * c_, c_)
        for hh in range(GLA_HPS):
            head_chunk(hh, r0)
        return carry

    lax.fori_loop(0, nchunk, chunk, 0)
    for hh in range(GLA_HPS):
        vsl = slice(hh * GLA_DV, (hh + 1) * GLA_DV)
        og_ref[:, vsl] = _gla_out(o_scr[:, vsl], nw_ref[...], gr_ref[:, vsl])

    @pl.when(t == pl.num_programs(2) - 1)
    def _():
        for hh in range(GLA_HPS):
            sfin_ref[0, hh] = st_scr[hh].T


def _gla_prompt(z, w2p, gbias, norm_w, b, s):
    tb = min(512, s)
    nt = s // tb
    body = functools.partial(_gla_prompt_body, nchunk=tb // GLA_C)

    def rowblk(bb, t):
        return bb * nt + t

    kw, vw = GLA_HPS * GLA_DK, GLA_HPS * GLA_DV
    return pl.pallas_call(
        body,
        out_shape=(jax.ShapeDtypeStruct((b * s, GLA_VAL_WIDTH), BF16),
                   jax.ShapeDtypeStruct((b, GLA_HEADS, GLA_DK, GLA_DV), F32)),
        grid=(b, GLA_HEADS // GLA_HPS, nt),
        in_specs=[
            pl.BlockSpec((tb, kw), lambda bb, h, t: (rowblk(bb, t), C_GQ // kw + h)),
            pl.BlockSpec((tb, kw), lambda bb, h, t: (rowblk(bb, t), C_GK // kw + h)),
            pl.BlockSpec((tb, vw), lambda bb, h, t: (rowblk(bb, t), C_GV // vw + h)),
            pl.BlockSpec((tb, LANES), lambda bb, h, t: (rowblk(bb, t), C_MISC // LANES)),
            pl.BlockSpec((tb, vw), lambda bb, h, t: (rowblk(bb, t), C_GR // vw + h)),
            pl.BlockSpec((LANES, kw), lambda bb, h, t: (0, h)),
            pl.BlockSpec((1, kw), lambda bb, h, t: (0, h)),
            pl.BlockSpec((1, GLA_DV), lambda bb, h, t: (0, 0)),
        ],
        out_specs=(pl.BlockSpec((tb, vw), lambda bb, h, t: (rowblk(bb, t), h)),
                   pl.BlockSpec((1, GLA_HPS, GLA_DK, GLA_DV), lambda bb, h, t: (bb, h, 0, 0))),
        scratch_shapes=[
            pltpu.VMEM((GLA_HPS, GLA_DV, GLA_DK), F32),
            pltpu.VMEM((GLA_HPS, GLA_C, GLA_C), F32),
            pltpu.VMEM((GLA_HPS, GLA_C, GLA_DK), F32),
            pltpu.VMEM((GLA_HPS, GLA_C, GLA_DK), F32),
            pltpu.VMEM((tb, vw), F32),
        ],
        compiler_params=_params(("parallel", "parallel", "arbitrary")),
        name="gla_prompt",
    )(z, z, z, z, z, w2p, gbias, norm_w.reshape(1, GLA_DV))


def _merge_body(oa_ref, og_ref, ga_ref, gg_ref, wa_ref, wg_ref, o_ref):
    pa = jnp.dot(oa_ref[...], wa_ref[...], preferred_element_type=F32)
    pg = jnp.dot(og_ref[...], wg_ref[...], preferred_element_type=F32)
    o_ref[...] = (jax.nn.sigmoid(ga_ref[...]) * pa + jax.nn.sigmoid(gg_ref[...]) * pg).astype(BF16)


def _merge(o_attn, o_gla, z, wa, wg, tm):
    rows = o_attn.shape[0]
    d = wa.shape[1]
    return pl.pallas_call(
        _merge_body,
        out_shape=jax.ShapeDtypeStruct((rows, d), BF16),
        grid=(rows // tm,),
        in_specs=[
            pl.BlockSpec((tm, ATTN_WIDTH), lambda i: (i, 0)),
            pl.BlockSpec((tm, GLA_VAL_WIDTH), lambda i: (i, 0)),
            pl.BlockSpec((tm, d), lambda i: (i, C_GA // d)),
            pl.BlockSpec((tm, d), lambda i: (i, C_GA // d + 1)),
            pl.BlockSpec((ATTN_WIDTH, d), lambda i: (0, 0)),
            pl.BlockSpec((GLA_VAL_WIDTH, d), lambda i: (0, 0)),
        ],
        out_specs=pl.BlockSpec((tm, d), lambda i: (i, 0)),
        compiler_params=_params(("parallel",)),
        name="merge",
    )(o_attn, o_gla, z, z, wa, wg)


def _out_proj_body(m_ref, h_ref, w_ref, pw_ref, o_ref):
    y = jnp.dot(m_ref[...], w_ref[...], preferred_element_type=F32)
    o_ref[...] = h_ref[...] + _rms(y, pw_ref[...])


def _out_proj(merged, h, w_out, post_w, tm):
    rows, d = h.shape
    return pl.pallas_call(
        _out_proj_body,
        out_shape=jax.ShapeDtypeStruct((rows, d), F32),
        grid=(rows // tm,),
        in_specs=[
            pl.BlockSpec((tm, d), lambda i: (i, 0)),
            pl.BlockSpec((tm, d), lambda i: (i, 0)),
            pl.BlockSpec((d, d), lambda i: (0, 0)),
            pl.BlockSpec((1, d), lambda i: (0, 0)),
        ],
        out_specs=pl.BlockSpec((tm, d), lambda i: (i, 0)),
        compiler_params=_params(("parallel",)),
        name="out_proj",
    )(merged, h, w_out, post_w.reshape(1, d))


def _page_copies(pt_ref, hbm, buf, sem, seq, slot, n_pages, dst):
    return [pltpu.make_async_copy(hbm.at[pt_ref[seq, p]], buf.at[slot].at[dst(p)], sem)
            for p in range(n_pages)]


def _gather_step(fetch):
    b = pl.program_id(0)
    slot = b % 2

    @pl.when(b == 0)
    def _():
        for cp in fetch(b, slot):
            cp.start()

    @pl.when(b + 1 < pl.num_programs(0))
    def _():
        for cp in fetch(b + 1, 1 - slot):
            cp.start()

    for cp in fetch(b, slot):
        cp.wait()
    return slot


def _sample_scores_body(pt_ref, qi_ref, wi_ref, kidx_hbm, o_ref, kbuf, sem, *, n_pages, page):
    def fetch(seq, slot):
        return _page_copies(pt_ref, kidx_hbm, kbuf, sem.at[slot], seq, slot, n_pages,
                            lambda p: (slice(None), pl.ds(p * page, page)))

    slot = _gather_step(fetch)
    s = jnp.dot(qi_ref[0].astype(BF16), kbuf[slot].astype(BF16), preferred_element_type=F32)
    o_ref[0] = jnp.sum(jnp.maximum(s, 0.0) * wi_ref[0], axis=0, keepdims=True)


def _sample_scores(page_table, qi3, wi3, kidx_t):
    nd, n_pages = page_table.shape
    page = kidx_t.shape[2]
    past = n_pages * page
    body = functools.partial(_sample_scores_body, n_pages=n_pages, page=page)
    grid_spec = pltpu.PrefetchScalarGridSpec(
        num_scalar_prefetch=1,
        grid=(nd,),
        in_specs=[
            pl.BlockSpec((1, IDX_HEADS, IDX_DIM), lambda b, pt: (b, 0, 0)),
            pl.BlockSpec((1, IDX_HEADS, 1), lambda b, pt: (b, 0, 0)),
            pl.BlockSpec(memory_space=pl.ANY),
        ],
        out_specs=pl.BlockSpec((1, 1, past), lambda b, pt: (b, 0, 0)),
        scratch_shapes=[pltpu.VMEM((2, IDX_DIM, past), F32), pltpu.SemaphoreType.DMA((2,))],
    )
    return pl.pallas_call(
        body,
        out_shape=jax.ShapeDtypeStruct((nd, 1, past), F32),
        grid_spec=grid_spec,
        compiler_params=_params(("arbitrary",)),
        name="sample_scores",
    )(page_table, qi3, wi3, kidx_t)


def _sample_select_body(sp_ref, qi_ref, misc_ref, bp_ref, bs_ref, sc_scr, j_scr, *, past, tk, topk, idx_bits):
    rows = sp_ref.shape[0]
    misc = misc_ref[...]
    d_in = lax.broadcasted_iota(I32, (LANES, QI_WIDTH), 0)
    c_out = lax.broadcasted_iota(I32, (LANES, QI_WIDTH), 1)
    rep = jnp.where((d_in < IDX_DIM) & (c_out % IDX_DIM == d_in), 1.0, 0.0)
    ki_t = jnp.dot(misc, rep, precision=HIGHEST, preferred_element_type=F32)
    c_in = lax.broadcasted_iota(I32, (QI_WIDTH, LANES), 0)
    l_out = lax.broadcasted_iota(I32, (QI_WIDTH, LANES), 1)
    seg = jnp.where(l_out == MISC_WI + c_in // IDX_DIM, 1.0, 0.0)
    hd = jnp.dot(qi_ref[...] * ki_t, seg, precision=HIGHEST, preferred_element_type=F32)
    lane = lax.broadcasted_iota(I32, (rows, LANES), 1)
    is_wi = (lane >= MISC_WI) & (lane < MISC_GLR)
    s_self = jnp.sum(jnp.where(is_wi, jnp.maximum(hd, 0.0) * misc, 0.0), axis=1, keepdims=True)

    sc_scr[:, 0:past] = sp_ref[...]
    sc_scr[:, past:] = jnp.where(lax.broadcasted_iota(I32, (rows, tk), 1) == 0, s_self, -jnp.inf)
    nk = sc_scr.shape[1] // tk
    thr = _select_threshold(sc_scr, j_scr, nk, tk, topk, idx_bits, 1)
    jb = j_scr[...]
    kpos = lax.broadcasted_iota(I32, sc_scr.shape, 1)
    bias = jnp.where(_selected(sc_scr[...], kpos, thr, jb), 0.0, NEG)
    bp_ref[...] = bias[:, 0:past]
    bs_ref[...] = bias[:, past:past + LANES]


def _sample_select(s_past, qi2, misc, topk):
    nd, past = s_past.shape
    tk = LANES
    width = past + tk
    idx_bits = max(1, (width - 1).bit_length()) + 1
    body = functools.partial(_sample_select_body, past=past, tk=tk, topk=topk, idx_bits=idx_bits)
    return pl.pallas_call(
        body,
        out_shape=(jax.ShapeDtypeStruct((nd, past), F32), jax.ShapeDtypeStruct((nd, LANES), F32)),
        grid=(1,),
        in_specs=[
            pl.BlockSpec((nd, past), lambda i: (0, 0)),
            pl.BlockSpec((nd, QI_WIDTH), lambda i: (0, 0)),
            pl.BlockSpec((nd, LANES), lambda i: (0, 0)),
        ],
        out_specs=(pl.BlockSpec((nd, past), lambda i: (0, 0)), pl.BlockSpec((nd, LANES), lambda i: (0, 0))),
        scratch_shapes=[pltpu.VMEM((nd, width), F32), pltpu.VMEM((nd, 1), I32)],
        compiler_params=_params(("arbitrary",)),
        name="sample_select",
    )(s_past, qi2, misc)


SAMPLE_KCHUNK = 1024


def _sample_attn_body(pt_ref, q_ref, ks_ref, vs_ref, b2_ref, bs_ref, ck_hbm, cv_hbm, o_ref, kbuf, vbuf, sem,
                      *, n_pages, prows):
    def fetch(seq, slot):
        dst = lambda p: (pl.ds(p * prows, prows), slice(None))
        return (_page_copies(pt_ref, ck_hbm, kbuf, sem.at[0, slot], seq, slot, n_pages, dst)
                + _page_copies(pt_ref, cv_hbm, vbuf, sem.at[1, slot], seq, slot, n_pages, dst))

    slot = _gather_step(fetch)
    q = q_ref[0]
    qb = q.astype(BF16)
    total = n_pages * prows
    ch = min(SAMPLE_KCHUNK, total)
    head_grp = lax.broadcasted_iota(I32, (N_HEADS, ch), 0) // HEADS_PER_KV
    row_grp = lax.broadcasted_iota(I32, (N_HEADS, ch), 1) % N_KV_HEADS
    own = head_grp == row_grp
    s_chunks = []
    for c in range(total // ch):
        kc = kbuf[slot, c * ch:(c + 1) * ch, :].astype(BF16)
        s_chunks.append(jnp.where(own, _dot_nt(qb, kc) + b2_ref[0][:, c * ch:(c + 1) * ch], NEG))

    grp0 = lax.broadcasted_iota(I32, (N_HEADS, HEAD_DIM), 0) < HEADS_PER_KV
    k_self = jnp.where(grp0, ks_ref[0][:, 0:HEAD_DIM], ks_ref[0][:, HEAD_DIM:])
    v_self = jnp.where(grp0, vs_ref[0][:, 0:HEAD_DIM], vs_ref[0][:, HEAD_DIM:])
    s_self = jnp.sum(q * k_self, axis=1, keepdims=True) + bs_ref[0][:, 0:1]

    m = s_self
    for s in s_chunks:
        m = jnp.maximum(m, jnp.max(s, axis=1, keepdims=True))
    p_self = jnp.exp(s_self - m)
    l = p_self
    acc = p_self * v_self
    for c, s in enumerate(s_chunks):
        p = jnp.exp(s - m)
        l = l + jnp.sum(p, axis=1, keepdims=True)
        vc = vbuf[slot, c * ch:(c + 1) * ch, :].astype(BF16)
        acc = acc + jnp.dot(p.astype(BF16), vc, preferred_element_type=F32)
    o_ref[0] = (acc / l).astype(BF16)


def _sample_attn(page_table, q3, k_self, v_self, bias2, bias_self, ck2, cv2):
    nd, n_pages = page_table.shape
    prows = ck2.shape[1]
    total = n_pages * prows
    body = functools.partial(_sample_attn_body, n_pages=n_pages, prows=prows)
    grid_spec = pltpu.PrefetchScalarGridSpec(
        num_scalar_prefetch=1,
        grid=(nd,),
        in_specs=[
            pl.BlockSpec((1, N_HEADS, HEAD_DIM), lambda b, pt: (b, 0, 0)),
            pl.BlockSpec((1, 1, KV_WIDTH), lambda b, pt: (b, 0, 0)),
            pl.BlockSpec((1, 1, KV_WIDTH), lambda b, pt: (b, 0, 0)),
            pl.BlockSpec((1, 1, total), lambda b, pt: (b, 0, 0)),
            pl.BlockSpec((1, 1, LANES), lambda b, pt: (b, 0, 0)),
            pl.BlockSpec(memory_space=pl.ANY),
            pl.BlockSpec(memory_space=pl.ANY),
        ],
        out_specs=pl.BlockSpec((1, N_HEADS, HEAD_DIM), lambda b, pt: (b, 0, 0)),
        scratch_shapes=[
            pltpu.VMEM((2, total, HEAD_DIM), F32),
            pltpu.VMEM((2, total, HEAD_DIM), F32),
            pltpu.SemaphoreType.DMA((2, 2)),
        ],
    )
    return pl.pallas_call(
        body,
        out_shape=jax.ShapeDtypeStruct((nd, N_HEADS, HEAD_DIM), BF16),
        grid_spec=grid_spec,
        compiler_params=_params(("arbitrary",)),
        name="sample_attn",
    )(page_table, q3, k_self, v_self, bias2, bias_self, ck2, cv2)


def _gla_sample_body(gq_ref, gk_ref, gv_ref, misc_ref, gr_ref, s_ref, w2_ref, gb_ref, nw_ref, og_ref, so_ref):
    eye = jnp.where(lax.broadcasted_iota(I32, (GLA_DK, GLA_DK), 0)
                    == lax.broadcasted_iota(I32, (GLA_DK, GLA_DK), 1), 1.0, 0.0)

    def column(row):
        return jnp.sum(eye * row, axis=1, keepdims=True)

    for sq in range(s_ref.shape[0]):
        misc = misc_ref[sq]
        for h in range(GLA_HEADS):
            ksl = slice(h * GLA_DK, (h + 1) * GLA_DK)
            vsl = slice(h * GLA_DV, (h + 1) * GLA_DV)
            g = _log_decay(misc, w2_ref[:, ksl], gb_ref[:, ksl])
            s_new = column(jnp.exp(g)) * s_ref[sq, h] + column(gk_ref[sq][:, ksl]) * gv_ref[sq][:, vsl]
            so_ref[sq, h] = s_new
            o = jnp.sum(column(gq_ref[sq][:, ksl] * (GLA_DK ** -0.5)) * s_new, axis=0, keepdims=True)
            og_ref[sq, :, vsl] = _gla_out(o, nw_ref[...], gr_ref[sq][:, vsl])


def _gla_sample(gq, gk, gv, misc, gr, state, w2p, gbias, norm_w):
    nd = state.shape[0]
    sb = 4 if nd % 4 == 0 else 1

    def row3(w):
        return pl.BlockSpec((sb, 1, w), lambda b: (b, 0, 0))

    st_spec = pl.BlockSpec((sb, GLA_HEADS, GLA_DK, GLA_DV), lambda b: (b, 0, 0, 0))
    return pl.pallas_call(
        _gla_sample_body,
        out_shape=(jax.ShapeDtypeStruct((nd, 1, GLA_VAL_WIDTH), BF16),
                   jax.ShapeDtypeStruct(state.shape, F32)),
        grid=(nd // sb,),
        in_specs=[
            row3(GLA_KEY_WIDTH), row3(GLA_KEY_WIDTH), row3(GLA_VAL_WIDTH), row3(LANES), row3(GLA_VAL_WIDTH),
            st_spec,
            pl.BlockSpec((LANES, GLA_KEY_WIDTH), lambda b: (0, 0)),
            pl.BlockSpec((1, GLA_KEY_WIDTH), lambda b: (0, 0)),
            pl.BlockSpec((1, GLA_DV), lambda b: (0, 0)),
        ],
        out_specs=(row3(GLA_VAL_WIDTH), st_spec),
        compiler_params=_params(("parallel",)),
        name="gla_sample",
    )(gq, gk, gv, misc, gr, state, w2p, gbias, norm_w.reshape(1, GLA_DV))


def _rope_tables(pos, d):
    inv = ROPE_THETA ** (-jnp.arange(0, d, 2, dtype=F32) / d)
    ang = pos.astype(F32)[:, None] * inv[None, :]
    cos, sin = jnp.cos(ang), jnp.sin(ang)
    reps = LANES // d
    return (jnp.tile(jnp.concatenate([cos, cos], axis=-1), (1, reps)),
            jnp.tile(jnp.concatenate([-sin, sin], axis=-1), (1, reps)))


def _permute_w_in(w_in, d):
    sizes = (ATTN_WIDTH, KV_WIDTH, KV_WIDTH, QI_WIDTH, IDX_DIM, IDX_HEADS, GLA_KEY_WIDTH, GLA_KEY_WIDTH,
             GLA_VAL_WIDTH, GLA_GATE_RANK, GLA_VAL_WIDTH, d, d)
    pts, acc = [], 0
    for sz in sizes[:-1]:
        acc += sz
        pts.append(acc)
    q, k, v, qi, ki, wi, gq, gk, gv, glr, gr, ga, gg = jnp.split(w_in, pts, axis=1)
    pad = jnp.zeros((w_in.shape[0], C_GQ - (C_MISC + IDX_DIM + IDX_HEADS + GLA_GATE_RANK)), w_in.dtype)
    return jnp.concatenate([q, k, v, qi, ki, wi, glr, pad, gq, gk, gv, gr, ga, gg], axis=1).astype(BF16)


def kernel(x_prompt, x_sample, cache_k, cache_v, cache_kidx, page_table, state_gla,
           ffn1_pre_w, ffn1_w_gate, ffn1_w_up, ffn1_w_down, ffn1_post_w,
           mix_pre_w, w_in, gla_gate_w2, gla_gate_b, gla_norm_w,
           w_proj_attn, w_proj_gla, w_out, mix_post_w,
           ffn2_pre_w, ffn2_w_gate, ffn2_w_up, ffn2_w_down, ffn2_post_w):
    b, s, d = x_prompt.shape
    nd, td, _ = x_sample.shape
    n_pool, page = cache_k.shape[:2]
    n_pages = page_table.shape[1]
    past = n_pages * page
    assert td == 1 and s % Q_BLOCK == 0 and d % MIX_TN == 0 and C_GA % d == 0

    w_perm = _permute_w_in(w_in, d)
    wpa, wpg, wo = w_proj_attn.astype(BF16), w_proj_gla.astype(BF16), w_out.astype(BF16)
    w2p = jnp.zeros((LANES, GLA_KEY_WIDTH), F32).at[MISC_GLR:MISC_GLR + GLA_GATE_RANK].set(gla_gate_w2)
    gbias = gla_gate_b.reshape(1, GLA_KEY_WIDTH)

    def trunk(x, tm, tm_mix, tabs, tab_rows, mixer, f1, f2):
        h = _ffn(x, ffn1_pre_w, *f1, ffn1_post_w, tm)
        h, f1b = h if isinstance(h, tuple) else (h, f1)
        z, k_rows, v_rows = _mix_in(h, mix_pre_w, w_perm, tabs, tm_mix, tab_rows // tm_mix)
        o_attn, o_gla, s_fin = mixer(z)
        tmm = min(tm, 512)
        merged = _merge(o_attn, o_gla, z, wpa, wpg, tmm)
        h = _out_proj(merged, h, wo, mix_post_w, tmm)
        y = _ffn(h, ffn2_pre_w, *f2, ffn2_post_w, tm)
        y, f2b = y if isinstance(y, tuple) else (y, f2)
        return y, (k_rows, v_rows, z[:, C_MISC:C_MISC + IDX_DIM]), s_fin, f1b, f2b

    pos_p = jnp.arange(s, dtype=I32)
    tm_p = 512 if s % 512 == 0 else Q_BLOCK
    tabs_p = _rope_tables(pos_p, HEAD_DIM) + _rope_tables(pos_p, IDX_DIM)
    top_p = min(TOPK_MAX, s // 4)

    def mixer_p(z):
        o_attn = _attn_prompt(z, b, s, top_p)
        o_gla, s_fin = _gla_prompt(z, w2p, gbias, gla_norm_w, b, s)
        return o_attn, o_gla, s_fin

    tm_mix = 1024 if s % 1024 == 0 else tm_p

    pos_s = jnp.full((nd,), past, I32)
    tabs_s = _rope_tables(pos_s, HEAD_DIM) + _rope_tables(pos_s, IDX_DIM)
    top_s = min(TOPK_MAX, (past + td) // 4)
    ck2 = cache_k.reshape(n_pool, page * N_KV_HEADS, HEAD_DIM)
    cv2 = cache_v.reshape(n_pool, page * N_KV_HEADS, HEAD_DIM)
    kidx_t = jnp.swapaxes(cache_kidx, 1, 2)

    def mixer_s(z):
        misc = z[:, C_MISC:C_MISC + LANES]
        qi2 = z[:, C_QI:C_QI + QI_WIDTH]
        s_past = _sample_scores(page_table, qi2.reshape(nd, IDX_HEADS, IDX_DIM),
                                misc[:, MISC_WI:MISC_GLR].reshape(nd, IDX_HEADS, 1), kidx_t)
        bias_past, bias_self = _sample_select(s_past.reshape(nd, past), qi2, misc, top_s)
        bias2 = jnp.repeat(bias_past, N_KV_HEADS, axis=1).reshape(nd, 1, past * N_KV_HEADS)
        o_attn = _sample_attn(page_table, z[:, C_Q:C_Q + ATTN_WIDTH].reshape(nd, N_HEADS, HEAD_DIM),
                              z[:, C_K:C_K + KV_WIDTH].reshape(nd, 1, KV_WIDTH),
                              z[:, C_V:C_V + KV_WIDTH].reshape(nd, 1, KV_WIDTH),
                              bias2, bias_self.reshape(nd, 1, LANES), ck2, cv2)
        o_gla, s_fin = _gla_sample(z[:, C_GQ:C_GQ + GLA_KEY_WIDTH].reshape(nd, 1, GLA_KEY_WIDTH),
                                   z[:, C_GK:C_GK + GLA_KEY_WIDTH].reshape(nd, 1, GLA_KEY_WIDTH),
                                   z[:, C_GV:C_GV + GLA_VAL_WIDTH].reshape(nd, 1, GLA_VAL_WIDTH),
                                   misc.reshape(nd, 1, LANES),
                                   z[:, C_GR:C_GR + GLA_VAL_WIDTH].reshape(nd, 1, GLA_VAL_WIDTH),
                                   state_gla, w2p, gbias, gla_norm_w)
        return o_attn.reshape(nd, ATTN_WIDTH), o_gla.reshape(nd, GLA_VAL_WIDTH), s_fin

    y_s, kv_s, gla_s, f1b, f2b = trunk(x_sample.reshape(nd, d), nd, nd, tabs_s, nd, mixer_s,
                                       (ffn1_w_gate, ffn1_w_up, ffn1_w_down), (ffn2_w_gate, ffn2_w_up, ffn2_w_down))
    y_p, kv_p, gla_p, _, _ = trunk(x_prompt.reshape(b * s, d), tm_p, tm_mix, tabs_p, s, mixer_p, f1b, f2b)

    def kv_out(kv, n, t):
        k_rows, v_rows, ki = kv
        return (k_rows.reshape(n, t, N_KV_HEADS, HEAD_DIM), v_rows.reshape(n, t, N_KV_HEADS, HEAD_DIM),
                ki.reshape(n, t, IDX_DIM))

    k_p, v_p, ki_p = kv_out(kv_p, b, s)
    k_s, v_s, ki_s = kv_out(kv_s, nd, td)
    return (y_p.reshape(b, s, d), y_s.reshape(nd, td, d), k_p, v_p, ki_p, gla_p, k_s, v_s, ki_s, gla_s)
```

```python
import functools

import jax
import jax.numpy as jnp
from jax import lax
from jax.experimental import pallas as pl
from jax.experimental.pallas import tpu as pltpu

F32, BF16, I32 = jnp.float32, jnp.bfloat16, jnp.int32
HIGHEST = lax.Precision.HIGHEST

N_HEADS = 8
N_KV_HEADS = 2
HEAD_DIM = 128
IDX_HEADS = 16
IDX_DIM = 64
TOPK_MAX = 256
Q_BLOCK = 128
ROPE_THETA = 10000.0
GLA_HEADS = 4
GLA_DK = 128
GLA_DV = 256
GLA_GATE_RANK = 16
GLA_GATE_TAU = 16.0
RMS_EPS = 1e-6

LANES = 128
ATTN_WIDTH = N_HEADS * HEAD_DIM
KV_WIDTH = N_KV_HEADS * HEAD_DIM
QI_WIDTH = IDX_HEADS * IDX_DIM
GLA_KEY_WIDTH = GLA_HEADS * GLA_DK
GLA_VAL_WIDTH = GLA_HEADS * GLA_DV
HEADS_PER_KV = N_HEADS // N_KV_HEADS

C_Q, C_K, C_V, C_QI, C_MISC = 0, 1024, 1280, 1536, 2560
C_GQ, C_GK, C_GV, C_GR, C_GA = 3072, 3584, 4096, 5120, 6144
MISC_WI, MISC_GLR = IDX_DIM, IDX_DIM + IDX_HEADS
MIX_TN = 512

NEG = -1e30
SOFTMAX_MIN_DENOM = 2.0 ** -60
INT_MIN = -2 ** 31
NEG_INF_KEY = -2139095041
VMEM_LIMIT = 56 * 1024 * 1024


def _params(sem, vmem=VMEM_LIMIT):
    return pltpu.CompilerParams(dimension_semantics=sem, vmem_limit_bytes=vmem)


def _rms(x, w):
    return x * lax.rsqrt(jnp.mean(x * x, axis=-1, keepdims=True) + RMS_EPS) * w


def _dot_nt(a, b):
    return lax.dot_general(a, b, (((1,), (1,)), ((), ())), preferred_element_type=F32)


def _ffn_body(x_ref, prew_ref, wg_ref, wu_ref, wd_ref, postw_ref, o_ref, *rest):
    *wb_refs, z_scr, acc_scr = rest
    j = pl.program_id(1)

    @pl.when(j == 0)
    def _():
        z_scr[...] = _rms(x_ref[...], prew_ref[...]).astype(BF16)
        acc_scr[...] = jnp.zeros_like(acc_scr)

    wg, wu, wd = wg_ref[...].astype(BF16), wu_ref[...].astype(BF16), wd_ref[...].astype(BF16)
    for ref, w in zip(wb_refs, (wg, wu, wd)):
        ref[...] = w
    z = z_scr[...]
    g = jnp.dot(z, wg, preferred_element_type=F32)
    u = jnp.dot(z, wu, preferred_element_type=F32)
    a = (g * jax.nn.sigmoid(g) * u).astype(BF16)
    acc_scr[...] += jnp.dot(a, wd, preferred_element_type=F32)

    @pl.when(j == pl.num_programs(1) - 1)
    def _():
        o_ref[...] = x_ref[...] + 0.5 * _rms(acc_scr[...], postw_ref[...])


def _ffn(x, pre_w, wg, wu, wd, post_w, tm):
    rows, d = x.shape
    dff = wg.shape[1]
    tf = 512 if dff % 512 == 0 else dff
    emit = wg.dtype == F32
    assert not emit or rows == tm
    w_specs = [
        pl.BlockSpec((d, tf), lambda i, j: (0, j)),
        pl.BlockSpec((d, tf), lambda i, j: (0, j)),
        pl.BlockSpec((tf, d), lambda i, j: (j, 0)),
    ]
    y_shape = jax.ShapeDtypeStruct((rows, d), F32)
    y_spec = pl.BlockSpec((tm, d), lambda i, j: (i, 0))
    out = pl.pallas_call(
        _ffn_body,
        out_shape=(y_shape, *(jax.ShapeDtypeStruct(w.shape, BF16) for w in (wg, wu, wd))) if emit else y_shape,
        grid=(rows // tm, dff // tf),
        in_specs=[
            pl.BlockSpec((tm, d), lambda i, j: (i, 0)),
            pl.BlockSpec((1, d), lambda i, j: (0, 0)),
            *w_specs,
            pl.BlockSpec((1, d), lambda i, j: (0, 0)),
        ],
        out_specs=(y_spec, *w_specs) if emit else y_spec,
        scratch_shapes=[pltpu.VMEM((tm, d), BF16), pltpu.VMEM((tm, d), F32)],
        compiler_params=_params(("parallel", "arbitrary")),
        name="ffn",
    )(x, pre_w.reshape(1, d), wg, wu, wd, post_w.reshape(1, d))
    return (out[0], tuple(out[1:])) if emit else out


def _mix_in_body(x_ref, prew_ref, w_ref, c128_ref, s128_ref, c64_ref, s64_ref, o_ref, kf_ref, vf_ref, u_scr):
    j = pl.program_id(1)

    @pl.when(j == 0)
    def _():
        u_scr[...] = _rms(x_ref[...], prew_ref[...]).astype(BF16)

    o_ref[...] = jnp.dot(u_scr[...], w_ref[...], preferred_element_type=F32)

    def rope128(x):
        return x * c128_ref[...] + pltpu.roll(x, HEAD_DIM // 2, 1) * s128_ref[...]

    def rope64(x):
        lane = lax.broadcasted_iota(I32, x.shape, 1)
        first = (lane % IDX_DIM) < (IDX_DIM // 2)
        rot = jnp.where(first, pltpu.roll(x, LANES - IDX_DIM // 2, 1), pltpu.roll(x, IDX_DIM // 2, 1))
        return x * c64_ref[...] + rot * s64_ref[...]

    def sl(t):
        return slice(t * LANES, (t + 1) * LANES)

    @pl.when(j < 2)
    def _():
        for t in range(4):
            o_ref[:, sl(t)] = rope128(o_ref[:, sl(t)]) * (HEAD_DIM ** -0.5)

    @pl.when(j == 2)
    def _():
        tm = o_ref.shape[0]
        for t in range(N_KV_HEADS):
            k = rope128(o_ref[:, sl(t)])
            o_ref[:, sl(t)] = k
            kf_ref[pl.ds(t, tm, stride=N_KV_HEADS), :] = k
            vf_ref[pl.ds(t, tm, stride=N_KV_HEADS), :] = o_ref[:, sl(N_KV_HEADS + t)]

    @pl.when((j == 3) | (j == 4))
    def _():
        for t in range(4):
            o_ref[:, sl(t)] = rope64(o_ref[:, sl(t)])

    @pl.when(j == 5)
    def _():
        x = o_ref[:, sl(0)]
        lane = lax.broadcasted_iota(I32, x.shape, 1)
        wi_scale = IDX_HEADS ** -0.5 * IDX_DIM ** -0.5
        o_ref[:, sl(0)] = jnp.where(lane < MISC_WI, rope64(x), jnp.where(lane < MISC_GLR, x * wi_scale, x))


def _mix_in(h, pre_w, w_perm, tabs, tm, tab_blocks):
    rows, d = h.shape
    zw = w_perm.shape[1]
    c128, s128, c64, s64 = tabs
    tab_spec = pl.BlockSpec((tm, LANES), lambda i, j: (i % tab_blocks, 0))
    kv_shape = jax.ShapeDtypeStruct((rows * N_KV_HEADS, HEAD_DIM), F32)
    kv_spec = pl.BlockSpec((tm * N_KV_HEADS, HEAD_DIM), lambda i, j: (i, 0))
    return pl.pallas_call(
        _mix_in_body,
        out_shape=(jax.ShapeDtypeStruct((rows, zw), F32), kv_shape, kv_shape),
        grid=(rows // tm, zw // MIX_TN),
        in_specs=[
            pl.BlockSpec((tm, d), lambda i, j: (i, 0)),
            pl.BlockSpec((1, d), lambda i, j: (0, 0)),
            pl.BlockSpec((d, MIX_TN), lambda i, j: (0, j)),
            tab_spec, tab_spec, tab_spec, tab_spec,
        ],
        out_specs=(pl.BlockSpec((tm, MIX_TN), lambda i, j: (i, j)), kv_spec, kv_spec),
        scratch_shapes=[pltpu.VMEM((tm, d), BF16)],
        compiler_params=_params(("parallel", "arbitrary")),
        name="mix_in",
    )(h, pre_w.reshape(1, d), w_perm, c128, s128, c64, s64)


def _key_to_float(key):
    key = jnp.maximum(key, NEG_INF_KEY)
    return lax.bitcast_convert_type(key ^ ((key >> 31) & jnp.int32(0x7FFFFFFF)), F32)


SORT_N = 16


def _sort_pairs(n):
    pairs, p = [], 1
    while p < n:
        k = p
        while k >= 1:
            for j in range(k % p, n - k, 2 * k):
                for i in range(min(k, n - j - k)):
                    if (i + j) // (2 * p) == (i + j + k) // (2 * p):
                        pairs.append((i + j, i + j + k))
            k //= 2
        p *= 2
    return pairs


def _sort_groups_desc(x):
    v = [x[j * 8:(j + 1) * 8, :] for j in range(SORT_N)]
    for i, j in _sort_pairs(SORT_N):
        v[i], v[j] = jnp.maximum(v[i], v[j]), jnp.minimum(v[i], v[j])
    return jnp.concatenate(v, axis=0)


def _count_ge_sorted(v, t):
    c8 = v[7] >= t
    c4 = jnp.where(c8, v[11], v[3]) >= t
    c2 = jnp.where(c8, jnp.where(c4, v[13], v[9]), jnp.where(c4, v[5], v[1])) >= t
    e = [jnp.where(c2, v[4 * a + 2], v[4 * a]) for a in range(4)]
    c1 = jnp.where(c8, jnp.where(c4, e[3], e[2]), jnp.where(c4, e[1], e[0])) >= t
    low = (jnp.where(c8, 8.0, 0.0) + jnp.where(c4, 4.0, 0.0)) + (jnp.where(c2, 2.0, 0.0) + jnp.where(c1, 1.0, 0.0))
    return jnp.where(v[15] >= t, 16.0, low)


def _select_threshold(sc_ref, j_ref, nk, tk, topk, idx_bits, key_axis, srt_ref=None):
    n_other = sc_ref.shape[1 - key_axis]
    vec = (n_other, 1) if key_axis == 1 else (1, n_other)
    step = LANES if key_axis == 1 else 8

    def count_ge(t):
        if srt_ref is None:
            return count(lambda blk, ks: blk >= t)
        tb = jnp.broadcast_to(t, (8, n_other))

        def body(c, acc):
            ks = pl.multiple_of(c * tk, tk)
            for g0 in range(0, tk, SORT_N * 8):
                v = [srt_ref[pl.ds(ks + g0 + j * 8, 8), :] for j in range(SORT_N)]
                acc = acc + _count_ge_sorted(v, tb)
            return acc
        return jnp.sum(lax.fori_loop(0, nk, body, jnp.zeros((8, n_other), F32)), axis=0, keepdims=True)

    def count(pred):
        def body(c, acc):
            ks = pl.multiple_of(c * tk, tk)
            blk = sc_ref[:, pl.ds(ks, tk)] if key_axis == 1 else sc_ref[pl.ds(ks, tk), :]
            m = jnp.where(pred(blk, ks), 1.0, 0.0)
            parts = [lax.slice_in_dim(m, a, a + step, axis=key_axis) for a in range(0, tk, step)]
            while len(parts) > 1:
                parts = [parts[a] + parts[a + 1] for a in range(0, len(parts), 2)]
            return acc + parts[0]
        acc0 = jnp.zeros((n_other, step) if key_axis == 1 else (step, n_other), F32)
        return jnp.sum(lax.fori_loop(0, nk, body, acc0), axis=key_axis, keepdims=True)

    def bit_body(bi, t):
        cand = t ^ lax.shift_left(jnp.int32(1), 31 - bi)
        return jnp.where(count_ge(_key_to_float(cand)) >= topk, cand, t)

    thr_key = lax.fori_loop(0, 32, bit_body, jnp.full(vec, INT_MIN, I32))
    thr = _key_to_float(thr_key)

    cnt_ge = count(lambda blk, ks: blk >= thr)
    cnt_gt = count(lambda blk, ks: blk > thr)
    need = topk - cnt_gt
    tie = jnp.where((cnt_ge > topk) & (thr_key > NEG_INF_KEY), 1.0, 0.0)
    j_ref[...] = jnp.full(vec, 1 << idx_bits, I32)

    @pl.when(jnp.max(tie) > 0.0)
    def _():
        def jbit(bi, jb):
            cand = jb + lax.shift_left(jnp.int32(1), idx_bits - 1 - bi)

            def pred(blk, ks):
                kpos = ks + lax.broadcasted_iota(I32, blk.shape, key_axis)
                return (blk == thr) & (kpos < cand)
            return jnp.where(count(pred) <= need, cand, jb)
        j_ref[...] = lax.fori_loop(0, idx_bits, jbit, jnp.zeros(vec, I32))

    return thr


def _selected(score, kpos, thr, jb):
    return (score > thr) | ((score == thr) & (kpos < jb))


def _attn_prompt_body(q_ref, qia_ref, qib_ref, misc_ref, k_ref, v_ref, kim_ref, o_ref,
                      sc_scr, srt_scr, j_scr, bias_scr, kb_scr, vb_scr, kib_scr, kn_scr, qi_scr, wt_scr, qs_scr,
                      mx_scr, l_scr, acc_scr, *, tk, topk, idx_bits):
    i = pl.program_id(1)
    tq = Q_BLOCK
    nk = (i * tq + tq + tk - 1) // tk
    n_tiles = tk // LANES
    sub = 256
    qpos_t = i * tq + lax.broadcasted_iota(I32, (1, tq), 1)

    @pl.when(i == 0)
    def _():
        def cast_chunk(c, kn):
            rs = pl.ds(pl.multiple_of(c * tk, tk), tk)
            kf = k_ref[rs, :]
            kb_scr[rs, :] = kf.astype(BF16)
            vb_scr[rs, :] = v_ref[rs, :].astype(BF16)
            kib_scr[rs, :] = kim_ref[rs, 0:IDX_DIM].astype(BF16)
            sq = kf * kf
            return tuple(
                jnp.maximum(kn[g], jnp.max(jnp.sum(sq[:, g * HEAD_DIM:(g + 1) * HEAD_DIM], axis=1, keepdims=True),
                                           axis=0, keepdims=True))
                for g in range(N_KV_HEADS))
        kn = lax.fori_loop(0, k_ref.shape[0] // tk, cast_chunk, (jnp.zeros((1, 1), F32),) * N_KV_HEADS)
        for g in range(N_KV_HEADS):
            kn_scr[g] = jnp.broadcast_to(kn[g], kn_scr.shape[1:])

    for h in range(IDX_HEADS):
        ref = qia_ref if h < IDX_HEADS // 2 else qib_ref
        hh = h % (IDX_HEADS // 2)
        qi_scr[h // 2, (h % 2) * tq:(h % 2 + 1) * tq, :] = ref[:, hh * IDX_DIM:(hh + 1) * IDX_DIM].astype(BF16)
    wt_scr[...] = misc_ref[...].T

    def score_chunk(c, carry):
        for s0 in range(0, tk, sub):
            ks = pl.multiple_of(c * tk, tk) + s0
            ki = kib_scr[pl.ds(ks, sub), :]
            acc = jnp.zeros((sub, tq), F32)
            for hp in range(IDX_HEADS // 2):
                r = jnp.maximum(_dot_nt(ki, qi_scr[hp]), 0.0)
                acc = (acc + r[:, 0:tq] * wt_scr[pl.ds(MISC_WI + 2 * hp, 1), :]
                       + r[:, tq:] * wt_scr[pl.ds(MISC_WI + 2 * hp + 1, 1), :])
            kpos = ks + lax.broadcasted_iota(I32, (sub, tq), 0)
            masked = jnp.where(kpos <= qpos_t, acc, -jnp.inf)
            sc_scr[pl.ds(ks, sub), :] = masked
            for g0 in range(0, sub, SORT_N * 8):
                srt_scr[pl.ds(ks + g0, SORT_N * 8), :] = _sort_groups_desc(masked[g0:g0 + SORT_N * 8, :])
        return carry

    lax.fori_loop(0, nk, score_chunk, 0)
    thr = _select_threshold(sc_scr, j_scr, nk, tk, topk, idx_bits, 0, srt_scr)
    jb = j_scr[...]

    for g in range(N_KV_HEADS):
        kn = jnp.sqrt(kn_scr[g][0:1, :])
        for r in range(HEADS_PER_KV):
            hd = g * HEADS_PER_KV + r
            qh = q_ref[:, hd * HEAD_DIM:(hd + 1) * HEAD_DIM]
            qs_scr[g, r * tq:(r + 1) * tq, :] = qh.astype(BF16)
            mx_scr[g, r * tq:(r + 1) * tq, :] = jnp.sqrt(jnp.sum(qh * qh, axis=1, keepdims=True)) * kn * 1.02
    l_scr[...] = jnp.zeros(l_scr.shape, F32)
    acc_scr[...] = jnp.zeros(acc_scr.shape, F32)

    def logits(g, ks, bias4):
        kc = kb_scr[pl.ds(ks, tk), g * HEAD_DIM:(g + 1) * HEAD_DIM]
        return _dot_nt(qs_scr[g], kc) + bias4

    def exp_chunk(ks, bias):
        bias4 = jnp.concatenate([bias] * HEADS_PER_KV, axis=0)
        for g in range(N_KV_HEADS):
            s = logits(g, ks, bias4)
            m = mx_scr[g]
            p = [jnp.exp(s[:, t * LANES:(t + 1) * LANES] - m) for t in range(n_tiles)]
            l_scr[g] += functools.reduce(lambda x, y: x + y, p)
            pb = jnp.concatenate(p, axis=1).astype(BF16)
            vc = vb_scr[pl.ds(ks, tk), g * HEAD_DIM:(g + 1) * HEAD_DIM]
            acc_scr[g] += jnp.dot(pb, vc, preferred_element_type=F32)

    def bound_pass(c, carry):
        ks = pl.multiple_of(c * tk, tk)
        kpos = ks + lax.broadcasted_iota(I32, (tk, tq), 0)
        sel = _selected(sc_scr[pl.ds(ks, tk), :], kpos, thr, jb) & (kpos <= qpos_t)
        bias = jnp.where(sel, 0.0, NEG).T
        bias_scr[:, pl.ds(ks, tk)] = bias
        exp_chunk(ks, bias)
        return carry

    lax.fori_loop(0, nk, bound_pass, 0)

    def denominators():
        for g in range(N_KV_HEADS):
            mx_scr[g] = jnp.broadcast_to(jnp.sum(l_scr[g], axis=1, keepdims=True), mx_scr.shape[1:])

    denominators()
    l_min = jnp.min(functools.reduce(jnp.minimum, [mx_scr[g] for g in range(N_KV_HEADS)]))

    @pl.when(l_min < SOFTMAX_MIN_DENOM)
    def _():
        mx_scr[...] = jnp.full(mx_scr.shape, NEG, F32)
        l_scr[...] = jnp.zeros(l_scr.shape, F32)
        acc_scr[...] = jnp.zeros(acc_scr.shape, F32)

        def max_pass(c, carry):
            ks = pl.multiple_of(c * tk, tk)
            bias4 = jnp.concatenate([bias_scr[:, pl.ds(ks, tk)]] * HEADS_PER_KV, axis=0)
            for g in range(N_KV_HEADS):
                s = logits(g, ks, bias4)
                m = mx_scr[g]
                for t in range(n_tiles):
                    m = jnp.maximum(m, s[:, t * LANES:(t + 1) * LANES])
                mx_scr[g] = m
            return carry

        lax.fori_loop(0, nk, max_pass, 0)
        for g in range(N_KV_HEADS):
            mx_scr[g] = jnp.broadcast_to(jnp.max(mx_scr[g], axis=1, keepdims=True), mx_scr.shape[1:])

        def exp_pass(c, carry):
            ks = pl.multiple_of(c * tk, tk)
            exp_chunk(ks, bias_scr[:, pl.ds(ks, tk)])
            return carry

        lax.fori_loop(0, nk, exp_pass, 0)
        denominators()

    for g in range(N_KV_HEADS):
        o = acc_scr[g] / mx_scr[g]
        for r in range(HEADS_PER_KV):
            hd = g * HEADS_PER_KV + r
            o_ref[:, hd * HEAD_DIM:(hd + 1) * HEAD_DIM] = o[r * tq:(r + 1) * tq, :].astype(BF16)


def _attn_prompt(z, b, s, topk):
    nq = s // Q_BLOCK
    tk = min(512, s)
    idx_bits = max(1, (s - 1).bit_length()) + 1
    body = functools.partial(_attn_prompt_body, tk=tk, topk=topk, idx_bits=idx_bits)
    rows = HEADS_PER_KV * Q_BLOCK
    return pl.pallas_call(
        body,
        out_shape=jax.ShapeDtypeStruct((b * s, ATTN_WIDTH), BF16),
        grid=(b, nq),
        in_specs=[
            pl.BlockSpec((Q_BLOCK, ATTN_WIDTH), lambda bb, i: (bb * nq + i, 0)),
            pl.BlockSpec((Q_BLOCK, QI_WIDTH // 2), lambda bb, i: (bb * nq + i, C_QI // (QI_WIDTH // 2))),
            pl.BlockSpec((Q_BLOCK, QI_WIDTH // 2), lambda bb, i: (bb * nq + i, C_QI // (QI_WIDTH // 2) + 1)),
            pl.BlockSpec((Q_BLOCK, LANES), lambda bb, i: (bb * nq + i, C_MISC // LANES)),
            pl.BlockSpec((s, KV_WIDTH), lambda bb, i: (bb, C_K // KV_WIDTH)),
            pl.BlockSpec((s, KV_WIDTH), lambda bb, i: (bb, C_V // KV_WIDTH)),
            pl.BlockSpec((s, LANES), lambda bb, i: (bb, C_MISC // LANES)),
        ],
        out_specs=pl.BlockSpec((Q_BLOCK, ATTN_WIDTH), lambda bb, i: (bb * nq + i, 0)),
        scratch_shapes=[
            pltpu.VMEM((s, Q_BLOCK), F32),
            pltpu.VMEM((s, Q_BLOCK), F32),
            pltpu.VMEM((1, Q_BLOCK), I32),
            pltpu.VMEM((Q_BLOCK, s), F32),
            pltpu.VMEM((s, KV_WIDTH), BF16),
            pltpu.VMEM((s, KV_WIDTH), BF16),
            pltpu.VMEM((s, IDX_DIM), BF16),
            pltpu.VMEM((N_KV_HEADS, 8, LANES), F32),
            pltpu.VMEM((IDX_HEADS // 2, 2 * Q_BLOCK, IDX_DIM), BF16),
            pltpu.VMEM((LANES, Q_BLOCK), F32),
            pltpu.VMEM((N_KV_HEADS, rows, HEAD_DIM), BF16),
            pltpu.VMEM((N_KV_HEADS, rows, LANES), F32),
            pltpu.VMEM((N_KV_HEADS, rows, LANES), F32),
            pltpu.VMEM((N_KV_HEADS, rows, HEAD_DIM), F32),
        ],
        compiler_params=_params(("parallel", "arbitrary")),
        name="attn_prompt",
    )(z, z, z, z, z, z, z)


def _log_decay(misc, w2, gb):
    x = jnp.dot(misc, w2, precision=HIGHEST, preferred_element_type=F32) + gb
    return (jnp.minimum(x, 0.0) - jnp.log1p(jnp.exp(-jnp.abs(x)))) * (1.0 / GLA_GATE_TAU)


def _gla_out(o, nw, gr):
    return (_rms(o, nw) * (gr * jax.nn.sigmoid(gr))).astype(BF16)


GLA_C = 128
GLA_SUB = 8
GLA_HPS = 4


def _gla_prompt_body(gq_ref, gk_ref, gv_ref, misc_ref, gr_ref, w2_ref, gb_ref, nw_ref,
                     og_ref, sfin_ref, st_scr, a_scr, b_scr, k_scr, o_scr, *, nchunk):
    t = pl.program_id(2)
    c_ = GLA_C

    @pl.when(t == 0)
    def _():
        st_scr[...] = jnp.zeros_like(st_scr)

    a_scr[...] = jnp.zeros_like(a_scr)
    row = lax.broadcasted_iota(I32, (c_, c_), 0)
    col = lax.broadcasted_iota(I32, (c_, c_), 1)
    tri = jnp.where(col <= row, 1.0, 0.0)
    sub_row = lax.broadcasted_iota(I32, (GLA_SUB, LANES), 0)
    sub_col = lax.broadcasted_iota(I32, (GLA_SUB, GLA_SUB), 1)

    def head_chunk(hh, r0):
        ksl = slice(hh * GLA_DK, (hh + 1) * GLA_DK)
        vsl = slice(hh * GLA_DV, (hh + 1) * GLA_DV)
        q = gq_ref[pl.ds(r0, c_), ksl] * (GLA_DK ** -0.5)
        k = gk_ref[pl.ds(r0, c_), ksl]
        v = gv_ref[pl.ds(r0, c_), vsl].astype(BF16)
        g = _log_decay(misc_ref[pl.ds(r0, c_), :], w2_ref[:, ksl], gb_ref[:, ksl])
        b = jnp.dot(tri, g, precision=HIGHEST, preferred_element_type=F32)
        b_scr[hh] = b
        k_scr[hh] = k
        st = st_scr[hh]
        o = _dot_nt((q * jnp.exp(b)).astype(BF16), st.astype(BF16))

        n = c_ // 2
        while n >= GLA_SUB:
            for rb in range(n, c_, 2 * n):
                bref = b[rb:rb + 1, :]
                qs = q[rb:rb + n, :] * jnp.exp(b[rb:rb + n, :] - bref)
                ks = k[rb - n:rb, :] * jnp.exp(bref - b[rb - n:rb, :])
                a_scr[hh, rb:rb + n, rb - n:rb] = _dot_nt(qs.astype(BF16), ks.astype(BF16))
            n //= 2
        for blk in range(c_ // GLA_SUB):
            lo = blk * GLA_SUB
            qb = q[lo:lo + GLA_SUB, :]
            bb = b[lo:lo + GLA_SUB, :]
            ad = jnp.zeros((GLA_SUB, GLA_SUB), F32)
            for jj in range(GLA_SUB):
                bj = b_scr[hh, pl.ds(lo + jj, 1), :]
                kj = k_scr[hh, pl.ds(lo + jj, 1), :]
                w = jnp.exp(jnp.where(sub_row >= jj, bb - bj, NEG))
                colj = jnp.sum(qb * kj * w, axis=1, keepdims=True)
                ad = jnp.where(sub_col == jj, colj, ad)
            a_scr[hh, lo:lo + GLA_SUB, lo:lo + GLA_SUB] = ad

        o = o + jnp.dot(a_scr[hh].astype(BF16), v, preferred_element_type=F32)
        o_scr[pl.ds(r0, c_), vsl] = o
        b_last = b[c_ - 1:c_, :]
        kd = (k * jnp.exp(b_last - b)).astype(BF16)
        kv = lax.dot_general(v, kd, (((0,), (0,)), ((), ())), preferred_element_type=F32)
        st_scr[hh] = st * jnp.exp(b_last) + kv

    def chunk(ci, carry):
        r0 = pl.multiple_of(ci * c_, c_)
        for hh in range(GLA_HPS):
            head_chunk(hh, r0)
        return carry

    lax.fori_loop(0, nchunk, chunk, 0)
    for hh in range(GLA_HPS):
        vsl = slice(hh * GLA_DV, (hh + 1) * GLA_DV)
        og_ref[:, vsl] = _gla_out(o_scr[:, vsl], nw_ref[...], gr_ref[:, vsl])

    @pl.when(t == pl.num_programs(2) - 1)
    def _():
        for hh in range(GLA_HPS):
            sfin_ref[0, hh] = st_scr[hh].T


def _gla_prompt(z, w2p, gbias, norm_w, b, s):
    tb = min(512, s)
    nt = s // tb
    body = functools.partial(_gla_prompt_body, nchunk=tb // GLA_C)

    def rowblk(bb, t):
        return bb * nt + t

    kw, vw = GLA_HPS * GLA_DK, GLA_HPS * GLA_DV
    return pl.pallas_call(
        body,
        out_shape=(jax.ShapeDtypeStruct((b * s, GLA_VAL_WIDTH), BF16),
                   jax.ShapeDtypeStruct((b, GLA_HEADS, GLA_DK, GLA_DV), F32)),
        grid=(b, GLA_HEADS // GLA_HPS, nt),
        in_specs=[
            pl.BlockSpec((tb, kw), lambda bb, h, t: (rowblk(bb, t), C_GQ // kw + h)),
            pl.BlockSpec((tb, kw), lambda bb, h, t: (rowblk(bb, t), C_GK // kw + h)),
            pl.BlockSpec((tb, vw), lambda bb, h, t: (rowblk(bb, t), C_GV // vw + h)),
            pl.BlockSpec((tb, LANES), lambda bb, h, t: (rowblk(bb, t), C_MISC // LANES)),
            pl.BlockSpec((tb, vw), lambda bb, h, t: (rowblk(bb, t), C_GR // vw + h)),
            pl.BlockSpec((LANES, kw), lambda bb, h, t: (0, h)),
            pl.BlockSpec((1, kw), lambda bb, h, t: (0, h)),
            pl.BlockSpec((1, GLA_DV), lambda bb, h, t: (0, 0)),
        ],
        out_specs=(pl.BlockSpec((tb, vw), lambda bb, h, t: (rowblk(bb, t), h)),
                   pl.BlockSpec((1, GLA_HPS, GLA_DK, GLA_DV), lambda bb, h, t: (bb, h, 0, 0))),
        scratch_shapes=[
            pltpu.VMEM((GLA_HPS, GLA_DV, GLA_DK), F32),
            pltpu.VMEM((GLA_HPS, GLA_C, GLA_C), F32),
            pltpu.VMEM((GLA_HPS, GLA_C, GLA_DK), F32),
            pltpu.VMEM((GLA_HPS, GLA_C, GLA_DK), F32),
            pltpu.VMEM((tb, vw), F32),
        ],
        compiler_params=_params(("parallel", "parallel", "arbitrary")),
        name="gla_prompt",
    )(z, z, z, z, z, w2p, gbias, norm_w.reshape(1, GLA_DV))


def _merge_body(oa_ref, og_ref, ga_ref, gg_ref, wa_ref, wg_ref, o_ref):
    pa = jnp.dot(oa_ref[...], wa_ref[...], preferred_element_type=F32)
    pg = jnp.dot(og_ref[...], wg_ref[...], preferred_element_type=F32)
    o_ref[...] = (jax.nn.sigmoid(ga_ref[...]) * pa + jax.nn.sigmoid(gg_ref[...]) * pg).astype(BF16)


def _merge(o_attn, o_gla, z, wa, wg, tm):
    rows = o_attn.shape[0]
    d = wa.shape[1]
    return pl.pallas_call(
        _merge_body,
        out_shape=jax.ShapeDtypeStruct((rows, d), BF16),
        grid=(rows // tm,),
        in_specs=[
            pl.BlockSpec((tm, ATTN_WIDTH), lambda i: (i, 0)),
            pl.BlockSpec((tm, GLA_VAL_WIDTH), lambda i: (i, 0)),
            pl.BlockSpec((tm, d), lambda i: (i, C_GA // d)),
            pl.BlockSpec((tm, d), lambda i: (i, C_GA // d + 1)),
            pl.BlockSpec((ATTN_WIDTH, d), lambda i: (0, 0)),
            pl.BlockSpec((GLA_VAL_WIDTH, d), lambda i: (0, 0)),
        ],
        out_specs=pl.BlockSpec((tm, d), lambda i: (i, 0)),
        compiler_params=_params(("parallel",)),
        name="merge",
    )(o_attn, o_gla, z, z, wa, wg)


def _out_proj_body(m_ref, h_ref, w_ref, pw_ref, o_ref):
    y = jnp.dot(m_ref[...], w_ref[...], preferred_element_type=F32)
    o_ref[...] = h_ref[...] + _rms(y, pw_ref[...])


def _out_proj(merged, h, w_out, post_w, tm):
    rows, d = h.shape
    return pl.pallas_call(
        _out_proj_body,
        out_shape=jax.ShapeDtypeStruct((rows, d), F32),
        grid=(rows // tm,),
        in_specs=[
            pl.BlockSpec((tm, d), lambda i: (i, 0)),
            pl.BlockSpec((tm, d), lambda i: (i, 0)),
            pl.BlockSpec((d, d), lambda i: (0, 0)),
            pl.BlockSpec((1, d), lambda i: (0, 0)),
        ],
        out_specs=pl.BlockSpec((tm, d), lambda i: (i, 0)),
        compiler_params=_params(("parallel",)),
        name="out_proj",
    )(merged, h, w_out, post_w.reshape(1, d))


def _page_copies(pt_ref, hbm, buf, sem, seq, slot, n_pages, dst):
    return [pltpu.make_async_copy(hbm.at[pt_ref[seq, p]], buf.at[slot].at[dst(p)], sem)
            for p in range(n_pages)]


GATHER_SLOTS = 3


def _gather_step(fetch):
    b = pl.program_id(0)
    ahead = GATHER_SLOTS - 1

    @pl.when(b == 0)
    def _():
        for a in range(ahead):
            @pl.when(a < pl.num_programs(0))
            def _():
                for cp in fetch(a, a % GATHER_SLOTS):
                    cp.start()

    @pl.when(b + ahead < pl.num_programs(0))
    def _():
        for cp in fetch(b + ahead, (b + ahead) % GATHER_SLOTS):
            cp.start()

    slot = b % GATHER_SLOTS
    for cp in fetch(b, slot):
        cp.wait()
    return slot


def _sample_scores_body(pt_ref, qi_ref, wi_ref, kidx_hbm, o_ref, kbuf, sem, *, n_pages, page):
    def fetch(seq, slot):
        return _page_copies(pt_ref, kidx_hbm, kbuf, sem.at[slot], seq, slot, n_pages,
                            lambda p: (slice(None), pl.ds(p * page, page)))

    slot = _gather_step(fetch)
    s = jnp.dot(qi_ref[0].astype(BF16), kbuf[slot].astype(BF16), preferred_element_type=F32)
    o_ref[0] = jnp.sum(jnp.maximum(s, 0.0) * wi_ref[0], axis=0, keepdims=True)


def _sample_scores(page_table, qi3, wi3, kidx_t):
    nd, n_pages = page_table.shape
    page = kidx_t.shape[2]
    past = n_pages * page
    body = functools.partial(_sample_scores_body, n_pages=n_pages, page=page)
    grid_spec = pltpu.PrefetchScalarGridSpec(
        num_scalar_prefetch=1,
        grid=(nd,),
        in_specs=[
            pl.BlockSpec((1, IDX_HEADS, IDX_DIM), lambda b, pt: (b, 0, 0)),
            pl.BlockSpec((1, IDX_HEADS, 1), lambda b, pt: (b, 0, 0)),
            pl.BlockSpec(memory_space=pl.ANY),
        ],
        out_specs=pl.BlockSpec((1, 1, past), lambda b, pt: (b, 0, 0)),
        scratch_shapes=[pltpu.VMEM((GATHER_SLOTS, IDX_DIM, past), F32), pltpu.SemaphoreType.DMA((GATHER_SLOTS,))],
    )
    return pl.pallas_call(
        body,
        out_shape=jax.ShapeDtypeStruct((nd, 1, past), F32),
        grid_spec=grid_spec,
        compiler_params=_params(("arbitrary",)),
        name="sample_scores",
    )(page_table, qi3, wi3, kidx_t)


def _sample_select_body(sp_ref, qi_ref, misc_ref, bp_ref, bs_ref, sc_scr, j_scr, *, past, tk, topk, idx_bits):
    rows = sp_ref.shape[0]
    misc = misc_ref[...]
    d_in = lax.broadcasted_iota(I32, (LANES, QI_WIDTH), 0)
    c_out = lax.broadcasted_iota(I32, (LANES, QI_WIDTH), 1)
    rep = jnp.where((d_in < IDX_DIM) & (c_out % IDX_DIM == d_in), 1.0, 0.0)
    ki_t = jnp.dot(misc, rep, precision=HIGHEST, preferred_element_type=F32)
    c_in = lax.broadcasted_iota(I32, (QI_WIDTH, LANES), 0)
    l_out = lax.broadcasted_iota(I32, (QI_WIDTH, LANES), 1)
    seg = jnp.where(l_out == MISC_WI + c_in // IDX_DIM, 1.0, 0.0)
    hd = jnp.dot(qi_ref[...] * ki_t, seg, precision=HIGHEST, preferred_element_type=F32)
    lane = lax.broadcasted_iota(I32, (rows, LANES), 1)
    is_wi = (lane >= MISC_WI) & (lane < MISC_GLR)
    s_self = jnp.sum(jnp.where(is_wi, jnp.maximum(hd, 0.0) * misc, 0.0), axis=1, keepdims=True)

    sc_scr[:, 0:past] = sp_ref[...]
    sc_scr[:, past:] = jnp.where(lax.broadcasted_iota(I32, (rows, tk), 1) == 0, s_self, -jnp.inf)
    nk = sc_scr.shape[1] // tk
    thr = _select_threshold(sc_scr, j_scr, nk, tk, topk, idx_bits, 1)
    jb = j_scr[...]
    kpos = lax.broadcasted_iota(I32, sc_scr.shape, 1)
    bias = jnp.where(_selected(sc_scr[...], kpos, thr, jb), 0.0, NEG)
    bp_ref[...] = bias[:, 0:past]
    bs_ref[...] = bias[:, past:past + LANES]


def _sample_select(s_past, qi2, misc, topk):
    nd, past = s_past.shape
    tk = LANES
    width = past + tk
    idx_bits = max(1, (width - 1).bit_length()) + 1
    body = functools.partial(_sample_select_body, past=past, tk=tk, topk=topk, idx_bits=idx_bits)
    return pl.pallas_call(
        body,
        out_shape=(jax.ShapeDtypeStruct((nd, past), F32), jax.ShapeDtypeStruct((nd, LANES), F32)),
        grid=(1,),
        in_specs=[
            pl.BlockSpec((nd, past), lambda i: (0, 0)),
            pl.BlockSpec((nd, QI_WIDTH), lambda i: (0, 0)),
            pl.BlockSpec((nd, LANES), lambda i: (0, 0)),
        ],
        out_specs=(pl.BlockSpec((nd, past), lambda i: (0, 0)), pl.BlockSpec((nd, LANES), lambda i: (0, 0))),
        scratch_shapes=[pltpu.VMEM((nd, width), F32), pltpu.VMEM((nd, 1), I32)],
        compiler_params=_params(("arbitrary",)),
        name="sample_select",
    )(s_past, qi2, misc)


SAMPLE_KCHUNK = 1024


def _sample_attn_body(pt_ref, q_ref, ks_ref, vs_ref, b2_ref, bs_ref, ck_hbm, cv_hbm, o_ref, kbuf, vbuf, sem,
                      *, n_pages, prows):
    def fetch(seq, slot):
        dst = lambda p: (pl.ds(p * prows, prows), slice(None))
        return (_page_copies(pt_ref, ck_hbm, kbuf, sem.at[0, slot], seq, slot, n_pages, dst)
                + _page_copies(pt_ref, cv_hbm, vbuf, sem.at[1, slot], seq, slot, n_pages, dst))

    slot = _gather_step(fetch)
    q = q_ref[0]
    qb = q.astype(BF16)
    total = n_pages * prows
    ch = min(SAMPLE_KCHUNK, total)
    head_grp = lax.broadcasted_iota(I32, (N_HEADS, ch), 0) // HEADS_PER_KV
    row_grp = lax.broadcasted_iota(I32, (N_HEADS, ch), 1) % N_KV_HEADS
    own = head_grp == row_grp
    s_chunks = []
    for c in range(total // ch):
        kc = kbuf[slot, c * ch:(c + 1) * ch, :].astype(BF16)
        s_chunks.append(jnp.where(own, _dot_nt(qb, kc) + b2_ref[0][:, c * ch:(c + 1) * ch], NEG))

    grp0 = lax.broadcasted_iota(I32, (N_HEADS, HEAD_DIM), 0) < HEADS_PER_KV
    k_self = jnp.where(grp0, ks_ref[0][:, 0:HEAD_DIM], ks_ref[0][:, HEAD_DIM:])
    v_self = jnp.where(grp0, vs_ref[0][:, 0:HEAD_DIM], vs_ref[0][:, HEAD_DIM:])
    s_self = jnp.sum(q * k_self, axis=1, keepdims=True) + bs_ref[0][:, 0:1]

    m = s_self
    for s in s_chunks:
        m = jnp.maximum(m, jnp.max(s, axis=1, keepdims=True))
    p_self = jnp.exp(s_self - m)
    l = p_self
    acc = p_self * v_self
    for c, s in enumerate(s_chunks):
        p = jnp.exp(s - m)
        l = l + jnp.sum(p, axis=1, keepdims=True)
        vc = vbuf[slot, c * ch:(c + 1) * ch, :].astype(BF16)
        acc = acc + jnp.dot(p.astype(BF16), vc, preferred_element_type=F32)
    o_ref[0] = (acc / l).astype(BF16)


def _sample_attn(page_table, q3, k_self, v_self, bias2, bias_self, ck2, cv2):
    nd, n_pages = page_table.shape
    prows = ck2.shape[1]
    total = n_pages * prows
    body = functools.partial(_sample_attn_body, n_pages=n_pages, prows=prows)
    grid_spec = pltpu.PrefetchScalarGridSpec(
        num_scalar_prefetch=1,
        grid=(nd,),
        in_specs=[
            pl.BlockSpec((1, N_HEADS, HEAD_DIM), lambda b, pt: (b, 0, 0)),
            pl.BlockSpec((1, 1, KV_WIDTH), lambda b, pt: (b, 0, 0)),
            pl.BlockSpec((1, 1, KV_WIDTH), lambda b, pt: (b, 0, 0)),
            pl.BlockSpec((1, 1, total), lambda b, pt: (b, 0, 0)),
            pl.BlockSpec((1, 1, LANES), lambda b, pt: (b, 0, 0)),
            pl.BlockSpec(memory_space=pl.ANY),
            pl.BlockSpec(memory_space=pl.ANY),
        ],
        out_specs=pl.BlockSpec((1, N_HEADS, HEAD_DIM), lambda b, pt: (b, 0, 0)),
        scratch_shapes=[
            pltpu.VMEM((GATHER_SLOTS, total, HEAD_DIM), F32),
            pltpu.VMEM((GATHER_SLOTS, total, HEAD_DIM), F32),
            pltpu.SemaphoreType.DMA((2, GATHER_SLOTS)),
        ],
    )
    return pl.pallas_call(
        body,
        out_shape=jax.ShapeDtypeStruct((nd, N_HEADS, HEAD_DIM), BF16),
        grid_spec=grid_spec,
        compiler_params=_params(("arbitrary",)),
        name="sample_attn",
    )(page_table, q3, k_self, v_self, bias2, bias_self, ck2, cv2)


def _gla_sample_body(gq_ref, gk_ref, gv_ref, misc_ref, gr_ref, s_ref, w2_ref, gb_ref, nw_ref, og_ref, so_ref):
    eye = jnp.where(lax.broadcasted_iota(I32, (GLA_DK, GLA_DK), 0)
                    == lax.broadcasted_iota(I32, (GLA_DK, GLA_DK), 1), 1.0, 0.0)

    def column(row):
        return jnp.sum(eye * row, axis=1, keepdims=True)

    for sq in range(s_ref.shape[0]):
        misc = misc_ref[sq]
        for h in range(GLA_HEADS):
            ksl = slice(h * GLA_DK, (h + 1) * GLA_DK)
            vsl = slice(h * GLA_DV, (h + 1) * GLA_DV)
            g = _log_decay(misc, w2_ref[:, ksl], gb_ref[:, ksl])
            s_new = column(jnp.exp(g)) * s_ref[sq, h] + column(gk_ref[sq][:, ksl]) * gv_ref[sq][:, vsl]
            so_ref[sq, h] = s_new
            o = jnp.sum(column(gq_ref[sq][:, ksl] * (GLA_DK ** -0.5)) * s_new, axis=0, keepdims=True)
            og_ref[sq, :, vsl] = _gla_out(o, nw_ref[...], gr_ref[sq][:, vsl])


def _gla_sample(gq, gk, gv, misc, gr, state, w2p, gbias, norm_w):
    nd = state.shape[0]
    sb = 4 if nd % 4 == 0 else 1

    def row3(w):
        return pl.BlockSpec((sb, 1, w), lambda b: (b, 0, 0))

    st_spec = pl.BlockSpec((sb, GLA_HEADS, GLA_DK, GLA_DV), lambda b: (b, 0, 0, 0))
    return pl.pallas_call(
        _gla_sample_body,
        out_shape=(jax.ShapeDtypeStruct((nd, 1, GLA_VAL_WIDTH), BF16),
                   jax.ShapeDtypeStruct(state.shape, F32)),
        grid=(nd // sb,),
        in_specs=[
            row3(GLA_KEY_WIDTH), row3(GLA_KEY_WIDTH), row3(GLA_VAL_WIDTH), row3(LANES), row3(GLA_VAL_WIDTH),
            st_spec,
            pl.BlockSpec((LANES, GLA_KEY_WIDTH), lambda b: (0, 0)),
            pl.BlockSpec((1, GLA_KEY_WIDTH), lambda b: (0, 0)),
            pl.BlockSpec((1, GLA_DV), lambda b: (0, 0)),
        ],
        out_specs=(row3(GLA_VAL_WIDTH), st_spec),
        compiler_params=_params(("parallel",)),
        name="gla_sample",
    )(gq, gk, gv, misc, gr, state, w2p, gbias, norm_w.reshape(1, GLA_DV))


def _rope_tables(pos, d):
    inv = ROPE_THETA ** (-jnp.arange(0, d, 2, dtype=F32) / d)
    ang = pos.astype(F32)[:, None] * inv[None, :]
    cos, sin = jnp.cos(ang), jnp.sin(ang)
    reps = LANES // d
    return (jnp.tile(jnp.concatenate([cos, cos], axis=-1), (1, reps)),
            jnp.tile(jnp.concatenate([-sin, sin], axis=-1), (1, reps)))


def _permute_w_in(w_in, d):
    head = C_MISC + MISC_GLR
    gla_end = head + 2 * GLA_KEY_WIDTH + GLA_VAL_WIDTH
    w = jnp.zeros((w_in.shape[0], C_GA + 2 * d), BF16)
    for src0, src1, dst in ((0, head, 0), (gla_end, gla_end + GLA_GATE_RANK, head), (head, gla_end, C_GQ),
                            (gla_end + GLA_GATE_RANK, w_in.shape[1], C_GR)):
        w = lax.dynamic_update_slice(w, w_in[:, src0:src1].astype(BF16), (0, dst))
    return w


def kernel(x_prompt, x_sample, cache_k, cache_v, cache_kidx, page_table, state_gla,
           ffn1_pre_w, ffn1_w_gate, ffn1_w_up, ffn1_w_down, ffn1_post_w,
           mix_pre_w, w_in, gla_gate_w2, gla_gate_b, gla_norm_w,
           w_proj_attn, w_proj_gla, w_out, mix_post_w,
           ffn2_pre_w, ffn2_w_gate, ffn2_w_up, ffn2_w_down, ffn2_post_w):
    b, s, d = x_prompt.shape
    nd, td, _ = x_sample.shape
    n_pool, page = cache_k.shape[:2]
    n_pages = page_table.shape[1]
    past = n_pages * page
    assert td == 1 and s % Q_BLOCK == 0 and d % MIX_TN == 0 and C_GA % d == 0

    w_perm = _permute_w_in(w_in, d)
    wpa, wpg, wo = w_proj_attn.astype(BF16), w_proj_gla.astype(BF16), w_out.astype(BF16)
    w2p = jnp.zeros((LANES, GLA_KEY_WIDTH), F32).at[MISC_GLR:MISC_GLR + GLA_GATE_RANK].set(gla_gate_w2)
    gbias = gla_gate_b.reshape(1, GLA_KEY_WIDTH)

    def trunk(x, tm, tm_mix, tabs, tab_rows, mixer, f1, f2):
        h = _ffn(x, ffn1_pre_w, *f1, ffn1_post_w, tm)
        h, f1b = h if isinstance(h, tuple) else (h, f1)
        z, k_rows, v_rows = _mix_in(h, mix_pre_w, w_perm, tabs, tm_mix, tab_rows // tm_mix)
        o_attn, o_gla, s_fin = mixer(z)
        tmm = min(tm, 512)
        merged = _merge(o_attn, o_gla, z, wpa, wpg, tmm)
        h = _out_proj(merged, h, wo, mix_post_w, tmm)
        y = _ffn(h, ffn2_pre_w, *f2, ffn2_post_w, tm)
        y, f2b = y if isinstance(y, tuple) else (y, f2)
        return y, (k_rows, v_rows, z[:, C_MISC:C_MISC + IDX_DIM]), s_fin, f1b, f2b

    pos_p = jnp.arange(s, dtype=I32)
    tm_p = 512 if s % 512 == 0 else Q_BLOCK
    tabs_p = _rope_tables(pos_p, HEAD_DIM) + _rope_tables(pos_p, IDX_DIM)
    top_p = min(TOPK_MAX, s // 4)

    def mixer_p(z):
        o_attn = _attn_prompt(z, b, s, top_p)
        o_gla, s_fin = _gla_prompt(z, w2p, gbias, gla_norm_w, b, s)
        return o_attn, o_gla, s_fin

    tm_mix = 1024 if s % 1024 == 0 else tm_p

    pos_s = jnp.full((nd,), past, I32)
    tabs_s = _rope_tables(pos_s, HEAD_DIM) + _rope_tables(pos_s, IDX_DIM)
    top_s = min(TOPK_MAX, (past + td) // 4)
    ck2 = cache_k.reshape(n_pool, page * N_KV_HEADS, HEAD_DIM)
    cv2 = cache_v.reshape(n_pool, page * N_KV_HEADS, HEAD_DIM)
    kidx_t = jnp.swapaxes(cache_kidx, 1, 2)

    def mixer_s(z):
        misc = z[:, C_MISC:C_MISC + LANES]
        qi2 = z[:, C_QI:C_QI + QI_WIDTH]
        s_past = _sample_scores(page_table, qi2.reshape(nd, IDX_HEADS, IDX_DIM),
                                misc[:, MISC_WI:MISC_GLR].reshape(nd, IDX_HEADS, 1), kidx_t)
        bias_past, bias_self = _sample_select(s_past.reshape(nd, past), qi2, misc, top_s)
        bias2 = jnp.repeat(bias_past, N_KV_HEADS, axis=1).reshape(nd, 1, past * N_KV_HEADS)
        o_attn = _sample_attn(page_table, z[:, C_Q:C_Q + ATTN_WIDTH].reshape(nd, N_HEADS, HEAD_DIM),
                              z[:, C_K:C_K + KV_WIDTH].reshape(nd, 1, KV_WIDTH),
                              z[:, C_V:C_V + KV_WIDTH].reshape(nd, 1, KV_WIDTH),
                              bias2, bias_self.reshape(nd, 1, LANES), ck2, cv2)
        o_gla, s_fin = _gla_sample(z[:, C_GQ:C_GQ + GLA_KEY_WIDTH].reshape(nd, 1, GLA_KEY_WIDTH),
                                   z[:, C_GK:C_GK + GLA_KEY_WIDTH].reshape(nd, 1, GLA_KEY_WIDTH),
                                   z[:, C_GV:C_GV + GLA_VAL_WIDTH].reshape(nd, 1, GLA_VAL_WIDTH),
                                   misc.reshape(nd, 1, LANES),
                                   z[:, C_GR:C_GR + GLA_VAL_WIDTH].reshape(nd, 1, GLA_VAL_WIDTH),
                                   state_gla, w2p, gbias, gla_norm_w)
        return o_attn.reshape(nd, ATTN_WIDTH), o_gla.reshape(nd, GLA_VAL_WIDTH), s_fin

    y_s, kv_s, gla_s, f1b, f2b = trunk(x_sample.reshape(nd, d), nd, nd, tabs_s, nd, mixer_s,
                                       (ffn1_w_gate, ffn1_w_up, ffn1_w_down), (ffn2_w_gate, ffn2_w_up, ffn2_w_down))
    y_p, kv_p, gla_p, _, _ = trunk(x_prompt.reshape(b * s, d), tm_p, tm_mix, tabs_p, s, mixer_p, f1b, f2b)

    def kv_out(kv, n, t):
        k_rows, v_rows, ki = kv
        return (k_rows.reshape(n, t, N_KV_HEADS, HEAD_DIM), v_rows.reshape(n, t, N_KV_HEADS, HEAD_DIM),
                ki.reshape(n, t, IDX_DIM))

    k_p, v_p, ki_p = kv_out(kv_p, b, s)
    k_s, v_s, ki_s = kv_out(kv_s, nd, td)
    return (y_p.reshape(b, s, d), y_s.reshape(nd, td, d), k_p, v_p, ki_p, gla_p, k_s, v_s, ki_s, gla_s)
```

```python
import functools

import jax
import jax.numpy as jnp
from jax import lax
from jax.experimental import pallas as pl
from jax.experimental.pallas import tpu as pltpu

F32, BF16, I32 = jnp.float32, jnp.bfloat16, jnp.int32
HIGHEST = lax.Precision.HIGHEST

N_HEADS = 8
N_KV_HEADS = 2
HEAD_DIM = 128
IDX_HEADS = 16
IDX_DIM = 64
TOPK_MAX = 256
Q_BLOCK = 128
ROPE_THETA = 10000.0
GLA_HEADS = 4
GLA_DK = 128
GLA_DV = 256
GLA_GATE_RANK = 16
GLA_GATE_TAU = 16.0
RMS_EPS = 1e-6

LANES = 128
ATTN_WIDTH = N_HEADS * HEAD_DIM
KV_WIDTH = N_KV_HEADS * HEAD_DIM
QI_WIDTH = IDX_HEADS * IDX_DIM
GLA_KEY_WIDTH = GLA_HEADS * GLA_DK
GLA_VAL_WIDTH = GLA_HEADS * GLA_DV
HEADS_PER_KV = N_HEADS // N_KV_HEADS

C_Q, C_K, C_V, C_QI, C_MISC = 0, 1024, 1280, 1536, 2560
C_GQ, C_GK, C_GV, C_GR, C_GA = 3072, 3584, 4096, 5120, 6144
MISC_WI, MISC_GLR = IDX_DIM, IDX_DIM + IDX_HEADS
MIX_TN = 512

NEG = -1e30
SOFTMAX_MIN_DENOM = 2.0 ** -60
INT_MIN = -2 ** 31
NEG_INF_KEY = -2139095041
VMEM_LIMIT = 56 * 1024 * 1024


def _params(sem, vmem=VMEM_LIMIT):
    return pltpu.CompilerParams(dimension_semantics=sem, vmem_limit_bytes=vmem)


def _rms(x, w):
    return x * lax.rsqrt(jnp.mean(x * x, axis=-1, keepdims=True) + RMS_EPS) * w


def _dot_nt(a, b):
    return lax.dot_general(a, b, (((1,), (1,)), ((), ())), preferred_element_type=F32)


def _ffn_body(x_ref, prew_ref, wg_ref, wu_ref, wd_ref, postw_ref, o_ref, *rest):
    *wb_refs, z_scr, acc_scr = rest
    j = pl.program_id(1)

    @pl.when(j == 0)
    def _():
        z_scr[...] = _rms(x_ref[...], prew_ref[...]).astype(BF16)
        acc_scr[...] = jnp.zeros_like(acc_scr)

    wg, wu, wd = wg_ref[...].astype(BF16), wu_ref[...].astype(BF16), wd_ref[...].astype(BF16)
    for ref, w in zip(wb_refs, (wg, wu, wd)):
        ref[...] = w
    z = z_scr[...]
    g = jnp.dot(z, wg, preferred_element_type=F32)
    u = jnp.dot(z, wu, preferred_element_type=F32)
    a = (g * jax.nn.sigmoid(g) * u).astype(BF16)
    acc_scr[...] += jnp.dot(a, wd, preferred_element_type=F32)

    @pl.when(j == pl.num_programs(1) - 1)
    def _():
        o_ref[...] = x_ref[...] + 0.5 * _rms(acc_scr[...], postw_ref[...])


def _ffn(x, pre_w, wg, wu, wd, post_w, tm):
    rows, d = x.shape
    dff = wg.shape[1]
    tf = 512 if dff % 512 == 0 else dff
    emit = wg.dtype == F32
    assert not emit or rows == tm
    w_specs = [
        pl.BlockSpec((d, tf), lambda i, j: (0, j)),
        pl.BlockSpec((d, tf), lambda i, j: (0, j)),
        pl.BlockSpec((tf, d), lambda i, j: (j, 0)),
    ]
    y_shape = jax.ShapeDtypeStruct((rows, d), F32)
    y_spec = pl.BlockSpec((tm, d), lambda i, j: (i, 0))
    out = pl.pallas_call(
        _ffn_body,
        out_shape=(y_shape, *(jax.ShapeDtypeStruct(w.shape, BF16) for w in (wg, wu, wd))) if emit else y_shape,
        grid=(rows // tm, dff // tf),
        in_specs=[
            pl.BlockSpec((tm, d), lambda i, j: (i, 0)),
            pl.BlockSpec((1, d), lambda i, j: (0, 0)),
            *w_specs,
            pl.BlockSpec((1, d), lambda i, j: (0, 0)),
        ],
        out_specs=(y_spec, *w_specs) if emit else y_spec,
        scratch_shapes=[pltpu.VMEM((tm, d), BF16), pltpu.VMEM((tm, d), F32)],
        compiler_params=_params(("parallel", "arbitrary")),
        name="ffn",
    )(x, pre_w.reshape(1, d), wg, wu, wd, post_w.reshape(1, d))
    return (out[0], tuple(out[1:])) if emit else out


def _mix_in_body(x_ref, prew_ref, wt_ref, wglr_ref, c128_ref, s128_ref, c64_ref, s64_ref, o_ref, kf_ref, vf_ref,
                 u_scr):
    j = pl.program_id(1)

    @pl.when(j == 0)
    def _():
        u_scr[...] = _rms(x_ref[...], prew_ref[...]).astype(BF16)

    o_ref[...] = _dot_nt(u_scr[...], wt_ref[...].astype(BF16))

    def rope128(x):
        return x * c128_ref[...] + pltpu.roll(x, HEAD_DIM // 2, 1) * s128_ref[...]

    def rope64(x):
        lane = lax.broadcasted_iota(I32, x.shape, 1)
        first = (lane % IDX_DIM) < (IDX_DIM // 2)
        rot = jnp.where(first, pltpu.roll(x, LANES - IDX_DIM // 2, 1), pltpu.roll(x, IDX_DIM // 2, 1))
        return x * c64_ref[...] + rot * s64_ref[...]

    def sl(t):
        return slice(t * LANES, (t + 1) * LANES)

    @pl.when(j < 2)
    def _():
        for t in range(4):
            o_ref[:, sl(t)] = rope128(o_ref[:, sl(t)]) * (HEAD_DIM ** -0.5)

    @pl.when(j == 2)
    def _():
        tm = o_ref.shape[0]
        for t in range(N_KV_HEADS):
            k = rope128(o_ref[:, sl(t)])
            o_ref[:, sl(t)] = k
            kf_ref[pl.ds(t, tm, stride=N_KV_HEADS), :] = k
            vf_ref[pl.ds(t, tm, stride=N_KV_HEADS), :] = o_ref[:, sl(N_KV_HEADS + t)]

    @pl.when((j == 3) | (j == 4))
    def _():
        for t in range(4):
            o_ref[:, sl(t)] = rope64(o_ref[:, sl(t)])

    @pl.when(j == 5)
    def _():
        x = o_ref[:, sl(0)]
        glr = _dot_nt(u_scr[...], wglr_ref[...].astype(BF16))
        lane = lax.broadcasted_iota(I32, x.shape, 1)
        wi_scale = IDX_HEADS ** -0.5 * IDX_DIM ** -0.5
        o_ref[:, sl(0)] = jnp.where(lane < MISC_WI, rope64(x), jnp.where(
            lane < MISC_GLR, x * wi_scale, jnp.where(lane < MISC_GLR + GLA_GATE_RANK, glr, 0.0)))
        o_ref[:, LANES:] = jnp.zeros((o_ref.shape[0], MIX_TN - LANES), F32)


def _mix_in(h, pre_w, w_t, tabs, tm, tab_blocks):
    rows, d = h.shape
    zw = C_GA + 2 * d
    head = C_MISC + MISC_GLR
    gla_end = head + 2 * GLA_KEY_WIDTH + GLA_VAL_WIDTH
    tail = gla_end + GLA_GATE_RANK
    assert w_t.shape == (tail + GLA_VAL_WIDTH + 2 * d, d) and gla_end >= MISC_GLR

    sub = 8
    assert head % sub == 0 and tail % sub == 0 and (gla_end - MISC_GLR) % sub == 0
    tn8 = MIX_TN // sub

    def src_row(i, j):
        return (sub * jnp.where(j <= C_MISC // MIX_TN, j * tn8,
                                jnp.where(j < C_GR // MIX_TN, head // sub + (j - C_GQ // MIX_TN) * tn8,
                                          tail // sub + (j - C_GR // MIX_TN) * tn8)), 0)

    c128, s128, c64, s64 = tabs
    tab_spec = pl.BlockSpec((tm, LANES), lambda i, j: (i % tab_blocks, 0))
    kv_shape = jax.ShapeDtypeStruct((rows * N_KV_HEADS, HEAD_DIM), F32)
    kv_spec = pl.BlockSpec((tm * N_KV_HEADS, HEAD_DIM), lambda i, j: (i, 0))
    return pl.pallas_call(
        _mix_in_body,
        out_shape=(jax.ShapeDtypeStruct((rows, zw), F32), kv_shape, kv_shape),
        grid=(rows // tm, zw // MIX_TN),
        in_specs=[
            pl.BlockSpec((tm, d), lambda i, j: (i, 0)),
            pl.BlockSpec((1, d), lambda i, j: (0, 0)),
            pl.BlockSpec((pl.Element(MIX_TN), pl.Element(d)), src_row),
            pl.BlockSpec((pl.Element(LANES), pl.Element(d)), lambda i, j: (gla_end - MISC_GLR, 0)),
            tab_spec, tab_spec, tab_spec, tab_spec,
        ],
        out_specs=(pl.BlockSpec((tm, MIX_TN), lambda i, j: (i, j)), kv_spec, kv_spec),
        scratch_shapes=[pltpu.VMEM((tm, d), BF16)],
        compiler_params=_params(("parallel", "arbitrary")),
        name="mix_in",
    )(h, pre_w.reshape(1, d), w_t, w_t, c128, s128, c64, s64)


def _key_to_float(key):
    key = jnp.maximum(key, NEG_INF_KEY)
    return lax.bitcast_convert_type(key ^ ((key >> 31) & jnp.int32(0x7FFFFFFF)), F32)


SORT_N = 16


def _sort_pairs(n):
    pairs, p = [], 1
    while p < n:
        k = p
        while k >= 1:
            for j in range(k % p, n - k, 2 * k):
                for i in range(min(k, n - j - k)):
                    if (i + j) // (2 * p) == (i + j + k) // (2 * p):
                        pairs.append((i + j, i + j + k))
            k //= 2
        p *= 2
    return pairs


def _sort_groups_desc(x):
    v = [x[j * 8:(j + 1) * 8, :] for j in range(SORT_N)]
    for i, j in _sort_pairs(SORT_N):
        v[i], v[j] = jnp.maximum(v[i], v[j]), jnp.minimum(v[i], v[j])
    return jnp.concatenate(v, axis=0)


def _count_ge_sorted(v, t):
    c8 = v[7] >= t
    c4 = jnp.where(c8, v[11], v[3]) >= t
    c2 = jnp.where(c8, jnp.where(c4, v[13], v[9]), jnp.where(c4, v[5], v[1])) >= t
    e = [jnp.where(c2, v[4 * a + 2], v[4 * a]) for a in range(4)]
    c1 = jnp.where(c8, jnp.where(c4, e[3], e[2]), jnp.where(c4, e[1], e[0])) >= t
    low = (jnp.where(c8, 8.0, 0.0) + jnp.where(c4, 4.0, 0.0)) + (jnp.where(c2, 2.0, 0.0) + jnp.where(c1, 1.0, 0.0))
    return jnp.where(v[15] >= t, 16.0, low)


def _select_threshold(sc_ref, j_ref, nk, tk, topk, idx_bits, key_axis, srt_ref=None):
    n_other = sc_ref.shape[1 - key_axis]
    vec = (n_other, 1) if key_axis == 1 else (1, n_other)
    step = LANES if key_axis == 1 else 8

    def count_ge(t):
        if srt_ref is None:
            return count(lambda blk, ks: blk >= t)
        tb = jnp.broadcast_to(t, (8, n_other))

        def body(c, acc):
            ks = pl.multiple_of(c * tk, tk)
            for g0 in range(0, tk, SORT_N * 8):
                v = [srt_ref[pl.ds(ks + g0 + j * 8, 8), :] for j in range(SORT_N)]
                acc = acc + _count_ge_sorted(v, tb)
            return acc
        return jnp.sum(lax.fori_loop(0, nk, body, jnp.zeros((8, n_other), F32)), axis=0, keepdims=True)

    def count(pred):
        def body(c, acc):
            ks = pl.multiple_of(c * tk, tk)
            blk = sc_ref[:, pl.ds(ks, tk)] if key_axis == 1 else sc_ref[pl.ds(ks, tk), :]
            m = jnp.where(pred(blk, ks), 1.0, 0.0)
            parts = [lax.slice_in_dim(m, a, a + step, axis=key_axis) for a in range(0, tk, step)]
            while len(parts) > 1:
                parts = [parts[a] + parts[a + 1] for a in range(0, len(parts), 2)]
            return acc + parts[0]
        acc0 = jnp.zeros((n_other, step) if key_axis == 1 else (step, n_other), F32)
        return jnp.sum(lax.fori_loop(0, nk, body, acc0), axis=key_axis, keepdims=True)

    def bit_body(bi, t):
        cand = t ^ lax.shift_left(jnp.int32(1), 31 - bi)
        return jnp.where(count_ge(_key_to_float(cand)) >= topk, cand, t)

    thr_key = lax.fori_loop(0, 32, bit_body, jnp.full(vec, INT_MIN, I32))
    thr = _key_to_float(thr_key)

    cnt_ge = count(lambda blk, ks: blk >= thr)
    cnt_gt = count(lambda blk, ks: blk > thr)
    need = topk - cnt_gt
    tie = jnp.where((cnt_ge > topk) & (thr_key > NEG_INF_KEY), 1.0, 0.0)
    j_ref[...] = jnp.full(vec, 1 << idx_bits, I32)

    @pl.when(jnp.max(tie) > 0.0)
    def _():
        def jbit(bi, jb):
            cand = jb + lax.shift_left(jnp.int32(1), idx_bits - 1 - bi)

            def pred(blk, ks):
                kpos = ks + lax.broadcasted_iota(I32, blk.shape, key_axis)
                return (blk == thr) & (kpos < cand)
            return jnp.where(count(pred) <= need, cand, jb)
        j_ref[...] = lax.fori_loop(0, idx_bits, jbit, jnp.zeros(vec, I32))

    return thr


def _selected(score, kpos, thr, jb):
    return (score > thr) | ((score == thr) & (kpos < jb))


def _attn_prompt_body(q_ref, qia_ref, qib_ref, misc_ref, k_ref, v_ref, kim_ref, o_ref,
                      sc_scr, srt_scr, j_scr, bias_scr, kb_scr, vb_scr, kib_scr, kn_scr, qi_scr, wt_scr, qs_scr,
                      mx_scr, l_scr, acc_scr, *, tk, topk, idx_bits):
    i = pl.program_id(1)
    tq = Q_BLOCK
    nk = (i * tq + tq + tk - 1) // tk
    n_tiles = tk // LANES
    sub = 256
    qpos_t = i * tq + lax.broadcasted_iota(I32, (1, tq), 1)

    @pl.when(i == 0)
    def _():
        def cast_chunk(c, kn):
            rs = pl.ds(pl.multiple_of(c * tk, tk), tk)
            kf = k_ref[rs, :]
            kb_scr[rs, :] = kf.astype(BF16)
            vb_scr[rs, :] = v_ref[rs, :].astype(BF16)
            kib_scr[rs, :] = kim_ref[rs, 0:IDX_DIM].astype(BF16)
            sq = kf * kf
            return tuple(
                jnp.maximum(kn[g], jnp.max(jnp.sum(sq[:, g * HEAD_DIM:(g + 1) * HEAD_DIM], axis=1, keepdims=True),
                                           axis=0, keepdims=True))
                for g in range(N_KV_HEADS))
        kn = lax.fori_loop(0, k_ref.shape[0] // tk, cast_chunk, (jnp.zeros((1, 1), F32),) * N_KV_HEADS)
        for g in range(N_KV_HEADS):
            kn_scr[g] = jnp.broadcast_to(kn[g], kn_scr.shape[1:])

    for h in range(IDX_HEADS):
        ref = qia_ref if h < IDX_HEADS // 2 else qib_ref
        hh = h % (IDX_HEADS // 2)
        qi_scr[h // 2, (h % 2) * tq:(h % 2 + 1) * tq, :] = ref[:, hh * IDX_DIM:(hh + 1) * IDX_DIM].astype(BF16)
    wt_scr[...] = misc_ref[...].T

    def score_chunk(c, carry):
        for s0 in range(0, tk, sub):
            ks = pl.multiple_of(c * tk, tk) + s0
            ki = kib_scr[pl.ds(ks, sub), :]
            acc = jnp.zeros((sub, tq), F32)
            for hp in range(IDX_HEADS // 2):
                r = jnp.maximum(_dot_nt(ki, qi_scr[hp]), 0.0)
                acc = (acc + r[:, 0:tq] * wt_scr[pl.ds(MISC_WI + 2 * hp, 1), :]
                       + r[:, tq:] * wt_scr[pl.ds(MISC_WI + 2 * hp + 1, 1), :])
            kpos = ks + lax.broadcasted_iota(I32, (sub, tq), 0)
            masked = jnp.where(kpos <= qpos_t, acc, -jnp.inf)
            sc_scr[pl.ds(ks, sub), :] = masked
            for g0 in range(0, sub, SORT_N * 8):
                srt_scr[pl.ds(ks + g0, SORT_N * 8), :] = _sort_groups_desc(masked[g0:g0 + SORT_N * 8, :])
        return carry

    lax.fori_loop(0, nk, score_chunk, 0)
    thr = _select_threshold(sc_scr, j_scr, nk, tk, topk, idx_bits, 0, srt_scr)
    jb = j_scr[...]

    for g in range(N_KV_HEADS):
        kn = jnp.sqrt(kn_scr[g][0:1, :])
        for r in range(HEADS_PER_KV):
            hd = g * HEADS_PER_KV + r
            qh = q_ref[:, hd * HEAD_DIM:(hd + 1) * HEAD_DIM]
            qs_scr[g, r * tq:(r + 1) * tq, :] = qh.astype(BF16)
            mx_scr[g, r * tq:(r + 1) * tq, :] = jnp.sqrt(jnp.sum(qh * qh, axis=1, keepdims=True)) * kn * 1.02
    l_scr[...] = jnp.zeros(l_scr.shape, F32)
    acc_scr[...] = jnp.zeros(acc_scr.shape, F32)

    def logits(g, ks, bias4):
        kc = kb_scr[pl.ds(ks, tk), g * HEAD_DIM:(g + 1) * HEAD_DIM]
        return _dot_nt(qs_scr[g], kc) + bias4

    def exp_chunk(ks, bias):
        bias4 = jnp.concatenate([bias] * HEADS_PER_KV, axis=0)
        for g in range(N_KV_HEADS):
            s = logits(g, ks, bias4)
            m = mx_scr[g]
            p = [jnp.exp(s[:, t * LANES:(t + 1) * LANES] - m) for t in range(n_tiles)]
            l_scr[g] += functools.reduce(lambda x, y: x + y, p)
            pb = jnp.concatenate(p, axis=1).astype(BF16)
            vc = vb_scr[pl.ds(ks, tk), g * HEAD_DIM:(g + 1) * HEAD_DIM]
            acc_scr[g] += jnp.dot(pb, vc, preferred_element_type=F32)

    def bound_pass(c, carry):
        ks = pl.multiple_of(c * tk, tk)
        kpos = ks + lax.broadcasted_iota(I32, (tk, tq), 0)
        sel = _selected(sc_scr[pl.ds(ks, tk), :], kpos, thr, jb) & (kpos <= qpos_t)
        bias = jnp.where(sel, 0.0, NEG).T
        bias_scr[:, pl.ds(ks, tk)] = bias
        exp_chunk(ks, bias)
        return carry

    lax.fori_loop(0, nk, bound_pass, 0)

    def denominators():
        for g in range(N_KV_HEADS):
            mx_scr[g] = jnp.broadcast_to(jnp.sum(l_scr[g], axis=1, keepdims=True), mx_scr.shape[1:])

    denominators()
    l_min = jnp.min(functools.reduce(jnp.minimum, [mx_scr[g] for g in range(N_KV_HEADS)]))

    @pl.when(l_min < SOFTMAX_MIN_DENOM)
    def _():
        mx_scr[...] = jnp.full(mx_scr.shape, NEG, F32)
        l_scr[...] = jnp.zeros(l_scr.shape, F32)
        acc_scr[...] = jnp.zeros(acc_scr.shape, F32)

        def max_pass(c, carry):
            ks = pl.multiple_of(c * tk, tk)
            bias4 = jnp.concatenate([bias_scr[:, pl.ds(ks, tk)]] * HEADS_PER_KV, axis=0)
            for g in range(N_KV_HEADS):
                s = logits(g, ks, bias4)
                m = mx_scr[g]
                for t in range(n_tiles):
                    m = jnp.maximum(m, s[:, t * LANES:(t + 1) * LANES])
                mx_scr[g] = m
            return carry

        lax.fori_loop(0, nk, max_pass, 0)
        for g in range(N_KV_HEADS):
            mx_scr[g] = jnp.broadcast_to(jnp.max(mx_scr[g], axis=1, keepdims=True), mx_scr.shape[1:])

        def exp_pass(c, carry):
            ks = pl.multiple_of(c * tk, tk)
            exp_chunk(ks, bias_scr[:, pl.ds(ks, tk)])
            return carry

        lax.fori_loop(0, nk, exp_pass, 0)
        denominators()

    for g in range(N_KV_HEADS):
        o = acc_scr[g] / mx_scr[g]
        for r in range(HEADS_PER_KV):
            hd = g * HEADS_PER_KV + r
            o_ref[:, hd * HEAD_DIM:(hd + 1) * HEAD_DIM] = o[r * tq:(r + 1) * tq, :].astype(BF16)


def _attn_prompt(z, b, s, topk):
    nq = s // Q_BLOCK
    tk = min(512, s)
    idx_bits = max(1, (s - 1).bit_length()) + 1
    body = functools.partial(_attn_prompt_body, tk=tk, topk=topk, idx_bits=idx_bits)
    rows = HEADS_PER_KV * Q_BLOCK
    return pl.pallas_call(
        body,
        out_shape=jax.ShapeDtypeStruct((b * s, ATTN_WIDTH), BF16),
        grid=(b, nq),
        in_specs=[
            pl.BlockSpec((Q_BLOCK, ATTN_WIDTH), lambda bb, i: (bb * nq + i, 0)),
            pl.BlockSpec((Q_BLOCK, QI_WIDTH // 2), lambda bb, i: (bb * nq + i, C_QI // (QI_WIDTH // 2))),
            pl.BlockSpec((Q_BLOCK, QI_WIDTH // 2), lambda bb, i: (bb * nq + i, C_QI // (QI_WIDTH // 2) + 1)),
            pl.BlockSpec((Q_BLOCK, LANES), lambda bb, i: (bb * nq + i, C_MISC // LANES)),
            pl.BlockSpec((s, KV_WIDTH), lambda bb, i: (bb, C_K // KV_WIDTH)),
            pl.BlockSpec((s, KV_WIDTH), lambda bb, i: (bb, C_V // KV_WIDTH)),
            pl.BlockSpec((s, LANES), lambda bb, i: (bb, C_MISC // LANES)),
        ],
        out_specs=pl.BlockSpec((Q_BLOCK, ATTN_WIDTH), lambda bb, i: (bb * nq + i, 0)),
        scratch_shapes=[
            pltpu.VMEM((s, Q_BLOCK), F32),
            pltpu.VMEM((s, Q_BLOCK), F32),
            pltpu.VMEM((1, Q_BLOCK), I32),
            pltpu.VMEM((Q_BLOCK, s), F32),
            pltpu.VMEM((s, KV_WIDTH), BF16),
            pltpu.VMEM((s, KV_WIDTH), BF16),
            pltpu.VMEM((s, IDX_DIM), BF16),
            pltpu.VMEM((N_KV_HEADS, 8, LANES), F32),
            pltpu.VMEM((IDX_HEADS // 2, 2 * Q_BLOCK, IDX_DIM), BF16),
            pltpu.VMEM((LANES, Q_BLOCK), F32),
            pltpu.VMEM((N_KV_HEADS, rows, HEAD_DIM), BF16),
            pltpu.VMEM((N_KV_HEADS, rows, LANES), F32),
            pltpu.VMEM((N_KV_HEADS, rows, LANES), F32),
            pltpu.VMEM((N_KV_HEADS, rows, HEAD_DIM), F32),
        ],
        compiler_params=_params(("parallel", "arbitrary")),
        name="attn_prompt",
    )(z, z, z, z, z, z, z)


def _log_decay(misc, w2, gb):
    x = jnp.dot(misc, w2, precision=HIGHEST, preferred_element_type=F32) + gb
    return (jnp.minimum(x, 0.0) - jnp.log1p(jnp.exp(-jnp.abs(x)))) * (1.0 / GLA_GATE_TAU)


def _gla_out(o, nw, gr):
    return (_rms(o, nw) * (gr * jax.nn.sigmoid(gr))).astype(BF16)


GLA_C = 128
GLA_SUB = 8
GLA_HPS = 4


def _gla_prompt_body(gq_ref, gk_ref, gv_ref, misc_ref, gr_ref, w2_ref, gb_ref, nw_ref,
                     og_ref, sfin_ref, st_scr, a_scr, b_scr, k_scr, o_scr, *, nchunk):
    t = pl.program_id(2)
    c_ = GLA_C

    @pl.when(t == 0)
    def _():
        st_scr[...] = jnp.zeros_like(st_scr)

    a_scr[...] = jnp.zeros_like(a_scr)
    row = lax.broadcasted_iota(I32, (c_, c_), 0)
    col = lax.broadcasted_iota(I32, (c_, c_), 1)
    tri = jnp.where(col <= row, 1.0, 0.0)
    sub_row = lax.broadcasted_iota(I32, (GLA_SUB, LANES), 0)
    sub_col = lax.broadcasted_iota(I32, (GLA_SUB, GLA_SUB), 1)

    def head_chunk(hh, r0):
        ksl = slice(hh * GLA_DK, (hh + 1) * GLA_DK)
        vsl = slice(hh * GLA_DV, (hh + 1) * GLA_DV)
        q = gq_ref[pl.ds(r0, c_), ksl] * (GLA_DK ** -0.5)
        k = gk_ref[pl.ds(r0, c_), ksl]
        v = gv_ref[pl.ds(r0, c_), vsl].astype(BF16)
        g = _log_decay(misc_ref[pl.ds(r0, c_), :], w2_ref[:, ksl], gb_ref[:, ksl])
        b = jnp.dot(tri, g, precision=HIGHEST, preferred_element_type=F32)
        b_scr[hh] = b
        k_scr[hh] = k
        st = st_scr[hh]
        o = _dot_nt((q * jnp.exp(b)).astype(BF16), st.astype(BF16))

        n = c_ // 2
        while n >= GLA_SUB:
            for rb in range(n, c_, 2 * n):
                bref = b[rb:rb + 1, :]
                qs = q[rb:rb + n, :] * jnp.exp(b[rb:rb + n, :] - bref)
                ks = k[rb - n:rb, :] * jnp.exp(bref - b[rb - n:rb, :])
                a_scr[hh, rb:rb + n, rb - n:rb] = _dot_nt(qs.astype(BF16), ks.astype(BF16))
            n //= 2
        for blk in range(c_ // GLA_SUB):
            lo = blk * GLA_SUB
            qb = q[lo:lo + GLA_SUB, :]
            bb = b[lo:lo + GLA_SUB, :]
            ad = jnp.zeros((GLA_SUB, GLA_SUB), F32)
            for jj in range(GLA_SUB):
                bj = b_scr[hh, pl.ds(lo + jj, 1), :]
                kj = k_scr[hh, pl.ds(lo + jj, 1), :]
                w = jnp.exp(jnp.where(sub_row >= jj, bb - bj, NEG))
                colj = jnp.sum(qb * kj * w, axis=1, keepdims=True)
                ad = jnp.where(sub_col == jj, colj, ad)
            a_scr[hh, lo:lo + GLA_SUB, lo:lo + GLA_SUB] = ad

        o = o + jnp.dot(a_scr[hh].astype(BF16), v, preferred_element_type=F32)
        o_scr[pl.ds(r0, c_), vsl] = o
        b_last = b[c_ - 1:c_, :]
        kd = (k * jnp.exp(b_last - b)).astype(BF16)
        kv = lax.dot_general(v, kd, (((0,), (0,)), ((), ())), preferred_element_type=F32)
        st_scr[hh] = st * jnp.exp(b_last) + kv

    def chunk(ci, carry):
        r0 = pl.multiple_of(ci * c_, c_)
        for hh in range(GLA_HPS):
            head_chunk(hh, r0)
        return carry

    lax.fori_loop(0, nchunk, chunk, 0)
    for hh in range(GLA_HPS):
        vsl = slice(hh * GLA_DV, (hh + 1) * GLA_DV)
        og_ref[:, vsl] = _gla_out(o_scr[:, vsl], nw_ref[...], gr_ref[:, vsl])

    @pl.when(t == pl.num_programs(2) - 1)
    def _():
        for hh in range(GLA_HPS):
            sfin_ref[0, hh] = st_scr[hh].T


def _gla_prompt(z, w2p, gbias, norm_w, b, s):
    tb = min(512, s)
    nt = s // tb
    body = functools.partial(_gla_prompt_body, nchunk=tb // GLA_C)

    def rowblk(bb, t):
        return bb * nt + t

    kw, vw = GLA_HPS * GLA_DK, GLA_HPS * GLA_DV
    return pl.pallas_call(
        body,
        out_shape=(jax.ShapeDtypeStruct((b * s, GLA_VAL_WIDTH), BF16),
                   jax.ShapeDtypeStruct((b, GLA_HEADS, GLA_DK, GLA_DV), F32)),
        grid=(b, GLA_HEADS // GLA_HPS, nt),
        in_specs=[
            pl.BlockSpec((tb, kw), lambda bb, h, t: (rowblk(bb, t), C_GQ // kw + h)),
            pl.BlockSpec((tb, kw), lambda bb, h, t: (rowblk(bb, t), C_GK // kw + h)),
            pl.BlockSpec((tb, vw), lambda bb, h, t: (rowblk(bb, t), C_GV // vw + h)),
            pl.BlockSpec((tb, LANES), lambda bb, h, t: (rowblk(bb, t), C_MISC // LANES)),
            pl.BlockSpec((tb, vw), lambda bb, h, t: (rowblk(bb, t), C_GR // vw + h)),
            pl.BlockSpec((LANES, kw), lambda bb, h, t: (0, h)),
            pl.BlockSpec((1, kw), lambda bb, h, t: (0, h)),
            pl.BlockSpec((1, GLA_DV), lambda bb, h, t: (0, 0)),
        ],
        out_specs=(pl.BlockSpec((tb, vw), lambda bb, h, t: (rowblk(bb, t), h)),
                   pl.BlockSpec((1, GLA_HPS, GLA_DK, GLA_DV), lambda bb, h, t: (bb, h, 0, 0))),
        scratch_shapes=[
            pltpu.VMEM((GLA_HPS, GLA_DV, GLA_DK), F32),
            pltpu.VMEM((GLA_HPS, GLA_C, GLA_C), F32),
            pltpu.VMEM((GLA_HPS, GLA_C, GLA_DK), F32),
            pltpu.VMEM((GLA_HPS, GLA_C, GLA_DK), F32),
            pltpu.VMEM((tb, vw), F32),
        ],
        compiler_params=_params(("parallel", "parallel", "arbitrary")),
        name="gla_prompt",
    )(z, z, z, z, z, w2p, gbias, norm_w.reshape(1, GLA_DV))


def _merge_body(oa_ref, og_ref, ga_ref, gg_ref, wa_ref, wg_ref, o_ref):
    pa = jnp.dot(oa_ref[...], wa_ref[...], preferred_element_type=F32)
    pg = jnp.dot(og_ref[...], wg_ref[...], preferred_element_type=F32)
    o_ref[...] = (jax.nn.sigmoid(ga_ref[...]) * pa + jax.nn.sigmoid(gg_ref[...]) * pg).astype(BF16)


def _merge(o_attn, o_gla, z, wa, wg, tm):
    rows = o_attn.shape[0]
    d = wa.shape[1]
    return pl.pallas_call(
        _merge_body,
        out_shape=jax.ShapeDtypeStruct((rows, d), BF16),
        grid=(rows // tm,),
        in_specs=[
            pl.BlockSpec((tm, ATTN_WIDTH), lambda i: (i, 0)),
            pl.BlockSpec((tm, GLA_VAL_WIDTH), lambda i: (i, 0)),
            pl.BlockSpec((tm, d), lambda i: (i, C_GA // d)),
            pl.BlockSpec((tm, d), lambda i: (i, C_GA // d + 1)),
            pl.BlockSpec((ATTN_WIDTH, d), lambda i: (0, 0)),
            pl.BlockSpec((GLA_VAL_WIDTH, d), lambda i: (0, 0)),
        ],
        out_specs=pl.BlockSpec((tm, d), lambda i: (i, 0)),
        compiler_params=_params(("parallel",)),
        name="merge",
    )(o_attn, o_gla, z, z, wa, wg)


def _out_proj_body(m_ref, h_ref, w_ref, pw_ref, o_ref):
    y = jnp.dot(m_ref[...], w_ref[...], preferred_element_type=F32)
    o_ref[...] = h_ref[...] + _rms(y, pw_ref[...])


def _out_proj(merged, h, w_out, post_w, tm):
    rows, d = h.shape
    return pl.pallas_call(
        _out_proj_body,
        out_shape=jax.ShapeDtypeStruct((rows, d), F32),
        grid=(rows // tm,),
        in_specs=[
            pl.BlockSpec((tm, d), lambda i: (i, 0)),
            pl.BlockSpec((tm, d), lambda i: (i, 0)),
            pl.BlockSpec((d, d), lambda i: (0, 0)),
            pl.BlockSpec((1, d), lambda i: (0, 0)),
        ],
        out_specs=pl.BlockSpec((tm, d), lambda i: (i, 0)),
        compiler_params=_params(("parallel",)),
        name="out_proj",
    )(merged, h, w_out, post_w.reshape(1, d))


def _page_copies(pt_ref, hbm, buf, sem, seq, slot, n_pages, dst):
    return [pltpu.make_async_copy(hbm.at[pt_ref[seq, p]], buf.at[slot].at[dst(p)], sem)
            for p in range(n_pages)]


GATHER_SLOTS = 3


def _gather_step(fetch):
    b = pl.program_id(0)
    ahead = GATHER_SLOTS - 1

    @pl.when(b == 0)
    def _():
        for a in range(ahead):
            @pl.when(a < pl.num_programs(0))
            def _():
                for cp in fetch(a, a % GATHER_SLOTS):
                    cp.start()

    @pl.when(b + ahead < pl.num_programs(0))
    def _():
        for cp in fetch(b + ahead, (b + ahead) % GATHER_SLOTS):
            cp.start()

    slot = b % GATHER_SLOTS
    for cp in fetch(b, slot):
        cp.wait()
    return slot


def _sample_scores_body(pt_ref, qi_ref, wi_ref, kidx_hbm, o_ref, kbuf, sem, *, n_pages, page):
    def fetch(seq, slot):
        return _page_copies(pt_ref, kidx_hbm, kbuf, sem.at[slot], seq, slot, n_pages,
                            lambda p: (slice(None), pl.ds(p * page, page)))

    slot = _gather_step(fetch)
    s = jnp.dot(qi_ref[0].astype(BF16), kbuf[slot].astype(BF16), preferred_element_type=F32)
    o_ref[0] = jnp.sum(jnp.maximum(s, 0.0) * wi_ref[0], axis=0, keepdims=True)


def _sample_scores(page_table, qi3, wi3, kidx_t):
    nd, n_pages = page_table.shape
    page = kidx_t.shape[2]
    past = n_pages * page
    body = functools.partial(_sample_scores_body, n_pages=n_pages, page=page)
    grid_spec = pltpu.PrefetchScalarGridSpec(
        num_scalar_prefetch=1,
        grid=(nd,),
        in_specs=[
            pl.BlockSpec((1, IDX_HEADS, IDX_DIM), lambda b, pt: (b, 0, 0)),
            pl.BlockSpec((1, IDX_HEADS, 1), lambda b, pt: (b, 0, 0)),
            pl.BlockSpec(memory_space=pl.ANY),
        ],
        out_specs=pl.BlockSpec((1, 1, past), lambda b, pt: (b, 0, 0)),
        scratch_shapes=[pltpu.VMEM((GATHER_SLOTS, IDX_DIM, past), F32), pltpu.SemaphoreType.DMA((GATHER_SLOTS,))],
    )
    return pl.pallas_call(
        body,
        out_shape=jax.ShapeDtypeStruct((nd, 1, past), F32),
        grid_spec=grid_spec,
        compiler_params=_params(("arbitrary",)),
        name="sample_scores",
    )(page_table, qi3, wi3, kidx_t)


def _sample_select_body(sp_ref, qi_ref, misc_ref, bp_ref, bs_ref, sc_scr, j_scr, *, past, tk, topk, idx_bits):
    rows = sp_ref.shape[0]
    misc = misc_ref[...]
    d_in = lax.broadcasted_iota(I32, (LANES, QI_WIDTH), 0)
    c_out = lax.broadcasted_iota(I32, (LANES, QI_WIDTH), 1)
    rep = jnp.where((d_in < IDX_DIM) & (c_out % IDX_DIM == d_in), 1.0, 0.0)
    ki_t = jnp.dot(misc, rep, precision=HIGHEST, preferred_element_type=F32)
    c_in = lax.broadcasted_iota(I32, (QI_WIDTH, LANES), 0)
    l_out = lax.broadcasted_iota(I32, (QI_WIDTH, LANES), 1)
    seg = jnp.where(l_out == MISC_WI + c_in // IDX_DIM, 1.0, 0.0)
    hd = jnp.dot(qi_ref[...] * ki_t, seg, precision=HIGHEST, preferred_element_type=F32)
    lane = lax.broadcasted_iota(I32, (rows, LANES), 1)
    is_wi = (lane >= MISC_WI) & (lane < MISC_GLR)
    s_self = jnp.sum(jnp.where(is_wi, jnp.maximum(hd, 0.0) * misc, 0.0), axis=1, keepdims=True)

    sc_scr[:, 0:past] = sp_ref[...]
    sc_scr[:, past:] = jnp.where(lax.broadcasted_iota(I32, (rows, tk), 1) == 0, s_self, -jnp.inf)
    nk = sc_scr.shape[1] // tk
    thr = _select_threshold(sc_scr, j_scr, nk, tk, topk, idx_bits, 1)
    jb = j_scr[...]
    kpos = lax.broadcasted_iota(I32, sc_scr.shape, 1)
    bias = jnp.where(_selected(sc_scr[...], kpos, thr, jb), 0.0, NEG)
    bp_ref[...] = bias[:, 0:past]
    bs_ref[...] = bias[:, past:past + LANES]


def _sample_select(s_past, qi2, misc, topk):
    nd, past = s_past.shape
    tk = LANES
    width = past + tk
    idx_bits = max(1, (width - 1).bit_length()) + 1
    body = functools.partial(_sample_select_body, past=past, tk=tk, topk=topk, idx_bits=idx_bits)
    return pl.pallas_call(
        body,
        out_shape=(jax.ShapeDtypeStruct((nd, past), F32), jax.ShapeDtypeStruct((nd, LANES), F32)),
        grid=(1,),
        in_specs=[
            pl.BlockSpec((nd, past), lambda i: (0, 0)),
            pl.BlockSpec((nd, QI_WIDTH), lambda i: (0, 0)),
            pl.BlockSpec((nd, LANES), lambda i: (0, 0)),
        ],
        out_specs=(pl.BlockSpec((nd, past), lambda i: (0, 0)), pl.BlockSpec((nd, LANES), lambda i: (0, 0))),
        scratch_shapes=[pltpu.VMEM((nd, width), F32), pltpu.VMEM((nd, 1), I32)],
        compiler_params=_params(("arbitrary",)),
        name="sample_select",
    )(s_past, qi2, misc)


SAMPLE_KCHUNK = 1024


def _sample_attn_body(pt_ref, q_ref, ks_ref, vs_ref, b2_ref, bs_ref, ck_hbm, cv_hbm, o_ref, kbuf, vbuf, sem,
                      *, n_pages, prows):
    def fetch(seq, slot):
        dst = lambda p: (pl.ds(p * prows, prows), slice(None))
        return (_page_copies(pt_ref, ck_hbm, kbuf, sem.at[0, slot], seq, slot, n_pages, dst)
                + _page_copies(pt_ref, cv_hbm, vbuf, sem.at[1, slot], seq, slot, n_pages, dst))

    slot = _gather_step(fetch)
    q = q_ref[0]
    qb = q.astype(BF16)
    total = n_pages * prows
    ch = min(SAMPLE_KCHUNK, total)
    head_grp = lax.broadcasted_iota(I32, (N_HEADS, ch), 0) // HEADS_PER_KV
    row_grp = lax.broadcasted_iota(I32, (N_HEADS, ch), 1) % N_KV_HEADS
    own = head_grp == row_grp
    s_chunks = []
    for c in range(total // ch):
        kc = kbuf[slot, c * ch:(c + 1) * ch, :].astype(BF16)
        s_chunks.append(jnp.where(own, _dot_nt(qb, kc) + b2_ref[0][:, c * ch:(c + 1) * ch], NEG))

    grp0 = lax.broadcasted_iota(I32, (N_HEADS, HEAD_DIM), 0) < HEADS_PER_KV
    k_self = jnp.where(grp0, ks_ref[0][:, 0:HEAD_DIM], ks_ref[0][:, HEAD_DIM:])
    v_self = jnp.where(grp0, vs_ref[0][:, 0:HEAD_DIM], vs_ref[0][:, HEAD_DIM:])
    s_self = jnp.sum(q * k_self, axis=1, keepdims=True) + bs_ref[0][:, 0:1]

    m = s_self
    for s in s_chunks:
        m = jnp.maximum(m, jnp.max(s, axis=1, keepdims=True))
    p_self = jnp.exp(s_self - m)
    l = p_self
    acc = p_self * v_self
    for c, s in enumerate(s_chunks):
        p = jnp.exp(s - m)
        l = l + jnp.sum(p, axis=1, keepdims=True)
        vc = vbuf[slot, c * ch:(c + 1) * ch, :].astype(BF16)
        acc = acc + jnp.dot(p.astype(BF16), vc, preferred_element_type=F32)
    o_ref[0] = (acc / l).astype(BF16)


def _sample_attn(page_table, q3, k_self, v_self, bias2, bias_self, ck2, cv2):
    nd, n_pages = page_table.shape
    prows = ck2.shape[1]
    total = n_pages * prows
    body = functools.partial(_sample_attn_body, n_pages=n_pages, prows=prows)
    grid_spec = pltpu.PrefetchScalarGridSpec(
        num_scalar_prefetch=1,
        grid=(nd,),
        in_specs=[
            pl.BlockSpec((1, N_HEADS, HEAD_DIM), lambda b, pt: (b, 0, 0)),
            pl.BlockSpec((1, 1, KV_WIDTH), lambda b, pt: (b, 0, 0)),
            pl.BlockSpec((1, 1, KV_WIDTH), lambda b, pt: (b, 0, 0)),
            pl.BlockSpec((1, 1, total), lambda b, pt: (b, 0, 0)),
            pl.BlockSpec((1, 1, LANES), lambda b, pt: (b, 0, 0)),
            pl.BlockSpec(memory_space=pl.ANY),
            pl.BlockSpec(memory_space=pl.ANY),
        ],
        out_specs=pl.BlockSpec((1, N_HEADS, HEAD_DIM), lambda b, pt: (b, 0, 0)),
        scratch_shapes=[
            pltpu.VMEM((GATHER_SLOTS, total, HEAD_DIM), F32),
            pltpu.VMEM((GATHER_SLOTS, total, HEAD_DIM), F32),
            pltpu.SemaphoreType.DMA((2, GATHER_SLOTS)),
        ],
    )
    return pl.pallas_call(
        body,
        out_shape=jax.ShapeDtypeStruct((nd, N_HEADS, HEAD_DIM), BF16),
        grid_spec=grid_spec,
        compiler_params=_params(("arbitrary",)),
        name="sample_attn",
    )(page_table, q3, k_self, v_self, bias2, bias_self, ck2, cv2)


def _gla_sample_body(gq_ref, gk_ref, gv_ref, misc_ref, gr_ref, s_ref, w2_ref, gb_ref, nw_ref, og_ref, so_ref):
    eye = jnp.where(lax.broadcasted_iota(I32, (GLA_DK, GLA_DK), 0)
                    == lax.broadcasted_iota(I32, (GLA_DK, GLA_DK), 1), 1.0, 0.0)

    def column(row):
        return jnp.sum(eye * row, axis=1, keepdims=True)

    for sq in range(s_ref.shape[0]):
        misc = misc_ref[sq]
        for h in range(GLA_HEADS):
            ksl = slice(h * GLA_DK, (h + 1) * GLA_DK)
            vsl = slice(h * GLA_DV, (h + 1) * GLA_DV)
            g = _log_decay(misc, w2_ref[:, ksl], gb_ref[:, ksl])
            s_new = column(jnp.exp(g)) * s_ref[sq, h] + column(gk_ref[sq][:, ksl]) * gv_ref[sq][:, vsl]
            so_ref[sq, h] = s_new
            o = jnp.sum(column(gq_ref[sq][:, ksl] * (GLA_DK ** -0.5)) * s_new, axis=0, keepdims=True)
            og_ref[sq, :, vsl] = _gla_out(o, nw_ref[...], gr_ref[sq][:, vsl])


def _gla_sample(gq, gk, gv, misc, gr, state, w2p, gbias, norm_w):
    nd = state.shape[0]
    sb = 4 if nd % 4 == 0 else 1

    def row3(w):
        return pl.BlockSpec((sb, 1, w), lambda b: (b, 0, 0))

    st_spec = pl.BlockSpec((sb, GLA_HEADS, GLA_DK, GLA_DV), lambda b: (b, 0, 0, 0))
    return pl.pallas_call(
        _gla_sample_body,
        out_shape=(jax.ShapeDtypeStruct((nd, 1, GLA_VAL_WIDTH), BF16),
                   jax.ShapeDtypeStruct(state.shape, F32)),
        grid=(nd // sb,),
        in_specs=[
            row3(GLA_KEY_WIDTH), row3(GLA_KEY_WIDTH), row3(GLA_VAL_WIDTH), row3(LANES), row3(GLA_VAL_WIDTH),
            st_spec,
            pl.BlockSpec((LANES, GLA_KEY_WIDTH), lambda b: (0, 0)),
            pl.BlockSpec((1, GLA_KEY_WIDTH), lambda b: (0, 0)),
            pl.BlockSpec((1, GLA_DV), lambda b: (0, 0)),
        ],
        out_specs=(row3(GLA_VAL_WIDTH), st_spec),
        compiler_params=_params(("parallel",)),
        name="gla_sample",
    )(gq, gk, gv, misc, gr, state, w2p, gbias, norm_w.reshape(1, GLA_DV))


def _rope_tables(pos, d):
    inv = ROPE_THETA ** (-jnp.arange(0, d, 2, dtype=F32) / d)
    ang = pos.astype(F32)[:, None] * inv[None, :]
    cos, sin = jnp.cos(ang), jnp.sin(ang)
    reps = LANES // d
    return (jnp.tile(jnp.concatenate([cos, cos], axis=-1), (1, reps)),
            jnp.tile(jnp.concatenate([-sin, sin], axis=-1), (1, reps)))


def kernel(x_prompt, x_sample, cache_k, cache_v, cache_kidx, page_table, state_gla,
           ffn1_pre_w, ffn1_w_gate, ffn1_w_up, ffn1_w_down, ffn1_post_w,
           mix_pre_w, w_in, gla_gate_w2, gla_gate_b, gla_norm_w,
           w_proj_attn, w_proj_gla, w_out, mix_post_w,
           ffn2_pre_w, ffn2_w_gate, ffn2_w_up, ffn2_w_down, ffn2_post_w):
    b, s, d = x_prompt.shape
    nd, td, _ = x_sample.shape
    n_pool, page = cache_k.shape[:2]
    n_pages = page_table.shape[1]
    past = n_pages * page
    assert td == 1 and s % Q_BLOCK == 0 and d % MIX_TN == 0 and C_GA % d == 0

    w_t = w_in.T
    wpa, wpg, wo = w_proj_attn.astype(BF16), w_proj_gla.astype(BF16), w_out.astype(BF16)
    w2p = jnp.zeros((LANES, GLA_KEY_WIDTH), F32).at[MISC_GLR:MISC_GLR + GLA_GATE_RANK].set(gla_gate_w2)
    gbias = gla_gate_b.reshape(1, GLA_KEY_WIDTH)

    def trunk(x, tm, tm_mix, tabs, tab_rows, mixer, f1, f2):
        h = _ffn(x, ffn1_pre_w, *f1, ffn1_post_w, tm)
        h, f1b = h if isinstance(h, tuple) else (h, f1)
        z, k_rows, v_rows = _mix_in(h, mix_pre_w, w_t, tabs, tm_mix, tab_rows // tm_mix)
        o_attn, o_gla, s_fin = mixer(z)
        tmm = min(tm, 512)
        merged = _merge(o_attn, o_gla, z, wpa, wpg, tmm)
        h = _out_proj(merged, h, wo, mix_post_w, tmm)
        y = _ffn(h, ffn2_pre_w, *f2, ffn2_post_w, tm)
        y, f2b = y if isinstance(y, tuple) else (y, f2)
        return y, (k_rows, v_rows, z[:, C_MISC:C_MISC + IDX_DIM]), s_fin, f1b, f2b

    pos_p = jnp.arange(s, dtype=I32)
    tm_p = 512 if s % 512 == 0 else Q_BLOCK
    tabs_p = _rope_tables(pos_p, HEAD_DIM) + _rope_tables(pos_p, IDX_DIM)
    top_p = min(TOPK_MAX, s // 4)

    def mixer_p(z):
        o_attn = _attn_prompt(z, b, s, top_p)
        o_gla, s_fin = _gla_prompt(z, w2p, gbias, gla_norm_w, b, s)
        return o_attn, o_gla, s_fin

    tm_mix = 1024 if s % 1024 == 0 else tm_p

    pos_s = jnp.full((nd,), past, I32)
    tabs_s = _rope_tables(pos_s, HEAD_DIM) + _rope_tables(pos_s, IDX_DIM)
    top_s = min(TOPK_MAX, (past + td) // 4)
    ck2 = cache_k.reshape(n_pool, page * N_KV_HEADS, HEAD_DIM)
    cv2 = cache_v.reshape(n_pool, page * N_KV_HEADS, HEAD_DIM)
    kidx_t = jnp.swapaxes(cache_kidx, 1, 2)

    def mixer_s(z):
        misc = z[:, C_MISC:C_MISC + LANES]
        qi2 = z[:, C_QI:C_QI + QI_WIDTH]
        s_past = _sample_scores(page_table, qi2.reshape(nd, IDX_HEADS, IDX_DIM),
                                misc[:, MISC_WI:MISC_GLR].reshape(nd, IDX_HEADS, 1), kidx_t)
        bias_past, bias_self = _sample_select(s_past.reshape(nd, past), qi2, misc, top_s)
        bias2 = jnp.repeat(bias_past, N_KV_HEADS, axis=1).reshape(nd, 1, past * N_KV_HEADS)
        o_attn = _sample_attn(page_table, z[:, C_Q:C_Q + ATTN_WIDTH].reshape(nd, N_HEADS, HEAD_DIM),
                              z[:, C_K:C_K + KV_WIDTH].reshape(nd, 1, KV_WIDTH),
                              z[:, C_V:C_V + KV_WIDTH].reshape(nd, 1, KV_WIDTH),
                              bias2, bias_self.reshape(nd, 1, LANES), ck2, cv2)
        o_gla, s_fin = _gla_sample(z[:, C_GQ:C_GQ + GLA_KEY_WIDTH].reshape(nd, 1, GLA_KEY_WIDTH),
                                   z[:, C_GK:C_GK + GLA_KEY_WIDTH].reshape(nd, 1, GLA_KEY_WIDTH),
                                   z[:, C_GV:C_GV + GLA_VAL_WIDTH].reshape(nd, 1, GLA_VAL_WIDTH),
                                   misc.reshape(nd, 1, LANES),
                                   z[:, C_GR:C_GR + GLA_VAL_WIDTH].reshape(nd, 1, GLA_VAL_WIDTH),
                                   state_gla, w2p, gbias, gla_norm_w)
        return o_attn.reshape(nd, ATTN_WIDTH), o_gla.reshape(nd, GLA_VAL_WIDTH), s_fin

    y_s, kv_s, gla_s, f1b, f2b = trunk(x_sample.reshape(nd, d), nd, nd, tabs_s, nd, mixer_s,
                                       (ffn1_w_gate, ffn1_w_up, ffn1_w_down), (ffn2_w_gate, ffn2_w_up, ffn2_w_down))
    y_p, kv_p, gla_p, _, _ = trunk(x_prompt.reshape(b * s, d), tm_p, tm_mix, tabs_p, s, mixer_p, f1b, f2b)

    def kv_out(kv, n, t):
        k_rows, v_rows, ki = kv
        return (k_rows.reshape(n, t, N_KV_HEADS, HEAD_DIM), v_rows.reshape(n, t, N_KV_HEADS, HEAD_DIM),
                ki.reshape(n, t, IDX_DIM))

    k_p, v_p, ki_p = kv_out(kv_p, b, s)
    k_s, v_s, ki_s = kv_out(kv_s, nd, td)
    return (y_p.reshape(b, s, d), y_s.reshape(nd, td, d), k_p, v_p, ki_p, gla_p, k_s, v_s, ki_s, gla_s)
```

```python
import functools

import jax
import jax.numpy as jnp
from jax import lax
from jax.experimental import pallas as pl
from jax.experimental.pallas import tpu as pltpu

F32, BF16, I32 = jnp.float32, jnp.bfloat16, jnp.int32
HIGHEST = lax.Precision.HIGHEST

N_HEADS = 8
N_KV_HEADS = 2
HEAD_DIM = 128
IDX_HEADS = 16
IDX_DIM = 64
TOPK_MAX = 256
Q_BLOCK = 128
ROPE_THETA = 10000.0
GLA_HEADS = 4
GLA_DK = 128
GLA_DV = 256
GLA_GATE_RANK = 16
GLA_GATE_TAU = 16.0
RMS_EPS = 1e-6

LANES = 128
ATTN_WIDTH = N_HEADS * HEAD_DIM
KV_WIDTH = N_KV_HEADS * HEAD_DIM
QI_WIDTH = IDX_HEADS * IDX_DIM
GLA_KEY_WIDTH = GLA_HEADS * GLA_DK
GLA_VAL_WIDTH = GLA_HEADS * GLA_DV
HEADS_PER_KV = N_HEADS // N_KV_HEADS

C_Q, C_K, C_V, C_QI, C_MISC = 0, 1024, 1280, 1536, 2560
C_GQ, C_GK, C_GV, C_GR, C_GA = 3072, 3584, 4096, 5120, 6144
MISC_WI, MISC_GLR = IDX_DIM, IDX_DIM + IDX_HEADS
MIX_TN = 512

NEG = -1e30
SOFTMAX_MIN_DENOM = 2.0 ** -60
INT_MIN = -2 ** 31
NEG_INF_KEY = -2139095041
VMEM_LIMIT = 56 * 1024 * 1024


def _params(sem, vmem=VMEM_LIMIT):
    return pltpu.CompilerParams(dimension_semantics=sem, vmem_limit_bytes=vmem)


def _rms(x, w):
    return x * lax.rsqrt(jnp.mean(x * x, axis=-1, keepdims=True) + RMS_EPS) * w


def _dot_nt(a, b):
    return lax.dot_general(a, b, (((1,), (1,)), ((), ())), preferred_element_type=F32)


def _ffn_body(x_ref, prew_ref, wg_ref, wu_ref, wd_ref, postw_ref, o_ref, *rest):
    *wb_refs, z_scr, acc_scr = rest
    j = pl.program_id(1)

    @pl.when(j == 0)
    def _():
        z_scr[...] = _rms(x_ref[...], prew_ref[...]).astype(BF16)
        acc_scr[...] = jnp.zeros_like(acc_scr)

    wg, wu, wd = wg_ref[...].astype(BF16), wu_ref[...].astype(BF16), wd_ref[...].astype(BF16)
    for ref, w in zip(wb_refs, (wg, wu, wd)):
        ref[...] = w
    z = z_scr[...]
    g = jnp.dot(z, wg, preferred_element_type=F32)
    u = jnp.dot(z, wu, preferred_element_type=F32)
    a = (g * jax.nn.sigmoid(g) * u).astype(BF16)
    acc_scr[...] += jnp.dot(a, wd, preferred_element_type=F32)

    @pl.when(j == pl.num_programs(1) - 1)
    def _():
        o_ref[...] = x_ref[...] + 0.5 * _rms(acc_scr[...], postw_ref[...])


def _ffn(x, pre_w, wg, wu, wd, post_w, tm):
    rows, d = x.shape
    dff = wg.shape[1]
    tf = 512 if dff % 512 == 0 else dff
    emit = wg.dtype == F32
    assert not emit or rows == tm
    w_specs = [
        pl.BlockSpec((d, tf), lambda i, j: (0, j)),
        pl.BlockSpec((d, tf), lambda i, j: (0, j)),
        pl.BlockSpec((tf, d), lambda i, j: (j, 0)),
    ]
    y_shape = jax.ShapeDtypeStruct((rows, d), F32)
    y_spec = pl.BlockSpec((tm, d), lambda i, j: (i, 0))
    out = pl.pallas_call(
        _ffn_body,
        out_shape=(y_shape, *(jax.ShapeDtypeStruct(w.shape, BF16) for w in (wg, wu, wd))) if emit else y_shape,
        grid=(rows // tm, dff // tf),
        in_specs=[
            pl.BlockSpec((tm, d), lambda i, j: (i, 0)),
            pl.BlockSpec((1, d), lambda i, j: (0, 0)),
            *w_specs,
            pl.BlockSpec((1, d), lambda i, j: (0, 0)),
        ],
        out_specs=(y_spec, *w_specs) if emit else y_spec,
        scratch_shapes=[pltpu.VMEM((tm, d), BF16), pltpu.VMEM((tm, d), F32)],
        compiler_params=_params(("parallel", "arbitrary")),
        name="ffn",
    )(x, pre_w.reshape(1, d), wg, wu, wd, post_w.reshape(1, d))
    return (out[0], tuple(out[1:])) if emit else out


def _mix_in_body(x_ref, prew_ref, wt_ref, wglr_ref, c128_ref, s128_ref, c64_ref, s64_ref, o_ref, kf_ref, vf_ref,
                 *rest):
    *wb_ref, u_scr = rest
    j = pl.program_id(1)

    @pl.when(j == 0)
    def _():
        u_scr[...] = _rms(x_ref[...], prew_ref[...]).astype(BF16)

    if wb_ref:
        w_ref = wb_ref[0]
        w_ref[...] = wt_ref[...].astype(BF16)

        @pl.when(j == C_MISC // MIX_TN)
        def _():
            row = lax.broadcasted_iota(I32, (LANES, w_ref.shape[1]), 0)
            w_ref[0:LANES, :] = jnp.where(row < MISC_GLR, w_ref[0:LANES, :], jnp.where(
                row < MISC_GLR + GLA_GATE_RANK, wglr_ref[...].astype(BF16), jnp.zeros((), BF16)))
            w_ref[LANES:, :] = jnp.zeros((MIX_TN - LANES, w_ref.shape[1]), BF16)
    else:
        w_ref = wt_ref

    o_ref[...] = _dot_nt(u_scr[...], w_ref[...])

    def rope128(x):
        return x * c128_ref[...] + pltpu.roll(x, HEAD_DIM // 2, 1) * s128_ref[...]

    def rope64(x):
        lane = lax.broadcasted_iota(I32, x.shape, 1)
        first = (lane % IDX_DIM) < (IDX_DIM // 2)
        rot = jnp.where(first, pltpu.roll(x, LANES - IDX_DIM // 2, 1), pltpu.roll(x, IDX_DIM // 2, 1))
        return x * c64_ref[...] + rot * s64_ref[...]

    def sl(t):
        return slice(t * LANES, (t + 1) * LANES)

    @pl.when(j < 2)
    def _():
        for t in range(4):
            o_ref[:, sl(t)] = rope128(o_ref[:, sl(t)]) * (HEAD_DIM ** -0.5)

    @pl.when(j == 2)
    def _():
        tm = o_ref.shape[0]
        for t in range(N_KV_HEADS):
            k = rope128(o_ref[:, sl(t)])
            o_ref[:, sl(t)] = k
            kf_ref[pl.ds(t, tm, stride=N_KV_HEADS), :] = k
            vf_ref[pl.ds(t, tm, stride=N_KV_HEADS), :] = o_ref[:, sl(N_KV_HEADS + t)]

    @pl.when((j == 3) | (j == 4))
    def _():
        for t in range(4):
            o_ref[:, sl(t)] = rope64(o_ref[:, sl(t)])

    @pl.when(j == 5)
    def _():
        x = o_ref[:, sl(0)]
        lane = lax.broadcasted_iota(I32, x.shape, 1)
        wi_scale = IDX_HEADS ** -0.5 * IDX_DIM ** -0.5
        o_ref[:, sl(0)] = jnp.where(lane < MISC_WI, rope64(x), jnp.where(lane < MISC_GLR, x * wi_scale, x))


def _mix_in(h, pre_w, w, tabs, tm, tab_blocks):
    rows, d = h.shape
    zw = C_GA + 2 * d
    emit = w.dtype == F32
    if emit:
        head = C_MISC + MISC_GLR
        gla_end = head + 2 * GLA_KEY_WIDTH + GLA_VAL_WIDTH
        tail = gla_end + GLA_GATE_RANK
        assert rows == tm and w.shape == (tail + GLA_VAL_WIDTH + 2 * d, d) and gla_end >= MISC_GLR
        sub = 8
        assert head % sub == 0 and tail % sub == 0 and (gla_end - MISC_GLR) % sub == 0
        tn8 = MIX_TN // sub

        def src_row(i, j):
            return (sub * jnp.where(j <= C_MISC // MIX_TN, j * tn8,
                                    jnp.where(j < C_GR // MIX_TN, head // sub + (j - C_GQ // MIX_TN) * tn8,
                                              tail // sub + (j - C_GR // MIX_TN) * tn8)), 0)

        w_specs = [pl.BlockSpec((pl.Element(MIX_TN), pl.Element(d)), src_row),
                   pl.BlockSpec((pl.Element(LANES), pl.Element(d)), lambda i, j: (gla_end - MISC_GLR, 0))]
    else:
        assert w.shape == (zw, d)
        w_specs = [pl.BlockSpec((MIX_TN, d), lambda i, j: (j, 0)), pl.BlockSpec((LANES, d), lambda i, j: (0, 0))]
    wb_shape = (jax.ShapeDtypeStruct((zw, d), BF16),) if emit else ()
    wb_spec = (pl.BlockSpec((MIX_TN, d), lambda i, j: (j, 0)),) if emit else ()

    c128, s128, c64, s64 = tabs
    tab_spec = pl.BlockSpec((tm, LANES), lambda i, j: (i % tab_blocks, 0))
    kv_shape = jax.ShapeDtypeStruct((rows * N_KV_HEADS, HEAD_DIM), F32)
    kv_spec = pl.BlockSpec((tm * N_KV_HEADS, HEAD_DIM), lambda i, j: (i, 0))
    return pl.pallas_call(
        _mix_in_body,
        out_shape=(jax.ShapeDtypeStruct((rows, zw), F32), kv_shape, kv_shape, *wb_shape),
        grid=(rows // tm, zw // MIX_TN),
        in_specs=[
            pl.BlockSpec((tm, d), lambda i, j: (i, 0)),
            pl.BlockSpec((1, d), lambda i, j: (0, 0)),
            *w_specs,
            tab_spec, tab_spec, tab_spec, tab_spec,
        ],
        out_specs=(pl.BlockSpec((tm, MIX_TN), lambda i, j: (i, j)), kv_spec, kv_spec, *wb_spec),
        scratch_shapes=[pltpu.VMEM((tm, d), BF16)],
        compiler_params=_params(("parallel", "arbitrary")),
        name="mix_in",
    )(h, pre_w.reshape(1, d), w, w, c128, s128, c64, s64)


def _key_to_float(key):
    key = jnp.maximum(key, NEG_INF_KEY)
    return lax.bitcast_convert_type(key ^ ((key >> 31) & jnp.int32(0x7FFFFFFF)), F32)


SORT_N = 16


def _sort_pairs(n):
    pairs, p = [], 1
    while p < n:
        k = p
        while k >= 1:
            for j in range(k % p, n - k, 2 * k):
                for i in range(min(k, n - j - k)):
                    if (i + j) // (2 * p) == (i + j + k) // (2 * p):
                        pairs.append((i + j, i + j + k))
            k //= 2
        p *= 2
    return pairs


def _sort_groups_desc(x):
    v = [x[j * 8:(j + 1) * 8, :] for j in range(SORT_N)]
    for i, j in _sort_pairs(SORT_N):
        v[i], v[j] = jnp.maximum(v[i], v[j]), jnp.minimum(v[i], v[j])
    return jnp.concatenate(v, axis=0)


def _count_sorted(v, t, cmp):
    c8 = cmp(v[7], t)
    c4 = cmp(jnp.where(c8, v[11], v[3]), t)
    c2 = cmp(jnp.where(c8, jnp.where(c4, v[13], v[9]), jnp.where(c4, v[5], v[1])), t)
    e = [jnp.where(c2, v[4 * a + 2], v[4 * a]) for a in range(4)]
    c1 = cmp(jnp.where(c8, jnp.where(c4, e[3], e[2]), jnp.where(c4, e[1], e[0])), t)
    low = (jnp.where(c8, 8.0, 0.0) + jnp.where(c4, 4.0, 0.0)) + (jnp.where(c2, 2.0, 0.0) + jnp.where(c1, 1.0, 0.0))
    return jnp.where(cmp(v[15], t), 16.0, low)


def _select_threshold(sc_ref, j_ref, nk, tk, topk, idx_bits, key_axis, srt_ref=None):
    n_other = sc_ref.shape[1 - key_axis]
    vec = (n_other, 1) if key_axis == 1 else (1, n_other)
    step = LANES if key_axis == 1 else 8

    def count_cmp(t, cmp):
        if srt_ref is None:
            return count(lambda blk, ks: cmp(blk, t))
        tb = jnp.broadcast_to(t, (8, n_other))

        def body(c, acc):
            ks = pl.multiple_of(c * tk, tk)
            for g0 in range(0, tk, SORT_N * 8):
                v = [srt_ref[pl.ds(ks + g0 + j * 8, 8), :] for j in range(SORT_N)]
                acc = acc + _count_sorted(v, tb, cmp)
            return acc
        return jnp.sum(lax.fori_loop(0, nk, body, jnp.zeros((8, n_other), F32)), axis=0, keepdims=True)

    def count(pred):
        def body(c, acc):
            ks = pl.multiple_of(c * tk, tk)
            blk = sc_ref[:, pl.ds(ks, tk)] if key_axis == 1 else sc_ref[pl.ds(ks, tk), :]
            m = jnp.where(pred(blk, ks), 1.0, 0.0)
            parts = [lax.slice_in_dim(m, a, a + step, axis=key_axis) for a in range(0, tk, step)]
            while len(parts) > 1:
                parts = [parts[a] + parts[a + 1] for a in range(0, len(parts), 2)]
            return acc + parts[0]
        acc0 = jnp.zeros((n_other, step) if key_axis == 1 else (step, n_other), F32)
        return jnp.sum(lax.fori_loop(0, nk, body, acc0), axis=key_axis, keepdims=True)

    def bit_body(bi, t):
        cand = t ^ lax.shift_left(jnp.int32(1), 31 - bi)
        return jnp.where(count_cmp(_key_to_float(cand), jnp.greater_equal) >= topk, cand, t)

    thr_key = lax.fori_loop(0, 32, bit_body, jnp.full(vec, INT_MIN, I32))
    thr = _key_to_float(thr_key)

    cnt_ge = count_cmp(thr, jnp.greater_equal)
    cnt_gt = count_cmp(thr, jnp.greater)
    need = topk - cnt_gt
    tie = jnp.where((cnt_ge > topk) & (thr_key > NEG_INF_KEY), 1.0, 0.0)
    j_ref[...] = jnp.full(vec, 1 << idx_bits, I32)

    @pl.when(jnp.max(tie) > 0.0)
    def _():
        def jbit(bi, jb):
            cand = jb + lax.shift_left(jnp.int32(1), idx_bits - 1 - bi)

            def pred(blk, ks):
                kpos = ks + lax.broadcasted_iota(I32, blk.shape, key_axis)
                return (blk == thr) & (kpos < cand)
            return jnp.where(count(pred) <= need, cand, jb)
        j_ref[...] = lax.fori_loop(0, idx_bits, jbit, jnp.zeros(vec, I32))

    return thr


def _selected(score, kpos, thr, jb):
    return (score > thr) | ((score == thr) & (kpos < jb))


def _attn_prompt_body(q_ref, qia_ref, qib_ref, misc_ref, k_ref, v_ref, kim_ref, o_ref,
                      sc_scr, srt_scr, j_scr, bias_scr, kb_scr, vb_scr, kib_scr, kn_scr, qi_scr, wt_scr, qs_scr,
                      mx_scr, l_scr, acc_scr, *, tk, topk, idx_bits):
    i = pl.program_id(1)
    tq = Q_BLOCK
    nk = (i * tq + tq + tk - 1) // tk
    n_tiles = tk // LANES
    sub = 256
    qpos_t = i * tq + lax.broadcasted_iota(I32, (1, tq), 1)

    @pl.when(i == 0)
    def _():
        def cast_chunk(c, kn):
            rs = pl.ds(pl.multiple_of(c * tk, tk), tk)
            kf = k_ref[rs, :]
            kb_scr[rs, :] = kf.astype(BF16)
            vb_scr[rs, :] = v_ref[rs, :].astype(BF16)
            kib_scr[rs, :] = kim_ref[rs, 0:IDX_DIM].astype(BF16)
            sq = kf * kf
            return tuple(
                jnp.maximum(kn[g], jnp.max(jnp.sum(sq[:, g * HEAD_DIM:(g + 1) * HEAD_DIM], axis=1, keepdims=True),
                                           axis=0, keepdims=True))
                for g in range(N_KV_HEADS))
        kn = lax.fori_loop(0, k_ref.shape[0] // tk, cast_chunk, (jnp.zeros((1, 1), F32),) * N_KV_HEADS)
        for g in range(N_KV_HEADS):
            kn_scr[g] = jnp.broadcast_to(kn[g], kn_scr.shape[1:])

    for h in range(IDX_HEADS):
        ref = qia_ref if h < IDX_HEADS // 2 else qib_ref
        hh = h % (IDX_HEADS // 2)
        qi_scr[h // 2, (h % 2) * tq:(h % 2 + 1) * tq, :] = ref[:, hh * IDX_DIM:(hh + 1) * IDX_DIM].astype(BF16)
    wt_scr[...] = misc_ref[...].T

    def score_chunk(c, carry):
        for s0 in range(0, tk, sub):
            ks = pl.multiple_of(c * tk, tk) + s0
            ki = kib_scr[pl.ds(ks, sub), :]
            acc = jnp.zeros((sub, tq), F32)
            for hp in range(IDX_HEADS // 2):
                r = jnp.maximum(_dot_nt(ki, qi_scr[hp]), 0.0)
                acc = (acc + r[:, 0:tq] * wt_scr[pl.ds(MISC_WI + 2 * hp, 1), :]
                       + r[:, tq:] * wt_scr[pl.ds(MISC_WI + 2 * hp + 1, 1), :])
            kpos = ks + lax.broadcasted_iota(I32, (sub, tq), 0)
            masked = jnp.where(kpos <= qpos_t, acc, -jnp.inf)
            sc_scr[pl.ds(ks, sub), :] = masked
            for g0 in range(0, sub, SORT_N * 8):
                srt_scr[pl.ds(ks + g0, SORT_N * 8), :] = _sort_groups_desc(masked[g0:g0 + SORT_N * 8, :])
        return carry

    lax.fori_loop(0, nk, score_chunk, 0)
    thr = _select_threshold(sc_scr, j_scr, nk, tk, topk, idx_bits, 0, srt_scr)
    jb = j_scr[...]

    for g in range(N_KV_HEADS):
        kn = jnp.sqrt(kn_scr[g][0:1, :])
        for r in range(HEADS_PER_KV):
            hd = g * HEADS_PER_KV + r
            qh = q_ref[:, hd * HEAD_DIM:(hd + 1) * HEAD_DIM]
            qs_scr[g, r * tq:(r + 1) * tq, :] = qh.astype(BF16)
            mx_scr[g, r * tq:(r + 1) * tq, :] = jnp.sqrt(jnp.sum(qh * qh, axis=1, keepdims=True)) * kn * 1.02
    l_scr[...] = jnp.zeros(l_scr.shape, F32)
    acc_scr[...] = jnp.zeros(acc_scr.shape, F32)

    def logits(g, ks, bias4):
        kc = kb_scr[pl.ds(ks, tk), g * HEAD_DIM:(g + 1) * HEAD_DIM]
        return _dot_nt(qs_scr[g], kc) + bias4

    def exp_chunk(ks, bias):
        bias4 = jnp.concatenate([bias] * HEADS_PER_KV, axis=0)
        for g in range(N_KV_HEADS):
            s = logits(g, ks, bias4)
            m = mx_scr[g]
            p = [jnp.exp(s[:, t * LANES:(t + 1) * LANES] - m) for t in range(n_tiles)]
            l_scr[g] += functools.reduce(lambda x, y: x + y, p)
            pb = jnp.concatenate(p, axis=1).astype(BF16)
            vc = vb_scr[pl.ds(ks, tk), g * HEAD_DIM:(g + 1) * HEAD_DIM]
            acc_scr[g] += jnp.dot(pb, vc, preferred_element_type=F32)

    def bound_pass(c, carry):
        ks = pl.multiple_of(c * tk, tk)
        kpos = ks + lax.broadcasted_iota(I32, (tk, tq), 0)
        sel = _selected(sc_scr[pl.ds(ks, tk), :], kpos, thr, jb) & (kpos <= qpos_t)
        bias = jnp.where(sel, 0.0, NEG).T
        bias_scr[:, pl.ds(ks, tk)] = bias
        exp_chunk(ks, bias)
        return carry

    lax.fori_loop(0, nk, bound_pass, 0)

    def denominators():
        for g in range(N_KV_HEADS):
            mx_scr[g] = jnp.broadcast_to(jnp.sum(l_scr[g], axis=1, keepdims=True), mx_scr.shape[1:])

    denominators()
    l_min = jnp.min(functools.reduce(jnp.minimum, [mx_scr[g] for g in range(N_KV_HEADS)]))

    @pl.when(l_min < SOFTMAX_MIN_DENOM)
    def _():
        mx_scr[...] = jnp.full(mx_scr.shape, NEG, F32)
        l_scr[...] = jnp.zeros(l_scr.shape, F32)
        acc_scr[...] = jnp.zeros(acc_scr.shape, F32)

        def max_pass(c, carry):
            ks = pl.multiple_of(c * tk, tk)
            bias4 = jnp.concatenate([bias_scr[:, pl.ds(ks, tk)]] * HEADS_PER_KV, axis=0)
            for g in range(N_KV_HEADS):
                s = logits(g, ks, bias4)
                m = mx_scr[g]
                for t in range(n_tiles):
                    m = jnp.maximum(m, s[:, t * LANES:(t + 1) * LANES])
                mx_scr[g] = m
            return carry

        lax.fori_loop(0, nk, max_pass, 0)
        for g in range(N_KV_HEADS):
            mx_scr[g] = jnp.broadcast_to(jnp.max(mx_scr[g], axis=1, keepdims=True), mx_scr.shape[1:])

        def exp_pass(c, carry):
            ks = pl.multiple_of(c * tk, tk)
            exp_chunk(ks, bias_scr[:, pl.ds(ks, tk)])
            return carry

        lax.fori_loop(0, nk, exp_pass, 0)
        denominators()

    for g in range(N_KV_HEADS):
        o = acc_scr[g] / mx_scr[g]
        for r in range(HEADS_PER_KV):
            hd = g * HEADS_PER_KV + r
            o_ref[:, hd * HEAD_DIM:(hd + 1) * HEAD_DIM] = o[r * tq:(r + 1) * tq, :].astype(BF16)


def _attn_prompt(z, b, s, topk):
    nq = s // Q_BLOCK
    tk = min(512, s)
    idx_bits = max(1, (s - 1).bit_length()) + 1
    body = functools.partial(_attn_prompt_body, tk=tk, topk=topk, idx_bits=idx_bits)
    rows = HEADS_PER_KV * Q_BLOCK
    return pl.pallas_call(
        body,
        out_shape=jax.ShapeDtypeStruct((b * s, ATTN_WIDTH), BF16),
        grid=(b, nq),
        in_specs=[
            pl.BlockSpec((Q_BLOCK, ATTN_WIDTH), lambda bb, i: (bb * nq + i, 0)),
            pl.BlockSpec((Q_BLOCK, QI_WIDTH // 2), lambda bb, i: (bb * nq + i, C_QI // (QI_WIDTH // 2))),
            pl.BlockSpec((Q_BLOCK, QI_WIDTH // 2), lambda bb, i: (bb * nq + i, C_QI // (QI_WIDTH // 2) + 1)),
            pl.BlockSpec((Q_BLOCK, LANES), lambda bb, i: (bb * nq + i, C_MISC // LANES)),
            pl.BlockSpec((s, KV_WIDTH), lambda bb, i: (bb, C_K // KV_WIDTH)),
            pl.BlockSpec((s, KV_WIDTH), lambda bb, i: (bb, C_V // KV_WIDTH)),
            pl.BlockSpec((s, LANES), lambda bb, i: (bb, C_MISC // LANES)),
        ],
        out_specs=pl.BlockSpec((Q_BLOCK, ATTN_WIDTH), lambda bb, i: (bb * nq + i, 0)),
        scratch_shapes=[
            pltpu.VMEM((s, Q_BLOCK), F32),
            pltpu.VMEM((s, Q_BLOCK), F32),
            pltpu.VMEM((1, Q_BLOCK), I32),
            pltpu.VMEM((Q_BLOCK, s), F32),
            pltpu.VMEM((s, KV_WIDTH), BF16),
            pltpu.VMEM((s, KV_WIDTH), BF16),
            pltpu.VMEM((s, IDX_DIM), BF16),
            pltpu.VMEM((N_KV_HEADS, 8, LANES), F32),
            pltpu.VMEM((IDX_HEADS // 2, 2 * Q_BLOCK, IDX_DIM), BF16),
            pltpu.VMEM((LANES, Q_BLOCK), F32),
            pltpu.VMEM((N_KV_HEADS, rows, HEAD_DIM), BF16),
            pltpu.VMEM((N_KV_HEADS, rows, LANES), F32),
            pltpu.VMEM((N_KV_HEADS, rows, LANES), F32),
            pltpu.VMEM((N_KV_HEADS, rows, HEAD_DIM), F32),
        ],
        compiler_params=_params(("parallel", "arbitrary")),
        name="attn_prompt",
    )(z, z, z, z, z, z, z)


def _log_decay(misc, w2, gb):
    x = jnp.dot(misc, w2, precision=HIGHEST, preferred_element_type=F32) + gb
    return (jnp.minimum(x, 0.0) - jnp.log1p(jnp.exp(-jnp.abs(x)))) * (1.0 / GLA_GATE_TAU)


def _gla_out(o, nw, gr):
    return (_rms(o, nw) * (gr * jax.nn.sigmoid(gr))).astype(BF16)


GLA_C = 128
GLA_SUB = 8
GLA_HPS = 4


def _gla_prompt_body(gq_ref, gk_ref, gv_ref, misc_ref, gr_ref, w2_ref, gb_ref, nw_ref,
                     og_ref, sfin_ref, st_scr, a_scr, b_scr, k_scr, o_scr, *, nchunk):
    t = pl.program_id(2)
    c_ = GLA_C

    @pl.when(t == 0)
    def _():
        st_scr[...] = jnp.zeros_like(st_scr)

    a_scr[...] = jnp.zeros_like(a_scr)
    row = lax.broadcasted_iota(I32, (c_, c_), 0)
    col = lax.broadcasted_iota(I32, (c_, c_), 1)
    tri = jnp.where(col <= row, 1.0, 0.0)
    sub_row = lax.broadcasted_iota(I32, (GLA_SUB, LANES), 0)
    sub_col = lax.broadcasted_iota(I32, (GLA_SUB, GLA_SUB), 1)

    def head_chunk(hh, r0):
        ksl = slice(hh * GLA_DK, (hh + 1) * GLA_DK)
        vsl = slice(hh * GLA_DV, (hh + 1) * GLA_DV)
        q = gq_ref[pl.ds(r0, c_), ksl] * (GLA_DK ** -0.5)
        k = gk_ref[pl.ds(r0, c_), ksl]
        v = gv_ref[pl.ds(r0, c_), vsl].astype(BF16)
        g = _log_decay(misc_ref[pl.ds(r0, c_), :], w2_ref[:, ksl], gb_ref[:, ksl])
        b = jnp.dot(tri, g, precision=HIGHEST, preferred_element_type=F32)
        b_scr[hh] = b
        k_scr[hh] = k
        st = st_scr[hh]
        o = _dot_nt((q * jnp.exp(b)).astype(BF16), st.astype(BF16))

        n = c_ // 2
        while n >= GLA_SUB:
            for rb in range(n, c_, 2 * n):
                bref = b[rb:rb + 1, :]
                qs = q[rb:rb + n, :] * jnp.exp(b[rb:rb + n, :] - bref)
                ks = k[rb - n:rb, :] * jnp.exp(bref - b[rb - n:rb, :])
                a_scr[hh, rb:rb + n, rb - n:rb] = _dot_nt(qs.astype(BF16), ks.astype(BF16))
            n //= 2
        for blk in range(c_ // GLA_SUB):
            lo = blk * GLA_SUB
            qb = q[lo:lo + GLA_SUB, :]
            bb = b[lo:lo + GLA_SUB, :]
            ad = jnp.zeros((GLA_SUB, GLA_SUB), F32)
            for jj in range(GLA_SUB):
                bj = b_scr[hh, pl.ds(lo + jj, 1), :]
                kj = k_scr[hh, pl.ds(lo + jj, 1), :]
                w = jnp.exp(jnp.where(sub_row >= jj, bb - bj, NEG))
                colj = jnp.sum(qb * kj * w, axis=1, keepdims=True)
                ad = jnp.where(sub_col == jj, colj, ad)
            a_scr[hh, lo:lo + GLA_SUB, lo:lo + GLA_SUB] = ad

        o = o + jnp.dot(a_scr[hh].astype(BF16), v, preferred_element_type=F32)
        o_scr[pl.ds(r0, c_), vsl] = o
        b_last = b[c_ - 1:c_, :]
        kd = (k * jnp.exp(b_last - b)).astype(BF16)
        kv = lax.dot_general(v, kd, (((0,), (0,)), ((), ())), preferred_element_type=F32)
        st_scr[hh] = st * jnp.exp(b_last) + kv

    def chunk(ci, carry):
        r0 = pl.multiple_of(ci * c_, c_)
        for hh in range(GLA_HPS):
            head_chunk(hh, r0)
        return carry

    lax.fori_loop(0, nchunk, chunk, 0)
    for hh in range(GLA_HPS):
        vsl = slice(hh * GLA_DV, (hh + 1) * GLA_DV)
        og_ref[:, vsl] = _gla_out(o_scr[:, vsl], nw_ref[...], gr_ref[:, vsl])

    @pl.when(t == pl.num_programs(2) - 1)
    def _():
        for hh in range(GLA_HPS):
            sfin_ref[0, hh] = st_scr[hh].T


def _gla_prompt(z, w2p, gbias, norm_w, b, s):
    tb = min(512, s)
    nt = s // tb
    body = functools.partial(_gla_prompt_body, nchunk=tb // GLA_C)

    def rowblk(bb, t):
        return bb * nt + t

    kw, vw = GLA_HPS * GLA_DK, GLA_HPS * GLA_DV
    return pl.pallas_call(
        body,
        out_shape=(jax.ShapeDtypeStruct((b * s, GLA_VAL_WIDTH), BF16),
                   jax.ShapeDtypeStruct((b, GLA_HEADS, GLA_DK, GLA_DV), F32)),
        grid=(b, GLA_HEADS // GLA_HPS, nt),
        in_specs=[
            pl.BlockSpec((tb, kw), lambda bb, h, t: (rowblk(bb, t), C_GQ // kw + h)),
            pl.BlockSpec((tb, kw), lambda bb, h, t: (rowblk(bb, t), C_GK // kw + h)),
            pl.BlockSpec((tb, vw), lambda bb, h, t: (rowblk(bb, t), C_GV // vw + h)),
            pl.BlockSpec((tb, LANES), lambda bb, h, t: (rowblk(bb, t), C_MISC // LANES)),
            pl.BlockSpec((tb, vw), lambda bb, h, t: (rowblk(bb, t), C_GR // vw + h)),
            pl.BlockSpec((LANES, kw), lambda bb, h, t: (0, h)),
            pl.BlockSpec((1, kw), lambda bb, h, t: (0, h)),
            pl.BlockSpec((1, GLA_DV), lambda bb, h, t: (0, 0)),
        ],
        out_specs=(pl.BlockSpec((tb, vw), lambda bb, h, t: (rowblk(bb, t), h)),
                   pl.BlockSpec((1, GLA_HPS, GLA_DK, GLA_DV), lambda bb, h, t: (bb, h, 0, 0))),
        scratch_shapes=[
            pltpu.VMEM((GLA_HPS, GLA_DV, GLA_DK), F32),
            pltpu.VMEM((GLA_HPS, GLA_C, GLA_C), F32),
            pltpu.VMEM((GLA_HPS, GLA_C, GLA_DK), F32),
            pltpu.VMEM((GLA_HPS, GLA_C, GLA_DK), F32),
            pltpu.VMEM((tb, vw), F32),
        ],
        compiler_params=_params(("parallel", "parallel", "arbitrary")),
        name="gla_prompt",
    )(z, z, z, z, z, w2p, gbias, norm_w.reshape(1, GLA_DV))


def _merge_body(oa_ref, og_ref, ga_ref, gg_ref, wa_ref, wg_ref, o_ref):
    pa = jnp.dot(oa_ref[...], wa_ref[...], preferred_element_type=F32)
    pg = jnp.dot(og_ref[...], wg_ref[...], preferred_element_type=F32)
    o_ref[...] = (jax.nn.sigmoid(ga_ref[...]) * pa + jax.nn.sigmoid(gg_ref[...]) * pg).astype(BF16)


def _merge(o_attn, o_gla, z, wa, wg, tm):
    rows = o_attn.shape[0]
    d = wa.shape[1]
    return pl.pallas_call(
        _merge_body,
        out_shape=jax.ShapeDtypeStruct((rows, d), BF16),
        grid=(rows // tm,),
        in_specs=[
            pl.BlockSpec((tm, ATTN_WIDTH), lambda i: (i, 0)),
            pl.BlockSpec((tm, GLA_VAL_WIDTH), lambda i: (i, 0)),
            pl.BlockSpec((tm, d), lambda i: (i, C_GA // d)),
            pl.BlockSpec((tm, d), lambda i: (i, C_GA // d + 1)),
            pl.BlockSpec((ATTN_WIDTH, d), lambda i: (0, 0)),
            pl.BlockSpec((GLA_VAL_WIDTH, d), lambda i: (0, 0)),
        ],
        out_specs=pl.BlockSpec((tm, d), lambda i: (i, 0)),
        compiler_params=_params(("parallel",)),
        name="merge",
    )(o_attn, o_gla, z, z, wa, wg)


def _out_proj_body(m_ref, h_ref, w_ref, pw_ref, o_ref):
    y = jnp.dot(m_ref[...], w_ref[...], preferred_element_type=F32)
    o_ref[...] = h_ref[...] + _rms(y, pw_ref[...])


def _out_proj(merged, h, w_out, post_w, tm):
    rows, d = h.shape
    return pl.pallas_call(
        _out_proj_body,
        out_shape=jax.ShapeDtypeStruct((rows, d), F32),
        grid=(rows // tm,),
        in_specs=[
            pl.BlockSpec((tm, d), lambda i: (i, 0)),
            pl.BlockSpec((tm, d), lambda i: (i, 0)),
            pl.BlockSpec((d, d), lambda i: (0, 0)),
            pl.BlockSpec((1, d), lambda i: (0, 0)),
        ],
        out_specs=pl.BlockSpec((tm, d), lambda i: (i, 0)),
        compiler_params=_params(("parallel",)),
        name="out_proj",
    )(merged, h, w_out, post_w.reshape(1, d))


def _page_copies(pt_ref, hbm, buf, sem, seq, slot, n_pages, dst):
    return [pltpu.make_async_copy(hbm.at[pt_ref[seq, p]], buf.at[slot].at[dst(p)], sem)
            for p in range(n_pages)]


GATHER_SLOTS = 3


def _gather_step(fetch):
    b = pl.program_id(0)
    ahead = GATHER_SLOTS - 1

    @pl.when(b == 0)
    def _():
        for a in range(ahead):
            @pl.when(a < pl.num_programs(0))
            def _():
                for cp in fetch(a, a % GATHER_SLOTS):
                    cp.start()

    @pl.when(b + ahead < pl.num_programs(0))
    def _():
        for cp in fetch(b + ahead, (b + ahead) % GATHER_SLOTS):
            cp.start()

    slot = b % GATHER_SLOTS
    for cp in fetch(b, slot):
        cp.wait()
    return slot


def _sample_scores_body(pt_ref, qi_ref, wi_ref, kidx_hbm, o_ref, kbuf, sem, *, n_pages, page):
    def fetch(seq, slot):
        return _page_copies(pt_ref, kidx_hbm, kbuf, sem.at[slot], seq, slot, n_pages,
                            lambda p: (slice(None), pl.ds(p * page, page)))

    slot = _gather_step(fetch)
    s = jnp.dot(qi_ref[0].astype(BF16), kbuf[slot].astype(BF16), preferred_element_type=F32)
    o_ref[0] = jnp.sum(jnp.maximum(s, 0.0) * wi_ref[0], axis=0, keepdims=True)


def _sample_scores(page_table, qi3, wi3, kidx_t):
    nd, n_pages = page_table.shape
    page = kidx_t.shape[2]
    past = n_pages * page
    body = functools.partial(_sample_scores_body, n_pages=n_pages, page=page)
    grid_spec = pltpu.PrefetchScalarGridSpec(
        num_scalar_prefetch=1,
        grid=(nd,),
        in_specs=[
            pl.BlockSpec((1, IDX_HEADS, IDX_DIM), lambda b, pt: (b, 0, 0)),
            pl.BlockSpec((1, IDX_HEADS, 1), lambda b, pt: (b, 0, 0)),
            pl.BlockSpec(memory_space=pl.ANY),
        ],
        out_specs=pl.BlockSpec((1, 1, past), lambda b, pt: (b, 0, 0)),
        scratch_shapes=[pltpu.VMEM((GATHER_SLOTS, IDX_DIM, past), F32), pltpu.SemaphoreType.DMA((GATHER_SLOTS,))],
    )
    return pl.pallas_call(
        body,
        out_shape=jax.ShapeDtypeStruct((nd, 1, past), F32),
        grid_spec=grid_spec,
        compiler_params=_params(("arbitrary",)),
        name="sample_scores",
    )(page_table, qi3, wi3, kidx_t)


def _sample_select_body(sp_ref, qi_ref, misc_ref, bp_ref, bs_ref, sc_scr, j_scr, *, past, tk, topk, idx_bits):
    rows = sp_ref.shape[0]
    misc = misc_ref[...]
    d_in = lax.broadcasted_iota(I32, (LANES, QI_WIDTH), 0)
    c_out = lax.broadcasted_iota(I32, (LANES, QI_WIDTH), 1)
    rep = jnp.where((d_in < IDX_DIM) & (c_out % IDX_DIM == d_in), 1.0, 0.0)
    ki_t = jnp.dot(misc, rep, precision=HIGHEST, preferred_element_type=F32)
    c_in = lax.broadcasted_iota(I32, (QI_WIDTH, LANES), 0)
    l_out = lax.broadcasted_iota(I32, (QI_WIDTH, LANES), 1)
    seg = jnp.where(l_out == MISC_WI + c_in // IDX_DIM, 1.0, 0.0)
    hd = jnp.dot(qi_ref[...] * ki_t, seg, precision=HIGHEST, preferred_element_type=F32)
    lane = lax.broadcasted_iota(I32, (rows, LANES), 1)
    is_wi = (lane >= MISC_WI) & (lane < MISC_GLR)
    s_self = jnp.sum(jnp.where(is_wi, jnp.maximum(hd, 0.0) * misc, 0.0), axis=1, keepdims=True)

    sc_scr[:, 0:past] = sp_ref[...]
    sc_scr[:, past:] = jnp.where(lax.broadcasted_iota(I32, (rows, tk), 1) == 0, s_self, -jnp.inf)
    nk = sc_scr.shape[1] // tk
    thr = _select_threshold(sc_scr, j_scr, nk, tk, topk, idx_bits, 1)
    jb = j_scr[...]
    kpos = lax.broadcasted_iota(I32, sc_scr.shape, 1)
    bias = jnp.where(_selected(sc_scr[...], kpos, thr, jb), 0.0, NEG)
    bp_ref[...] = bias[:, 0:past]
    bs_ref[...] = bias[:, past:past + LANES]


def _sample_select(s_past, qi2, misc, topk):
    nd, past = s_past.shape
    tk = LANES
    width = past + tk
    idx_bits = max(1, (width - 1).bit_length()) + 1
    body = functools.partial(_sample_select_body, past=past, tk=tk, topk=topk, idx_bits=idx_bits)
    return pl.pallas_call(
        body,
        out_shape=(jax.ShapeDtypeStruct((nd, past), F32), jax.ShapeDtypeStruct((nd, LANES), F32)),
        grid=(1,),
        in_specs=[
            pl.BlockSpec((nd, past), lambda i: (0, 0)),
            pl.BlockSpec((nd, QI_WIDTH), lambda i: (0, 0)),
            pl.BlockSpec((nd, LANES), lambda i: (0, 0)),
        ],
        out_specs=(pl.BlockSpec((nd, past), lambda i: (0, 0)), pl.BlockSpec((nd, LANES), lambda i: (0, 0))),
        scratch_shapes=[pltpu.VMEM((nd, width), F32), pltpu.VMEM((nd, 1), I32)],
        compiler_params=_params(("arbitrary",)),
        name="sample_select",
    )(s_past, qi2, misc)


SAMPLE_KCHUNK = 1024


def _sample_attn_body(pt_ref, q_ref, ks_ref, vs_ref, b2_ref, bs_ref, ck_hbm, cv_hbm, o_ref, kbuf, vbuf, sem,
                      *, n_pages, prows):
    def fetch(seq, slot):
        dst = lambda p: (pl.ds(p * prows, prows), slice(None))
        return (_page_copies(pt_ref, ck_hbm, kbuf, sem.at[0, slot], seq, slot, n_pages, dst)
                + _page_copies(pt_ref, cv_hbm, vbuf, sem.at[1, slot], seq, slot, n_pages, dst))

    slot = _gather_step(fetch)
    q = q_ref[0]
    qb = q.astype(BF16)
    total = n_pages * prows
    ch = min(SAMPLE_KCHUNK, total)
    head_grp = lax.broadcasted_iota(I32, (N_HEADS, ch), 0) // HEADS_PER_KV
    row_grp = lax.broadcasted_iota(I32, (N_HEADS, ch), 1) % N_KV_HEADS
    own = head_grp == row_grp
    s_chunks = []
    for c in range(total // ch):
        kc = kbuf[slot, c * ch:(c + 1) * ch, :].astype(BF16)
        s_chunks.append(jnp.where(own, _dot_nt(qb, kc) + b2_ref[0][:, c * ch:(c + 1) * ch], NEG))

    grp0 = lax.broadcasted_iota(I32, (N_HEADS, HEAD_DIM), 0) < HEADS_PER_KV
    k_self = jnp.where(grp0, ks_ref[0][:, 0:HEAD_DIM], ks_ref[0][:, HEAD_DIM:])
    v_self = jnp.where(grp0, vs_ref[0][:, 0:HEAD_DIM], vs_ref[0][:, HEAD_DIM:])
    s_self = jnp.sum(q * k_self, axis=1, keepdims=True) + bs_ref[0][:, 0:1]

    m = s_self
    for s in s_chunks:
        m = jnp.maximum(m, jnp.max(s, axis=1, keepdims=True))
    p_self = jnp.exp(s_self - m)
    l = p_self
    acc = p_self * v_self
    for c, s in enumerate(s_chunks):
        p = jnp.exp(s - m)
        l = l + jnp.sum(p, axis=1, keepdims=True)
        vc = vbuf[slot, c * ch:(c + 1) * ch, :].astype(BF16)
        acc = acc + jnp.dot(p.astype(BF16), vc, preferred_element_type=F32)
    o_ref[0] = (acc / l).astype(BF16)


def _sample_attn(page_table, q3, k_self, v_self, bias2, bias_self, ck2, cv2):
    nd, n_pages = page_table.shape
    prows = ck2.shape[1]
    total = n_pages * prows
    body = functools.partial(_sample_attn_body, n_pages=n_pages, prows=prows)
    grid_spec = pltpu.PrefetchScalarGridSpec(
        num_scalar_prefetch=1,
        grid=(nd,),
        in_specs=[
            pl.BlockSpec((1, N_HEADS, HEAD_DIM), lambda b, pt: (b, 0, 0)),
            pl.BlockSpec((1, 1, KV_WIDTH), lambda b, pt: (b, 0, 0)),
            pl.BlockSpec((1, 1, KV_WIDTH), lambda b, pt: (b, 0, 0)),
            pl.BlockSpec((1, 1, total), lambda b, pt: (b, 0, 0)),
            pl.BlockSpec((1, 1, LANES), lambda b, pt: (b, 0, 0)),
            pl.BlockSpec(memory_space=pl.ANY),
            pl.BlockSpec(memory_space=pl.ANY),
        ],
        out_specs=pl.BlockSpec((1, N_HEADS, HEAD_DIM), lambda b, pt: (b, 0, 0)),
        scratch_shapes=[
            pltpu.VMEM((GATHER_SLOTS, total, HEAD_DIM), F32),
            pltpu.VMEM((GATHER_SLOTS, total, HEAD_DIM), F32),
            pltpu.SemaphoreType.DMA((2, GATHER_SLOTS)),
        ],
    )
    return pl.pallas_call(
        body,
        out_shape=jax.ShapeDtypeStruct((nd, N_HEADS, HEAD_DIM), BF16),
        grid_spec=grid_spec,
        compiler_params=_params(("arbitrary",)),
        name="sample_attn",
    )(page_table, q3, k_self, v_self, bias2, bias_self, ck2, cv2)


def _gla_sample_body(gq_ref, gk_ref, gv_ref, misc_ref, gr_ref, s_ref, w2_ref, gb_ref, nw_ref, og_ref, so_ref):
    eye = jnp.where(lax.broadcasted_iota(I32, (GLA_DK, GLA_DK), 0)
                    == lax.broadcasted_iota(I32, (GLA_DK, GLA_DK), 1), 1.0, 0.0)

    def column(row):
        return jnp.sum(eye * row, axis=1, keepdims=True)

    for sq in range(s_ref.shape[0]):
        misc = misc_ref[sq]
        for h in range(GLA_HEADS):
            ksl = slice(h * GLA_DK, (h + 1) * GLA_DK)
            vsl = slice(h * GLA_DV, (h + 1) * GLA_DV)
            g = _log_decay(misc, w2_ref[:, ksl], gb_ref[:, ksl])
            s_new = column(jnp.exp(g)) * s_ref[sq, h] + column(gk_ref[sq][:, ksl]) * gv_ref[sq][:, vsl]
            so_ref[sq, h] = s_new
            o = jnp.sum(column(gq_ref[sq][:, ksl] * (GLA_DK ** -0.5)) * s_new, axis=0, keepdims=True)
            og_ref[sq, :, vsl] = _gla_out(o, nw_ref[...], gr_ref[sq][:, vsl])


def _gla_sample(gq, gk, gv, misc, gr, state, w2p, gbias, norm_w):
    nd = state.shape[0]
    sb = 4 if nd % 4 == 0 else 1

    def row3(w):
        return pl.BlockSpec((sb, 1, w), lambda b: (b, 0, 0))

    st_spec = pl.BlockSpec((sb, GLA_HEADS, GLA_DK, GLA_DV), lambda b: (b, 0, 0, 0))
    return pl.pallas_call(
        _gla_sample_body,
        out_shape=(jax.ShapeDtypeStruct((nd, 1, GLA_VAL_WIDTH), BF16),
                   jax.ShapeDtypeStruct(state.shape, F32)),
        grid=(nd // sb,),
        in_specs=[
            row3(GLA_KEY_WIDTH), row3(GLA_KEY_WIDTH), row3(GLA_VAL_WIDTH), row3(LANES), row3(GLA_VAL_WIDTH),
            st_spec,
            pl.BlockSpec((LANES, GLA_KEY_WIDTH), lambda b: (0, 0)),
            pl.BlockSpec((1, GLA_KEY_WIDTH), lambda b: (0, 0)),
            pl.BlockSpec((1, GLA_DV), lambda b: (0, 0)),
        ],
        out_specs=(row3(GLA_VAL_WIDTH), st_spec),
        compiler_params=_params(("parallel",)),
        name="gla_sample",
    )(gq, gk, gv, misc, gr, state, w2p, gbias, norm_w.reshape(1, GLA_DV))


def _rope_tables(pos, d):
    inv = ROPE_THETA ** (-jnp.arange(0, d, 2, dtype=F32) / d)
    ang = pos.astype(F32)[:, None] * inv[None, :]
    cos, sin = jnp.cos(ang), jnp.sin(ang)
    reps = LANES // d
    return (jnp.tile(jnp.concatenate([cos, cos], axis=-1), (1, reps)),
            jnp.tile(jnp.concatenate([-sin, sin], axis=-1), (1, reps)))


def kernel(x_prompt, x_sample, cache_k, cache_v, cache_kidx, page_table, state_gla,
           ffn1_pre_w, ffn1_w_gate, ffn1_w_up, ffn1_w_down, ffn1_post_w,
           mix_pre_w, w_in, gla_gate_w2, gla_gate_b, gla_norm_w,
           w_proj_attn, w_proj_gla, w_out, mix_post_w,
           ffn2_pre_w, ffn2_w_gate, ffn2_w_up, ffn2_w_down, ffn2_post_w):
    b, s, d = x_prompt.shape
    nd, td, _ = x_sample.shape
    n_pool, page = cache_k.shape[:2]
    n_pages = page_table.shape[1]
    past = n_pages * page
    assert td == 1 and s % Q_BLOCK == 0 and d % MIX_TN == 0 and C_GA % d == 0

    wpa, wpg, wo = w_proj_attn.astype(BF16), w_proj_gla.astype(BF16), w_out.astype(BF16)
    w2p = jnp.zeros((LANES, GLA_KEY_WIDTH), F32).at[MISC_GLR:MISC_GLR + GLA_GATE_RANK].set(gla_gate_w2)
    gbias = gla_gate_b.reshape(1, GLA_KEY_WIDTH)

    def trunk(x, tm, tm_mix, tabs, tab_rows, mixer, f1, f2, w_mix):
        h = _ffn(x, ffn1_pre_w, *f1, ffn1_post_w, tm)
        h, f1b = h if isinstance(h, tuple) else (h, f1)
        z, k_rows, v_rows, *w_mix_b = _mix_in(h, mix_pre_w, w_mix, tabs, tm_mix, tab_rows // tm_mix)
        o_attn, o_gla, s_fin = mixer(z)
        tmm = min(tm, 512)
        merged = _merge(o_attn, o_gla, z, wpa, wpg, tmm)
        h = _out_proj(merged, h, wo, mix_post_w, tmm)
        y = _ffn(h, ffn2_pre_w, *f2, ffn2_post_w, tm)
        y, f2b = y if isinstance(y, tuple) else (y, f2)
        return y, (k_rows, v_rows, z[:, C_MISC:C_MISC + IDX_DIM]), s_fin, f1b, f2b, (w_mix_b or [w_mix])[0]

    pos_p = jnp.arange(s, dtype=I32)
    tm_p = 512 if s % 512 == 0 else Q_BLOCK
    tabs_p = _rope_tables(pos_p, HEAD_DIM) + _rope_tables(pos_p, IDX_DIM)
    top_p = min(TOPK_MAX, s // 4)

    def mixer_p(z):
        o_attn = _attn_prompt(z, b, s, top_p)
        o_gla, s_fin = _gla_prompt(z, w2p, gbias, gla_norm_w, b, s)
        return o_attn, o_gla, s_fin

    tm_mix = 1024 if s % 1024 == 0 else tm_p

    pos_s = jnp.full((nd,), past, I32)
    tabs_s = _rope_tables(pos_s, HEAD_DIM) + _rope_tables(pos_s, IDX_DIM)
    top_s = min(TOPK_MAX, (past + td) // 4)
    ck2 = cache_k.reshape(n_pool, page * N_KV_HEADS, HEAD_DIM)
    cv2 = cache_v.reshape(n_pool, page * N_KV_HEADS, HEAD_DIM)
    kidx_t = jnp.swapaxes(cache_kidx, 1, 2)

    def mixer_s(z):
        misc = z[:, C_MISC:C_MISC + LANES]
        qi2 = z[:, C_QI:C_QI + QI_WIDTH]
        s_past = _sample_scores(page_table, qi2.reshape(nd, IDX_HEADS, IDX_DIM),
                                misc[:, MISC_WI:MISC_GLR].reshape(nd, IDX_HEADS, 1), kidx_t)
        bias_past, bias_self = _sample_select(s_past.reshape(nd, past), qi2, misc, top_s)
        bias2 = jnp.repeat(bias_past, N_KV_HEADS, axis=1).reshape(nd, 1, past * N_KV_HEADS)
        o_attn = _sample_attn(page_table, z[:, C_Q:C_Q + ATTN_WIDTH].reshape(nd, N_HEADS, HEAD_DIM),
                              z[:, C_K:C_K + KV_WIDTH].reshape(nd, 1, KV_WIDTH),
                              z[:, C_V:C_V + KV_WIDTH].reshape(nd, 1, KV_WIDTH),
                              bias2, bias_self.reshape(nd, 1, LANES), ck2, cv2)
        o_gla, s_fin = _gla_sample(z[:, C_GQ:C_GQ + GLA_KEY_WIDTH].reshape(nd, 1, GLA_KEY_WIDTH),
                                   z[:, C_GK:C_GK + GLA_KEY_WIDTH].reshape(nd, 1, GLA_KEY_WIDTH),
                                   z[:, C_GV:C_GV + GLA_VAL_WIDTH].reshape(nd, 1, GLA_VAL_WIDTH),
                                   misc.reshape(nd, 1, LANES),
                                   z[:, C_GR:C_GR + GLA_VAL_WIDTH].reshape(nd, 1, GLA_VAL_WIDTH),
                                   state_gla, w2p, gbias, gla_norm_w)
        return o_attn.reshape(nd, ATTN_WIDTH), o_gla.reshape(nd, GLA_VAL_WIDTH), s_fin

    y_s, kv_s, gla_s, f1b, f2b, w_mix_b = trunk(
        x_sample.reshape(nd, d), nd, nd, tabs_s, nd, mixer_s,
        (ffn1_w_gate, ffn1_w_up, ffn1_w_down), (ffn2_w_gate, ffn2_w_up, ffn2_w_down), w_in.T)
    y_p, kv_p, gla_p, _, _, _ = trunk(x_prompt.reshape(b * s, d), tm_p, tm_mix, tabs_p, s, mixer_p, f1b, f2b, w_mix_b)

    def kv_out(kv, n, t):
        k_rows, v_rows, ki = kv
        return (k_rows.reshape(n, t, N_KV_HEADS, HEAD_DIM), v_rows.reshape(n, t, N_KV_HEADS, HEAD_DIM),
                ki.reshape(n, t, IDX_DIM))

    k_p, v_p, ki_p = kv_out(kv_p, b, s)
    k_s, v_s, ki_s = kv_out(kv_s, nd, td)
    return (y_p.reshape(b, s, d), y_s.reshape(nd, td, d), k_p, v_p, ki_p, gla_p, k_s, v_s, ki_s, gla_s)
```

```python
import functools

import jax
import jax.numpy as jnp
from jax import lax
from jax.experimental import pallas as pl
from jax.experimental.pallas import tpu as pltpu

F32, BF16, I32 = jnp.float32, jnp.bfloat16, jnp.int32
HIGHEST = lax.Precision.HIGHEST

N_HEADS = 8
N_KV_HEADS = 2
HEAD_DIM = 128
IDX_HEADS = 16
IDX_DIM = 64
TOPK_MAX = 256
Q_BLOCK = 128
ROPE_THETA = 10000.0
GLA_HEADS = 4
GLA_DK = 128
GLA_DV = 256
GLA_GATE_RANK = 16
GLA_GATE_TAU = 16.0
RMS_EPS = 1e-6

LANES = 128
ATTN_WIDTH = N_HEADS * HEAD_DIM
KV_WIDTH = N_KV_HEADS * HEAD_DIM
QI_WIDTH = IDX_HEADS * IDX_DIM
GLA_KEY_WIDTH = GLA_HEADS * GLA_DK
GLA_VAL_WIDTH = GLA_HEADS * GLA_DV
HEADS_PER_KV = N_HEADS // N_KV_HEADS

MIX_TN = 512

C_Q = 0
C_K = C_Q + ATTN_WIDTH
C_V = C_K + KV_WIDTH
C_QI = C_V + KV_WIDTH
C_MISC = C_QI + QI_WIDTH
C_GQ = -(-(C_MISC + LANES) // MIX_TN) * MIX_TN
C_GK = C_GQ + GLA_KEY_WIDTH
C_GV = C_GK + GLA_KEY_WIDTH
C_GR = C_GV + GLA_VAL_WIDTH
C_GA = C_GR + GLA_VAL_WIDTH
MISC_WI, MISC_GLR = IDX_DIM, IDX_DIM + IDX_HEADS

ROW_TM = 512
MIX_TM = 1024
FFN_TF = 512
ATTN_TK = 512
INDEXER_SUB = 256
GLA_TB = 512
GLA_SAMPLE_SB = 8
VMEM_LIMIT = 56 * 1024 * 1024

NEG = -1e30
SOFTMAX_MIN_DENOM = 2.0 ** -60
BOUND_SLACK = 1.02
INT_MIN = -2 ** 31
INT_ABS_MASK = 0x7FFFFFFF
NEG_INF_KEY = -2139095041


def _params(sem, vmem=VMEM_LIMIT):
    return pltpu.CompilerParams(dimension_semantics=sem, vmem_limit_bytes=vmem)


def _rms(x, w):
    return x * lax.rsqrt(jnp.mean(x * x, axis=-1, keepdims=True) + RMS_EPS) * w


def _dot_nt(a, b):
    return lax.dot_general(a, b, (((1,), (1,)), ((), ())), preferred_element_type=F32)


def _ffn_body(x_ref, prew_ref, wg_ref, wu_ref, wd_ref, postw_ref, o_ref, *rest):
    *wb_refs, z_scr, acc_scr = rest
    j = pl.program_id(1)

    @pl.when(j == 0)
    def _():
        z_scr[...] = _rms(x_ref[...], prew_ref[...]).astype(BF16)
        acc_scr[...] = jnp.zeros_like(acc_scr)

    wg, wu, wd = wg_ref[...].astype(BF16), wu_ref[...].astype(BF16), wd_ref[...].astype(BF16)
    for ref, w in zip(wb_refs, (wg, wu, wd)):
        ref[...] = w
    z = z_scr[...]
    g = jnp.dot(z, wg, preferred_element_type=F32)
    u = jnp.dot(z, wu, preferred_element_type=F32)
    a = (g * jax.nn.sigmoid(g) * u).astype(BF16)
    acc_scr[...] += jnp.dot(a, wd, preferred_element_type=F32)

    @pl.when(j == pl.num_programs(1) - 1)
    def _():
        o_ref[...] = x_ref[...] + 0.5 * _rms(acc_scr[...], postw_ref[...])


def _ffn(x, pre_w, wg, wu, wd, post_w, tm):
    rows, d = x.shape
    dff = wg.shape[1]
    tf = FFN_TF if dff % FFN_TF == 0 else dff
    emit = wg.dtype == F32
    assert not emit or rows == tm
    w_specs = [
        pl.BlockSpec((d, tf), lambda i, j: (0, j)),
        pl.BlockSpec((d, tf), lambda i, j: (0, j)),
        pl.BlockSpec((tf, d), lambda i, j: (j, 0)),
    ]
    y_shape = jax.ShapeDtypeStruct((rows, d), F32)
    y_spec = pl.BlockSpec((tm, d), lambda i, j: (i, 0))
    out = pl.pallas_call(
        _ffn_body,
        out_shape=(y_shape, *(jax.ShapeDtypeStruct(w.shape, BF16) for w in (wg, wu, wd))) if emit else y_shape,
        grid=(rows // tm, dff // tf),
        in_specs=[
            pl.BlockSpec((tm, d), lambda i, j: (i, 0)),
            pl.BlockSpec((1, d), lambda i, j: (0, 0)),
            *w_specs,
            pl.BlockSpec((1, d), lambda i, j: (0, 0)),
        ],
        out_specs=(y_spec, *w_specs) if emit else y_spec,
        scratch_shapes=[pltpu.VMEM((tm, d), BF16), pltpu.VMEM((tm, d), F32)],
        compiler_params=_params(("parallel", "arbitrary")),
        name="ffn",
    )(x, pre_w.reshape(1, d), wg, wu, wd, post_w.reshape(1, d))
    return (out[0], tuple(out[1:])) if emit else out


def _mix_in_body(x_ref, prew_ref, wt_ref, wglr_ref, c128_ref, s128_ref, c64_ref, s64_ref, o_ref, kf_ref, vf_ref,
                 *rest):
    *wb_ref, u_scr = rest
    j = pl.program_id(1)

    @pl.when(j == 0)
    def _():
        u_scr[...] = _rms(x_ref[...], prew_ref[...]).astype(BF16)

    if wb_ref:
        w_ref = wb_ref[0]
        w_ref[...] = wt_ref[...].astype(BF16)

        @pl.when(j == C_MISC // MIX_TN)
        def _():
            row = lax.broadcasted_iota(I32, (LANES, w_ref.shape[1]), 0)
            w_ref[0:LANES, :] = jnp.where(row < MISC_GLR, w_ref[0:LANES, :], jnp.where(
                row < MISC_GLR + GLA_GATE_RANK, wglr_ref[...].astype(BF16), jnp.zeros((), BF16)))
            w_ref[LANES:, :] = jnp.zeros((MIX_TN - LANES, w_ref.shape[1]), BF16)
    else:
        w_ref = wt_ref

    o_ref[...] = _dot_nt(u_scr[...], w_ref[...])

    def rope128(x):
        return x * c128_ref[...] + pltpu.roll(x, HEAD_DIM // 2, 1) * s128_ref[...]

    def rope64(x):
        lane = lax.broadcasted_iota(I32, x.shape, 1)
        first = (lane % IDX_DIM) < (IDX_DIM // 2)
        rot = jnp.where(first, pltpu.roll(x, LANES - IDX_DIM // 2, 1), pltpu.roll(x, IDX_DIM // 2, 1))
        return x * c64_ref[...] + rot * s64_ref[...]

    def sl(t):
        return slice(t * LANES, (t + 1) * LANES)

    @pl.when(j < 2)
    def _():
        for t in range(4):
            o_ref[:, sl(t)] = rope128(o_ref[:, sl(t)]) * (HEAD_DIM ** -0.5)

    @pl.when(j == 2)
    def _():
        tm = o_ref.shape[0]
        for t in range(N_KV_HEADS):
            k = rope128(o_ref[:, sl(t)])
            o_ref[:, sl(t)] = k
            kf_ref[pl.ds(t, tm, stride=N_KV_HEADS), :] = k
            vf_ref[pl.ds(t, tm, stride=N_KV_HEADS), :] = o_ref[:, sl(N_KV_HEADS + t)]

    @pl.when((j == 3) | (j == 4))
    def _():
        for t in range(4):
            o_ref[:, sl(t)] = rope64(o_ref[:, sl(t)])

    @pl.when(j == 5)
    def _():
        x = o_ref[:, sl(0)]
        lane = lax.broadcasted_iota(I32, x.shape, 1)
        wi_scale = IDX_HEADS ** -0.5 * IDX_DIM ** -0.5
        o_ref[:, sl(0)] = jnp.where(lane < MISC_WI, rope64(x), jnp.where(lane < MISC_GLR, x * wi_scale, x))


def _mix_in(h, pre_w, w, tabs, tm, tab_blocks):
    rows, d = h.shape
    zw = C_GA + 2 * d
    emit = w.dtype == F32
    if emit:
        head = C_MISC + MISC_GLR
        gla_end = head + 2 * GLA_KEY_WIDTH + GLA_VAL_WIDTH
        tail = gla_end + GLA_GATE_RANK
        assert rows == tm and w.shape == (tail + GLA_VAL_WIDTH + 2 * d, d) and gla_end >= MISC_GLR
        sub = 8
        assert head % sub == 0 and tail % sub == 0 and (gla_end - MISC_GLR) % sub == 0
        tn8 = MIX_TN // sub

        def src_row(i, j):
            return (sub * jnp.where(j <= C_MISC // MIX_TN, j * tn8,
                                    jnp.where(j < C_GR // MIX_TN, head // sub + (j - C_GQ // MIX_TN) * tn8,
                                              tail // sub + (j - C_GR // MIX_TN) * tn8)), 0)

        w_specs = [pl.BlockSpec((pl.Element(MIX_TN), pl.Element(d)), src_row),
                   pl.BlockSpec((pl.Element(LANES), pl.Element(d)), lambda i, j: (gla_end - MISC_GLR, 0))]
    else:
        assert w.shape == (zw, d)
        w_specs = [pl.BlockSpec((MIX_TN, d), lambda i, j: (j, 0)), pl.BlockSpec((LANES, d), lambda i, j: (0, 0))]
    wb_shape = (jax.ShapeDtypeStruct((zw, d), BF16),) if emit else ()
    wb_spec = (pl.BlockSpec((MIX_TN, d), lambda i, j: (j, 0)),) if emit else ()

    c128, s128, c64, s64 = tabs
    tab_spec = pl.BlockSpec((tm, LANES), lambda i, j: (i % tab_blocks, 0))
    kv_shape = jax.ShapeDtypeStruct((rows * N_KV_HEADS, HEAD_DIM), F32)
    kv_spec = pl.BlockSpec((tm * N_KV_HEADS, HEAD_DIM), lambda i, j: (i, 0))
    return pl.pallas_call(
        _mix_in_body,
        out_shape=(jax.ShapeDtypeStruct((rows, zw), F32), kv_shape, kv_shape, *wb_shape),
        grid=(rows // tm, zw // MIX_TN),
        in_specs=[
            pl.BlockSpec((tm, d), lambda i, j: (i, 0)),
            pl.BlockSpec((1, d), lambda i, j: (0, 0)),
            *w_specs,
            tab_spec, tab_spec, tab_spec, tab_spec,
        ],
        out_specs=(pl.BlockSpec((tm, MIX_TN), lambda i, j: (i, j)), kv_spec, kv_spec, *wb_spec),
        scratch_shapes=[pltpu.VMEM((tm, d), BF16)],
        compiler_params=_params(("parallel", "arbitrary")),
        name="mix_in",
    )(h, pre_w.reshape(1, d), w, w, c128, s128, c64, s64)


def _key_to_float(key):
    key = jnp.maximum(key, NEG_INF_KEY)
    return lax.bitcast_convert_type(key ^ ((key >> 31) & jnp.int32(INT_ABS_MASK)), F32)


SORT_N = 16


def _sort_pairs(n):
    pairs, p = [], 1
    while p < n:
        k = p
        while k >= 1:
            for j in range(k % p, n - k, 2 * k):
                for i in range(min(k, n - j - k)):
                    if (i + j) // (2 * p) == (i + j + k) // (2 * p):
                        pairs.append((i + j, i + j + k))
            k //= 2
        p *= 2
    return pairs


def _sort_groups_desc(x):
    v = [x[j * 8:(j + 1) * 8, :] for j in range(SORT_N)]
    for i, j in _sort_pairs(SORT_N):
        v[i], v[j] = jnp.maximum(v[i], v[j]), jnp.minimum(v[i], v[j])
    return jnp.concatenate(v, axis=0)


def _count_sorted(v, t, cmp):
    c8 = cmp(v[7], t)
    c4 = cmp(jnp.where(c8, v[11], v[3]), t)
    c2 = cmp(jnp.where(c8, jnp.where(c4, v[13], v[9]), jnp.where(c4, v[5], v[1])), t)
    e = [jnp.where(c2, v[4 * a + 2], v[4 * a]) for a in range(4)]
    c1 = cmp(jnp.where(c8, jnp.where(c4, e[3], e[2]), jnp.where(c4, e[1], e[0])), t)
    low = (jnp.where(c8, 8.0, 0.0) + jnp.where(c4, 4.0, 0.0)) + (jnp.where(c2, 2.0, 0.0) + jnp.where(c1, 1.0, 0.0))
    return jnp.where(cmp(v[15], t), 16.0, low)


def _select_threshold(sc_ref, j_ref, nk, tk, topk, idx_bits, key_axis, srt_ref=None):
    n_other = sc_ref.shape[1 - key_axis]
    vec = (n_other, 1) if key_axis == 1 else (1, n_other)
    step = LANES if key_axis == 1 else 8

    def count_cmp(t, cmp):
        if srt_ref is None:
            return count(lambda blk, ks: cmp(blk, t))
        tb = jnp.broadcast_to(t, (8, n_other))

        def body(c, acc):
            ks = pl.multiple_of(c * tk, tk)
            for g0 in range(0, tk, SORT_N * 8):
                v = [srt_ref[pl.ds(ks + g0 + j * 8, 8), :] for j in range(SORT_N)]
                acc = acc + _count_sorted(v, tb, cmp)
            return acc
        return jnp.sum(lax.fori_loop(0, nk, body, jnp.zeros((8, n_other), F32)), axis=0, keepdims=True)

    def count(pred):
        def body(c, acc):
            ks = pl.multiple_of(c * tk, tk)
            blk = sc_ref[:, pl.ds(ks, tk)] if key_axis == 1 else sc_ref[pl.ds(ks, tk), :]
            m = jnp.where(pred(blk, ks), 1.0, 0.0)
            parts = [lax.slice_in_dim(m, a, a + step, axis=key_axis) for a in range(0, tk, step)]
            while len(parts) > 1:
                parts = [parts[a] + parts[a + 1] for a in range(0, len(parts), 2)]
            return acc + parts[0]
        acc0 = jnp.zeros((n_other, step) if key_axis == 1 else (step, n_other), F32)
        return jnp.sum(lax.fori_loop(0, nk, body, acc0), axis=key_axis, keepdims=True)

    def bit_body(bi, t):
        cand = t ^ lax.shift_left(jnp.int32(1), 31 - bi)
        return jnp.where(count_cmp(_key_to_float(cand), jnp.greater_equal) >= topk, cand, t)

    thr_key = lax.fori_loop(0, 32, bit_body, jnp.full(vec, INT_MIN, I32))
    thr = _key_to_float(thr_key)

    cnt_ge = count_cmp(thr, jnp.greater_equal)
    cnt_gt = count_cmp(thr, jnp.greater)
    need = topk - cnt_gt
    tie = jnp.where((cnt_ge > topk) & (thr_key > NEG_INF_KEY), 1.0, 0.0)
    j_ref[...] = jnp.full(vec, 1 << idx_bits, I32)

    @pl.when(jnp.max(tie) > 0.0)
    def _():
        def jbit(bi, jb):
            cand = jb + lax.shift_left(jnp.int32(1), idx_bits - 1 - bi)

            def pred(blk, ks):
                kpos = ks + lax.broadcasted_iota(I32, blk.shape, key_axis)
                return (blk == thr) & (kpos < cand)
            return jnp.where(count(pred) <= need, cand, jb)
        j_ref[...] = lax.fori_loop(0, idx_bits, jbit, jnp.zeros(vec, I32))

    return thr


def _selected(score, kpos, thr, jb):
    return (score > thr) | ((score == thr) & (kpos < jb))


def _attn_prompt_body(q_ref, qia_ref, qib_ref, misc_ref, k_ref, v_ref, kim_ref, o_ref,
                      sc_scr, srt_scr, j_scr, bias_scr, kb_scr, vb_scr, kib_scr, kn_scr, qi_scr, wt_scr, qs_scr,
                      mx_scr, l_scr, acc_scr, *, tk, topk, idx_bits):
    i = pl.program_id(1)
    tq = Q_BLOCK
    nk = (i * tq + tq + tk - 1) // tk
    n_tiles = tk // LANES
    sub = min(INDEXER_SUB, tk)
    qpos_t = i * tq + lax.broadcasted_iota(I32, (1, tq), 1)

    @pl.when(i == 0)
    def _():
        def cast_chunk(c, kn):
            rs = pl.ds(pl.multiple_of(c * tk, tk), tk)
            kf = k_ref[rs, :]
            kb_scr[rs, :] = kf.astype(BF16)
            vb_scr[rs, :] = v_ref[rs, :].astype(BF16)
            kib_scr[rs, :] = kim_ref[rs, 0:IDX_DIM].astype(BF16)
            sq = kf * kf
            return tuple(
                jnp.maximum(kn[g], jnp.max(jnp.sum(sq[:, g * HEAD_DIM:(g + 1) * HEAD_DIM], axis=1, keepdims=True),
                                           axis=0, keepdims=True))
                for g in range(N_KV_HEADS))
        kn = lax.fori_loop(0, k_ref.shape[0] // tk, cast_chunk, (jnp.zeros((1, 1), F32),) * N_KV_HEADS)
        for g in range(N_KV_HEADS):
            kn_scr[g] = jnp.broadcast_to(kn[g], kn_scr.shape[1:])

    for h in range(IDX_HEADS):
        ref = qia_ref if h < IDX_HEADS // 2 else qib_ref
        hh = h % (IDX_HEADS // 2)
        qi_scr[h // 2, (h % 2) * tq:(h % 2 + 1) * tq, :] = ref[:, hh * IDX_DIM:(hh + 1) * IDX_DIM].astype(BF16)
    wt_scr[...] = misc_ref[...].T

    def score_chunk(c, carry):
        for s0 in range(0, tk, sub):
            ks = pl.multiple_of(c * tk, tk) + s0
            ki = kib_scr[pl.ds(ks, sub), :]
            acc = jnp.zeros((sub, tq), F32)
            for hp in range(IDX_HEADS // 2):
                r = jnp.maximum(_dot_nt(ki, qi_scr[hp]), 0.0)
                acc = (acc + r[:, 0:tq] * wt_scr[pl.ds(MISC_WI + 2 * hp, 1), :]
                       + r[:, tq:] * wt_scr[pl.ds(MISC_WI + 2 * hp + 1, 1), :])
            kpos = ks + lax.broadcasted_iota(I32, (sub, tq), 0)
            masked = jnp.where(kpos <= qpos_t, acc, -jnp.inf)
            sc_scr[pl.ds(ks, sub), :] = masked
            for g0 in range(0, sub, SORT_N * 8):
                srt_scr[pl.ds(ks + g0, SORT_N * 8), :] = _sort_groups_desc(masked[g0:g0 + SORT_N * 8, :])
        return carry

    lax.fori_loop(0, nk, score_chunk, 0)
    thr = _select_threshold(sc_scr, j_scr, nk, tk, topk, idx_bits, 0, srt_scr)
    jb = j_scr[...]

    for g in range(N_KV_HEADS):
        kn = jnp.sqrt(kn_scr[g][0:1, :])
        for r in range(HEADS_PER_KV):
            hd = g * HEADS_PER_KV + r
            qh = q_ref[:, hd * HEAD_DIM:(hd + 1) * HEAD_DIM]
            qs_scr[g, r * tq:(r + 1) * tq, :] = qh.astype(BF16)
            mx_scr[g, r * tq:(r + 1) * tq, :] = jnp.sqrt(jnp.sum(qh * qh, axis=1, keepdims=True)) * kn * BOUND_SLACK
    l_scr[...] = jnp.zeros(l_scr.shape, F32)
    acc_scr[...] = jnp.zeros(acc_scr.shape, F32)

    def logits(g, ks, bias4):
        kc = kb_scr[pl.ds(ks, tk), g * HEAD_DIM:(g + 1) * HEAD_DIM]
        return _dot_nt(qs_scr[g], kc) + bias4

    def exp_chunk(ks, bias):
        bias4 = jnp.concatenate([bias] * HEADS_PER_KV, axis=0)
        for g in range(N_KV_HEADS):
            s = logits(g, ks, bias4)
            m = mx_scr[g]
            p = [jnp.exp(s[:, t * LANES:(t + 1) * LANES] - m) for t in range(n_tiles)]
            l_scr[g] += functools.reduce(lambda x, y: x + y, p)
            pb = jnp.concatenate(p, axis=1).astype(BF16)
            vc = vb_scr[pl.ds(ks, tk), g * HEAD_DIM:(g + 1) * HEAD_DIM]
            acc_scr[g] += jnp.dot(pb, vc, preferred_element_type=F32)

    def bound_pass(c, carry):
        ks = pl.multiple_of(c * tk, tk)
        kpos = ks + lax.broadcasted_iota(I32, (tk, tq), 0)
        sel = _selected(sc_scr[pl.ds(ks, tk), :], kpos, thr, jb) & (kpos <= qpos_t)
        bias = jnp.where(sel, 0.0, NEG).T
        bias_scr[:, pl.ds(ks, tk)] = bias
        exp_chunk(ks, bias)
        return carry

    lax.fori_loop(0, nk, bound_pass, 0)

    def denominators():
        for g in range(N_KV_HEADS):
            mx_scr[g] = jnp.broadcast_to(jnp.sum(l_scr[g], axis=1, keepdims=True), mx_scr.shape[1:])

    denominators()
    l_min = jnp.min(functools.reduce(jnp.minimum, [mx_scr[g] for g in range(N_KV_HEADS)]))

    @pl.when(l_min < SOFTMAX_MIN_DENOM)
    def _():
        mx_scr[...] = jnp.full(mx_scr.shape, NEG, F32)
        l_scr[...] = jnp.zeros(l_scr.shape, F32)
        acc_scr[...] = jnp.zeros(acc_scr.shape, F32)

        def max_pass(c, carry):
            ks = pl.multiple_of(c * tk, tk)
            bias4 = jnp.concatenate([bias_scr[:, pl.ds(ks, tk)]] * HEADS_PER_KV, axis=0)
            for g in range(N_KV_HEADS):
                s = logits(g, ks, bias4)
                m = mx_scr[g]
                for t in range(n_tiles):
                    m = jnp.maximum(m, s[:, t * LANES:(t + 1) * LANES])
                mx_scr[g] = m
            return carry

        lax.fori_loop(0, nk, max_pass, 0)
        for g in range(N_KV_HEADS):
            mx_scr[g] = jnp.broadcast_to(jnp.max(mx_scr[g], axis=1, keepdims=True), mx_scr.shape[1:])

        def exp_pass(c, carry):
            ks = pl.multiple_of(c * tk, tk)
            exp_chunk(ks, bias_scr[:, pl.ds(ks, tk)])
            return carry

        lax.fori_loop(0, nk, exp_pass, 0)
        denominators()

    for g in range(N_KV_HEADS):
        o = acc_scr[g] / mx_scr[g]
        for r in range(HEADS_PER_KV):
            hd = g * HEADS_PER_KV + r
            o_ref[:, hd * HEAD_DIM:(hd + 1) * HEAD_DIM] = o[r * tq:(r + 1) * tq, :].astype(BF16)


def _attn_prompt(z, b, s, topk):
    nq = s // Q_BLOCK
    tk = min(ATTN_TK, s)
    idx_bits = max(1, (s - 1).bit_length()) + 1
    body = functools.partial(_attn_prompt_body, tk=tk, topk=topk, idx_bits=idx_bits)
    rows = HEADS_PER_KV * Q_BLOCK
    return pl.pallas_call(
        body,
        out_shape=jax.ShapeDtypeStruct((b * s, ATTN_WIDTH), BF16),
        grid=(b, nq),
        in_specs=[
            pl.BlockSpec((Q_BLOCK, ATTN_WIDTH), lambda bb, i: (bb * nq + i, 0)),
            pl.BlockSpec((Q_BLOCK, QI_WIDTH // 2), lambda bb, i: (bb * nq + i, C_QI // (QI_WIDTH // 2))),
            pl.BlockSpec((Q_BLOCK, QI_WIDTH // 2), lambda bb, i: (bb * nq + i, C_QI // (QI_WIDTH // 2) + 1)),
            pl.BlockSpec((Q_BLOCK, LANES), lambda bb, i: (bb * nq + i, C_MISC // LANES)),
            pl.BlockSpec((s, KV_WIDTH), lambda bb, i: (bb, C_K // KV_WIDTH)),
            pl.BlockSpec((s, KV_WIDTH), lambda bb, i: (bb, C_V // KV_WIDTH)),
            pl.BlockSpec((s, LANES), lambda bb, i: (bb, C_MISC // LANES)),
        ],
        out_specs=pl.BlockSpec((Q_BLOCK, ATTN_WIDTH), lambda bb, i: (bb * nq + i, 0)),
        scratch_shapes=[
            pltpu.VMEM((s, Q_BLOCK), F32),
            pltpu.VMEM((s, Q_BLOCK), F32),
            pltpu.VMEM((1, Q_BLOCK), I32),
            pltpu.VMEM((Q_BLOCK, s), F32),
            pltpu.VMEM((s, KV_WIDTH), BF16),
            pltpu.VMEM((s, KV_WIDTH), BF16),
            pltpu.VMEM((s, IDX_DIM), BF16),
            pltpu.VMEM((N_KV_HEADS, 8, LANES), F32),
            pltpu.VMEM((IDX_HEADS // 2, 2 * Q_BLOCK, IDX_DIM), BF16),
            pltpu.VMEM((LANES, Q_BLOCK), F32),
            pltpu.VMEM((N_KV_HEADS, rows, HEAD_DIM), BF16),
            pltpu.VMEM((N_KV_HEADS, rows, LANES), F32),
            pltpu.VMEM((N_KV_HEADS, rows, LANES), F32),
            pltpu.VMEM((N_KV_HEADS, rows, HEAD_DIM), F32),
        ],
        compiler_params=_params(("parallel", "arbitrary")),
        name="attn_prompt",
    )(z, z, z, z, z, z, z)


def _log_decay(misc, w2, gb):
    x = jnp.dot(misc, w2, precision=HIGHEST, preferred_element_type=F32) + gb
    return (jnp.minimum(x, 0.0) - jnp.log1p(jnp.exp(-jnp.abs(x)))) * (1.0 / GLA_GATE_TAU)


def _gla_out(o, nw, gr):
    return (_rms(o, nw) * (gr * jax.nn.sigmoid(gr))).astype(BF16)


GLA_C = 128
GLA_SUB = 8
GLA_HPS = 4


def _gla_prompt_body(gq_ref, gk_ref, gv_ref, misc_ref, gr_ref, w2_ref, gb_ref, nw_ref,
                     og_ref, sfin_ref, st_scr, a_scr, b_scr, k_scr, o_scr, *, nchunk):
    t = pl.program_id(2)
    c_ = GLA_C

    @pl.when(t == 0)
    def _():
        st_scr[...] = jnp.zeros_like(st_scr)

    a_scr[...] = jnp.zeros_like(a_scr)
    row = lax.broadcasted_iota(I32, (c_, c_), 0)
    col = lax.broadcasted_iota(I32, (c_, c_), 1)
    tri = jnp.where(col <= row, 1.0, 0.0)
    sub_row = lax.broadcasted_iota(I32, (GLA_SUB, LANES), 0)
    sub_col = lax.broadcasted_iota(I32, (GLA_SUB, GLA_SUB), 1)

    def head_chunk(hh, r0):
        ksl = slice(hh * GLA_DK, (hh + 1) * GLA_DK)
        vsl = slice(hh * GLA_DV, (hh + 1) * GLA_DV)
        q = gq_ref[pl.ds(r0, c_), ksl] * (GLA_DK ** -0.5)
        k = gk_ref[pl.ds(r0, c_), ksl]
        v = gv_ref[pl.ds(r0, c_), vsl].astype(BF16)
        g = _log_decay(misc_ref[pl.ds(r0, c_), :], w2_ref[:, ksl], gb_ref[:, ksl])
        b = jnp.dot(tri, g, precision=HIGHEST, preferred_element_type=F32)
        b_scr[hh] = b
        k_scr[hh] = k
        st = st_scr[hh]
        o = _dot_nt((q * jnp.exp(b)).astype(BF16), st.astype(BF16))

        n = c_ // 2
        while n >= GLA_SUB:
            for rb in range(n, c_, 2 * n):
                bref = b[rb:rb + 1, :]
                qs = q[rb:rb + n, :] * jnp.exp(b[rb:rb + n, :] - bref)
                ks = k[rb - n:rb, :] * jnp.exp(bref - b[rb - n:rb, :])
                a_scr[hh, rb:rb + n, rb - n:rb] = _dot_nt(qs.astype(BF16), ks.astype(BF16))
            n //= 2
        for blk in range(c_ // GLA_SUB):
            lo = blk * GLA_SUB
            qb = q[lo:lo + GLA_SUB, :]
            bb = b[lo:lo + GLA_SUB, :]
            ad = jnp.zeros((GLA_SUB, GLA_SUB), F32)
            for jj in range(GLA_SUB):
                bj = b_scr[hh, pl.ds(lo + jj, 1), :]
                kj = k_scr[hh, pl.ds(lo + jj, 1), :]
                w = jnp.exp(jnp.where(sub_row >= jj, bb - bj, NEG))
                colj = jnp.sum(qb * kj * w, axis=1, keepdims=True)
                ad = jnp.where(sub_col == jj, colj, ad)
            a_scr[hh, lo:lo + GLA_SUB, lo:lo + GLA_SUB] = ad

        o = o + jnp.dot(a_scr[hh].astype(BF16), v, preferred_element_type=F32)
        o_scr[pl.ds(r0, c_), vsl] = o
        b_last = b[c_ - 1:c_, :]
        kd = (k * jnp.exp(b_last - b)).astype(BF16)
        kv = lax.dot_general(v, kd, (((0,), (0,)), ((), ())), preferred_element_type=F32)
        st_scr[hh] = st * jnp.exp(b_last) + kv

    def chunk(ci, carry):
        r0 = pl.multiple_of(ci * c_, c_)
        for hh in range(GLA_HPS):
            head_chunk(hh, r0)
        return carry

    lax.fori_loop(0, nchunk, chunk, 0)
    for hh in range(GLA_HPS):
        vsl = slice(hh * GLA_DV, (hh + 1) * GLA_DV)
        og_ref[:, vsl] = _gla_out(o_scr[:, vsl], nw_ref[...], gr_ref[:, vsl])

    @pl.when(t == pl.num_programs(2) - 1)
    def _():
        for hh in range(GLA_HPS):
            sfin_ref[0, hh] = st_scr[hh].T


def _gla_prompt(z, w2p, gbias, norm_w, b, s):
    tb = min(GLA_TB, s)
    nt = s // tb
    body = functools.partial(_gla_prompt_body, nchunk=tb // GLA_C)

    def rowblk(bb, t):
        return bb * nt + t

    kw, vw = GLA_HPS * GLA_DK, GLA_HPS * GLA_DV
    return pl.pallas_call(
        body,
        out_shape=(jax.ShapeDtypeStruct((b * s, GLA_VAL_WIDTH), BF16),
                   jax.ShapeDtypeStruct((b, GLA_HEADS, GLA_DK, GLA_DV), F32)),
        grid=(b, GLA_HEADS // GLA_HPS, nt),
        in_specs=[
            pl.BlockSpec((tb, kw), lambda bb, h, t: (rowblk(bb, t), C_GQ // kw + h)),
            pl.BlockSpec((tb, kw), lambda bb, h, t: (rowblk(bb, t), C_GK // kw + h)),
            pl.BlockSpec((tb, vw), lambda bb, h, t: (rowblk(bb, t), C_GV // vw + h)),
            pl.BlockSpec((tb, LANES), lambda bb, h, t: (rowblk(bb, t), C_MISC // LANES)),
            pl.BlockSpec((tb, vw), lambda bb, h, t: (rowblk(bb, t), C_GR // vw + h)),
            pl.BlockSpec((LANES, kw), lambda bb, h, t: (0, h)),
            pl.BlockSpec((1, kw), lambda bb, h, t: (0, h)),
            pl.BlockSpec((1, GLA_DV), lambda bb, h, t: (0, 0)),
        ],
        out_specs=(pl.BlockSpec((tb, vw), lambda bb, h, t: (rowblk(bb, t), h)),
                   pl.BlockSpec((1, GLA_HPS, GLA_DK, GLA_DV), lambda bb, h, t: (bb, h, 0, 0))),
        scratch_shapes=[
            pltpu.VMEM((GLA_HPS, GLA_DV, GLA_DK), F32),
            pltpu.VMEM((GLA_HPS, GLA_C, GLA_C), F32),
            pltpu.VMEM((GLA_HPS, GLA_C, GLA_DK), F32),
            pltpu.VMEM((GLA_HPS, GLA_C, GLA_DK), F32),
            pltpu.VMEM((tb, vw), F32),
        ],
        compiler_params=_params(("parallel", "parallel", "arbitrary")),
        name="gla_prompt",
    )(z, z, z, z, z, w2p, gbias, norm_w.reshape(1, GLA_DV))


def _merge_body(oa_ref, og_ref, ga_ref, gg_ref, wa_ref, wg_ref, o_ref):
    pa = jnp.dot(oa_ref[...], wa_ref[...], preferred_element_type=F32)
    pg = jnp.dot(og_ref[...], wg_ref[...], preferred_element_type=F32)
    o_ref[...] = (jax.nn.sigmoid(ga_ref[...]) * pa + jax.nn.sigmoid(gg_ref[...]) * pg).astype(BF16)


def _merge(o_attn, o_gla, z, wa, wg, tm):
    rows = o_attn.shape[0]
    d = wa.shape[1]
    return pl.pallas_call(
        _merge_body,
        out_shape=jax.ShapeDtypeStruct((rows, d), BF16),
        grid=(rows // tm,),
        in_specs=[
            pl.BlockSpec((tm, ATTN_WIDTH), lambda i: (i, 0)),
            pl.BlockSpec((tm, GLA_VAL_WIDTH), lambda i: (i, 0)),
            pl.BlockSpec((tm, d), lambda i: (i, C_GA // d)),
            pl.BlockSpec((tm, d), lambda i: (i, C_GA // d + 1)),
            pl.BlockSpec((ATTN_WIDTH, d), lambda i: (0, 0)),
            pl.BlockSpec((GLA_VAL_WIDTH, d), lambda i: (0, 0)),
        ],
        out_specs=pl.BlockSpec((tm, d), lambda i: (i, 0)),
        compiler_params=_params(("parallel",)),
        name="merge",
    )(o_attn, o_gla, z, z, wa, wg)


def _out_proj_body(m_ref, h_ref, w_ref, pw_ref, o_ref):
    y = jnp.dot(m_ref[...], w_ref[...], preferred_element_type=F32)
    o_ref[...] = h_ref[...] + _rms(y, pw_ref[...])


def _out_proj(merged, h, w_out, post_w, tm):
    rows, d = h.shape
    return pl.pallas_call(
        _out_proj_body,
        out_shape=jax.ShapeDtypeStruct((rows, d), F32),
        grid=(rows // tm,),
        in_specs=[
            pl.BlockSpec((tm, d), lambda i: (i, 0)),
            pl.BlockSpec((tm, d), lambda i: (i, 0)),
            pl.BlockSpec((d, d), lambda i: (0, 0)),
            pl.BlockSpec((1, d), lambda i: (0, 0)),
        ],
        out_specs=pl.BlockSpec((tm, d), lambda i: (i, 0)),
        compiler_params=_params(("parallel",)),
        name="out_proj",
    )(merged, h, w_out, post_w.reshape(1, d))


def _page_copies(pt_ref, hbm, buf, sem, seq, slot, n_pages, dst):
    return [pltpu.make_async_copy(hbm.at[pt_ref[seq, p]], buf.at[slot].at[dst(p)], sem)
            for p in range(n_pages)]


GATHER_SLOTS = 3


def _gather_step(fetch):
    b = pl.program_id(0)
    ahead = GATHER_SLOTS - 1

    @pl.when(b == 0)
    def _():
        for a in range(ahead):
            @pl.when(a < pl.num_programs(0))
            def _():
                for cp in fetch(a, a % GATHER_SLOTS):
                    cp.start()

    @pl.when(b + ahead < pl.num_programs(0))
    def _():
        for cp in fetch(b + ahead, (b + ahead) % GATHER_SLOTS):
            cp.start()

    slot = b % GATHER_SLOTS
    for cp in fetch(b, slot):
        cp.wait()
    return slot


def _sample_scores_body(pt_ref, qi_ref, wi_ref, kidx_hbm, o_ref, kbuf, sem, *, n_pages, page):
    def fetch(seq, slot):
        return _page_copies(pt_ref, kidx_hbm, kbuf, sem.at[slot], seq, slot, n_pages,
                            lambda p: (slice(None), pl.ds(p * page, page)))

    slot = _gather_step(fetch)
    s = jnp.dot(qi_ref[0].astype(BF16), kbuf[slot].astype(BF16), preferred_element_type=F32)
    o_ref[0] = jnp.sum(jnp.maximum(s, 0.0) * wi_ref[0], axis=0, keepdims=True)


def _sample_scores(page_table, qi3, wi3, kidx_t):
    nd, n_pages = page_table.shape
    page = kidx_t.shape[2]
    past = n_pages * page
    body = functools.partial(_sample_scores_body, n_pages=n_pages, page=page)
    grid_spec = pltpu.PrefetchScalarGridSpec(
        num_scalar_prefetch=1,
        grid=(nd,),
        in_specs=[
            pl.BlockSpec((1, IDX_HEADS, IDX_DIM), lambda b, pt: (b, 0, 0)),
            pl.BlockSpec((1, IDX_HEADS, 1), lambda b, pt: (b, 0, 0)),
            pl.BlockSpec(memory_space=pl.ANY),
        ],
        out_specs=pl.BlockSpec((1, 1, past), lambda b, pt: (b, 0, 0)),
        scratch_shapes=[pltpu.VMEM((GATHER_SLOTS, IDX_DIM, past), F32), pltpu.SemaphoreType.DMA((GATHER_SLOTS,))],
    )
    return pl.pallas_call(
        body,
        out_shape=jax.ShapeDtypeStruct((nd, 1, past), F32),
        grid_spec=grid_spec,
        compiler_params=_params(("arbitrary",)),
        name="sample_scores",
    )(page_table, qi3, wi3, kidx_t)


def _sample_select_body(sp_ref, qi_ref, misc_ref, bp_ref, bs_ref, sc_scr, j_scr, *, past, tk, topk, idx_bits):
    rows = sp_ref.shape[0]
    misc = misc_ref[...]
    d_in = lax.broadcasted_iota(I32, (LANES, QI_WIDTH), 0)
    c_out = lax.broadcasted_iota(I32, (LANES, QI_WIDTH), 1)
    rep = jnp.where((d_in < IDX_DIM) & (c_out % IDX_DIM == d_in), 1.0, 0.0)
    ki_t = jnp.dot(misc, rep, precision=HIGHEST, preferred_element_type=F32)
    c_in = lax.broadcasted_iota(I32, (QI_WIDTH, LANES), 0)
    l_out = lax.broadcasted_iota(I32, (QI_WIDTH, LANES), 1)
    seg = jnp.where(l_out == MISC_WI + c_in // IDX_DIM, 1.0, 0.0)
    hd = jnp.dot(qi_ref[...] * ki_t, seg, precision=HIGHEST, preferred_element_type=F32)
    lane = lax.broadcasted_iota(I32, (rows, LANES), 1)
    is_wi = (lane >= MISC_WI) & (lane < MISC_GLR)
    s_self = jnp.sum(jnp.where(is_wi, jnp.maximum(hd, 0.0) * misc, 0.0), axis=1, keepdims=True)

    sc_scr[:, 0:past] = sp_ref[...]
    sc_scr[:, past:] = jnp.where(lax.broadcasted_iota(I32, (rows, tk), 1) == 0, s_self, -jnp.inf)
    nk = sc_scr.shape[1] // tk
    thr = _select_threshold(sc_scr, j_scr, nk, tk, topk, idx_bits, 1)
    jb = j_scr[...]
    kpos = lax.broadcasted_iota(I32, sc_scr.shape, 1)
    bias = jnp.where(_selected(sc_scr[...], kpos, thr, jb), 0.0, NEG)
    bp_ref[...] = bias[:, 0:past]
    bs_ref[...] = bias[:, past:past + LANES]


def _sample_select(s_past, qi2, misc, topk):
    nd, past = s_past.shape
    tk = LANES
    width = past + tk
    idx_bits = max(1, (width - 1).bit_length()) + 1
    body = functools.partial(_sample_select_body, past=past, tk=tk, topk=topk, idx_bits=idx_bits)
    return pl.pallas_call(
        body,
        out_shape=(jax.ShapeDtypeStruct((nd, past), F32), jax.ShapeDtypeStruct((nd, LANES), F32)),
        grid=(1,),
        in_specs=[
            pl.BlockSpec((nd, past), lambda i: (0, 0)),
            pl.BlockSpec((nd, QI_WIDTH), lambda i: (0, 0)),
            pl.BlockSpec((nd, LANES), lambda i: (0, 0)),
        ],
        out_specs=(pl.BlockSpec((nd, past), lambda i: (0, 0)), pl.BlockSpec((nd, LANES), lambda i: (0, 0))),
        scratch_shapes=[pltpu.VMEM((nd, width), F32), pltpu.VMEM((nd, 1), I32)],
        compiler_params=_params(("arbitrary",)),
        name="sample_select",
    )(s_past, qi2, misc)


SAMPLE_KCHUNK = 1024


def _sample_attn_body(pt_ref, q_ref, ks_ref, vs_ref, b2_ref, bs_ref, ck_hbm, cv_hbm, o_ref, kbuf, vbuf, sem,
                      *, n_pages, prows):
    def fetch(seq, slot):
        dst = lambda p: (pl.ds(p * prows, prows), slice(None))
        return (_page_copies(pt_ref, ck_hbm, kbuf, sem.at[0, slot], seq, slot, n_pages, dst)
                + _page_copies(pt_ref, cv_hbm, vbuf, sem.at[1, slot], seq, slot, n_pages, dst))

    slot = _gather_step(fetch)
    q = q_ref[0]
    qb = q.astype(BF16)
    total = n_pages * prows
    ch = min(SAMPLE_KCHUNK, total)
    head_grp = lax.broadcasted_iota(I32, (N_HEADS, ch), 0) // HEADS_PER_KV
    row_grp = lax.broadcasted_iota(I32, (N_HEADS, ch), 1) % N_KV_HEADS
    own = head_grp == row_grp
    s_chunks = []
    for c in range(total // ch):
        kc = kbuf[slot, c * ch:(c + 1) * ch, :].astype(BF16)
        s_chunks.append(jnp.where(own, _dot_nt(qb, kc) + b2_ref[0][:, c * ch:(c + 1) * ch], NEG))

    grp0 = lax.broadcasted_iota(I32, (N_HEADS, HEAD_DIM), 0) < HEADS_PER_KV
    k_self = jnp.where(grp0, ks_ref[0][:, 0:HEAD_DIM], ks_ref[0][:, HEAD_DIM:])
    v_self = jnp.where(grp0, vs_ref[0][:, 0:HEAD_DIM], vs_ref[0][:, HEAD_DIM:])
    s_self = jnp.sum(q * k_self, axis=1, keepdims=True) + bs_ref[0][:, 0:1]

    m = s_self
    for s in s_chunks:
        m = jnp.maximum(m, jnp.max(s, axis=1, keepdims=True))
    p_self = jnp.exp(s_self - m)
    l = p_self
    acc = p_self * v_self
    for c, s in enumerate(s_chunks):
        p = jnp.exp(s - m)
        l = l + jnp.sum(p, axis=1, keepdims=True)
        vc = vbuf[slot, c * ch:(c + 1) * ch, :].astype(BF16)
        acc = acc + jnp.dot(p.astype(BF16), vc, preferred_element_type=F32)
    o_ref[0] = (acc / l).astype(BF16)


def _sample_attn(page_table, q3, k_self, v_self, bias2, bias_self, ck2, cv2):
    nd, n_pages = page_table.shape
    prows = ck2.shape[1]
    total = n_pages * prows
    body = functools.partial(_sample_attn_body, n_pages=n_pages, prows=prows)
    grid_spec = pltpu.PrefetchScalarGridSpec(
        num_scalar_prefetch=1,
        grid=(nd,),
        in_specs=[
            pl.BlockSpec((1, N_HEADS, HEAD_DIM), lambda b, pt: (b, 0, 0)),
            pl.BlockSpec((1, 1, KV_WIDTH), lambda b, pt: (b, 0, 0)),
            pl.BlockSpec((1, 1, KV_WIDTH), lambda b, pt: (b, 0, 0)),
            pl.BlockSpec((1, 1, total), lambda b, pt: (b, 0, 0)),
            pl.BlockSpec((1, 1, LANES), lambda b, pt: (b, 0, 0)),
            pl.BlockSpec(memory_space=pl.ANY),
            pl.BlockSpec(memory_space=pl.ANY),
        ],
        out_specs=pl.BlockSpec((1, N_HEADS, HEAD_DIM), lambda b, pt: (b, 0, 0)),
        scratch_shapes=[
            pltpu.VMEM((GATHER_SLOTS, total, HEAD_DIM), F32),
            pltpu.VMEM((GATHER_SLOTS, total, HEAD_DIM), F32),
            pltpu.SemaphoreType.DMA((2, GATHER_SLOTS)),
        ],
    )
    return pl.pallas_call(
        body,
        out_shape=jax.ShapeDtypeStruct((nd, N_HEADS, HEAD_DIM), BF16),
        grid_spec=grid_spec,
        compiler_params=_params(("arbitrary",)),
        name="sample_attn",
    )(page_table, q3, k_self, v_self, bias2, bias_self, ck2, cv2)


def _gla_sample_body(gq_ref, gk_ref, gv_ref, misc_ref, gr_ref, s_ref, w2_ref, gb_ref, nw_ref, og_ref, so_ref):
    eye = jnp.where(lax.broadcasted_iota(I32, (GLA_DK, GLA_DK), 0)
                    == lax.broadcasted_iota(I32, (GLA_DK, GLA_DK), 1), 1.0, 0.0)

    def column(row):
        return jnp.sum(eye * row, axis=1, keepdims=True)

    for sq in range(s_ref.shape[0]):
        misc = misc_ref[sq]
        for h in range(GLA_HEADS):
            ksl = slice(h * GLA_DK, (h + 1) * GLA_DK)
            vsl = slice(h * GLA_DV, (h + 1) * GLA_DV)
            g = _log_decay(misc, w2_ref[:, ksl], gb_ref[:, ksl])
            s_new = column(jnp.exp(g)) * s_ref[sq, h] + column(gk_ref[sq][:, ksl]) * gv_ref[sq][:, vsl]
            so_ref[sq, h] = s_new
            o = jnp.sum(column(gq_ref[sq][:, ksl] * (GLA_DK ** -0.5)) * s_new, axis=0, keepdims=True)
            og_ref[sq, :, vsl] = _gla_out(o, nw_ref[...], gr_ref[sq][:, vsl])


def _gla_sample(gq, gk, gv, misc, gr, state, w2p, gbias, norm_w):
    nd = state.shape[0]
    sb = GLA_SAMPLE_SB if nd % GLA_SAMPLE_SB == 0 else 1

    def row3(w):
        return pl.BlockSpec((sb, 1, w), lambda b: (b, 0, 0))

    st_spec = pl.BlockSpec((sb, GLA_HEADS, GLA_DK, GLA_DV), lambda b: (b, 0, 0, 0))
    return pl.pallas_call(
        _gla_sample_body,
        out_shape=(jax.ShapeDtypeStruct((nd, 1, GLA_VAL_WIDTH), BF16),
                   jax.ShapeDtypeStruct(state.shape, F32)),
        grid=(nd // sb,),
        in_specs=[
            row3(GLA_KEY_WIDTH), row3(GLA_KEY_WIDTH), row3(GLA_VAL_WIDTH), row3(LANES), row3(GLA_VAL_WIDTH),
            st_spec,
            pl.BlockSpec((LANES, GLA_KEY_WIDTH), lambda b: (0, 0)),
            pl.BlockSpec((1, GLA_KEY_WIDTH), lambda b: (0, 0)),
            pl.BlockSpec((1, GLA_DV), lambda b: (0, 0)),
        ],
        out_specs=(row3(GLA_VAL_WIDTH), st_spec),
        compiler_params=_params(("parallel",)),
        name="gla_sample",
    )(gq, gk, gv, misc, gr, state, w2p, gbias, norm_w.reshape(1, GLA_DV))


def _rope_tables(pos, d):
    inv = ROPE_THETA ** (-jnp.arange(0, d, 2, dtype=F32) / d)
    ang = pos.astype(F32)[:, None] * inv[None, :]
    cos, sin = jnp.cos(ang), jnp.sin(ang)
    reps = LANES // d
    return (jnp.tile(jnp.concatenate([cos, cos], axis=-1), (1, reps)),
            jnp.tile(jnp.concatenate([-sin, sin], axis=-1), (1, reps)))


def kernel(x_prompt, x_sample, cache_k, cache_v, cache_kidx, page_table, state_gla,
           ffn1_pre_w, ffn1_w_gate, ffn1_w_up, ffn1_w_down, ffn1_post_w,
           mix_pre_w, w_in, gla_gate_w2, gla_gate_b, gla_norm_w,
           w_proj_attn, w_proj_gla, w_out, mix_post_w,
           ffn2_pre_w, ffn2_w_gate, ffn2_w_up, ffn2_w_down, ffn2_post_w):
    b, s, d = x_prompt.shape
    nd, td, _ = x_sample.shape
    n_pool, page = cache_k.shape[:2]
    n_pages = page_table.shape[1]
    past = n_pages * page
    assert td == 1 and s % Q_BLOCK == 0 and d % MIX_TN == 0 and C_GA % d == 0

    wpa, wpg, wo = w_proj_attn.astype(BF16), w_proj_gla.astype(BF16), w_out.astype(BF16)
    w2p = jnp.zeros((LANES, GLA_KEY_WIDTH), F32).at[MISC_GLR:MISC_GLR + GLA_GATE_RANK].set(gla_gate_w2)
    gbias = gla_gate_b.reshape(1, GLA_KEY_WIDTH)

    def trunk(x, tm, tm_mix, tabs, tab_rows, mixer, f1, f2, w_mix):
        h = _ffn(x, ffn1_pre_w, *f1, ffn1_post_w, tm)
        h, f1b = h if isinstance(h, tuple) else (h, f1)
        z, k_rows, v_rows, *w_mix_b = _mix_in(h, mix_pre_w, w_mix, tabs, tm_mix, tab_rows // tm_mix)
        o_attn, o_gla, s_fin = mixer(z)
        tmm = min(tm, ROW_TM)
        merged = _merge(o_attn, o_gla, z, wpa, wpg, tmm)
        h = _out_proj(merged, h, wo, mix_post_w, tmm)
        y = _ffn(h, ffn2_pre_w, *f2, ffn2_post_w, tm)
        y, f2b = y if isinstance(y, tuple) else (y, f2)
        return y, (k_rows, v_rows, z[:, C_MISC:C_MISC + IDX_DIM]), s_fin, f1b, f2b, (w_mix_b or [w_mix])[0]

    pos_p = jnp.arange(s, dtype=I32)
    tm_p = ROW_TM if s % ROW_TM == 0 else Q_BLOCK
    tabs_p = _rope_tables(pos_p, HEAD_DIM) + _rope_tables(pos_p, IDX_DIM)
    top_p = min(TOPK_MAX, s // 4)

    def mixer_p(z):
        o_attn = _attn_prompt(z, b, s, top_p)
        o_gla, s_fin = _gla_prompt(z, w2p, gbias, gla_norm_w, b, s)
        return o_attn, o_gla, s_fin

    tm_mix = MIX_TM if s % MIX_TM == 0 else tm_p

    pos_s = jnp.full((nd,), past, I32)
    tabs_s = _rope_tables(pos_s, HEAD_DIM) + _rope_tables(pos_s, IDX_DIM)
    top_s = min(TOPK_MAX, (past + td) // 4)
    ck2 = cache_k.reshape(n_pool, page * N_KV_HEADS, HEAD_DIM)
    cv2 = cache_v.reshape(n_pool, page * N_KV_HEADS, HEAD_DIM)
    kidx_t = jnp.swapaxes(cache_kidx, 1, 2)

    def mixer_s(z):
        misc = z[:, C_MISC:C_MISC + LANES]
        qi2 = z[:, C_QI:C_QI + QI_WIDTH]
        s_past = _sample_scores(page_table, qi2.reshape(nd, IDX_HEADS, IDX_DIM),
                                misc[:, MISC_WI:MISC_GLR].reshape(nd, IDX_HEADS, 1), kidx_t)
        bias_past, bias_self = _sample_select(s_past.reshape(nd, past), qi2, misc, top_s)
        bias2 = jnp.repeat(bias_past, N_KV_HEADS, axis=1).reshape(nd, 1, past * N_KV_HEADS)
        o_attn = _sample_attn(page_table, z[:, C_Q:C_Q + ATTN_WIDTH].reshape(nd, N_HEADS, HEAD_DIM),
                              z[:, C_K:C_K + KV_WIDTH].reshape(nd, 1, KV_WIDTH),
                              z[:, C_V:C_V + KV_WIDTH].reshape(nd, 1, KV_WIDTH),
                              bias2, bias_self.reshape(nd, 1, LANES), ck2, cv2)
        o_gla, s_fin = _gla_sample(z[:, C_GQ:C_GQ + GLA_KEY_WIDTH].reshape(nd, 1, GLA_KEY_WIDTH),
                                   z[:, C_GK:C_GK + GLA_KEY_WIDTH].reshape(nd, 1, GLA_KEY_WIDTH),
                                   z[:, C_GV:C_GV + GLA_VAL_WIDTH].reshape(nd, 1, GLA_VAL_WIDTH),
                                   misc.reshape(nd, 1, LANES),
                                   z[:, C_GR:C_GR + GLA_VAL_WIDTH].reshape(nd, 1, GLA_VAL_WIDTH),
                                   state_gla, w2p, gbias, gla_norm_w)
        return o_attn.reshape(nd, ATTN_WIDTH), o_gla.reshape(nd, GLA_VAL_WIDTH), s_fin

    y_s, kv_s, gla_s, f1b, f2b, w_mix_b = trunk(
        x_sample.reshape(nd, d), nd, nd, tabs_s, nd, mixer_s,
        (ffn1_w_gate, ffn1_w_up, ffn1_w_down), (ffn2_w_gate, ffn2_w_up, ffn2_w_down), w_in.T)
    y_p, kv_p, gla_p, _, _, _ = trunk(x_prompt.reshape(b * s, d), tm_p, tm_mix, tabs_p, s, mixer_p, f1b, f2b, w_mix_b)

    def kv_out(kv, n, t):
        k_rows, v_rows, ki = kv
        return (k_rows.reshape(n, t, N_KV_HEADS, HEAD_DIM), v_rows.reshape(n, t, N_KV_HEADS, HEAD_DIM),
                ki.reshape(n, t, IDX_DIM))

    k_p, v_p, ki_p = kv_out(kv_p, b, s)
    k_s, v_s, ki_s = kv_out(kv_s, nd, td)
    return (y_p.reshape(b, s, d), y_s.reshape(nd, td, d), k_p, v_p, ki_p, gla_p, k_s, v_s, ki_s, gla_s)
```

```python
import functools

import jax
import jax.numpy as jnp
from jax import lax
from jax.experimental import pallas as pl
from jax.experimental.pallas import tpu as pltpu

F32, BF16, I32 = jnp.float32, jnp.bfloat16, jnp.int32
HIGHEST = lax.Precision.HIGHEST

N_HEADS = 8
N_KV_HEADS = 2
HEAD_DIM = 128
IDX_HEADS = 16
IDX_DIM = 64
TOPK_MAX = 256
Q_BLOCK = 128
ROPE_THETA = 10000.0
GLA_HEADS = 4
GLA_DK = 128
GLA_DV = 256
GLA_GATE_RANK = 16
GLA_GATE_TAU = 16.0
RMS_EPS = 1e-6

LANES = 128
ATTN_WIDTH = N_HEADS * HEAD_DIM
KV_WIDTH = N_KV_HEADS * HEAD_DIM
QI_WIDTH = IDX_HEADS * IDX_DIM
GLA_KEY_WIDTH = GLA_HEADS * GLA_DK
GLA_VAL_WIDTH = GLA_HEADS * GLA_DV
HEADS_PER_KV = N_HEADS // N_KV_HEADS

MIX_TN = 512

C_Q = 0
C_K = C_Q + ATTN_WIDTH
C_V = C_K + KV_WIDTH
C_QI = C_V + KV_WIDTH
C_MISC = C_QI + QI_WIDTH
C_GQ = -(-(C_MISC + LANES) // MIX_TN) * MIX_TN
C_GK = C_GQ + GLA_KEY_WIDTH
C_GV = C_GK + GLA_KEY_WIDTH
C_GR = C_GV + GLA_VAL_WIDTH
C_GA = C_GR + GLA_VAL_WIDTH
MISC_WI, MISC_GLR = IDX_DIM, IDX_DIM + IDX_HEADS

ROW_TM = 512
MIX_TM = 1024
FFN_TF = 512
ATTN_TK = 512
INDEXER_SUB = 256
GLA_TB = 512
GLA_SAMPLE_SB = 16
VMEM_LIMIT = 56 * 1024 * 1024

NEG = -1e30
SOFTMAX_MIN_DENOM = 2.0 ** -60
BOUND_SLACK = 1.02
INT_MIN = -2 ** 31
INT_ABS_MASK = 0x7FFFFFFF
NEG_INF_KEY = -2139095041


def _params(sem, vmem=VMEM_LIMIT):
    return pltpu.CompilerParams(dimension_semantics=sem, vmem_limit_bytes=vmem)


def _rms(x, w):
    return x * lax.rsqrt(jnp.mean(x * x, axis=-1, keepdims=True) + RMS_EPS) * w


def _dot_nt(a, b):
    return lax.dot_general(a, b, (((1,), (1,)), ((), ())), preferred_element_type=F32)


def _ffn_body(x_ref, prew_ref, wg_ref, wu_ref, wd_ref, postw_ref, o_ref, *rest):
    *wb_refs, z_scr, acc_scr = rest
    j = pl.program_id(1)

    @pl.when(j == 0)
    def _():
        z_scr[...] = _rms(x_ref[...], prew_ref[...]).astype(BF16)
        acc_scr[...] = jnp.zeros_like(acc_scr)

    wg, wu, wd = wg_ref[...].astype(BF16), wu_ref[...].astype(BF16), wd_ref[...].astype(BF16)
    for ref, w in zip(wb_refs, (wg, wu, wd)):
        ref[...] = w
    z = z_scr[...]
    g = jnp.dot(z, wg, preferred_element_type=F32)
    u = jnp.dot(z, wu, preferred_element_type=F32)
    a = (g * jax.nn.sigmoid(g) * u).astype(BF16)
    acc_scr[...] += jnp.dot(a, wd, preferred_element_type=F32)

    @pl.when(j == pl.num_programs(1) - 1)
    def _():
        o_ref[...] = x_ref[...] + 0.5 * _rms(acc_scr[...], postw_ref[...])


def _ffn(x, pre_w, wg, wu, wd, post_w, tm):
    rows, d = x.shape
    dff = wg.shape[1]
    tf = FFN_TF if dff % FFN_TF == 0 else dff
    emit = wg.dtype == F32
    assert not emit or rows == tm
    w_specs = [
        pl.BlockSpec((d, tf), lambda i, j: (0, j)),
        pl.BlockSpec((d, tf), lambda i, j: (0, j)),
        pl.BlockSpec((tf, d), lambda i, j: (j, 0)),
    ]
    y_shape = jax.ShapeDtypeStruct((rows, d), F32)
    y_spec = pl.BlockSpec((tm, d), lambda i, j: (i, 0))
    out = pl.pallas_call(
        _ffn_body,
        out_shape=(y_shape, *(jax.ShapeDtypeStruct(w.shape, BF16) for w in (wg, wu, wd))) if emit else y_shape,
        grid=(rows // tm, dff // tf),
        in_specs=[
            pl.BlockSpec((tm, d), lambda i, j: (i, 0)),
            pl.BlockSpec((1, d), lambda i, j: (0, 0)),
            *w_specs,
            pl.BlockSpec((1, d), lambda i, j: (0, 0)),
        ],
        out_specs=(y_spec, *w_specs) if emit else y_spec,
        scratch_shapes=[pltpu.VMEM((tm, d), BF16), pltpu.VMEM((tm, d), F32)],
        compiler_params=_params(("parallel", "arbitrary")),
        name="ffn",
    )(x, pre_w.reshape(1, d), wg, wu, wd, post_w.reshape(1, d))
    return (out[0], tuple(out[1:])) if emit else out


def _mix_in_body(x_ref, prew_ref, wt_ref, wglr_ref, c128_ref, s128_ref, c64_ref, s64_ref, o_ref, kf_ref, vf_ref,
                 *rest):
    *wb_ref, u_scr = rest
    j = pl.program_id(1)

    @pl.when(j == 0)
    def _():
        u_scr[...] = _rms(x_ref[...], prew_ref[...]).astype(BF16)

    if wb_ref:
        w_ref = wb_ref[0]
        w_ref[...] = wt_ref[...].astype(BF16)

        @pl.when(j == C_MISC // MIX_TN)
        def _():
            row = lax.broadcasted_iota(I32, (LANES, w_ref.shape[1]), 0)
            w_ref[0:LANES, :] = jnp.where(row < MISC_GLR, w_ref[0:LANES, :], jnp.where(
                row < MISC_GLR + GLA_GATE_RANK, wglr_ref[...].astype(BF16), jnp.zeros((), BF16)))
            w_ref[LANES:, :] = jnp.zeros((MIX_TN - LANES, w_ref.shape[1]), BF16)
    else:
        w_ref = wt_ref

    o_ref[...] = _dot_nt(u_scr[...], w_ref[...])

    def rope128(x):
        return x * c128_ref[...] + pltpu.roll(x, HEAD_DIM // 2, 1) * s128_ref[...]

    def rope64(x):
        lane = lax.broadcasted_iota(I32, x.shape, 1)
        first = (lane % IDX_DIM) < (IDX_DIM // 2)
        rot = jnp.where(first, pltpu.roll(x, LANES - IDX_DIM // 2, 1), pltpu.roll(x, IDX_DIM // 2, 1))
        return x * c64_ref[...] + rot * s64_ref[...]

    def sl(t):
        return slice(t * LANES, (t + 1) * LANES)

    @pl.when(j < 2)
    def _():
        for t in range(4):
            o_ref[:, sl(t)] = rope128(o_ref[:, sl(t)]) * (HEAD_DIM ** -0.5)

    @pl.when(j == 2)
    def _():
        tm = o_ref.shape[0]
        for t in range(N_KV_HEADS):
            k = rope128(o_ref[:, sl(t)])
            o_ref[:, sl(t)] = k
            kf_ref[pl.ds(t, tm, stride=N_KV_HEADS), :] = k
            vf_ref[pl.ds(t, tm, stride=N_KV_HEADS), :] = o_ref[:, sl(N_KV_HEADS + t)]

    @pl.when((j == 3) | (j == 4))
    def _():
        for t in range(4):
            o_ref[:, sl(t)] = rope64(o_ref[:, sl(t)])

    @pl.when(j == 5)
    def _():
        x = o_ref[:, sl(0)]
        lane = lax.broadcasted_iota(I32, x.shape, 1)
        wi_scale = IDX_HEADS ** -0.5 * IDX_DIM ** -0.5
        o_ref[:, sl(0)] = jnp.where(lane < MISC_WI, rope64(x), jnp.where(lane < MISC_GLR, x * wi_scale, x))


def _mix_in(h, pre_w, w, tabs, tm, tab_blocks):
    rows, d = h.shape
    zw = C_GA + 2 * d
    emit = w.dtype == F32
    if emit:
        head = C_MISC + MISC_GLR
        gla_end = head + 2 * GLA_KEY_WIDTH + GLA_VAL_WIDTH
        tail = gla_end + GLA_GATE_RANK
        assert rows == tm and w.shape == (tail + GLA_VAL_WIDTH + 2 * d, d) and gla_end >= MISC_GLR
        sub = 8
        assert head % sub == 0 and tail % sub == 0 and (gla_end - MISC_GLR) % sub == 0
        tn8 = MIX_TN // sub

        def src_row(i, j):
            return (sub * jnp.where(j <= C_MISC // MIX_TN, j * tn8,
                                    jnp.where(j < C_GR // MIX_TN, head // sub + (j - C_GQ // MIX_TN) * tn8,
                                              tail // sub + (j - C_GR // MIX_TN) * tn8)), 0)

        w_specs = [pl.BlockSpec((pl.Element(MIX_TN), pl.Element(d)), src_row),
                   pl.BlockSpec((pl.Element(LANES), pl.Element(d)), lambda i, j: (gla_end - MISC_GLR, 0))]
    else:
        assert w.shape == (zw, d)
        w_specs = [pl.BlockSpec((MIX_TN, d), lambda i, j: (j, 0)), pl.BlockSpec((LANES, d), lambda i, j: (0, 0))]
    wb_shape = (jax.ShapeDtypeStruct((zw, d), BF16),) if emit else ()
    wb_spec = (pl.BlockSpec((MIX_TN, d), lambda i, j: (j, 0)),) if emit else ()

    c128, s128, c64, s64 = tabs
    tab_spec = pl.BlockSpec((tm, LANES), lambda i, j: (i % tab_blocks, 0))
    kv_shape = jax.ShapeDtypeStruct((rows * N_KV_HEADS, HEAD_DIM), F32)
    kv_spec = pl.BlockSpec((tm * N_KV_HEADS, HEAD_DIM), lambda i, j: (i, 0))
    return pl.pallas_call(
        _mix_in_body,
        out_shape=(jax.ShapeDtypeStruct((rows, zw), F32), kv_shape, kv_shape, *wb_shape),
        grid=(rows // tm, zw // MIX_TN),
        in_specs=[
            pl.BlockSpec((tm, d), lambda i, j: (i, 0)),
            pl.BlockSpec((1, d), lambda i, j: (0, 0)),
            *w_specs,
            tab_spec, tab_spec, tab_spec, tab_spec,
        ],
        out_specs=(pl.BlockSpec((tm, MIX_TN), lambda i, j: (i, j)), kv_spec, kv_spec, *wb_spec),
        scratch_shapes=[pltpu.VMEM((tm, d), BF16)],
        compiler_params=_params(("parallel", "arbitrary")),
        name="mix_in",
    )(h, pre_w.reshape(1, d), w, w, c128, s128, c64, s64)


def _key_to_float(key):
    key = jnp.maximum(key, NEG_INF_KEY)
    return lax.bitcast_convert_type(key ^ ((key >> 31) & jnp.int32(INT_ABS_MASK)), F32)


SORT_N = 16


def _sort_pairs(n):
    pairs, p = [], 1
    while p < n:
        k = p
        while k >= 1:
            for j in range(k % p, n - k, 2 * k):
                for i in range(min(k, n - j - k)):
                    if (i + j) // (2 * p) == (i + j + k) // (2 * p):
                        pairs.append((i + j, i + j + k))
            k //= 2
        p *= 2
    return pairs


def _sort_groups_desc(x):
    v = [x[j * 8:(j + 1) * 8, :] for j in range(SORT_N)]
    for i, j in _sort_pairs(SORT_N):
        v[i], v[j] = jnp.maximum(v[i], v[j]), jnp.minimum(v[i], v[j])
    return jnp.concatenate(v, axis=0)


def _count_sorted(v, t, cmp):
    c8 = cmp(v[7], t)
    c4 = cmp(jnp.where(c8, v[11], v[3]), t)
    c2 = cmp(jnp.where(c8, jnp.where(c4, v[13], v[9]), jnp.where(c4, v[5], v[1])), t)
    e = [jnp.where(c2, v[4 * a + 2], v[4 * a]) for a in range(4)]
    c1 = cmp(jnp.where(c8, jnp.where(c4, e[3], e[2]), jnp.where(c4, e[1], e[0])), t)
    low = (jnp.where(c8, 8.0, 0.0) + jnp.where(c4, 4.0, 0.0)) + (jnp.where(c2, 2.0, 0.0) + jnp.where(c1, 1.0, 0.0))
    return jnp.where(cmp(v[15], t), 16.0, low)


def _select_threshold(sc_ref, j_ref, nk, tk, topk, idx_bits, key_axis, srt_ref=None):
    n_other = sc_ref.shape[1 - key_axis]
    vec = (n_other, 1) if key_axis == 1 else (1, n_other)
    step = LANES if key_axis == 1 else 8

    def count_cmp(t, cmp):
        if srt_ref is None:
            return count(lambda blk, ks: cmp(blk, t))
        tb = jnp.broadcast_to(t, (8, n_other))

        def body(c, acc):
            ks = pl.multiple_of(c * tk, tk)
            for g0 in range(0, tk, SORT_N * 8):
                v = [srt_ref[pl.ds(ks + g0 + j * 8, 8), :] for j in range(SORT_N)]
                acc = acc + _count_sorted(v, tb, cmp)
            return acc
        return jnp.sum(lax.fori_loop(0, nk, body, jnp.zeros((8, n_other), F32)), axis=0, keepdims=True)

    def count(pred):
        def body(c, acc):
            ks = pl.multiple_of(c * tk, tk)
            blk = sc_ref[:, pl.ds(ks, tk)] if key_axis == 1 else sc_ref[pl.ds(ks, tk), :]
            m = jnp.where(pred(blk, ks), 1.0, 0.0)
            parts = [lax.slice_in_dim(m, a, a + step, axis=key_axis) for a in range(0, tk, step)]
            while len(parts) > 1:
                parts = [parts[a] + parts[a + 1] for a in range(0, len(parts), 2)]
            return acc + parts[0]
        acc0 = jnp.zeros((n_other, step) if key_axis == 1 else (step, n_other), F32)
        return jnp.sum(lax.fori_loop(0, nk, body, acc0), axis=key_axis, keepdims=True)

    def bit_body(bi, t):
        cand = t ^ lax.shift_left(jnp.int32(1), 31 - bi)
        return jnp.where(count_cmp(_key_to_float(cand), jnp.greater_equal) >= topk, cand, t)

    thr_key = lax.fori_loop(0, 32, bit_body, jnp.full(vec, INT_MIN, I32))
    thr = _key_to_float(thr_key)

    cnt_ge = count_cmp(thr, jnp.greater_equal)
    cnt_gt = count_cmp(thr, jnp.greater)
    need = topk - cnt_gt
    tie = jnp.where((cnt_ge > topk) & (thr_key > NEG_INF_KEY), 1.0, 0.0)
    j_ref[...] = jnp.full(vec, 1 << idx_bits, I32)

    @pl.when(jnp.max(tie) > 0.0)
    def _():
        def jbit(bi, jb):
            cand = jb + lax.shift_left(jnp.int32(1), idx_bits - 1 - bi)

            def pred(blk, ks):
                kpos = ks + lax.broadcasted_iota(I32, blk.shape, key_axis)
                return (blk == thr) & (kpos < cand)
            return jnp.where(count(pred) <= need, cand, jb)
        j_ref[...] = lax.fori_loop(0, idx_bits, jbit, jnp.zeros(vec, I32))

    return thr


def _selected(score, kpos, thr, jb):
    return (score > thr) | ((score == thr) & (kpos < jb))


def _attn_prompt_body(q_ref, qia_ref, qib_ref, misc_ref, k_ref, v_ref, kim_ref, o_ref,
                      sc_scr, srt_scr, j_scr, bias_scr, kb_scr, vb_scr, kib_scr, kn_scr, qi_scr, wt_scr, qs_scr,
                      mx_scr, l_scr, acc_scr, *, tk, topk, idx_bits):
    i = pl.program_id(1)
    tq = Q_BLOCK
    nk = (i * tq + tq + tk - 1) // tk
    n_tiles = tk // LANES
    sub = min(INDEXER_SUB, tk)
    qpos_t = i * tq + lax.broadcasted_iota(I32, (1, tq), 1)

    @pl.when(i == 0)
    def _():
        def cast_chunk(c, kn):
            rs = pl.ds(pl.multiple_of(c * tk, tk), tk)
            kf = k_ref[rs, :]
            kb_scr[rs, :] = kf.astype(BF16)
            vb_scr[rs, :] = v_ref[rs, :].astype(BF16)
            kib_scr[rs, :] = kim_ref[rs, 0:IDX_DIM].astype(BF16)
            sq = kf * kf
            return tuple(
                jnp.maximum(kn[g], jnp.max(jnp.sum(sq[:, g * HEAD_DIM:(g + 1) * HEAD_DIM], axis=1, keepdims=True),
                                           axis=0, keepdims=True))
                for g in range(N_KV_HEADS))
        kn = lax.fori_loop(0, k_ref.shape[0] // tk, cast_chunk, (jnp.zeros((1, 1), F32),) * N_KV_HEADS)
        for g in range(N_KV_HEADS):
            kn_scr[g] = jnp.broadcast_to(kn[g], kn_scr.shape[1:])

    for h in range(IDX_HEADS):
        ref = qia_ref if h < IDX_HEADS // 2 else qib_ref
        hh = h % (IDX_HEADS // 2)
        qi_scr[h // 2, (h % 2) * tq:(h % 2 + 1) * tq, :] = ref[:, hh * IDX_DIM:(hh + 1) * IDX_DIM].astype(BF16)
    wt_scr[...] = misc_ref[...].T

    def score_chunk(c, carry):
        for s0 in range(0, tk, sub):
            ks = pl.multiple_of(c * tk, tk) + s0
            ki = kib_scr[pl.ds(ks, sub), :]
            acc = jnp.zeros((sub, tq), F32)
            for hp in range(IDX_HEADS // 2):
                r = jnp.maximum(_dot_nt(ki, qi_scr[hp]), 0.0)
                acc = (acc + r[:, 0:tq] * wt_scr[pl.ds(MISC_WI + 2 * hp, 1), :]
                       + r[:, tq:] * wt_scr[pl.ds(MISC_WI + 2 * hp + 1, 1), :])
            kpos = ks + lax.broadcasted_iota(I32, (sub, tq), 0)
            masked = jnp.where(kpos <= qpos_t, acc, -jnp.inf)
            sc_scr[pl.ds(ks, sub), :] = masked
            for g0 in range(0, sub, SORT_N * 8):
                srt_scr[pl.ds(ks + g0, SORT_N * 8), :] = _sort_groups_desc(masked[g0:g0 + SORT_N * 8, :])
        return carry

    lax.fori_loop(0, nk, score_chunk, 0)
    thr = _select_threshold(sc_scr, j_scr, nk, tk, topk, idx_bits, 0, srt_scr)
    jb = j_scr[...]

    for g in range(N_KV_HEADS):
        kn = jnp.sqrt(kn_scr[g][0:1, :])
        for r in range(HEADS_PER_KV):
            hd = g * HEADS_PER_KV + r
            qh = q_ref[:, hd * HEAD_DIM:(hd + 1) * HEAD_DIM]
            qs_scr[g, r * tq:(r + 1) * tq, :] = qh.astype(BF16)
            mx_scr[g, r * tq:(r + 1) * tq, :] = jnp.sqrt(jnp.sum(qh * qh, axis=1, keepdims=True)) * kn * BOUND_SLACK
    l_scr[...] = jnp.zeros(l_scr.shape, F32)
    acc_scr[...] = jnp.zeros(acc_scr.shape, F32)

    def logits(g, ks, bias4):
        kc = kb_scr[pl.ds(ks, tk), g * HEAD_DIM:(g + 1) * HEAD_DIM]
        return _dot_nt(qs_scr[g], kc) + bias4

    def exp_chunk(ks, bias):
        bias4 = jnp.concatenate([bias] * HEADS_PER_KV, axis=0)
        for g in range(N_KV_HEADS):
            s = logits(g, ks, bias4)
            m = mx_scr[g]
            p = [jnp.exp(s[:, t * LANES:(t + 1) * LANES] - m) for t in range(n_tiles)]
            l_scr[g] += functools.reduce(lambda x, y: x + y, p)
            pb = jnp.concatenate(p, axis=1).astype(BF16)
            vc = vb_scr[pl.ds(ks, tk), g * HEAD_DIM:(g + 1) * HEAD_DIM]
            acc_scr[g] += jnp.dot(pb, vc, preferred_element_type=F32)

    def bound_pass(c, carry):
        ks = pl.multiple_of(c * tk, tk)
        kpos = ks + lax.broadcasted_iota(I32, (tk, tq), 0)
        sel = _selected(sc_scr[pl.ds(ks, tk), :], kpos, thr, jb) & (kpos <= qpos_t)
        bias = jnp.where(sel, 0.0, NEG).T
        bias_scr[:, pl.ds(ks, tk)] = bias
        exp_chunk(ks, bias)
        return carry

    lax.fori_loop(0, nk, bound_pass, 0)

    def denominators():
        for g in range(N_KV_HEADS):
            mx_scr[g] = jnp.broadcast_to(jnp.sum(l_scr[g], axis=1, keepdims=True), mx_scr.shape[1:])

    denominators()
    l_min = jnp.min(functools.reduce(jnp.minimum, [mx_scr[g] for g in range(N_KV_HEADS)]))

    @pl.when(l_min < SOFTMAX_MIN_DENOM)
    def _():
        mx_scr[...] = jnp.full(mx_scr.shape, NEG, F32)
        l_scr[...] = jnp.zeros(l_scr.shape, F32)
        acc_scr[...] = jnp.zeros(acc_scr.shape, F32)

        def max_pass(c, carry):
            ks = pl.multiple_of(c * tk, tk)
            bias4 = jnp.concatenate([bias_scr[:, pl.ds(ks, tk)]] * HEADS_PER_KV, axis=0)
            for g in range(N_KV_HEADS):
                s = logits(g, ks, bias4)
                m = mx_scr[g]
                for t in range(n_tiles):
                    m = jnp.maximum(m, s[:, t * LANES:(t + 1) * LANES])
                mx_scr[g] = m
            return carry

        lax.fori_loop(0, nk, max_pass, 0)
        for g in range(N_KV_HEADS):
            mx_scr[g] = jnp.broadcast_to(jnp.max(mx_scr[g], axis=1, keepdims=True), mx_scr.shape[1:])

        def exp_pass(c, carry):
            ks = pl.multiple_of(c * tk, tk)
            exp_chunk(ks, bias_scr[:, pl.ds(ks, tk)])
            return carry

        lax.fori_loop(0, nk, exp_pass, 0)
        denominators()

    for g in range(N_KV_HEADS):
        o = acc_scr[g] / mx_scr[g]
        for r in range(HEADS_PER_KV):
            hd = g * HEADS_PER_KV + r
            o_ref[:, hd * HEAD_DIM:(hd + 1) * HEAD_DIM] = o[r * tq:(r + 1) * tq, :].astype(BF16)


def _attn_prompt(z, b, s, topk):
    nq = s // Q_BLOCK
    tk = min(ATTN_TK, s)
    idx_bits = max(1, (s - 1).bit_length()) + 1
    body = functools.partial(_attn_prompt_body, tk=tk, topk=topk, idx_bits=idx_bits)
    rows = HEADS_PER_KV * Q_BLOCK
    return pl.pallas_call(
        body,
        out_shape=jax.ShapeDtypeStruct((b * s, ATTN_WIDTH), BF16),
        grid=(b, nq),
        in_specs=[
            pl.BlockSpec((Q_BLOCK, ATTN_WIDTH), lambda bb, i: (bb * nq + i, 0)),
            pl.BlockSpec((Q_BLOCK, QI_WIDTH // 2), lambda bb, i: (bb * nq + i, C_QI // (QI_WIDTH // 2))),
            pl.BlockSpec((Q_BLOCK, QI_WIDTH // 2), lambda bb, i: (bb * nq + i, C_QI // (QI_WIDTH // 2) + 1)),
            pl.BlockSpec((Q_BLOCK, LANES), lambda bb, i: (bb * nq + i, C_MISC // LANES)),
            pl.BlockSpec((s, KV_WIDTH), lambda bb, i: (bb, C_K // KV_WIDTH)),
            pl.BlockSpec((s, KV_WIDTH), lambda bb, i: (bb, C_V // KV_WIDTH)),
            pl.BlockSpec((s, LANES), lambda bb, i: (bb, C_MISC // LANES)),
        ],
        out_specs=pl.BlockSpec((Q_BLOCK, ATTN_WIDTH), lambda bb, i: (bb * nq + i, 0)),
        scratch_shapes=[
            pltpu.VMEM((s, Q_BLOCK), F32),
            pltpu.VMEM((s, Q_BLOCK), F32),
            pltpu.VMEM((1, Q_BLOCK), I32),
            pltpu.VMEM((Q_BLOCK, s), F32),
            pltpu.VMEM((s, KV_WIDTH), BF16),
            pltpu.VMEM((s, KV_WIDTH), BF16),
            pltpu.VMEM((s, IDX_DIM), BF16),
            pltpu.VMEM((N_KV_HEADS, 8, LANES), F32),
            pltpu.VMEM((IDX_HEADS // 2, 2 * Q_BLOCK, IDX_DIM), BF16),
            pltpu.VMEM((LANES, Q_BLOCK), F32),
            pltpu.VMEM((N_KV_HEADS, rows, HEAD_DIM), BF16),
            pltpu.VMEM((N_KV_HEADS, rows, LANES), F32),
            pltpu.VMEM((N_KV_HEADS, rows, LANES), F32),
            pltpu.VMEM((N_KV_HEADS, rows, HEAD_DIM), F32),
        ],
        compiler_params=_params(("parallel", "arbitrary")),
        name="attn_prompt",
    )(z, z, z, z, z, z, z)


def _log_decay(misc, w2, gb):
    x = jnp.dot(misc, w2, precision=HIGHEST, preferred_element_type=F32) + gb
    return (jnp.minimum(x, 0.0) - jnp.log1p(jnp.exp(-jnp.abs(x)))) * (1.0 / GLA_GATE_TAU)


def _gla_out(o, nw, gr):
    return (_rms(o, nw) * (gr * jax.nn.sigmoid(gr))).astype(BF16)


GLA_C = 128
GLA_SUB = 8
GLA_HPS = 4


def _gla_prompt_body(gq_ref, gk_ref, gv_ref, misc_ref, gr_ref, w2_ref, gb_ref, nw_ref,
                     og_ref, sfin_ref, st_scr, a_scr, b_scr, k_scr, o_scr, *, nchunk):
    t = pl.program_id(2)
    c_ = GLA_C

    @pl.when(t == 0)
    def _():
        st_scr[...] = jnp.zeros_like(st_scr)

    a_scr[...] = jnp.zeros_like(a_scr)
    row = lax.broadcasted_iota(I32, (c_, c_), 0)
    col = lax.broadcasted_iota(I32, (c_, c_), 1)
    tri = jnp.where(col <= row, 1.0, 0.0)
    sub_row = lax.broadcasted_iota(I32, (GLA_SUB, LANES), 0)
    sub_col = lax.broadcasted_iota(I32, (GLA_SUB, GLA_SUB), 1)

    def head_chunk(hh, r0):
        ksl = slice(hh * GLA_DK, (hh + 1) * GLA_DK)
        vsl = slice(hh * GLA_DV, (hh + 1) * GLA_DV)
        q = gq_ref[pl.ds(r0, c_), ksl] * (GLA_DK ** -0.5)
        k = gk_ref[pl.ds(r0, c_), ksl]
        v = gv_ref[pl.ds(r0, c_), vsl].astype(BF16)
        g = _log_decay(misc_ref[pl.ds(r0, c_), :], w2_ref[:, ksl], gb_ref[:, ksl])
        b = jnp.dot(tri, g, precision=HIGHEST, preferred_element_type=F32)
        b_scr[hh] = b
        k_scr[hh] = k
        st = st_scr[hh]
        o = _dot_nt((q * jnp.exp(b)).astype(BF16), st.astype(BF16))

        n = c_ // 2
        while n >= GLA_SUB:
            for rb in range(n, c_, 2 * n):
                bref = b[rb:rb + 1, :]
                qs = q[rb:rb + n, :] * jnp.exp(b[rb:rb + n, :] - bref)
                ks = k[rb - n:rb, :] * jnp.exp(bref - b[rb - n:rb, :])
                a_scr[hh, rb:rb + n, rb - n:rb] = _dot_nt(qs.astype(BF16), ks.astype(BF16))
            n //= 2
        for blk in range(c_ // GLA_SUB):
            lo = blk * GLA_SUB
            qb = q[lo:lo + GLA_SUB, :]
            bb = b[lo:lo + GLA_SUB, :]
            ad = jnp.zeros((GLA_SUB, GLA_SUB), F32)
            for jj in range(GLA_SUB):
                bj = b_scr[hh, pl.ds(lo + jj, 1), :]
                kj = k_scr[hh, pl.ds(lo + jj, 1), :]
                w = jnp.exp(jnp.where(sub_row >= jj, bb - bj, NEG))
                colj = jnp.sum(qb * kj * w, axis=1, keepdims=True)
                ad = jnp.where(sub_col == jj, colj, ad)
            a_scr[hh, lo:lo + GLA_SUB, lo:lo + GLA_SUB] = ad

        o = o + jnp.dot(a_scr[hh].astype(BF16), v, preferred_element_type=F32)
        o_scr[pl.ds(r0, c_), vsl] = o
        b_last = b[c_ - 1:c_, :]
        kd = (k * jnp.exp(b_last - b)).astype(BF16)
        kv = lax.dot_general(v, kd, (((0,), (0,)), ((), ())), preferred_element_type=F32)
        st_scr[hh] = st * jnp.exp(b_last) + kv

    def chunk(ci, carry):
        r0 = pl.multiple_of(ci * c_, c_)
        for hh in range(GLA_HPS):
            head_chunk(hh, r0)
        return carry

    lax.fori_loop(0, nchunk, chunk, 0)
    for hh in range(GLA_HPS):
        vsl = slice(hh * GLA_DV, (hh + 1) * GLA_DV)
        og_ref[:, vsl] = _gla_out(o_scr[:, vsl], nw_ref[...], gr_ref[:, vsl])

    @pl.when(t == pl.num_programs(2) - 1)
    def _():
        for hh in range(GLA_HPS):
            sfin_ref[0, hh] = st_scr[hh].T


def _gla_prompt(z, w2p, gbias, norm_w, b, s):
    tb = min(GLA_TB, s)
    nt = s // tb
    body = functools.partial(_gla_prompt_body, nchunk=tb // GLA_C)

    def rowblk(bb, t):
        return bb * nt + t

    kw, vw = GLA_HPS * GLA_DK, GLA_HPS * GLA_DV
    return pl.pallas_call(
        body,
        out_shape=(jax.ShapeDtypeStruct((b * s, GLA_VAL_WIDTH), BF16),
                   jax.ShapeDtypeStruct((b, GLA_HEADS, GLA_DK, GLA_DV), F32)),
        grid=(b, GLA_HEADS // GLA_HPS, nt),
        in_specs=[
            pl.BlockSpec((tb, kw), lambda bb, h, t: (rowblk(bb, t), C_GQ // kw + h)),
            pl.BlockSpec((tb, kw), lambda bb, h, t: (rowblk(bb, t), C_GK // kw + h)),
            pl.BlockSpec((tb, vw), lambda bb, h, t: (rowblk(bb, t), C_GV // vw + h)),
            pl.BlockSpec((tb, LANES), lambda bb, h, t: (rowblk(bb, t), C_MISC // LANES)),
            pl.BlockSpec((tb, vw), lambda bb, h, t: (rowblk(bb, t), C_GR // vw + h)),
            pl.BlockSpec((LANES, kw), lambda bb, h, t: (0, h)),
            pl.BlockSpec((1, kw), lambda bb, h, t: (0, h)),
            pl.BlockSpec((1, GLA_DV), lambda bb, h, t: (0, 0)),
        ],
        out_specs=(pl.BlockSpec((tb, vw), lambda bb, h, t: (rowblk(bb, t), h)),
                   pl.BlockSpec((1, GLA_HPS, GLA_DK, GLA_DV), lambda bb, h, t: (bb, h, 0, 0))),
        scratch_shapes=[
            pltpu.VMEM((GLA_HPS, GLA_DV, GLA_DK), F32),
            pltpu.VMEM((GLA_HPS, GLA_C, GLA_C), F32),
            pltpu.VMEM((GLA_HPS, GLA_C, GLA_DK), F32),
            pltpu.VMEM((GLA_HPS, GLA_C, GLA_DK), F32),
            pltpu.VMEM((tb, vw), F32),
        ],
        compiler_params=_params(("parallel", "parallel", "arbitrary")),
        name="gla_prompt",
    )(z, z, z, z, z, w2p, gbias, norm_w.reshape(1, GLA_DV))


def _merge_body(oa_ref, og_ref, ga_ref, gg_ref, wa_ref, wg_ref, o_ref):
    pa = jnp.dot(oa_ref[...], wa_ref[...], preferred_element_type=F32)
    pg = jnp.dot(og_ref[...], wg_ref[...], preferred_element_type=F32)
    o_ref[...] = (jax.nn.sigmoid(ga_ref[...]) * pa + jax.nn.sigmoid(gg_ref[...]) * pg).astype(BF16)


def _merge(o_attn, o_gla, z, wa, wg, tm):
    rows = o_attn.shape[0]
    d = wa.shape[1]
    return pl.pallas_call(
        _merge_body,
        out_shape=jax.ShapeDtypeStruct((rows, d), BF16),
        grid=(rows // tm,),
        in_specs=[
            pl.BlockSpec((tm, ATTN_WIDTH), lambda i: (i, 0)),
            pl.BlockSpec((tm, GLA_VAL_WIDTH), lambda i: (i, 0)),
            pl.BlockSpec((tm, d), lambda i: (i, C_GA // d)),
            pl.BlockSpec((tm, d), lambda i: (i, C_GA // d + 1)),
            pl.BlockSpec((ATTN_WIDTH, d), lambda i: (0, 0)),
            pl.BlockSpec((GLA_VAL_WIDTH, d), lambda i: (0, 0)),
        ],
        out_specs=pl.BlockSpec((tm, d), lambda i: (i, 0)),
        compiler_params=_params(("parallel",)),
        name="merge",
    )(o_attn, o_gla, z, z, wa, wg)


def _out_proj_body(m_ref, h_ref, w_ref, pw_ref, o_ref):
    y = jnp.dot(m_ref[...], w_ref[...], preferred_element_type=F32)
    o_ref[...] = h_ref[...] + _rms(y, pw_ref[...])


def _out_proj(merged, h, w_out, post_w, tm):
    rows, d = h.shape
    return pl.pallas_call(
        _out_proj_body,
        out_shape=jax.ShapeDtypeStruct((rows, d), F32),
        grid=(rows // tm,),
        in_specs=[
            pl.BlockSpec((tm, d), lambda i: (i, 0)),
            pl.BlockSpec((tm, d), lambda i: (i, 0)),
            pl.BlockSpec((d, d), lambda i: (0, 0)),
            pl.BlockSpec((1, d), lambda i: (0, 0)),
        ],
        out_specs=pl.BlockSpec((tm, d), lambda i: (i, 0)),
        compiler_params=_params(("parallel",)),
        name="out_proj",
    )(merged, h, w_out, post_w.reshape(1, d))


def _page_copies(pt_ref, hbm, buf, sem, seq, slot, n_pages, dst):
    return [pltpu.make_async_copy(hbm.at[pt_ref[seq, p]], buf.at[slot].at[dst(p)], sem)
            for p in range(n_pages)]


GATHER_SLOTS = 4


def _gather_step(fetch):
    b = pl.program_id(0)
    ahead = GATHER_SLOTS - 1

    @pl.when(b == 0)
    def _():
        for a in range(ahead):
            @pl.when(a < pl.num_programs(0))
            def _():
                for cp in fetch(a, a % GATHER_SLOTS):
                    cp.start()

    @pl.when(b + ahead < pl.num_programs(0))
    def _():
        for cp in fetch(b + ahead, (b + ahead) % GATHER_SLOTS):
            cp.start()

    slot = b % GATHER_SLOTS
    for cp in fetch(b, slot):
        cp.wait()
    return slot


def _sample_scores_body(pt_ref, qi_ref, wi_ref, kidx_hbm, o_ref, kbuf, sem, *, n_pages, page):
    def fetch(seq, slot):
        return _page_copies(pt_ref, kidx_hbm, kbuf, sem.at[slot], seq, slot, n_pages,
                            lambda p: (slice(None), pl.ds(p * page, page)))

    slot = _gather_step(fetch)
    s = jnp.dot(qi_ref[0].astype(BF16), kbuf[slot].astype(BF16), preferred_element_type=F32)
    o_ref[0] = jnp.sum(jnp.maximum(s, 0.0) * wi_ref[0], axis=0, keepdims=True)


def _sample_scores(page_table, qi3, wi3, kidx_t):
    nd, n_pages = page_table.shape
    page = kidx_t.shape[2]
    past = n_pages * page
    body = functools.partial(_sample_scores_body, n_pages=n_pages, page=page)
    grid_spec = pltpu.PrefetchScalarGridSpec(
        num_scalar_prefetch=1,
        grid=(nd,),
        in_specs=[
            pl.BlockSpec((1, IDX_HEADS, IDX_DIM), lambda b, pt: (b, 0, 0)),
            pl.BlockSpec((1, IDX_HEADS, 1), lambda b, pt: (b, 0, 0)),
            pl.BlockSpec(memory_space=pl.ANY),
        ],
        out_specs=pl.BlockSpec((1, 1, past), lambda b, pt: (b, 0, 0)),
        scratch_shapes=[pltpu.VMEM((GATHER_SLOTS, IDX_DIM, past), F32), pltpu.SemaphoreType.DMA((GATHER_SLOTS,))],
    )
    return pl.pallas_call(
        body,
        out_shape=jax.ShapeDtypeStruct((nd, 1, past), F32),
        grid_spec=grid_spec,
        compiler_params=_params(("arbitrary",)),
        name="sample_scores",
    )(page_table, qi3, wi3, kidx_t)


def _sample_select_body(sp_ref, qi_ref, misc_ref, bp_ref, bs_ref, sc_scr, j_scr, *, past, tk, topk, idx_bits):
    rows = sp_ref.shape[0]
    misc = misc_ref[...]
    d_in = lax.broadcasted_iota(I32, (LANES, QI_WIDTH), 0)
    c_out = lax.broadcasted_iota(I32, (LANES, QI_WIDTH), 1)
    rep = jnp.where((d_in < IDX_DIM) & (c_out % IDX_DIM == d_in), 1.0, 0.0)
    ki_t = jnp.dot(misc, rep, precision=HIGHEST, preferred_element_type=F32)
    c_in = lax.broadcasted_iota(I32, (QI_WIDTH, LANES), 0)
    l_out = lax.broadcasted_iota(I32, (QI_WIDTH, LANES), 1)
    seg = jnp.where(l_out == MISC_WI + c_in // IDX_DIM, 1.0, 0.0)
    hd = jnp.dot(qi_ref[...] * ki_t, seg, precision=HIGHEST, preferred_element_type=F32)
    lane = lax.broadcasted_iota(I32, (rows, LANES), 1)
    is_wi = (lane >= MISC_WI) & (lane < MISC_GLR)
    s_self = jnp.sum(jnp.where(is_wi, jnp.maximum(hd, 0.0) * misc, 0.0), axis=1, keepdims=True)

    sc_scr[:, 0:past] = sp_ref[...]
    sc_scr[:, past:] = jnp.where(lax.broadcasted_iota(I32, (rows, tk), 1) == 0, s_self, -jnp.inf)
    nk = sc_scr.shape[1] // tk
    thr = _select_threshold(sc_scr, j_scr, nk, tk, topk, idx_bits, 1)
    jb = j_scr[...]
    kpos = lax.broadcasted_iota(I32, sc_scr.shape, 1)
    bias = jnp.where(_selected(sc_scr[...], kpos, thr, jb), 0.0, NEG)
    bp_ref[...] = bias[:, 0:past]
    bs_ref[...] = bias[:, past:past + LANES]


def _sample_select(s_past, qi2, misc, topk):
    nd, past = s_past.shape
    tk = LANES
    width = past + tk
    idx_bits = max(1, (width - 1).bit_length()) + 1
    body = functools.partial(_sample_select_body, past=past, tk=tk, topk=topk, idx_bits=idx_bits)
    return pl.pallas_call(
        body,
        out_shape=(jax.ShapeDtypeStruct((nd, past), F32), jax.ShapeDtypeStruct((nd, LANES), F32)),
        grid=(1,),
        in_specs=[
            pl.BlockSpec((nd, past), lambda i: (0, 0)),
            pl.BlockSpec((nd, QI_WIDTH), lambda i: (0, 0)),
            pl.BlockSpec((nd, LANES), lambda i: (0, 0)),
        ],
        out_specs=(pl.BlockSpec((nd, past), lambda i: (0, 0)), pl.BlockSpec((nd, LANES), lambda i: (0, 0))),
        scratch_shapes=[pltpu.VMEM((nd, width), F32), pltpu.VMEM((nd, 1), I32)],
        compiler_params=_params(("arbitrary",)),
        name="sample_select",
    )(s_past, qi2, misc)


SAMPLE_KCHUNK = 1024


def _sample_attn_body(pt_ref, q_ref, ks_ref, vs_ref, b2_ref, bs_ref, ck_hbm, cv_hbm, o_ref, kbuf, vbuf, sem,
                      *, n_pages, prows):
    def fetch(seq, slot):
        dst = lambda p: (pl.ds(p * prows, prows), slice(None))
        return (_page_copies(pt_ref, ck_hbm, kbuf, sem.at[0, slot], seq, slot, n_pages, dst)
                + _page_copies(pt_ref, cv_hbm, vbuf, sem.at[1, slot], seq, slot, n_pages, dst))

    slot = _gather_step(fetch)
    q = q_ref[0]
    qb = q.astype(BF16)
    total = n_pages * prows
    ch = min(SAMPLE_KCHUNK, total)
    head_grp = lax.broadcasted_iota(I32, (N_HEADS, ch), 0) // HEADS_PER_KV
    row_grp = lax.broadcasted_iota(I32, (N_HEADS, ch), 1) % N_KV_HEADS
    own = head_grp == row_grp
    s_chunks = []
    for c in range(total // ch):
        kc = kbuf[slot, c * ch:(c + 1) * ch, :].astype(BF16)
        s_chunks.append(jnp.where(own, _dot_nt(qb, kc) + b2_ref[0][:, c * ch:(c + 1) * ch], NEG))

    grp0 = lax.broadcasted_iota(I32, (N_HEADS, HEAD_DIM), 0) < HEADS_PER_KV
    k_self = jnp.where(grp0, ks_ref[0][:, 0:HEAD_DIM], ks_ref[0][:, HEAD_DIM:])
    v_self = jnp.where(grp0, vs_ref[0][:, 0:HEAD_DIM], vs_ref[0][:, HEAD_DIM:])
    s_self = jnp.sum(q * k_self, axis=1, keepdims=True) + bs_ref[0][:, 0:1]

    m = s_self
    for s in s_chunks:
        m = jnp.maximum(m, jnp.max(s, axis=1, keepdims=True))
    p_self = jnp.exp(s_self - m)
    l = p_self
    acc = p_self * v_self
    for c, s in enumerate(s_chunks):
        p = jnp.exp(s - m)
        l = l + jnp.sum(p, axis=1, keepdims=True)
        vc = vbuf[slot, c * ch:(c + 1) * ch, :].astype(BF16)
        acc = acc + jnp.dot(p.astype(BF16), vc, preferred_element_type=F32)
    o_ref[0] = (acc / l).astype(BF16)


def _sample_attn(page_table, q3, k_self, v_self, bias2, bias_self, ck2, cv2):
    nd, n_pages = page_table.shape
    prows = ck2.shape[1]
    total = n_pages * prows
    body = functools.partial(_sample_attn_body, n_pages=n_pages, prows=prows)
    grid_spec = pltpu.PrefetchScalarGridSpec(
        num_scalar_prefetch=1,
        grid=(nd,),
        in_specs=[
            pl.BlockSpec((1, N_HEADS, HEAD_DIM), lambda b, pt: (b, 0, 0)),
            pl.BlockSpec((1, 1, KV_WIDTH), lambda b, pt: (b, 0, 0)),
            pl.BlockSpec((1, 1, KV_WIDTH), lambda b, pt: (b, 0, 0)),
            pl.BlockSpec((1, 1, total), lambda b, pt: (b, 0, 0)),
            pl.BlockSpec((1, 1, LANES), lambda b, pt: (b, 0, 0)),
            pl.BlockSpec(memory_space=pl.ANY),
            pl.BlockSpec(memory_space=pl.ANY),
        ],
        out_specs=pl.BlockSpec((1, N_HEADS, HEAD_DIM), lambda b, pt: (b, 0, 0)),
        scratch_shapes=[
            pltpu.VMEM((GATHER_SLOTS, total, HEAD_DIM), F32),
            pltpu.VMEM((GATHER_SLOTS, total, HEAD_DIM), F32),
            pltpu.SemaphoreType.DMA((2, GATHER_SLOTS)),
        ],
    )
    return pl.pallas_call(
        body,
        out_shape=jax.ShapeDtypeStruct((nd, N_HEADS, HEAD_DIM), BF16),
        grid_spec=grid_spec,
        compiler_params=_params(("arbitrary",)),
        name="sample_attn",
    )(page_table, q3, k_self, v_self, bias2, bias_self, ck2, cv2)


def _gla_sample_body(gq_ref, gk_ref, gv_ref, misc_ref, gr_ref, s_ref, w2_ref, gb_ref, nw_ref, og_ref, so_ref):
    eye = jnp.where(lax.broadcasted_iota(I32, (GLA_DK, GLA_DK), 0)
                    == lax.broadcasted_iota(I32, (GLA_DK, GLA_DK), 1), 1.0, 0.0)

    def column(row):
        return jnp.sum(eye * row, axis=1, keepdims=True)

    for sq in range(s_ref.shape[0]):
        misc = misc_ref[sq]
        for h in range(GLA_HEADS):
            ksl = slice(h * GLA_DK, (h + 1) * GLA_DK)
            vsl = slice(h * GLA_DV, (h + 1) * GLA_DV)
            g = _log_decay(misc, w2_ref[:, ksl], gb_ref[:, ksl])
            s_new = column(jnp.exp(g)) * s_ref[sq, h] + column(gk_ref[sq][:, ksl]) * gv_ref[sq][:, vsl]
            so_ref[sq, h] = s_new
            o = jnp.sum(column(gq_ref[sq][:, ksl] * (GLA_DK ** -0.5)) * s_new, axis=0, keepdims=True)
            og_ref[sq, :, vsl] = _gla_out(o, nw_ref[...], gr_ref[sq][:, vsl])


def _gla_sample(gq, gk, gv, misc, gr, state, w2p, gbias, norm_w):
    nd = state.shape[0]
    sb = GLA_SAMPLE_SB if nd % GLA_SAMPLE_SB == 0 else 1

    def row3(w):
        return pl.BlockSpec((sb, 1, w), lambda b: (b, 0, 0))

    st_spec = pl.BlockSpec((sb, GLA_HEADS, GLA_DK, GLA_DV), lambda b: (b, 0, 0, 0))
    return pl.pallas_call(
        _gla_sample_body,
        out_shape=(jax.ShapeDtypeStruct((nd, 1, GLA_VAL_WIDTH), BF16),
                   jax.ShapeDtypeStruct(state.shape, F32)),
        grid=(nd // sb,),
        in_specs=[
            row3(GLA_KEY_WIDTH), row3(GLA_KEY_WIDTH), row3(GLA_VAL_WIDTH), row3(LANES), row3(GLA_VAL_WIDTH),
            st_spec,
            pl.BlockSpec((LANES, GLA_KEY_WIDTH), lambda b: (0, 0)),
            pl.BlockSpec((1, GLA_KEY_WIDTH), lambda b: (0, 0)),
            pl.BlockSpec((1, GLA_DV), lambda b: (0, 0)),
        ],
        out_specs=(row3(GLA_VAL_WIDTH), st_spec),
        compiler_params=_params(("parallel",)),
        name="gla_sample",
    )(gq, gk, gv, misc, gr, state, w2p, gbias, norm_w.reshape(1, GLA_DV))


def _rope_tables(pos, d):
    inv = ROPE_THETA ** (-jnp.arange(0, d, 2, dtype=F32) / d)
    ang = pos.astype(F32)[:, None] * inv[None, :]
    cos, sin = jnp.cos(ang), jnp.sin(ang)
    reps = LANES // d
    return (jnp.tile(jnp.concatenate([cos, cos], axis=-1), (1, reps)),
            jnp.tile(jnp.concatenate([-sin, sin], axis=-1), (1, reps)))


def kernel(x_prompt, x_sample, cache_k, cache_v, cache_kidx, page_table, state_gla,
           ffn1_pre_w, ffn1_w_gate, ffn1_w_up, ffn1_w_down, ffn1_post_w,
           mix_pre_w, w_in, gla_gate_w2, gla_gate_b, gla_norm_w,
           w_proj_attn, w_proj_gla, w_out, mix_post_w,
           ffn2_pre_w, ffn2_w_gate, ffn2_w_up, ffn2_w_down, ffn2_post_w):
    b, s, d = x_prompt.shape
    nd, td, _ = x_sample.shape
    n_pool, page = cache_k.shape[:2]
    n_pages = page_table.shape[1]
    past = n_pages * page
    assert td == 1 and s % Q_BLOCK == 0 and d % MIX_TN == 0 and C_GA % d == 0

    wpa, wpg, wo = w_proj_attn.astype(BF16), w_proj_gla.astype(BF16), w_out.astype(BF16)
    w2p = jnp.zeros((LANES, GLA_KEY_WIDTH), F32).at[MISC_GLR:MISC_GLR + GLA_GATE_RANK].set(gla_gate_w2)
    gbias = gla_gate_b.reshape(1, GLA_KEY_WIDTH)

    def trunk(x, tm, tm_mix, tabs, tab_rows, mixer, f1, f2, w_mix):
        h = _ffn(x, ffn1_pre_w, *f1, ffn1_post_w, tm)
        h, f1b = h if isinstance(h, tuple) else (h, f1)
        z, k_rows, v_rows, *w_mix_b = _mix_in(h, mix_pre_w, w_mix, tabs, tm_mix, tab_rows // tm_mix)
        o_attn, o_gla, s_fin = mixer(z)
        tmm = min(tm, ROW_TM)
        merged = _merge(o_attn, o_gla, z, wpa, wpg, tmm)
        h = _out_proj(merged, h, wo, mix_post_w, tmm)
        y = _ffn(h, ffn2_pre_w, *f2, ffn2_post_w, tm)
        y, f2b = y if isinstance(y, tuple) else (y, f2)
        return y, (k_rows, v_rows, z[:, C_MISC:C_MISC + IDX_DIM]), s_fin, f1b, f2b, (w_mix_b or [w_mix])[0]

    pos_p = jnp.arange(s, dtype=I32)
    tm_p = ROW_TM if s % ROW_TM == 0 else Q_BLOCK
    tabs_p = _rope_tables(pos_p, HEAD_DIM) + _rope_tables(pos_p, IDX_DIM)
    top_p = min(TOPK_MAX, s // 4)

    def mixer_p(z):
        o_attn = _attn_prompt(z, b, s, top_p)
        o_gla, s_fin = _gla_prompt(z, w2p, gbias, gla_norm_w, b, s)
        return o_attn, o_gla, s_fin

    tm_mix = MIX_TM if s % MIX_TM == 0 else tm_p

    pos_s = jnp.full((nd,), past, I32)
    tabs_s = _rope_tables(pos_s, HEAD_DIM) + _rope_tables(pos_s, IDX_DIM)
    top_s = min(TOPK_MAX, (past + td) // 4)
    ck2 = cache_k.reshape(n_pool, page * N_KV_HEADS, HEAD_DIM)
    cv2 = cache_v.reshape(n_pool, page * N_KV_HEADS, HEAD_DIM)
    kidx_t = jnp.swapaxes(cache_kidx, 1, 2)

    def mixer_s(z):
        misc = z[:, C_MISC:C_MISC + LANES]
        qi2 = z[:, C_QI:C_QI + QI_WIDTH]
        s_past = _sample_scores(page_table, qi2.reshape(nd, IDX_HEADS, IDX_DIM),
                                misc[:, MISC_WI:MISC_GLR].reshape(nd, IDX_HEADS, 1), kidx_t)
        bias_past, bias_self = _sample_select(s_past.reshape(nd, past), qi2, misc, top_s)
        bias2 = jnp.repeat(bias_past, N_KV_HEADS, axis=1).reshape(nd, 1, past * N_KV_HEADS)
        o_attn = _sample_attn(page_table, z[:, C_Q:C_Q + ATTN_WIDTH].reshape(nd, N_HEADS, HEAD_DIM),
                              z[:, C_K:C_K + KV_WIDTH].reshape(nd, 1, KV_WIDTH),
                              z[:, C_V:C_V + KV_WIDTH].reshape(nd, 1, KV_WIDTH),
                              bias2, bias_self.reshape(nd, 1, LANES), ck2, cv2)
        o_gla, s_fin = _gla_sample(z[:, C_GQ:C_GQ + GLA_KEY_WIDTH].reshape(nd, 1, GLA_KEY_WIDTH),
                                   z[:, C_GK:C_GK + GLA_KEY_WIDTH].reshape(nd, 1, GLA_KEY_WIDTH),
                                   z[:, C_GV:C_GV + GLA_VAL_WIDTH].reshape(nd, 1, GLA_VAL_WIDTH),
                                   misc.reshape(nd, 1, LANES),
                                   z[:, C_GR:C_GR + GLA_VAL_WIDTH].reshape(nd, 1, GLA_VAL_WIDTH),
                                   state_gla, w2p, gbias, gla_norm_w)
        return o_attn.reshape(nd, ATTN_WIDTH), o_gla.reshape(nd, GLA_VAL_WIDTH), s_fin

    y_s, kv_s, gla_s, f1b, f2b, w_mix_b = trunk(
        x_sample.reshape(nd, d), nd, nd, tabs_s, nd, mixer_s,
        (ffn1_w_gate, ffn1_w_up, ffn1_w_down), (ffn2_w_gate, ffn2_w_up, ffn2_w_down), w_in.T)
    y_p, kv_p, gla_p, _, _, _ = trunk(x_prompt.reshape(b * s, d), tm_p, tm_mix, tabs_p, s, mixer_p, f1b, f2b, w_mix_b)

    def kv_out(kv, n, t):
        k_rows, v_rows, ki = kv
        return (k_rows.reshape(n, t, N_KV_HEADS, HEAD_DIM), v_rows.reshape(n, t, N_KV_HEADS, HEAD_DIM),
                ki.reshape(n, t, IDX_DIM))

    k_p, v_p, ki_p = kv_out(kv_p, b, s)
    k_s, v_s, ki_s = kv_out(kv_s, nd, td)
    return (y_p.reshape(b, s, d), y_s.reshape(nd, td, d), k_p, v_p, ki_p, gla_p, k_s, v_s, ki_s, gla_s)
```

```python
import functools

import jax
import jax.numpy as jnp
from jax import lax
from jax.experimental import pallas as pl
from jax.experimental.pallas import tpu as pltpu

F32, BF16, I32 = jnp.float32, jnp.bfloat16, jnp.int32
HIGHEST = lax.Precision.HIGHEST

N_HEADS = 8
N_KV_HEADS = 2
HEAD_DIM = 128
IDX_HEADS = 16
IDX_DIM = 64
TOPK_MAX = 256
Q_BLOCK = 128
ROPE_THETA = 10000.0
GLA_HEADS = 4
GLA_DK = 128
GLA_DV = 256
GLA_GATE_RANK = 16
GLA_GATE_TAU = 16.0
RMS_EPS = 1e-6

LANES = 128
ATTN_WIDTH = N_HEADS * HEAD_DIM
KV_WIDTH = N_KV_HEADS * HEAD_DIM
QI_WIDTH = IDX_HEADS * IDX_DIM
GLA_KEY_WIDTH = GLA_HEADS * GLA_DK
GLA_VAL_WIDTH = GLA_HEADS * GLA_DV
HEADS_PER_KV = N_HEADS // N_KV_HEADS

MIX_TN = 512

C_Q = 0
C_K = C_Q + ATTN_WIDTH
C_V = C_K + KV_WIDTH
C_QI = C_V + KV_WIDTH
C_MISC = C_QI + QI_WIDTH
C_GQ = -(-(C_MISC + LANES) // MIX_TN) * MIX_TN
C_GK = C_GQ + GLA_KEY_WIDTH
C_GV = C_GK + GLA_KEY_WIDTH
C_GR = C_GV + GLA_VAL_WIDTH
C_GA = C_GR + GLA_VAL_WIDTH
MISC_WI, MISC_GLR = IDX_DIM, IDX_DIM + IDX_HEADS

ROW_TM = 512
MERGE_TM = 256
MIX_TM = 1024
FFN_TF = 512
ATTN_TK = 512
INDEXER_SUB = 256
GLA_TB = 512
GLA_SAMPLE_SB = 16
VMEM_LIMIT = 56 * 1024 * 1024

NEG = -1e30
SOFTMAX_MIN_DENOM = 2.0 ** -60
BOUND_SLACK = 1.02
INT_MIN = -2 ** 31
INT_ABS_MASK = 0x7FFFFFFF
NEG_INF_KEY = -2139095041


def _params(sem, vmem=VMEM_LIMIT):
    return pltpu.CompilerParams(dimension_semantics=sem, vmem_limit_bytes=vmem)


def _rms(x, w):
    return x * lax.rsqrt(jnp.mean(x * x, axis=-1, keepdims=True) + RMS_EPS) * w


def _dot_nt(a, b):
    return lax.dot_general(a, b, (((1,), (1,)), ((), ())), preferred_element_type=F32)


def _ffn_body(x_ref, prew_ref, wg_ref, wu_ref, wd_ref, postw_ref, o_ref, *rest):
    *wb_refs, z_scr, acc_scr = rest
    j = pl.program_id(1)

    @pl.when(j == 0)
    def _():
        z_scr[...] = _rms(x_ref[...], prew_ref[...]).astype(BF16)
        acc_scr[...] = jnp.zeros_like(acc_scr)

    wg, wu, wd = wg_ref[...].astype(BF16), wu_ref[...].astype(BF16), wd_ref[...].astype(BF16)
    for ref, w in zip(wb_refs, (wg, wu, wd)):
        ref[...] = w
    z = z_scr[...]
    g = jnp.dot(z, wg, preferred_element_type=F32)
    u = jnp.dot(z, wu, preferred_element_type=F32)
    a = (g * jax.nn.sigmoid(g) * u).astype(BF16)
    acc_scr[...] += jnp.dot(a, wd, preferred_element_type=F32)

    @pl.when(j == pl.num_programs(1) - 1)
    def _():
        o_ref[...] = x_ref[...] + 0.5 * _rms(acc_scr[...], postw_ref[...])


def _ffn(x, pre_w, wg, wu, wd, post_w, tm):
    rows, d = x.shape
    dff = wg.shape[1]
    tf = FFN_TF if dff % FFN_TF == 0 else dff
    emit = wg.dtype == F32
    assert not emit or rows == tm
    w_specs = [
        pl.BlockSpec((d, tf), lambda i, j: (0, j)),
        pl.BlockSpec((d, tf), lambda i, j: (0, j)),
        pl.BlockSpec((tf, d), lambda i, j: (j, 0)),
    ]
    y_shape = jax.ShapeDtypeStruct((rows, d), F32)
    y_spec = pl.BlockSpec((tm, d), lambda i, j: (i, 0))
    out = pl.pallas_call(
        _ffn_body,
        out_shape=(y_shape, *(jax.ShapeDtypeStruct(w.shape, BF16) for w in (wg, wu, wd))) if emit else y_shape,
        grid=(rows // tm, dff // tf),
        in_specs=[
            pl.BlockSpec((tm, d), lambda i, j: (i, 0)),
            pl.BlockSpec((1, d), lambda i, j: (0, 0)),
            *w_specs,
            pl.BlockSpec((1, d), lambda i, j: (0, 0)),
        ],
        out_specs=(y_spec, *w_specs) if emit else y_spec,
        scratch_shapes=[pltpu.VMEM((tm, d), BF16), pltpu.VMEM((tm, d), F32)],
        compiler_params=_params(("parallel", "arbitrary")),
        name="ffn",
    )(x, pre_w.reshape(1, d), wg, wu, wd, post_w.reshape(1, d))
    return (out[0], tuple(out[1:])) if emit else out


def _mix_in_body(x_ref, prew_ref, wt_ref, wglr_ref, c128_ref, s128_ref, c64_ref, s64_ref, o_ref, kf_ref, vf_ref,
                 *rest):
    *wb_ref, u_scr = rest
    j = pl.program_id(1)

    @pl.when(j == 0)
    def _():
        u_scr[...] = _rms(x_ref[...], prew_ref[...]).astype(BF16)

    if wb_ref:
        w_ref = wb_ref[0]
        w_ref[...] = wt_ref[...].astype(BF16)

        @pl.when(j == C_MISC // MIX_TN)
        def _():
            row = lax.broadcasted_iota(I32, (LANES, w_ref.shape[1]), 0)
            w_ref[0:LANES, :] = jnp.where(row < MISC_GLR, w_ref[0:LANES, :], jnp.where(
                row < MISC_GLR + GLA_GATE_RANK, wglr_ref[...].astype(BF16), jnp.zeros((), BF16)))
            w_ref[LANES:, :] = jnp.zeros((MIX_TN - LANES, w_ref.shape[1]), BF16)
    else:
        w_ref = wt_ref

    o_ref[...] = _dot_nt(u_scr[...], w_ref[...])

    def rope128(x):
        return x * c128_ref[...] + pltpu.roll(x, HEAD_DIM // 2, 1) * s128_ref[...]

    def rope64(x):
        lane = lax.broadcasted_iota(I32, x.shape, 1)
        first = (lane % IDX_DIM) < (IDX_DIM // 2)
        rot = jnp.where(first, pltpu.roll(x, LANES - IDX_DIM // 2, 1), pltpu.roll(x, IDX_DIM // 2, 1))
        return x * c64_ref[...] + rot * s64_ref[...]

    def sl(t):
        return slice(t * LANES, (t + 1) * LANES)

    @pl.when(j < 2)
    def _():
        for t in range(4):
            o_ref[:, sl(t)] = rope128(o_ref[:, sl(t)]) * (HEAD_DIM ** -0.5)

    @pl.when(j == 2)
    def _():
        tm = o_ref.shape[0]
        for t in range(N_KV_HEADS):
            k = rope128(o_ref[:, sl(t)])
            o_ref[:, sl(t)] = k
            kf_ref[pl.ds(t, tm, stride=N_KV_HEADS), :] = k
            vf_ref[pl.ds(t, tm, stride=N_KV_HEADS), :] = o_ref[:, sl(N_KV_HEADS + t)]

    @pl.when((j == 3) | (j == 4))
    def _():
        for t in range(4):
            o_ref[:, sl(t)] = rope64(o_ref[:, sl(t)])

    @pl.when(j == 5)
    def _():
        x = o_ref[:, sl(0)]
        lane = lax.broadcasted_iota(I32, x.shape, 1)
        wi_scale = IDX_HEADS ** -0.5 * IDX_DIM ** -0.5
        o_ref[:, sl(0)] = jnp.where(lane < MISC_WI, rope64(x), jnp.where(lane < MISC_GLR, x * wi_scale, x))


def _mix_in(h, pre_w, w, tabs, tm, tab_blocks):
    rows, d = h.shape
    zw = C_GA + 2 * d
    emit = w.dtype == F32
    if emit:
        head = C_MISC + MISC_GLR
        gla_end = head + 2 * GLA_KEY_WIDTH + GLA_VAL_WIDTH
        tail = gla_end + GLA_GATE_RANK
        assert rows == tm and w.shape == (tail + GLA_VAL_WIDTH + 2 * d, d) and gla_end >= MISC_GLR
        sub = 8
        assert head % sub == 0 and tail % sub == 0 and (gla_end - MISC_GLR) % sub == 0
        tn8 = MIX_TN // sub

        def src_row(i, j):
            return (sub * jnp.where(j <= C_MISC // MIX_TN, j * tn8,
                                    jnp.where(j < C_GR // MIX_TN, head // sub + (j - C_GQ // MIX_TN) * tn8,
                                              tail // sub + (j - C_GR // MIX_TN) * tn8)), 0)

        w_specs = [pl.BlockSpec((pl.Element(MIX_TN), pl.Element(d)), src_row),
                   pl.BlockSpec((pl.Element(LANES), pl.Element(d)), lambda i, j: (gla_end - MISC_GLR, 0))]
    else:
        assert w.shape == (zw, d)
        w_specs = [pl.BlockSpec((MIX_TN, d), lambda i, j: (j, 0)), pl.BlockSpec((LANES, d), lambda i, j: (0, 0))]
    wb_shape = (jax.ShapeDtypeStruct((zw, d), BF16),) if emit else ()
    wb_spec = (pl.BlockSpec((MIX_TN, d), lambda i, j: (j, 0)),) if emit else ()

    c128, s128, c64, s64 = tabs
    tab_spec = pl.BlockSpec((tm, LANES), lambda i, j: (i % tab_blocks, 0))
    kv_shape = jax.ShapeDtypeStruct((rows * N_KV_HEADS, HEAD_DIM), F32)
    kv_spec = pl.BlockSpec((tm * N_KV_HEADS, HEAD_DIM), lambda i, j: (i, 0))
    return pl.pallas_call(
        _mix_in_body,
        out_shape=(jax.ShapeDtypeStruct((rows, zw), F32), kv_shape, kv_shape, *wb_shape),
        grid=(rows // tm, zw // MIX_TN),
        in_specs=[
            pl.BlockSpec((tm, d), lambda i, j: (i, 0)),
            pl.BlockSpec((1, d), lambda i, j: (0, 0)),
            *w_specs,
            tab_spec, tab_spec, tab_spec, tab_spec,
        ],
        out_specs=(pl.BlockSpec((tm, MIX_TN), lambda i, j: (i, j)), kv_spec, kv_spec, *wb_spec),
        scratch_shapes=[pltpu.VMEM((tm, d), BF16)],
        compiler_params=_params(("parallel", "arbitrary")),
        name="mix_in",
    )(h, pre_w.reshape(1, d), w, w, c128, s128, c64, s64)


def _key_to_float(key):
    key = jnp.maximum(key, NEG_INF_KEY)
    return lax.bitcast_convert_type(key ^ ((key >> 31) & jnp.int32(INT_ABS_MASK)), F32)


SORT_N = 16


def _sort_pairs(n):
    pairs, p = [], 1
    while p < n:
        k = p
        while k >= 1:
            for j in range(k % p, n - k, 2 * k):
                for i in range(min(k, n - j - k)):
                    if (i + j) // (2 * p) == (i + j + k) // (2 * p):
                        pairs.append((i + j, i + j + k))
            k //= 2
        p *= 2
    return pairs


def _sort_groups_desc(x):
    v = [x[j * 8:(j + 1) * 8, :] for j in range(SORT_N)]
    for i, j in _sort_pairs(SORT_N):
        v[i], v[j] = jnp.maximum(v[i], v[j]), jnp.minimum(v[i], v[j])
    return jnp.concatenate(v, axis=0)


def _count_sorted(v, t, cmp):
    c8 = cmp(v[7], t)
    c4 = cmp(jnp.where(c8, v[11], v[3]), t)
    c2 = cmp(jnp.where(c8, jnp.where(c4, v[13], v[9]), jnp.where(c4, v[5], v[1])), t)
    e = [jnp.where(c2, v[4 * a + 2], v[4 * a]) for a in range(4)]
    c1 = cmp(jnp.where(c8, jnp.where(c4, e[3], e[2]), jnp.where(c4, e[1], e[0])), t)
    low = (jnp.where(c8, 8.0, 0.0) + jnp.where(c4, 4.0, 0.0)) + (jnp.where(c2, 2.0, 0.0) + jnp.where(c1, 1.0, 0.0))
    return jnp.where(cmp(v[15], t), 16.0, low)


def _select_threshold(sc_ref, j_ref, nk, tk, topk, idx_bits, key_axis, srt_ref=None):
    n_other = sc_ref.shape[1 - key_axis]
    vec = (n_other, 1) if key_axis == 1 else (1, n_other)
    step = LANES if key_axis == 1 else 8

    def count_cmp(t, cmp):
        if srt_ref is None:
            return count(lambda blk, ks: cmp(blk, t))
        tb = jnp.broadcast_to(t, (8, n_other))

        def body(c, acc):
            ks = pl.multiple_of(c * tk, tk)
            for g0 in range(0, tk, SORT_N * 8):
                v = [srt_ref[pl.ds(ks + g0 + j * 8, 8), :] for j in range(SORT_N)]
                acc = acc + _count_sorted(v, tb, cmp)
            return acc
        return jnp.sum(lax.fori_loop(0, nk, body, jnp.zeros((8, n_other), F32)), axis=0, keepdims=True)

    def count(pred):
        def body(c, acc):
            ks = pl.multiple_of(c * tk, tk)
            blk = sc_ref[:, pl.ds(ks, tk)] if key_axis == 1 else sc_ref[pl.ds(ks, tk), :]
            m = jnp.where(pred(blk, ks), 1.0, 0.0)
            parts = [lax.slice_in_dim(m, a, a + step, axis=key_axis) for a in range(0, tk, step)]
            while len(parts) > 1:
                parts = [parts[a] + parts[a + 1] for a in range(0, len(parts), 2)]
            return acc + parts[0]
        acc0 = jnp.zeros((n_other, step) if key_axis == 1 else (step, n_other), F32)
        return jnp.sum(lax.fori_loop(0, nk, body, acc0), axis=key_axis, keepdims=True)

    def bit_body(bi, t):
        cand = t ^ lax.shift_left(jnp.int32(1), 31 - bi)
        return jnp.where(count_cmp(_key_to_float(cand), jnp.greater_equal) >= topk, cand, t)

    thr_key = lax.fori_loop(0, 32, bit_body, jnp.full(vec, INT_MIN, I32))
    thr = _key_to_float(thr_key)

    cnt_ge = count_cmp(thr, jnp.greater_equal)
    cnt_gt = count_cmp(thr, jnp.greater)
    need = topk - cnt_gt
    tie = jnp.where((cnt_ge > topk) & (thr_key > NEG_INF_KEY), 1.0, 0.0)
    j_ref[...] = jnp.full(vec, 1 << idx_bits, I32)

    @pl.when(jnp.max(tie) > 0.0)
    def _():
        def jbit(bi, jb):
            cand = jb + lax.shift_left(jnp.int32(1), idx_bits - 1 - bi)

            def pred(blk, ks):
                kpos = ks + lax.broadcasted_iota(I32, blk.shape, key_axis)
                return (blk == thr) & (kpos < cand)
            return jnp.where(count(pred) <= need, cand, jb)
        j_ref[...] = lax.fori_loop(0, idx_bits, jbit, jnp.zeros(vec, I32))

    return thr


def _selected(score, kpos, thr, jb):
    return (score > thr) | ((score == thr) & (kpos < jb))


def _attn_prompt_body(q_ref, qia_ref, qib_ref, misc_ref, k_ref, v_ref, kim_ref, o_ref,
                      sc_scr, srt_scr, j_scr, bias_scr, kb_scr, vb_scr, kib_scr, kn_scr, qi_scr, wt_scr, qs_scr,
                      mx_scr, l_scr, acc_scr, *, tk, topk, idx_bits):
    i = pl.program_id(1)
    tq = Q_BLOCK
    nk = (i * tq + tq + tk - 1) // tk
    n_tiles = tk // LANES
    sub = min(INDEXER_SUB, tk)
    qpos_t = i * tq + lax.broadcasted_iota(I32, (1, tq), 1)

    @pl.when(i == 0)
    def _():
        def cast_chunk(c, kn):
            rs = pl.ds(pl.multiple_of(c * tk, tk), tk)
            kf = k_ref[rs, :]
            kb_scr[rs, :] = kf.astype(BF16)
            vb_scr[rs, :] = v_ref[rs, :].astype(BF16)
            kib_scr[rs, :] = kim_ref[rs, 0:IDX_DIM].astype(BF16)
            sq = kf * kf
            return tuple(
                jnp.maximum(kn[g], jnp.max(jnp.sum(sq[:, g * HEAD_DIM:(g + 1) * HEAD_DIM], axis=1, keepdims=True),
                                           axis=0, keepdims=True))
                for g in range(N_KV_HEADS))
        kn = lax.fori_loop(0, k_ref.shape[0] // tk, cast_chunk, (jnp.zeros((1, 1), F32),) * N_KV_HEADS)
        for g in range(N_KV_HEADS):
            kn_scr[g] = jnp.broadcast_to(kn[g], kn_scr.shape[1:])

    for h in range(IDX_HEADS):
        ref = qia_ref if h < IDX_HEADS // 2 else qib_ref
        hh = h % (IDX_HEADS // 2)
        qi_scr[h // 2, (h % 2) * tq:(h % 2 + 1) * tq, :] = ref[:, hh * IDX_DIM:(hh + 1) * IDX_DIM].astype(BF16)
    wt_scr[...] = misc_ref[...].T

    def score_chunk(c, carry):
        for s0 in range(0, tk, sub):
            ks = pl.multiple_of(c * tk, tk) + s0
            ki = kib_scr[pl.ds(ks, sub), :]
            acc = jnp.zeros((sub, tq), F32)
            for hp in range(IDX_HEADS // 2):
                r = jnp.maximum(_dot_nt(ki, qi_scr[hp]), 0.0)
                acc = (acc + r[:, 0:tq] * wt_scr[pl.ds(MISC_WI + 2 * hp, 1), :]
                       + r[:, tq:] * wt_scr[pl.ds(MISC_WI + 2 * hp + 1, 1), :])
            kpos = ks + lax.broadcasted_iota(I32, (sub, tq), 0)
            masked = jnp.where(kpos <= qpos_t, acc, -jnp.inf)
            sc_scr[pl.ds(ks, sub), :] = masked
            for g0 in range(0, sub, SORT_N * 8):
                srt_scr[pl.ds(ks + g0, SORT_N * 8), :] = _sort_groups_desc(masked[g0:g0 + SORT_N * 8, :])
        return carry

    lax.fori_loop(0, nk, score_chunk, 0)
    thr = _select_threshold(sc_scr, j_scr, nk, tk, topk, idx_bits, 0, srt_scr)
    jb = j_scr[...]

    for g in range(N_KV_HEADS):
        kn = jnp.sqrt(kn_scr[g][0:1, :])
        for r in range(HEADS_PER_KV):
            hd = g * HEADS_PER_KV + r
            qh = q_ref[:, hd * HEAD_DIM:(hd + 1) * HEAD_DIM]
            qs_scr[g, r * tq:(r + 1) * tq, :] = qh.astype(BF16)
            mx_scr[g, r * tq:(r + 1) * tq, :] = jnp.sqrt(jnp.sum(qh * qh, axis=1, keepdims=True)) * kn * BOUND_SLACK
    l_scr[...] = jnp.zeros(l_scr.shape, F32)
    acc_scr[...] = jnp.zeros(acc_scr.shape, F32)

    def logits(g, ks, bias4):
        kc = kb_scr[pl.ds(ks, tk), g * HEAD_DIM:(g + 1) * HEAD_DIM]
        return _dot_nt(qs_scr[g], kc) + bias4

    def exp_chunk(ks, bias):
        bias4 = jnp.concatenate([bias] * HEADS_PER_KV, axis=0)
        for g in range(N_KV_HEADS):
            s = logits(g, ks, bias4)
            m = mx_scr[g]
            p = [jnp.exp(s[:, t * LANES:(t + 1) * LANES] - m) for t in range(n_tiles)]
            l_scr[g] += functools.reduce(lambda x, y: x + y, p)
            pb = jnp.concatenate(p, axis=1).astype(BF16)
            vc = vb_scr[pl.ds(ks, tk), g * HEAD_DIM:(g + 1) * HEAD_DIM]
            acc_scr[g] += jnp.dot(pb, vc, preferred_element_type=F32)

    def bound_pass(c, carry):
        ks = pl.multiple_of(c * tk, tk)
        kpos = ks + lax.broadcasted_iota(I32, (tk, tq), 0)
        sel = _selected(sc_scr[pl.ds(ks, tk), :], kpos, thr, jb) & (kpos <= qpos_t)
        bias = jnp.where(sel, 0.0, NEG).T
        bias_scr[:, pl.ds(ks, tk)] = bias
        exp_chunk(ks, bias)
        return carry

    lax.fori_loop(0, nk, bound_pass, 0)

    def denominators():
        for g in range(N_KV_HEADS):
            mx_scr[g] = jnp.broadcast_to(jnp.sum(l_scr[g], axis=1, keepdims=True), mx_scr.shape[1:])

    denominators()
    l_min = jnp.min(functools.reduce(jnp.minimum, [mx_scr[g] for g in range(N_KV_HEADS)]))

    @pl.when(l_min < SOFTMAX_MIN_DENOM)
    def _():
        mx_scr[...] = jnp.full(mx_scr.shape, NEG, F32)
        l_scr[...] = jnp.zeros(l_scr.shape, F32)
        acc_scr[...] = jnp.zeros(acc_scr.shape, F32)

        def max_pass(c, carry):
            ks = pl.multiple_of(c * tk, tk)
            bias4 = jnp.concatenate([bias_scr[:, pl.ds(ks, tk)]] * HEADS_PER_KV, axis=0)
            for g in range(N_KV_HEADS):
                s = logits(g, ks, bias4)
                m = mx_scr[g]
                for t in range(n_tiles):
                    m = jnp.maximum(m, s[:, t * LANES:(t + 1) * LANES])
                mx_scr[g] = m
            return carry

        lax.fori_loop(0, nk, max_pass, 0)
        for g in range(N_KV_HEADS):
            mx_scr[g] = jnp.broadcast_to(jnp.max(mx_scr[g], axis=1, keepdims=True), mx_scr.shape[1:])

        def exp_pass(c, carry):
            ks = pl.multiple_of(c * tk, tk)
            exp_chunk(ks, bias_scr[:, pl.ds(ks, tk)])
            return carry

        lax.fori_loop(0, nk, exp_pass, 0)
        denominators()

    for g in range(N_KV_HEADS):
        o = acc_scr[g] / mx_scr[g]
        for r in range(HEADS_PER_KV):
            hd = g * HEADS_PER_KV + r
            o_ref[:, hd * HEAD_DIM:(hd + 1) * HEAD_DIM] = o[r * tq:(r + 1) * tq, :].astype(BF16)


def _attn_prompt(z, b, s, topk):
    nq = s // Q_BLOCK
    tk = min(ATTN_TK, s)
    idx_bits = max(1, (s - 1).bit_length()) + 1
    body = functools.partial(_attn_prompt_body, tk=tk, topk=topk, idx_bits=idx_bits)
    rows = HEADS_PER_KV * Q_BLOCK
    return pl.pallas_call(
        body,
        out_shape=jax.ShapeDtypeStruct((b * s, ATTN_WIDTH), BF16),
        grid=(b, nq),
        in_specs=[
            pl.BlockSpec((Q_BLOCK, ATTN_WIDTH), lambda bb, i: (bb * nq + i, 0)),
            pl.BlockSpec((Q_BLOCK, QI_WIDTH // 2), lambda bb, i: (bb * nq + i, C_QI // (QI_WIDTH // 2))),
            pl.BlockSpec((Q_BLOCK, QI_WIDTH // 2), lambda bb, i: (bb * nq + i, C_QI // (QI_WIDTH // 2) + 1)),
            pl.BlockSpec((Q_BLOCK, LANES), lambda bb, i: (bb * nq + i, C_MISC // LANES)),
            pl.BlockSpec((s, KV_WIDTH), lambda bb, i: (bb, C_K // KV_WIDTH)),
            pl.BlockSpec((s, KV_WIDTH), lambda bb, i: (bb, C_V // KV_WIDTH)),
            pl.BlockSpec((s, LANES), lambda bb, i: (bb, C_MISC // LANES)),
        ],
        out_specs=pl.BlockSpec((Q_BLOCK, ATTN_WIDTH), lambda bb, i: (bb * nq + i, 0)),
        scratch_shapes=[
            pltpu.VMEM((s, Q_BLOCK), F32),
            pltpu.VMEM((s, Q_BLOCK), F32),
            pltpu.VMEM((1, Q_BLOCK), I32),
            pltpu.VMEM((Q_BLOCK, s), F32),
            pltpu.VMEM((s, KV_WIDTH), BF16),
            pltpu.VMEM((s, KV_WIDTH), BF16),
            pltpu.VMEM((s, IDX_DIM), BF16),
            pltpu.VMEM((N_KV_HEADS, 8, LANES), F32),
            pltpu.VMEM((IDX_HEADS // 2, 2 * Q_BLOCK, IDX_DIM), BF16),
            pltpu.VMEM((LANES, Q_BLOCK), F32),
            pltpu.VMEM((N_KV_HEADS, rows, HEAD_DIM), BF16),
            pltpu.VMEM((N_KV_HEADS, rows, LANES), F32),
            pltpu.VMEM((N_KV_HEADS, rows, LANES), F32),
            pltpu.VMEM((N_KV_HEADS, rows, HEAD_DIM), F32),
        ],
        compiler_params=_params(("parallel", "arbitrary")),
        name="attn_prompt",
    )(z, z, z, z, z, z, z)


def _log_decay(misc, w2, gb):
    x = jnp.dot(misc, w2, precision=HIGHEST, preferred_element_type=F32) + gb
    return (jnp.minimum(x, 0.0) - jnp.log1p(jnp.exp(-jnp.abs(x)))) * (1.0 / GLA_GATE_TAU)


def _gla_out(o, nw, gr):
    return (_rms(o, nw) * (gr * jax.nn.sigmoid(gr))).astype(BF16)


GLA_C = 128
GLA_SUB = 8
GLA_HPS = 4


def _gla_prompt_body(gq_ref, gk_ref, gv_ref, misc_ref, gr_ref, w2_ref, gb_ref, nw_ref,
                     og_ref, sfin_ref, st_scr, a_scr, b_scr, k_scr, o_scr, *, nchunk):
    t = pl.program_id(2)
    c_ = GLA_C

    @pl.when(t == 0)
    def _():
        st_scr[...] = jnp.zeros_like(st_scr)

    a_scr[...] = jnp.zeros_like(a_scr)
    row = lax.broadcasted_iota(I32, (c_, c_), 0)
    col = lax.broadcasted_iota(I32, (c_, c_), 1)
    tri = jnp.where(col <= row, 1.0, 0.0)
    sub_row = lax.broadcasted_iota(I32, (GLA_SUB, LANES), 0)
    sub_col = lax.broadcasted_iota(I32, (GLA_SUB, GLA_SUB), 1)

    def head_chunk(hh, r0):
        ksl = slice(hh * GLA_DK, (hh + 1) * GLA_DK)
        vsl = slice(hh * GLA_DV, (hh + 1) * GLA_DV)
        q = gq_ref[pl.ds(r0, c_), ksl] * (GLA_DK ** -0.5)
        k = gk_ref[pl.ds(r0, c_), ksl]
        v = gv_ref[pl.ds(r0, c_), vsl].astype(BF16)
        g = _log_decay(misc_ref[pl.ds(r0, c_), :], w2_ref[:, ksl], gb_ref[:, ksl])
        b = jnp.dot(tri, g, precision=HIGHEST, preferred_element_type=F32)
        b_scr[hh] = b
        k_scr[hh] = k
        st = st_scr[hh]
        o = _dot_nt((q * jnp.exp(b)).astype(BF16), st.astype(BF16))

        n = c_ // 2
        while n >= GLA_SUB:
            for rb in range(n, c_, 2 * n):
                bref = b[rb:rb + 1, :]
                qs = q[rb:rb + n, :] * jnp.exp(b[rb:rb + n, :] - bref)
                ks = k[rb - n:rb, :] * jnp.exp(bref - b[rb - n:rb, :])
                a_scr[hh, rb:rb + n, rb - n:rb] = _dot_nt(qs.astype(BF16), ks.astype(BF16))
            n //= 2
        for blk in range(c_ // GLA_SUB):
            lo = blk * GLA_SUB
            qb = q[lo:lo + GLA_SUB, :]
            bb = b[lo:lo + GLA_SUB, :]
            ad = jnp.zeros((GLA_SUB, GLA_SUB), F32)
            for jj in range(GLA_SUB):
                bj = b_scr[hh, pl.ds(lo + jj, 1), :]
                kj = k_scr[hh, pl.ds(lo + jj, 1), :]
                w = jnp.exp(jnp.where(sub_row >= jj, bb - bj, NEG))
                colj = jnp.sum(qb * kj * w, axis=1, keepdims=True)
                ad = jnp.where(sub_col == jj, colj, ad)
            a_scr[hh, lo:lo + GLA_SUB, lo:lo + GLA_SUB] = ad

        o = o + jnp.dot(a_scr[hh].astype(BF16), v, preferred_element_type=F32)
        o_scr[pl.ds(r0, c_), vsl] = o
        b_last = b[c_ - 1:c_, :]
        kd = (k * jnp.exp(b_last - b)).astype(BF16)
        kv = lax.dot_general(v, kd, (((0,), (0,)), ((), ())), preferred_element_type=F32)
        st_scr[hh] = st * jnp.exp(b_last) + kv

    def chunk(ci, carry):
        r0 = pl.multiple_of(ci * c_, c_)
        for hh in range(GLA_HPS):
            head_chunk(hh, r0)
        return carry

    lax.fori_loop(0, nchunk, chunk, 0)
    for hh in range(GLA_HPS):
        vsl = slice(hh * GLA_DV, (hh + 1) * GLA_DV)
        og_ref[:, vsl] = _gla_out(o_scr[:, vsl], nw_ref[...], gr_ref[:, vsl])

    @pl.when(t == pl.num_programs(2) - 1)
    def _():
        for hh in range(GLA_HPS):
            sfin_ref[0, hh] = st_scr[hh].T


def _gla_prompt(z, w2p, gbias, norm_w, b, s):
    tb = min(GLA_TB, s)
    nt = s // tb
    body = functools.partial(_gla_prompt_body, nchunk=tb // GLA_C)

    def rowblk(bb, t):
        return bb * nt + t

    kw, vw = GLA_HPS * GLA_DK, GLA_HPS * GLA_DV
    return pl.pallas_call(
        body,
        out_shape=(jax.ShapeDtypeStruct((b * s, GLA_VAL_WIDTH), BF16),
                   jax.ShapeDtypeStruct((b, GLA_HEADS, GLA_DK, GLA_DV), F32)),
        grid=(b, GLA_HEADS // GLA_HPS, nt),
        in_specs=[
            pl.BlockSpec((tb, kw), lambda bb, h, t: (rowblk(bb, t), C_GQ // kw + h)),
            pl.BlockSpec((tb, kw), lambda bb, h, t: (rowblk(bb, t), C_GK // kw + h)),
            pl.BlockSpec((tb, vw), lambda bb, h, t: (rowblk(bb, t), C_GV // vw + h)),
            pl.BlockSpec((tb, LANES), lambda bb, h, t: (rowblk(bb, t), C_MISC // LANES)),
            pl.BlockSpec((tb, vw), lambda bb, h, t: (rowblk(bb, t), C_GR // vw + h)),
            pl.BlockSpec((LANES, kw), lambda bb, h, t: (0, h)),
            pl.BlockSpec((1, kw), lambda bb, h, t: (0, h)),
            pl.BlockSpec((1, GLA_DV), lambda bb, h, t: (0, 0)),
        ],
        out_specs=(pl.BlockSpec((tb, vw), lambda bb, h, t: (rowblk(bb, t), h)),
                   pl.BlockSpec((1, GLA_HPS, GLA_DK, GLA_DV), lambda bb, h, t: (bb, h, 0, 0))),
        scratch_shapes=[
            pltpu.VMEM((GLA_HPS, GLA_DV, GLA_DK), F32),
            pltpu.VMEM((GLA_HPS, GLA_C, GLA_C), F32),
            pltpu.VMEM((GLA_HPS, GLA_C, GLA_DK), F32),
            pltpu.VMEM((GLA_HPS, GLA_C, GLA_DK), F32),
            pltpu.VMEM((tb, vw), F32),
        ],
        compiler_params=_params(("parallel", "parallel", "arbitrary")),
        name="gla_prompt",
    )(z, z, z, z, z, w2p, gbias, norm_w.reshape(1, GLA_DV))


def _merge_out_body(oa_ref, og_ref, ga_ref, gg_ref, h_ref, wa_ref, wg_ref, wo_ref, pw_ref, o_ref):
    pa = jnp.dot(oa_ref[...], wa_ref[...], preferred_element_type=F32)
    pg = jnp.dot(og_ref[...], wg_ref[...], preferred_element_type=F32)
    merged = (jax.nn.sigmoid(ga_ref[...]) * pa + jax.nn.sigmoid(gg_ref[...]) * pg).astype(BF16)
    y = jnp.dot(merged, wo_ref[...], preferred_element_type=F32)
    o_ref[...] = h_ref[...] + _rms(y, pw_ref[...])


def _merge_out(o_attn, o_gla, z, h, wa, wg, w_out, post_w, tm):
    rows, d = h.shape
    once = pl.Buffered(1)
    return pl.pallas_call(
        _merge_out_body,
        out_shape=jax.ShapeDtypeStruct((rows, d), F32),
        grid=(rows // tm,),
        in_specs=[
            pl.BlockSpec((tm, ATTN_WIDTH), lambda i: (i, 0)),
            pl.BlockSpec((tm, GLA_VAL_WIDTH), lambda i: (i, 0)),
            pl.BlockSpec((tm, d), lambda i: (i, C_GA // d)),
            pl.BlockSpec((tm, d), lambda i: (i, C_GA // d + 1)),
            pl.BlockSpec((tm, d), lambda i: (i, 0)),
            pl.BlockSpec((ATTN_WIDTH, d), lambda i: (0, 0), pipeline_mode=once),
            pl.BlockSpec((GLA_VAL_WIDTH, d), lambda i: (0, 0), pipeline_mode=once),
            pl.BlockSpec((d, d), lambda i: (0, 0), pipeline_mode=once),
            pl.BlockSpec((1, d), lambda i: (0, 0)),
        ],
        out_specs=pl.BlockSpec((tm, d), lambda i: (i, 0)),
        compiler_params=_params(("parallel",)),
        name="merge_out",
    )(o_attn, o_gla, z, z, h, wa, wg, w_out, post_w.reshape(1, d))


def _page_copies(pt_ref, hbm, buf, sem, seq, slot, n_pages, dst):
    return [pltpu.make_async_copy(hbm.at[pt_ref[seq, p]], buf.at[slot].at[dst(p)], sem)
            for p in range(n_pages)]


GATHER_SLOTS = 4


def _gather_step(fetch):
    b = pl.program_id(0)
    ahead = GATHER_SLOTS - 1

    @pl.when(b == 0)
    def _():
        for a in range(ahead):
            @pl.when(a < pl.num_programs(0))
            def _():
                for cp in fetch(a, a % GATHER_SLOTS):
                    cp.start()

    @pl.when(b + ahead < pl.num_programs(0))
    def _():
        for cp in fetch(b + ahead, (b + ahead) % GATHER_SLOTS):
            cp.start()

    slot = b % GATHER_SLOTS
    for cp in fetch(b, slot):
        cp.wait()
    return slot


def _sample_scores_body(pt_ref, qi_ref, wi_ref, kidx_hbm, o_ref, kbuf, sem, *, n_pages, page):
    def fetch(seq, slot):
        return _page_copies(pt_ref, kidx_hbm, kbuf, sem.at[slot], seq, slot, n_pages,
                            lambda p: (slice(None), pl.ds(p * page, page)))

    slot = _gather_step(fetch)
    s = jnp.dot(qi_ref[0].astype(BF16), kbuf[slot].astype(BF16), preferred_element_type=F32)
    o_ref[0] = jnp.sum(jnp.maximum(s, 0.0) * wi_ref[0], axis=0, keepdims=True)


def _sample_scores(page_table, qi3, wi3, kidx_t):
    nd, n_pages = page_table.shape
    page = kidx_t.shape[2]
    past = n_pages * page
    body = functools.partial(_sample_scores_body, n_pages=n_pages, page=page)
    grid_spec = pltpu.PrefetchScalarGridSpec(
        num_scalar_prefetch=1,
        grid=(nd,),
        in_specs=[
            pl.BlockSpec((1, IDX_HEADS, IDX_DIM), lambda b, pt: (b, 0, 0)),
            pl.BlockSpec((1, IDX_HEADS, 1), lambda b, pt: (b, 0, 0)),
            pl.BlockSpec(memory_space=pl.ANY),
        ],
        out_specs=pl.BlockSpec((1, 1, past), lambda b, pt: (b, 0, 0)),
        scratch_shapes=[pltpu.VMEM((GATHER_SLOTS, IDX_DIM, past), F32), pltpu.SemaphoreType.DMA((GATHER_SLOTS,))],
    )
    return pl.pallas_call(
        body,
        out_shape=jax.ShapeDtypeStruct((nd, 1, past), F32),
        grid_spec=grid_spec,
        compiler_params=_params(("arbitrary",)),
        name="sample_scores",
    )(page_table, qi3, wi3, kidx_t)


def _sample_select_body(sp_ref, qi_ref, misc_ref, bp_ref, bs_ref, sc_scr, j_scr, *, past, tk, topk, idx_bits):
    rows = sp_ref.shape[0]
    misc = misc_ref[...]
    d_in = lax.broadcasted_iota(I32, (LANES, QI_WIDTH), 0)
    c_out = lax.broadcasted_iota(I32, (LANES, QI_WIDTH), 1)
    rep = jnp.where((d_in < IDX_DIM) & (c_out % IDX_DIM == d_in), 1.0, 0.0)
    ki_t = jnp.dot(misc, rep, precision=HIGHEST, preferred_element_type=F32)
    c_in = lax.broadcasted_iota(I32, (QI_WIDTH, LANES), 0)
    l_out = lax.broadcasted_iota(I32, (QI_WIDTH, LANES), 1)
    seg = jnp.where(l_out == MISC_WI + c_in // IDX_DIM, 1.0, 0.0)
    hd = jnp.dot(qi_ref[...] * ki_t, seg, precision=HIGHEST, preferred_element_type=F32)
    lane = lax.broadcasted_iota(I32, (rows, LANES), 1)
    is_wi = (lane >= MISC_WI) & (lane < MISC_GLR)
    s_self = jnp.sum(jnp.where(is_wi, jnp.maximum(hd, 0.0) * misc, 0.0), axis=1, keepdims=True)

    sc_scr[:, 0:past] = sp_ref[...]
    sc_scr[:, past:] = jnp.where(lax.broadcasted_iota(I32, (rows, tk), 1) == 0, s_self, -jnp.inf)
    nk = sc_scr.shape[1] // tk
    thr = _select_threshold(sc_scr, j_scr, nk, tk, topk, idx_bits, 1)
    jb = j_scr[...]
    kpos = lax.broadcasted_iota(I32, sc_scr.shape, 1)
    bias = jnp.where(_selected(sc_scr[...], kpos, thr, jb), 0.0, NEG)
    bp_ref[...] = bias[:, 0:past]
    bs_ref[...] = bias[:, past:past + LANES]


def _sample_select(s_past, qi2, misc, topk):
    nd, past = s_past.shape
    tk = LANES
    width = past + tk
    idx_bits = max(1, (width - 1).bit_length()) + 1
    body = functools.partial(_sample_select_body, past=past, tk=tk, topk=topk, idx_bits=idx_bits)
    return pl.pallas_call(
        body,
        out_shape=(jax.ShapeDtypeStruct((nd, past), F32), jax.ShapeDtypeStruct((nd, LANES), F32)),
        grid=(1,),
        in_specs=[
            pl.BlockSpec((nd, past), lambda i: (0, 0)),
            pl.BlockSpec((nd, QI_WIDTH), lambda i: (0, 0)),
            pl.BlockSpec((nd, LANES), lambda i: (0, 0)),
        ],
        out_specs=(pl.BlockSpec((nd, past), lambda i: (0, 0)), pl.BlockSpec((nd, LANES), lambda i: (0, 0))),
        scratch_shapes=[pltpu.VMEM((nd, width), F32), pltpu.VMEM((nd, 1), I32)],
        compiler_params=_params(("arbitrary",)),
        name="sample_select",
    )(s_past, qi2, misc)


SAMPLE_KCHUNK = 1024


def _sample_attn_body(pt_ref, q_ref, ks_ref, vs_ref, b2_ref, bs_ref, ck_hbm, cv_hbm, o_ref, kbuf, vbuf, sem,
                      *, n_pages, prows):
    def fetch(seq, slot):
        dst = lambda p: (pl.ds(p * prows, prows), slice(None))
        return (_page_copies(pt_ref, ck_hbm, kbuf, sem.at[0, slot], seq, slot, n_pages, dst)
                + _page_copies(pt_ref, cv_hbm, vbuf, sem.at[1, slot], seq, slot, n_pages, dst))

    slot = _gather_step(fetch)
    q = q_ref[0]
    qb = q.astype(BF16)
    total = n_pages * prows
    ch = min(SAMPLE_KCHUNK, total)
    head_grp = lax.broadcasted_iota(I32, (N_HEADS, ch), 0) // HEADS_PER_KV
    row_grp = lax.broadcasted_iota(I32, (N_HEADS, ch), 1) % N_KV_HEADS
    own = head_grp == row_grp
    s_chunks = []
    for c in range(total // ch):
        kc = kbuf[slot, c * ch:(c + 1) * ch, :].astype(BF16)
        s_chunks.append(jnp.where(own, _dot_nt(qb, kc) + b2_ref[0][:, c * ch:(c + 1) * ch], NEG))

    grp0 = lax.broadcasted_iota(I32, (N_HEADS, HEAD_DIM), 0) < HEADS_PER_KV
    k_self = jnp.where(grp0, ks_ref[0][:, 0:HEAD_DIM], ks_ref[0][:, HEAD_DIM:])
    v_self = jnp.where(grp0, vs_ref[0][:, 0:HEAD_DIM], vs_ref[0][:, HEAD_DIM:])
    s_self = jnp.sum(q * k_self, axis=1, keepdims=True) + bs_ref[0][:, 0:1]

    m = s_self
    for s in s_chunks:
        m = jnp.maximum(m, jnp.max(s, axis=1, keepdims=True))
    p_self = jnp.exp(s_self - m)
    l = p_self
    acc = p_self * v_self
    for c, s in enumerate(s_chunks):
        p = jnp.exp(s - m)
        l = l + jnp.sum(p, axis=1, keepdims=True)
        vc = vbuf[slot, c * ch:(c + 1) * ch, :].astype(BF16)
        acc = acc + jnp.dot(p.astype(BF16), vc, preferred_element_type=F32)
    o_ref[0] = (acc / l).astype(BF16)


def _sample_attn(page_table, q3, k_self, v_self, bias2, bias_self, ck2, cv2):
    nd, n_pages = page_table.shape
    prows = ck2.shape[1]
    total = n_pages * prows
    body = functools.partial(_sample_attn_body, n_pages=n_pages, prows=prows)
    grid_spec = pltpu.PrefetchScalarGridSpec(
        num_scalar_prefetch=1,
        grid=(nd,),
        in_specs=[
            pl.BlockSpec((1, N_HEADS, HEAD_DIM), lambda b, pt: (b, 0, 0)),
            pl.BlockSpec((1, 1, KV_WIDTH), lambda b, pt: (b, 0, 0)),
            pl.BlockSpec((1, 1, KV_WIDTH), lambda b, pt: (b, 0, 0)),
            pl.BlockSpec((1, 1, total), lambda b, pt: (b, 0, 0)),
            pl.BlockSpec((1, 1, LANES), lambda b, pt: (b, 0, 0)),
            pl.BlockSpec(memory_space=pl.ANY),
            pl.BlockSpec(memory_space=pl.ANY),
        ],
        out_specs=pl.BlockSpec((1, N_HEADS, HEAD_DIM), lambda b, pt: (b, 0, 0)),
        scratch_shapes=[
            pltpu.VMEM((GATHER_SLOTS, total, HEAD_DIM), F32),
            pltpu.VMEM((GATHER_SLOTS, total, HEAD_DIM), F32),
            pltpu.SemaphoreType.DMA((2, GATHER_SLOTS)),
        ],
    )
    return pl.pallas_call(
        body,
        out_shape=jax.ShapeDtypeStruct((nd, N_HEADS, HEAD_DIM), BF16),
        grid_spec=grid_spec,
        compiler_params=_params(("arbitrary",)),
        name="sample_attn",
    )(page_table, q3, k_self, v_self, bias2, bias_self, ck2, cv2)


def _gla_sample_body(gq_ref, gk_ref, gv_ref, misc_ref, gr_ref, s_ref, w2_ref, gb_ref, nw_ref, og_ref, so_ref):
    eye = jnp.where(lax.broadcasted_iota(I32, (GLA_DK, GLA_DK), 0)
                    == lax.broadcasted_iota(I32, (GLA_DK, GLA_DK), 1), 1.0, 0.0)

    def column(row):
        return jnp.sum(eye * row, axis=1, keepdims=True)

    for sq in range(s_ref.shape[0]):
        misc = misc_ref[sq]
        for h in range(GLA_HEADS):
            ksl = slice(h * GLA_DK, (h + 1) * GLA_DK)
            vsl = slice(h * GLA_DV, (h + 1) * GLA_DV)
            g = _log_decay(misc, w2_ref[:, ksl], gb_ref[:, ksl])
            s_new = column(jnp.exp(g)) * s_ref[sq, h] + column(gk_ref[sq][:, ksl]) * gv_ref[sq][:, vsl]
            so_ref[sq, h] = s_new
            o = jnp.sum(column(gq_ref[sq][:, ksl] * (GLA_DK ** -0.5)) * s_new, axis=0, keepdims=True)
            og_ref[sq, :, vsl] = _gla_out(o, nw_ref[...], gr_ref[sq][:, vsl])


def _gla_sample(gq, gk, gv, misc, gr, state, w2p, gbias, norm_w):
    nd = state.shape[0]
    sb = GLA_SAMPLE_SB if nd % GLA_SAMPLE_SB == 0 else 1

    def row3(w):
        return pl.BlockSpec((sb, 1, w), lambda b: (b, 0, 0))

    st_spec = pl.BlockSpec((sb, GLA_HEADS, GLA_DK, GLA_DV), lambda b: (b, 0, 0, 0))
    return pl.pallas_call(
        _gla_sample_body,
        out_shape=(jax.ShapeDtypeStruct((nd, 1, GLA_VAL_WIDTH), BF16),
                   jax.ShapeDtypeStruct(state.shape, F32)),
        grid=(nd // sb,),
        in_specs=[
            row3(GLA_KEY_WIDTH), row3(GLA_KEY_WIDTH), row3(GLA_VAL_WIDTH), row3(LANES), row3(GLA_VAL_WIDTH),
            st_spec,
            pl.BlockSpec((LANES, GLA_KEY_WIDTH), lambda b: (0, 0)),
            pl.BlockSpec((1, GLA_KEY_WIDTH), lambda b: (0, 0)),
            pl.BlockSpec((1, GLA_DV), lambda b: (0, 0)),
        ],
        out_specs=(row3(GLA_VAL_WIDTH), st_spec),
        compiler_params=_params(("parallel",)),
        name="gla_sample",
    )(gq, gk, gv, misc, gr, state, w2p, gbias, norm_w.reshape(1, GLA_DV))


def _rope_tables(pos, d):
    inv = ROPE_THETA ** (-jnp.arange(0, d, 2, dtype=F32) / d)
    ang = pos.astype(F32)[:, None] * inv[None, :]
    cos, sin = jnp.cos(ang), jnp.sin(ang)
    reps = LANES // d
    return (jnp.tile(jnp.concatenate([cos, cos], axis=-1), (1, reps)),
            jnp.tile(jnp.concatenate([-sin, sin], axis=-1), (1, reps)))


def kernel(x_prompt, x_sample, cache_k, cache_v, cache_kidx, page_table, state_gla,
           ffn1_pre_w, ffn1_w_gate, ffn1_w_up, ffn1_w_down, ffn1_post_w,
           mix_pre_w, w_in, gla_gate_w2, gla_gate_b, gla_norm_w,
           w_proj_attn, w_proj_gla, w_out, mix_post_w,
           ffn2_pre_w, ffn2_w_gate, ffn2_w_up, ffn2_w_down, ffn2_post_w):
    b, s, d = x_prompt.shape
    nd, td, _ = x_sample.shape
    n_pool, page = cache_k.shape[:2]
    n_pages = page_table.shape[1]
    past = n_pages * page
    assert td == 1 and s % Q_BLOCK == 0 and d % MIX_TN == 0 and C_GA % d == 0

    wpa, wpg, wo = w_proj_attn.astype(BF16), w_proj_gla.astype(BF16), w_out.astype(BF16)
    w2p = jnp.zeros((LANES, GLA_KEY_WIDTH), F32).at[MISC_GLR:MISC_GLR + GLA_GATE_RANK].set(gla_gate_w2)
    gbias = gla_gate_b.reshape(1, GLA_KEY_WIDTH)

    def trunk(x, tm, tm_mix, tabs, tab_rows, mixer, f1, f2, w_mix):
        h = _ffn(x, ffn1_pre_w, *f1, ffn1_post_w, tm)
        h, f1b = h if isinstance(h, tuple) else (h, f1)
        z, k_rows, v_rows, *w_mix_b = _mix_in(h, mix_pre_w, w_mix, tabs, tm_mix, tab_rows // tm_mix)
        o_attn, o_gla, s_fin = mixer(z)
        h = _merge_out(o_attn, o_gla, z, h, wpa, wpg, wo, mix_post_w, min(tm, MERGE_TM))
        y = _ffn(h, ffn2_pre_w, *f2, ffn2_post_w, tm)
        y, f2b = y if isinstance(y, tuple) else (y, f2)
        return y, (k_rows, v_rows, z[:, C_MISC:C_MISC + IDX_DIM]), s_fin, f1b, f2b, (w_mix_b or [w_mix])[0]

    pos_p = jnp.arange(s, dtype=I32)
    tm_p = ROW_TM if s % ROW_TM == 0 else Q_BLOCK
    tabs_p = _rope_tables(pos_p, HEAD_DIM) + _rope_tables(pos_p, IDX_DIM)
    top_p = min(TOPK_MAX, s // 4)

    def mixer_p(z):
        o_attn = _attn_prompt(z, b, s, top_p)
        o_gla, s_fin = _gla_prompt(z, w2p, gbias, gla_norm_w, b, s)
        return o_attn, o_gla, s_fin

    tm_mix = MIX_TM if s % MIX_TM == 0 else tm_p

    pos_s = jnp.full((nd,), past, I32)
    tabs_s = _rope_tables(pos_s, HEAD_DIM) + _rope_tables(pos_s, IDX_DIM)
    top_s = min(TOPK_MAX, (past + td) // 4)
    ck2 = cache_k.reshape(n_pool, page * N_KV_HEADS, HEAD_DIM)
    cv2 = cache_v.reshape(n_pool, page * N_KV_HEADS, HEAD_DIM)
    kidx_t = jnp.swapaxes(cache_kidx, 1, 2)

    def mixer_s(z):
        misc = z[:, C_MISC:C_MISC + LANES]
        qi2 = z[:, C_QI:C_QI + QI_WIDTH]
        s_past = _sample_scores(page_table, qi2.reshape(nd, IDX_HEADS, IDX_DIM),
                                misc[:, MISC_WI:MISC_GLR].reshape(nd, IDX_HEADS, 1), kidx_t)
        bias_past, bias_self = _sample_select(s_past.reshape(nd, past), qi2, misc, top_s)
        bias2 = jnp.repeat(bias_past, N_KV_HEADS, axis=1).reshape(nd, 1, past * N_KV_HEADS)
        o_attn = _sample_attn(page_table, z[:, C_Q:C_Q + ATTN_WIDTH].reshape(nd, N_HEADS, HEAD_DIM),
                              z[:, C_K:C_K + KV_WIDTH].reshape(nd, 1, KV_WIDTH),
                              z[:, C_V:C_V + KV_WIDTH].reshape(nd, 1, KV_WIDTH),
                              bias2, bias_self.reshape(nd, 1, LANES), ck2, cv2)
        o_gla, s_fin = _gla_sample(z[:, C_GQ:C_GQ + GLA_KEY_WIDTH].reshape(nd, 1, GLA_KEY_WIDTH),
                                   z[:, C_GK:C_GK + GLA_KEY_WIDTH].reshape(nd, 1, GLA_KEY_WIDTH),
                                   z[:, C_GV:C_GV + GLA_VAL_WIDTH].reshape(nd, 1, GLA_VAL_WIDTH),
                                   misc.reshape(nd, 1, LANES),
                                   z[:, C_GR:C_GR + GLA_VAL_WIDTH].reshape(nd, 1, GLA_VAL_WIDTH),
                                   state_gla, w2p, gbias, gla_norm_w)
        return o_attn.reshape(nd, ATTN_WIDTH), o_gla.reshape(nd, GLA_VAL_WIDTH), s_fin

    y_s, kv_s, gla_s, f1b, f2b, w_mix_b = trunk(
        x_sample.reshape(nd, d), nd, nd, tabs_s, nd, mixer_s,
        (ffn1_w_gate, ffn1_w_up, ffn1_w_down), (ffn2_w_gate, ffn2_w_up, ffn2_w_down), w_in.T)
    y_p, kv_p, gla_p, _, _, _ = trunk(x_prompt.reshape(b * s, d), tm_p, tm_mix, tabs_p, s, mixer_p, f1b, f2b, w_mix_b)

    def kv_out(kv, n, t):
        k_rows, v_rows, ki = kv
        return (k_rows.reshape(n, t, N_KV_HEADS, HEAD_DIM), v_rows.reshape(n, t, N_KV_HEADS, HEAD_DIM),
                ki.reshape(n, t, IDX_DIM))

    k_p, v_p, ki_p = kv_out(kv_p, b, s)
    k_s, v_s, ki_s = kv_out(kv_s, nd, td)
    return (y_p.reshape(b, s, d), y_s.reshape(nd, td, d), k_p, v_p, ki_p, gla_p, k_s, v_s, ki_s, gla_s)
```

```python
import functools

import jax
import jax.numpy as jnp
from jax import lax
from jax.experimental import pallas as pl
from jax.experimental.pallas import tpu as pltpu

F32, BF16, I32 = jnp.float32, jnp.bfloat16, jnp.int32
HIGHEST = lax.Precision.HIGHEST

N_HEADS = 8
N_KV_HEADS = 2
HEAD_DIM = 128
IDX_HEADS = 16
IDX_DIM = 64
TOPK_MAX = 256
Q_BLOCK = 128
ROPE_THETA = 10000.0
GLA_HEADS = 4
GLA_DK = 128
GLA_DV = 256
GLA_GATE_RANK = 16
GLA_GATE_TAU = 16.0
RMS_EPS = 1e-6

LANES = 128
ATTN_WIDTH = N_HEADS * HEAD_DIM
KV_WIDTH = N_KV_HEADS * HEAD_DIM
QI_WIDTH = IDX_HEADS * IDX_DIM
GLA_KEY_WIDTH = GLA_HEADS * GLA_DK
GLA_VAL_WIDTH = GLA_HEADS * GLA_DV
HEADS_PER_KV = N_HEADS // N_KV_HEADS

MIX_TN = 512

C_Q = 0
C_K = C_Q + ATTN_WIDTH
C_V = C_K + KV_WIDTH
C_QI = C_V + KV_WIDTH
C_MISC = C_QI + QI_WIDTH
C_GQ = -(-(C_MISC + LANES) // MIX_TN) * MIX_TN
C_GK = C_GQ + GLA_KEY_WIDTH
C_GV = C_GK + GLA_KEY_WIDTH
C_GR = C_GV + GLA_VAL_WIDTH
C_GA = C_GR + GLA_VAL_WIDTH
MISC_WI, MISC_GLR = IDX_DIM, IDX_DIM + IDX_HEADS

ROW_TM = 512
MERGE_TM = 256
MIX_TM = 1024
FFN_TF = 512
ATTN_TK = 512
INDEXER_SUB = 128
GLA_TB = 512
GLA_SAMPLE_SB = 16
VMEM_LIMIT = 56 * 1024 * 1024

NEG = -1e30
SOFTMAX_MIN_DENOM = 2.0 ** -60
BOUND_SLACK = 1.02
INT_MIN = -2 ** 31
INT_ABS_MASK = 0x7FFFFFFF
NEG_INF_KEY = -2139095041


def _params(sem, vmem=VMEM_LIMIT):
    return pltpu.CompilerParams(dimension_semantics=sem, vmem_limit_bytes=vmem)


def _rms(x, w):
    return x * lax.rsqrt(jnp.mean(x * x, axis=-1, keepdims=True) + RMS_EPS) * w


def _dot_nt(a, b):
    return lax.dot_general(a, b, (((1,), (1,)), ((), ())), preferred_element_type=F32)


def _ffn_body(x_ref, prew_ref, wg_ref, wu_ref, wd_ref, postw_ref, o_ref, *rest):
    *wb_refs, z_scr, acc_scr = rest
    j = pl.program_id(1)

    @pl.when(j == 0)
    def _():
        z_scr[...] = _rms(x_ref[...], prew_ref[...]).astype(BF16)
        acc_scr[...] = jnp.zeros_like(acc_scr)

    wg, wu, wd = wg_ref[...].astype(BF16), wu_ref[...].astype(BF16), wd_ref[...].astype(BF16)
    for ref, w in zip(wb_refs, (wg, wu, wd)):
        ref[...] = w
    z = z_scr[...]
    g = jnp.dot(z, wg, preferred_element_type=F32)
    u = jnp.dot(z, wu, preferred_element_type=F32)
    a = (g * jax.nn.sigmoid(g) * u).astype(BF16)
    acc_scr[...] += jnp.dot(a, wd, preferred_element_type=F32)

    @pl.when(j == pl.num_programs(1) - 1)
    def _():
        o_ref[...] = x_ref[...] + 0.5 * _rms(acc_scr[...], postw_ref[...])


def _ffn(x, pre_w, wg, wu, wd, post_w, tm):
    rows, d = x.shape
    dff = wg.shape[1]
    tf = FFN_TF if dff % FFN_TF == 0 else dff
    emit = wg.dtype == F32
    assert not emit or rows == tm
    w_specs = [
        pl.BlockSpec((d, tf), lambda i, j: (0, j)),
        pl.BlockSpec((d, tf), lambda i, j: (0, j)),
        pl.BlockSpec((tf, d), lambda i, j: (j, 0)),
    ]
    y_shape = jax.ShapeDtypeStruct((rows, d), F32)
    y_spec = pl.BlockSpec((tm, d), lambda i, j: (i, 0))
    out = pl.pallas_call(
        _ffn_body,
        out_shape=(y_shape, *(jax.ShapeDtypeStruct(w.shape, BF16) for w in (wg, wu, wd))) if emit else y_shape,
        grid=(rows // tm, dff // tf),
        in_specs=[
            pl.BlockSpec((tm, d), lambda i, j: (i, 0)),
            pl.BlockSpec((1, d), lambda i, j: (0, 0)),
            *w_specs,
            pl.BlockSpec((1, d), lambda i, j: (0, 0)),
        ],
        out_specs=(y_spec, *w_specs) if emit else y_spec,
        scratch_shapes=[pltpu.VMEM((tm, d), BF16), pltpu.VMEM((tm, d), F32)],
        compiler_params=_params(("parallel", "arbitrary")),
        name="ffn",
    )(x, pre_w.reshape(1, d), wg, wu, wd, post_w.reshape(1, d))
    return (out[0], tuple(out[1:])) if emit else out


def _mix_in_body(x_ref, prew_ref, wt_ref, wglr_ref, c128_ref, s128_ref, c64_ref, s64_ref, o_ref, kf_ref, vf_ref,
                 *rest):
    *wb_ref, u_scr = rest
    j = pl.program_id(1)

    @pl.when(j == 0)
    def _():
        u_scr[...] = _rms(x_ref[...], prew_ref[...]).astype(BF16)

    if wb_ref:
        w_ref = wb_ref[0]
        w_ref[...] = wt_ref[...].astype(BF16)

        @pl.when(j == C_MISC // MIX_TN)
        def _():
            row = lax.broadcasted_iota(I32, (LANES, w_ref.shape[1]), 0)
            w_ref[0:LANES, :] = jnp.where(row < MISC_GLR, w_ref[0:LANES, :], jnp.where(
                row < MISC_GLR + GLA_GATE_RANK, wglr_ref[...].astype(BF16), jnp.zeros((), BF16)))
            w_ref[LANES:, :] = jnp.zeros((MIX_TN - LANES, w_ref.shape[1]), BF16)
    else:
        w_ref = wt_ref

    o_ref[...] = _dot_nt(u_scr[...], w_ref[...])

    def rope128(x):
        return x * c128_ref[...] + pltpu.roll(x, HEAD_DIM // 2, 1) * s128_ref[...]

    def rope64(x):
        lane = lax.broadcasted_iota(I32, x.shape, 1)
        first = (lane % IDX_DIM) < (IDX_DIM // 2)
        rot = jnp.where(first, pltpu.roll(x, LANES - IDX_DIM // 2, 1), pltpu.roll(x, IDX_DIM // 2, 1))
        return x * c64_ref[...] + rot * s64_ref[...]

    def sl(t):
        return slice(t * LANES, (t + 1) * LANES)

    @pl.when(j < 2)
    def _():
        for t in range(4):
            o_ref[:, sl(t)] = rope128(o_ref[:, sl(t)]) * (HEAD_DIM ** -0.5)

    @pl.when(j == 2)
    def _():
        tm = o_ref.shape[0]
        for t in range(N_KV_HEADS):
            k = rope128(o_ref[:, sl(t)])
            o_ref[:, sl(t)] = k
            kf_ref[pl.ds(t, tm, stride=N_KV_HEADS), :] = k
            vf_ref[pl.ds(t, tm, stride=N_KV_HEADS), :] = o_ref[:, sl(N_KV_HEADS + t)]

    @pl.when((j == 3) | (j == 4))
    def _():
        for t in range(4):
            o_ref[:, sl(t)] = rope64(o_ref[:, sl(t)])

    @pl.when(j == 5)
    def _():
        x = o_ref[:, sl(0)]
        lane = lax.broadcasted_iota(I32, x.shape, 1)
        wi_scale = IDX_HEADS ** -0.5 * IDX_DIM ** -0.5
        o_ref[:, sl(0)] = jnp.where(lane < MISC_WI, rope64(x), jnp.where(lane < MISC_GLR, x * wi_scale, x))


def _mix_in(h, pre_w, w, tabs, tm, tab_blocks):
    rows, d = h.shape
    zw = C_GA + 2 * d
    emit = w.dtype == F32
    if emit:
        head = C_MISC + MISC_GLR
        gla_end = head + 2 * GLA_KEY_WIDTH + GLA_VAL_WIDTH
        tail = gla_end + GLA_GATE_RANK
        assert rows == tm and w.shape == (tail + GLA_VAL_WIDTH + 2 * d, d) and gla_end >= MISC_GLR
        sub = 8
        assert head % sub == 0 and tail % sub == 0 and (gla_end - MISC_GLR) % sub == 0
        tn8 = MIX_TN // sub

        def src_row(i, j):
            return (sub * jnp.where(j <= C_MISC // MIX_TN, j * tn8,
                                    jnp.where(j < C_GR // MIX_TN, head // sub + (j - C_GQ // MIX_TN) * tn8,
                                              tail // sub + (j - C_GR // MIX_TN) * tn8)), 0)

        w_specs = [pl.BlockSpec((pl.Element(MIX_TN), pl.Element(d)), src_row),
                   pl.BlockSpec((pl.Element(LANES), pl.Element(d)), lambda i, j: (gla_end - MISC_GLR, 0))]
    else:
        assert w.shape == (zw, d)
        w_specs = [pl.BlockSpec((MIX_TN, d), lambda i, j: (j, 0)), pl.BlockSpec((LANES, d), lambda i, j: (0, 0))]
    wb_shape = (jax.ShapeDtypeStruct((zw, d), BF16),) if emit else ()
    wb_spec = (pl.BlockSpec((MIX_TN, d), lambda i, j: (j, 0)),) if emit else ()

    c128, s128, c64, s64 = tabs
    tab_spec = pl.BlockSpec((tm, LANES), lambda i, j: (i % tab_blocks, 0))
    kv_shape = jax.ShapeDtypeStruct((rows * N_KV_HEADS, HEAD_DIM), F32)
    kv_spec = pl.BlockSpec((tm * N_KV_HEADS, HEAD_DIM), lambda i, j: (i, 0))
    return pl.pallas_call(
        _mix_in_body,
        out_shape=(jax.ShapeDtypeStruct((rows, zw), F32), kv_shape, kv_shape, *wb_shape),
        grid=(rows // tm, zw // MIX_TN),
        in_specs=[
            pl.BlockSpec((tm, d), lambda i, j: (i, 0)),
            pl.BlockSpec((1, d), lambda i, j: (0, 0)),
            *w_specs,
            tab_spec, tab_spec, tab_spec, tab_spec,
        ],
        out_specs=(pl.BlockSpec((tm, MIX_TN), lambda i, j: (i, j)), kv_spec, kv_spec, *wb_spec),
        scratch_shapes=[pltpu.VMEM((tm, d), BF16)],
        compiler_params=_params(("parallel", "arbitrary")),
        name="mix_in",
    )(h, pre_w.reshape(1, d), w, w, c128, s128, c64, s64)


def _key_to_float(key):
    key = jnp.maximum(key, NEG_INF_KEY)
    return lax.bitcast_convert_type(key ^ ((key >> 31) & jnp.int32(INT_ABS_MASK)), F32)


SORT_N = 16


def _sort_pairs(n):
    pairs, p = [], 1
    while p < n:
        k = p
        while k >= 1:
            for j in range(k % p, n - k, 2 * k):
                for i in range(min(k, n - j - k)):
                    if (i + j) // (2 * p) == (i + j + k) // (2 * p):
                        pairs.append((i + j, i + j + k))
            k //= 2
        p *= 2
    return pairs


def _sort_groups_desc(x):
    v = [x[j * 8:(j + 1) * 8, :] for j in range(SORT_N)]
    for i, j in _sort_pairs(SORT_N):
        v[i], v[j] = jnp.maximum(v[i], v[j]), jnp.minimum(v[i], v[j])
    return jnp.concatenate(v, axis=0)


def _count_sorted(v, t, cmp):
    c8 = cmp(v[7], t)
    c4 = cmp(jnp.where(c8, v[11], v[3]), t)
    c2 = cmp(jnp.where(c8, jnp.where(c4, v[13], v[9]), jnp.where(c4, v[5], v[1])), t)
    e = [jnp.where(c2, v[4 * a + 2], v[4 * a]) for a in range(4)]
    c1 = cmp(jnp.where(c8, jnp.where(c4, e[3], e[2]), jnp.where(c4, e[1], e[0])), t)
    low = (jnp.where(c8, 8.0, 0.0) + jnp.where(c4, 4.0, 0.0)) + (jnp.where(c2, 2.0, 0.0) + jnp.where(c1, 1.0, 0.0))
    return jnp.where(cmp(v[15], t), 16.0, low)


def _select_threshold(sc_ref, j_ref, nk, tk, topk, idx_bits, key_axis, srt_ref=None):
    n_other = sc_ref.shape[1 - key_axis]
    vec = (n_other, 1) if key_axis == 1 else (1, n_other)
    step = LANES if key_axis == 1 else 8

    def count_cmp(t, cmp):
        if srt_ref is None:
            return count(lambda blk, ks: cmp(blk, t))
        tb = jnp.broadcast_to(t, (8, n_other))

        def body(c, acc):
            ks = pl.multiple_of(c * tk, tk)
            for g0 in range(0, tk, SORT_N * 8):
                v = [srt_ref[pl.ds(ks + g0 + j * 8, 8), :] for j in range(SORT_N)]
                acc = acc + _count_sorted(v, tb, cmp)
            return acc
        return jnp.sum(lax.fori_loop(0, nk, body, jnp.zeros((8, n_other), F32)), axis=0, keepdims=True)

    def count(pred):
        def body(c, acc):
            ks = pl.multiple_of(c * tk, tk)
            blk = sc_ref[:, pl.ds(ks, tk)] if key_axis == 1 else sc_ref[pl.ds(ks, tk), :]
            m = jnp.where(pred(blk, ks), 1.0, 0.0)
            parts = [lax.slice_in_dim(m, a, a + step, axis=key_axis) for a in range(0, tk, step)]
            while len(parts) > 1:
                parts = [parts[a] + parts[a + 1] for a in range(0, len(parts), 2)]
            return acc + parts[0]
        acc0 = jnp.zeros((n_other, step) if key_axis == 1 else (step, n_other), F32)
        return jnp.sum(lax.fori_loop(0, nk, body, acc0), axis=key_axis, keepdims=True)

    def bit_body(bi, t):
        cand = t ^ lax.shift_left(jnp.int32(1), 31 - bi)
        return jnp.where(count_cmp(_key_to_float(cand), jnp.greater_equal) >= topk, cand, t)

    thr_key = lax.fori_loop(0, 32, bit_body, jnp.full(vec, INT_MIN, I32))
    thr = _key_to_float(thr_key)

    cnt_ge = count_cmp(thr, jnp.greater_equal)
    cnt_gt = count_cmp(thr, jnp.greater)
    need = topk - cnt_gt
    tie = jnp.where((cnt_ge > topk) & (thr_key > NEG_INF_KEY), 1.0, 0.0)
    j_ref[...] = jnp.full(vec, 1 << idx_bits, I32)

    @pl.when(jnp.max(tie) > 0.0)
    def _():
        def jbit(bi, jb):
            cand = jb + lax.shift_left(jnp.int32(1), idx_bits - 1 - bi)

            def pred(blk, ks):
                kpos = ks + lax.broadcasted_iota(I32, blk.shape, key_axis)
                return (blk == thr) & (kpos < cand)
            return jnp.where(count(pred) <= need, cand, jb)
        j_ref[...] = lax.fori_loop(0, idx_bits, jbit, jnp.zeros(vec, I32))

    return thr


def _selected(score, kpos, thr, jb):
    return (score > thr) | ((score == thr) & (kpos < jb))


def _attn_prompt_body(q_ref, qia_ref, qib_ref, misc_ref, k_ref, v_ref, kim_ref, o_ref,
                      sc_scr, srt_scr, j_scr, bias_scr, kb_scr, vb_scr, kib_scr, kn_scr, qi_scr, wt_scr, qs_scr,
                      mx_scr, l_scr, acc_scr, *, tk, topk, idx_bits):
    i = pl.program_id(1)
    tq = Q_BLOCK
    nk = (i * tq + tq + tk - 1) // tk
    n_tiles = tk // LANES
    sub = min(INDEXER_SUB, tk)
    qpos_t = i * tq + lax.broadcasted_iota(I32, (1, tq), 1)

    @pl.when(i == 0)
    def _():
        def cast_chunk(c, kn):
            rs = pl.ds(pl.multiple_of(c * tk, tk), tk)
            kf = k_ref[rs, :]
            kb_scr[rs, :] = kf.astype(BF16)
            vb_scr[rs, :] = v_ref[rs, :].astype(BF16)
            kib_scr[rs, :] = kim_ref[rs, 0:IDX_DIM].astype(BF16)
            sq = kf * kf
            return tuple(
                jnp.maximum(kn[g], jnp.max(jnp.sum(sq[:, g * HEAD_DIM:(g + 1) * HEAD_DIM], axis=1, keepdims=True),
                                           axis=0, keepdims=True))
                for g in range(N_KV_HEADS))
        kn = lax.fori_loop(0, k_ref.shape[0] // tk, cast_chunk, (jnp.zeros((1, 1), F32),) * N_KV_HEADS)
        for g in range(N_KV_HEADS):
            kn_scr[g] = jnp.broadcast_to(kn[g], kn_scr.shape[1:])

    for h in range(IDX_HEADS):
        ref = qia_ref if h < IDX_HEADS // 2 else qib_ref
        hh = h % (IDX_HEADS // 2)
        qi_scr[h // 2, (h % 2) * tq:(h % 2 + 1) * tq, :] = ref[:, hh * IDX_DIM:(hh + 1) * IDX_DIM].astype(BF16)
    wt_scr[...] = misc_ref[...].T

    def score_chunk(c, carry):
        for s0 in range(0, tk, sub):
            ks = pl.multiple_of(c * tk, tk) + s0
            ki = kib_scr[pl.ds(ks, sub), :]
            acc = jnp.zeros((sub, tq), F32)
            for hp in range(IDX_HEADS // 2):
                r = jnp.maximum(_dot_nt(ki, qi_scr[hp]), 0.0)
                acc = (acc + r[:, 0:tq] * wt_scr[pl.ds(MISC_WI + 2 * hp, 1), :]
                       + r[:, tq:] * wt_scr[pl.ds(MISC_WI + 2 * hp + 1, 1), :])
            kpos = ks + lax.broadcasted_iota(I32, (sub, tq), 0)
            masked = jnp.where(kpos <= qpos_t, acc, -jnp.inf)
            sc_scr[pl.ds(ks, sub), :] = masked
            for g0 in range(0, sub, SORT_N * 8):
                srt_scr[pl.ds(ks + g0, SORT_N * 8), :] = _sort_groups_desc(masked[g0:g0 + SORT_N * 8, :])
        return carry

    lax.fori_loop(0, nk, score_chunk, 0)
    thr = _select_threshold(sc_scr, j_scr, nk, tk, topk, idx_bits, 0, srt_scr)
    jb = j_scr[...]

    for g in range(N_KV_HEADS):
        kn = jnp.sqrt(kn_scr[g][0:1, :])
        for r in range(HEADS_PER_KV):
            hd = g * HEADS_PER_KV + r
            qh = q_ref[:, hd * HEAD_DIM:(hd + 1) * HEAD_DIM]
            qs_scr[g, r * tq:(r + 1) * tq, :] = qh.astype(BF16)
            mx_scr[g, r * tq:(r + 1) * tq, :] = jnp.sqrt(jnp.sum(qh * qh, axis=1, keepdims=True)) * kn * BOUND_SLACK
    l_scr[...] = jnp.zeros(l_scr.shape, F32)
    acc_scr[...] = jnp.zeros(acc_scr.shape, F32)

    def logits(g, ks, bias4):
        kc = kb_scr[pl.ds(ks, tk), g * HEAD_DIM:(g + 1) * HEAD_DIM]
        return _dot_nt(qs_scr[g], kc) + bias4

    def exp_chunk(ks, bias):
        bias4 = jnp.concatenate([bias] * HEADS_PER_KV, axis=0)
        for g in range(N_KV_HEADS):
            s = logits(g, ks, bias4)
            m = mx_scr[g]
            p = [jnp.exp(s[:, t * LANES:(t + 1) * LANES] - m) for t in range(n_tiles)]
            l_scr[g] += functools.reduce(lambda x, y: x + y, p)
            pb = jnp.concatenate(p, axis=1).astype(BF16)
            vc = vb_scr[pl.ds(ks, tk), g * HEAD_DIM:(g + 1) * HEAD_DIM]
            acc_scr[g] += jnp.dot(pb, vc, preferred_element_type=F32)

    def bound_pass(c, carry):
        ks = pl.multiple_of(c * tk, tk)
        kpos = ks + lax.broadcasted_iota(I32, (tk, tq), 0)
        sel = _selected(sc_scr[pl.ds(ks, tk), :], kpos, thr, jb) & (kpos <= qpos_t)
        bias = jnp.where(sel, 0.0, NEG).T
        bias_scr[:, pl.ds(ks, tk)] = bias
        exp_chunk(ks, bias)
        return carry

    lax.fori_loop(0, nk, bound_pass, 0)

    def denominators():
        for g in range(N_KV_HEADS):
            mx_scr[g] = jnp.broadcast_to(jnp.sum(l_scr[g], axis=1, keepdims=True), mx_scr.shape[1:])

    denominators()
    l_min = jnp.min(functools.reduce(jnp.minimum, [mx_scr[g] for g in range(N_KV_HEADS)]))

    @pl.when(l_min < SOFTMAX_MIN_DENOM)
    def _():
        mx_scr[...] = jnp.full(mx_scr.shape, NEG, F32)
        l_scr[...] = jnp.zeros(l_scr.shape, F32)
        acc_scr[...] = jnp.zeros(acc_scr.shape, F32)

        def max_pass(c, carry):
            ks = pl.multiple_of(c * tk, tk)
            bias4 = jnp.concatenate([bias_scr[:, pl.ds(ks, tk)]] * HEADS_PER_KV, axis=0)
            for g in range(N_KV_HEADS):
                s = logits(g, ks, bias4)
                m = mx_scr[g]
                for t in range(n_tiles):
                    m = jnp.maximum(m, s[:, t * LANES:(t + 1) * LANES])
                mx_scr[g] = m
            return carry

        lax.fori_loop(0, nk, max_pass, 0)
        for g in range(N_KV_HEADS):
            mx_scr[g] = jnp.broadcast_to(jnp.max(mx_scr[g], axis=1, keepdims=True), mx_scr.shape[1:])

        def exp_pass(c, carry):
            ks = pl.multiple_of(c * tk, tk)
            exp_chunk(ks, bias_scr[:, pl.ds(ks, tk)])
            return carry

        lax.fori_loop(0, nk, exp_pass, 0)
        denominators()

    for g in range(N_KV_HEADS):
        o = acc_scr[g] / mx_scr[g]
        for r in range(HEADS_PER_KV):
            hd = g * HEADS_PER_KV + r
            o_ref[:, hd * HEAD_DIM:(hd + 1) * HEAD_DIM] = o[r * tq:(r + 1) * tq, :].astype(BF16)


def _attn_prompt(z, b, s, topk):
    nq = s // Q_BLOCK
    tk = min(ATTN_TK, s)
    idx_bits = max(1, (s - 1).bit_length()) + 1
    body = functools.partial(_attn_prompt_body, tk=tk, topk=topk, idx_bits=idx_bits)
    rows = HEADS_PER_KV * Q_BLOCK
    return pl.pallas_call(
        body,
        out_shape=jax.ShapeDtypeStruct((b * s, ATTN_WIDTH), BF16),
        grid=(b, nq),
        in_specs=[
            pl.BlockSpec((Q_BLOCK, ATTN_WIDTH), lambda bb, i: (bb * nq + i, 0)),
            pl.BlockSpec((Q_BLOCK, QI_WIDTH // 2), lambda bb, i: (bb * nq + i, C_QI // (QI_WIDTH // 2))),
            pl.BlockSpec((Q_BLOCK, QI_WIDTH // 2), lambda bb, i: (bb * nq + i, C_QI // (QI_WIDTH // 2) + 1)),
            pl.BlockSpec((Q_BLOCK, LANES), lambda bb, i: (bb * nq + i, C_MISC // LANES)),
            pl.BlockSpec((s, KV_WIDTH), lambda bb, i: (bb, C_K // KV_WIDTH)),
            pl.BlockSpec((s, KV_WIDTH), lambda bb, i: (bb, C_V // KV_WIDTH)),
            pl.BlockSpec((s, LANES), lambda bb, i: (bb, C_MISC // LANES)),
        ],
        out_specs=pl.BlockSpec((Q_BLOCK, ATTN_WIDTH), lambda bb, i: (bb * nq + i, 0)),
        scratch_shapes=[
            pltpu.VMEM((s, Q_BLOCK), F32),
            pltpu.VMEM((s, Q_BLOCK), F32),
            pltpu.VMEM((1, Q_BLOCK), I32),
            pltpu.VMEM((Q_BLOCK, s), F32),
            pltpu.VMEM((s, KV_WIDTH), BF16),
            pltpu.VMEM((s, KV_WIDTH), BF16),
            pltpu.VMEM((s, IDX_DIM), BF16),
            pltpu.VMEM((N_KV_HEADS, 8, LANES), F32),
            pltpu.VMEM((IDX_HEADS // 2, 2 * Q_BLOCK, IDX_DIM), BF16),
            pltpu.VMEM((LANES, Q_BLOCK), F32),
            pltpu.VMEM((N_KV_HEADS, rows, HEAD_DIM), BF16),
            pltpu.VMEM((N_KV_HEADS, rows, LANES), F32),
            pltpu.VMEM((N_KV_HEADS, rows, LANES), F32),
            pltpu.VMEM((N_KV_HEADS, rows, HEAD_DIM), F32),
        ],
        compiler_params=_params(("parallel", "arbitrary")),
        name="attn_prompt",
    )(z, z, z, z, z, z, z)


def _log_decay(misc, w2, gb):
    x = jnp.dot(misc, w2, precision=HIGHEST, preferred_element_type=F32) + gb
    return (jnp.minimum(x, 0.0) - jnp.log1p(jnp.exp(-jnp.abs(x)))) * (1.0 / GLA_GATE_TAU)


def _gla_out(o, nw, gr):
    return (_rms(o, nw) * (gr * jax.nn.sigmoid(gr))).astype(BF16)


GLA_C = 128
GLA_SUB = 8
GLA_HPS = 4


def _gla_prompt_body(gq_ref, gk_ref, gv_ref, misc_ref, gr_ref, w2_ref, gb_ref, nw_ref,
                     og_ref, sfin_ref, st_scr, a_scr, b_scr, k_scr, o_scr, *, nchunk):
    t = pl.program_id(2)
    c_ = GLA_C

    @pl.when(t == 0)
    def _():
        st_scr[...] = jnp.zeros_like(st_scr)

    a_scr[...] = jnp.zeros_like(a_scr)
    row = lax.broadcasted_iota(I32, (c_, c_), 0)
    col = lax.broadcasted_iota(I32, (c_, c_), 1)
    tri = jnp.where(col <= row, 1.0, 0.0)
    sub_row = lax.broadcasted_iota(I32, (GLA_SUB, LANES), 0)
    sub_col = lax.broadcasted_iota(I32, (GLA_SUB, GLA_SUB), 1)

    def head_chunk(hh, r0):
        ksl = slice(hh * GLA_DK, (hh + 1) * GLA_DK)
        vsl = slice(hh * GLA_DV, (hh + 1) * GLA_DV)
        q = gq_ref[pl.ds(r0, c_), ksl] * (GLA_DK ** -0.5)
        k = gk_ref[pl.ds(r0, c_), ksl]
        v = gv_ref[pl.ds(r0, c_), vsl].astype(BF16)
        g = _log_decay(misc_ref[pl.ds(r0, c_), :], w2_ref[:, ksl], gb_ref[:, ksl])
        b = jnp.dot(tri, g, precision=HIGHEST, preferred_element_type=F32)
        b_scr[hh] = b
        k_scr[hh] = k
        st = st_scr[hh]
        o = _dot_nt((q * jnp.exp(b)).astype(BF16), st.astype(BF16))

        n = c_ // 2
        while n >= GLA_SUB:
            for rb in range(n, c_, 2 * n):
                bref = b[rb:rb + 1, :]
                qs = q[rb:rb + n, :] * jnp.exp(b[rb:rb + n, :] - bref)
                ks = k[rb - n:rb, :] * jnp.exp(bref - b[rb - n:rb, :])
                a_scr[hh, rb:rb + n, rb - n:rb] = _dot_nt(qs.astype(BF16), ks.astype(BF16))
            n //= 2
        for blk in range(c_ // GLA_SUB):
            lo = blk * GLA_SUB
            qb = q[lo:lo + GLA_SUB, :]
            bb = b[lo:lo + GLA_SUB, :]
            ad = jnp.zeros((GLA_SUB, GLA_SUB), F32)
            for jj in range(GLA_SUB):
                bj = b_scr[hh, pl.ds(lo + jj, 1), :]
                kj = k_scr[hh, pl.ds(lo + jj, 1), :]
                w = jnp.exp(jnp.where(sub_row >= jj, bb - bj, NEG))
                colj = jnp.sum(qb * kj * w, axis=1, keepdims=True)
                ad = jnp.where(sub_col == jj, colj, ad)
            a_scr[hh, lo:lo + GLA_SUB, lo:lo + GLA_SUB] = ad

        o = o + jnp.dot(a_scr[hh].astype(BF16), v, preferred_element_type=F32)
        o_scr[pl.ds(r0, c_), vsl] = o
        b_last = b[c_ - 1:c_, :]
        kd = (k * jnp.exp(b_last - b)).astype(BF16)
        kv = lax.dot_general(v, kd, (((0,), (0,)), ((), ())), preferred_element_type=F32)
        st_scr[hh] = st * jnp.exp(b_last) + kv

    def chunk(ci, carry):
        r0 = pl.multiple_of(ci * c_, c_)
        for hh in range(GLA_HPS):
            head_chunk(hh, r0)
        return carry

    lax.fori_loop(0, nchunk, chunk, 0)
    for hh in range(GLA_HPS):
        vsl = slice(hh * GLA_DV, (hh + 1) * GLA_DV)
        og_ref[:, vsl] = _gla_out(o_scr[:, vsl], nw_ref[...], gr_ref[:, vsl])

    @pl.when(t == pl.num_programs(2) - 1)
    def _():
        for hh in range(GLA_HPS):
            sfin_ref[0, hh] = st_scr[hh].T


def _gla_prompt(z, w2p, gbias, norm_w, b, s):
    tb = min(GLA_TB, s)
    nt = s // tb
    body = functools.partial(_gla_prompt_body, nchunk=tb // GLA_C)

    def rowblk(bb, t):
        return bb * nt + t

    kw, vw = GLA_HPS * GLA_DK, GLA_HPS * GLA_DV
    return pl.pallas_call(
        body,
        out_shape=(jax.ShapeDtypeStruct((b * s, GLA_VAL_WIDTH), BF16),
                   jax.ShapeDtypeStruct((b, GLA_HEADS, GLA_DK, GLA_DV), F32)),
        grid=(b, GLA_HEADS // GLA_HPS, nt),
        in_specs=[
            pl.BlockSpec((tb, kw), lambda bb, h, t: (rowblk(bb, t), C_GQ // kw + h)),
            pl.BlockSpec((tb, kw), lambda bb, h, t: (rowblk(bb, t), C_GK // kw + h)),
            pl.BlockSpec((tb, vw), lambda bb, h, t: (rowblk(bb, t), C_GV // vw + h)),
            pl.BlockSpec((tb, LANES), lambda bb, h, t: (rowblk(bb, t), C_MISC // LANES)),
            pl.BlockSpec((tb, vw), lambda bb, h, t: (rowblk(bb, t), C_GR // vw + h)),
            pl.BlockSpec((LANES, kw), lambda bb, h, t: (0, h)),
            pl.BlockSpec((1, kw), lambda bb, h, t: (0, h)),
            pl.BlockSpec((1, GLA_DV), lambda bb, h, t: (0, 0)),
        ],
        out_specs=(pl.BlockSpec((tb, vw), lambda bb, h, t: (rowblk(bb, t), h)),
                   pl.BlockSpec((1, GLA_HPS, GLA_DK, GLA_DV), lambda bb, h, t: (bb, h, 0, 0))),
        scratch_shapes=[
            pltpu.VMEM((GLA_HPS, GLA_DV, GLA_DK), F32),
            pltpu.VMEM((GLA_HPS, GLA_C, GLA_C), F32),
            pltpu.VMEM((GLA_HPS, GLA_C, GLA_DK), F32),
            pltpu.VMEM((GLA_HPS, GLA_C, GLA_DK), F32),
            pltpu.VMEM((tb, vw), F32),
        ],
        compiler_params=_params(("parallel", "parallel", "arbitrary")),
        name="gla_prompt",
    )(z, z, z, z, z, w2p, gbias, norm_w.reshape(1, GLA_DV))


def _merge_out_body(oa_ref, og_ref, ga_ref, gg_ref, h_ref, wa_ref, wg_ref, wo_ref, pw_ref, o_ref):
    pa = jnp.dot(oa_ref[...], wa_ref[...], preferred_element_type=F32)
    pg = jnp.dot(og_ref[...], wg_ref[...], preferred_element_type=F32)
    merged = (jax.nn.sigmoid(ga_ref[...]) * pa + jax.nn.sigmoid(gg_ref[...]) * pg).astype(BF16)
    y = jnp.dot(merged, wo_ref[...], preferred_element_type=F32)
    o_ref[...] = h_ref[...] + _rms(y, pw_ref[...])


def _merge_out(o_attn, o_gla, z, h, wa, wg, w_out, post_w, tm):
    rows, d = h.shape
    once = pl.Buffered(1)
    return pl.pallas_call(
        _merge_out_body,
        out_shape=jax.ShapeDtypeStruct((rows, d), F32),
        grid=(rows // tm,),
        in_specs=[
            pl.BlockSpec((tm, ATTN_WIDTH), lambda i: (i, 0)),
            pl.BlockSpec((tm, GLA_VAL_WIDTH), lambda i: (i, 0)),
            pl.BlockSpec((tm, d), lambda i: (i, C_GA // d)),
            pl.BlockSpec((tm, d), lambda i: (i, C_GA // d + 1)),
            pl.BlockSpec((tm, d), lambda i: (i, 0)),
            pl.BlockSpec((ATTN_WIDTH, d), lambda i: (0, 0), pipeline_mode=once),
            pl.BlockSpec((GLA_VAL_WIDTH, d), lambda i: (0, 0), pipeline_mode=once),
            pl.BlockSpec((d, d), lambda i: (0, 0), pipeline_mode=once),
            pl.BlockSpec((1, d), lambda i: (0, 0)),
        ],
        out_specs=pl.BlockSpec((tm, d), lambda i: (i, 0)),
        compiler_params=_params(("parallel",)),
        name="merge_out",
    )(o_attn, o_gla, z, z, h, wa, wg, w_out, post_w.reshape(1, d))


def _page_copies(pt_ref, hbm, buf, sem, seq, slot, n_pages, dst):
    return [pltpu.make_async_copy(hbm.at[pt_ref[seq, p]], buf.at[slot].at[dst(p)], sem)
            for p in range(n_pages)]


GATHER_SLOTS = 4


def _gather_step(fetch):
    b = pl.program_id(0)
    ahead = GATHER_SLOTS - 1

    @pl.when(b == 0)
    def _():
        for a in range(ahead):
            @pl.when(a < pl.num_programs(0))
            def _():
                for cp in fetch(a, a % GATHER_SLOTS):
                    cp.start()

    @pl.when(b + ahead < pl.num_programs(0))
    def _():
        for cp in fetch(b + ahead, (b + ahead) % GATHER_SLOTS):
            cp.start()

    slot = b % GATHER_SLOTS
    for cp in fetch(b, slot):
        cp.wait()
    return slot


def _sample_scores_body(pt_ref, qi_ref, wi_ref, kidx_hbm, o_ref, kbuf, sem, *, n_pages, page):
    def fetch(seq, slot):
        return _page_copies(pt_ref, kidx_hbm, kbuf, sem.at[slot], seq, slot, n_pages,
                            lambda p: (slice(None), pl.ds(p * page, page)))

    slot = _gather_step(fetch)
    s = jnp.dot(qi_ref[0].astype(BF16), kbuf[slot].astype(BF16), preferred_element_type=F32)
    o_ref[0] = jnp.sum(jnp.maximum(s, 0.0) * wi_ref[0], axis=0, keepdims=True)


def _sample_scores(page_table, qi3, wi3, kidx_t):
    nd, n_pages = page_table.shape
    page = kidx_t.shape[2]
    past = n_pages * page
    body = functools.partial(_sample_scores_body, n_pages=n_pages, page=page)
    grid_spec = pltpu.PrefetchScalarGridSpec(
        num_scalar_prefetch=1,
        grid=(nd,),
        in_specs=[
            pl.BlockSpec((1, IDX_HEADS, IDX_DIM), lambda b, pt: (b, 0, 0)),
            pl.BlockSpec((1, IDX_HEADS, 1), lambda b, pt: (b, 0, 0)),
            pl.BlockSpec(memory_space=pl.ANY),
        ],
        out_specs=pl.BlockSpec((1, 1, past), lambda b, pt: (b, 0, 0)),
        scratch_shapes=[pltpu.VMEM((GATHER_SLOTS, IDX_DIM, past), F32), pltpu.SemaphoreType.DMA((GATHER_SLOTS,))],
    )
    return pl.pallas_call(
        body,
        out_shape=jax.ShapeDtypeStruct((nd, 1, past), F32),
        grid_spec=grid_spec,
        compiler_params=_params(("arbitrary",)),
        name="sample_scores",
    )(page_table, qi3, wi3, kidx_t)


def _sample_select_body(sp_ref, qi_ref, misc_ref, bp_ref, bs_ref, sc_scr, j_scr, *, past, tk, topk, idx_bits):
    rows = sp_ref.shape[0]
    misc = misc_ref[...]
    d_in = lax.broadcasted_iota(I32, (LANES, QI_WIDTH), 0)
    c_out = lax.broadcasted_iota(I32, (LANES, QI_WIDTH), 1)
    rep = jnp.where((d_in < IDX_DIM) & (c_out % IDX_DIM == d_in), 1.0, 0.0)
    ki_t = jnp.dot(misc, rep, precision=HIGHEST, preferred_element_type=F32)
    c_in = lax.broadcasted_iota(I32, (QI_WIDTH, LANES), 0)
    l_out = lax.broadcasted_iota(I32, (QI_WIDTH, LANES), 1)
    seg = jnp.where(l_out == MISC_WI + c_in // IDX_DIM, 1.0, 0.0)
    hd = jnp.dot(qi_ref[...] * ki_t, seg, precision=HIGHEST, preferred_element_type=F32)
    lane = lax.broadcasted_iota(I32, (rows, LANES), 1)
    is_wi = (lane >= MISC_WI) & (lane < MISC_GLR)
    s_self = jnp.sum(jnp.where(is_wi, jnp.maximum(hd, 0.0) * misc, 0.0), axis=1, keepdims=True)

    sc_scr[:, 0:past] = sp_ref[...]
    sc_scr[:, past:] = jnp.where(lax.broadcasted_iota(I32, (rows, tk), 1) == 0, s_self, -jnp.inf)
    nk = sc_scr.shape[1] // tk
    thr = _select_threshold(sc_scr, j_scr, nk, tk, topk, idx_bits, 1)
    jb = j_scr[...]
    kpos = lax.broadcasted_iota(I32, sc_scr.shape, 1)
    bias = jnp.where(_selected(sc_scr[...], kpos, thr, jb), 0.0, NEG)
    bp_ref[...] = bias[:, 0:past]
    bs_ref[...] = bias[:, past:past + LANES]


def _sample_select(s_past, qi2, misc, topk):
    nd, past = s_past.shape
    tk = LANES
    width = past + tk
    idx_bits = max(1, (width - 1).bit_length()) + 1
    body = functools.partial(_sample_select_body, past=past, tk=tk, topk=topk, idx_bits=idx_bits)
    return pl.pallas_call(
        body,
        out_shape=(jax.ShapeDtypeStruct((nd, past), F32), jax.ShapeDtypeStruct((nd, LANES), F32)),
        grid=(1,),
        in_specs=[
            pl.BlockSpec((nd, past), lambda i: (0, 0)),
            pl.BlockSpec((nd, QI_WIDTH), lambda i: (0, 0)),
            pl.BlockSpec((nd, LANES), lambda i: (0, 0)),
        ],
        out_specs=(pl.BlockSpec((nd, past), lambda i: (0, 0)), pl.BlockSpec((nd, LANES), lambda i: (0, 0))),
        scratch_shapes=[pltpu.VMEM((nd, width), F32), pltpu.VMEM((nd, 1), I32)],
        compiler_params=_params(("arbitrary",)),
        name="sample_select",
    )(s_past, qi2, misc)


SAMPLE_KCHUNK = 1024


def _sample_attn_body(pt_ref, q_ref, ks_ref, vs_ref, b2_ref, bs_ref, ck_hbm, cv_hbm, o_ref, kbuf, vbuf, sem,
                      *, n_pages, prows):
    def fetch(seq, slot):
        dst = lambda p: (pl.ds(p * prows, prows), slice(None))
        return (_page_copies(pt_ref, ck_hbm, kbuf, sem.at[0, slot], seq, slot, n_pages, dst)
                + _page_copies(pt_ref, cv_hbm, vbuf, sem.at[1, slot], seq, slot, n_pages, dst))

    slot = _gather_step(fetch)
    q = q_ref[0]
    qb = q.astype(BF16)
    total = n_pages * prows
    ch = min(SAMPLE_KCHUNK, total)
    head_grp = lax.broadcasted_iota(I32, (N_HEADS, ch), 0) // HEADS_PER_KV
    row_grp = lax.broadcasted_iota(I32, (N_HEADS, ch), 1) % N_KV_HEADS
    own = head_grp == row_grp
    s_chunks = []
    for c in range(total // ch):
        kc = kbuf[slot, c * ch:(c + 1) * ch, :].astype(BF16)
        s_chunks.append(jnp.where(own, _dot_nt(qb, kc) + b2_ref[0][:, c * ch:(c + 1) * ch], NEG))

    grp0 = lax.broadcasted_iota(I32, (N_HEADS, HEAD_DIM), 0) < HEADS_PER_KV
    k_self = jnp.where(grp0, ks_ref[0][:, 0:HEAD_DIM], ks_ref[0][:, HEAD_DIM:])
    v_self = jnp.where(grp0, vs_ref[0][:, 0:HEAD_DIM], vs_ref[0][:, HEAD_DIM:])
    s_self = jnp.sum(q * k_self, axis=1, keepdims=True) + bs_ref[0][:, 0:1]

    m = s_self
    for s in s_chunks:
        m = jnp.maximum(m, jnp.max(s, axis=1, keepdims=True))
    p_self = jnp.exp(s_self - m)
    l = p_self
    acc = p_self * v_self
    for c, s in enumerate(s_chunks):
        p = jnp.exp(s - m)
        l = l + jnp.sum(p, axis=1, keepdims=True)
        vc = vbuf[slot, c * ch:(c + 1) * ch, :].astype(BF16)
        acc = acc + jnp.dot(p.astype(BF16), vc, preferred_element_type=F32)
    o_ref[0] = (acc / l).astype(BF16)


def _sample_attn(page_table, q3, k_self, v_self, bias2, bias_self, ck2, cv2):
    nd, n_pages = page_table.shape
    prows = ck2.shape[1]
    total = n_pages * prows
    body = functools.partial(_sample_attn_body, n_pages=n_pages, prows=prows)
    grid_spec = pltpu.PrefetchScalarGridSpec(
        num_scalar_prefetch=1,
        grid=(nd,),
        in_specs=[
            pl.BlockSpec((1, N_HEADS, HEAD_DIM), lambda b, pt: (b, 0, 0)),
            pl.BlockSpec((1, 1, KV_WIDTH), lambda b, pt: (b, 0, 0)),
            pl.BlockSpec((1, 1, KV_WIDTH), lambda b, pt: (b, 0, 0)),
            pl.BlockSpec((1, 1, total), lambda b, pt: (b, 0, 0)),
            pl.BlockSpec((1, 1, LANES), lambda b, pt: (b, 0, 0)),
            pl.BlockSpec(memory_space=pl.ANY),
            pl.BlockSpec(memory_space=pl.ANY),
        ],
        out_specs=pl.BlockSpec((1, N_HEADS, HEAD_DIM), lambda b, pt: (b, 0, 0)),
        scratch_shapes=[
            pltpu.VMEM((GATHER_SLOTS, total, HEAD_DIM), F32),
            pltpu.VMEM((GATHER_SLOTS, total, HEAD_DIM), F32),
            pltpu.SemaphoreType.DMA((2, GATHER_SLOTS)),
        ],
    )
    return pl.pallas_call(
        body,
        out_shape=jax.ShapeDtypeStruct((nd, N_HEADS, HEAD_DIM), BF16),
        grid_spec=grid_spec,
        compiler_params=_params(("arbitrary",)),
        name="sample_attn",
    )(page_table, q3, k_self, v_self, bias2, bias_self, ck2, cv2)


def _gla_sample_body(gq_ref, gk_ref, gv_ref, misc_ref, gr_ref, s_ref, w2_ref, gb_ref, nw_ref, og_ref, so_ref):
    eye = jnp.where(lax.broadcasted_iota(I32, (GLA_DK, GLA_DK), 0)
                    == lax.broadcasted_iota(I32, (GLA_DK, GLA_DK), 1), 1.0, 0.0)

    def column(row):
        return jnp.sum(eye * row, axis=1, keepdims=True)

    for sq in range(s_ref.shape[0]):
        misc = misc_ref[sq]
        for h in range(GLA_HEADS):
            ksl = slice(h * GLA_DK, (h + 1) * GLA_DK)
            vsl = slice(h * GLA_DV, (h + 1) * GLA_DV)
            g = _log_decay(misc, w2_ref[:, ksl], gb_ref[:, ksl])
            s_new = column(jnp.exp(g)) * s_ref[sq, h] + column(gk_ref[sq][:, ksl]) * gv_ref[sq][:, vsl]
            so_ref[sq, h] = s_new
            o = jnp.sum(column(gq_ref[sq][:, ksl] * (GLA_DK ** -0.5)) * s_new, axis=0, keepdims=True)
            og_ref[sq, :, vsl] = _gla_out(o, nw_ref[...], gr_ref[sq][:, vsl])


def _gla_sample(gq, gk, gv, misc, gr, state, w2p, gbias, norm_w):
    nd = state.shape[0]
    sb = GLA_SAMPLE_SB if nd % GLA_SAMPLE_SB == 0 else 1

    def row3(w):
        return pl.BlockSpec((sb, 1, w), lambda b: (b, 0, 0))

    st_spec = pl.BlockSpec((sb, GLA_HEADS, GLA_DK, GLA_DV), lambda b: (b, 0, 0, 0))
    return pl.pallas_call(
        _gla_sample_body,
        out_shape=(jax.ShapeDtypeStruct((nd, 1, GLA_VAL_WIDTH), BF16),
                   jax.ShapeDtypeStruct(state.shape, F32)),
        grid=(nd // sb,),
        in_specs=[
            row3(GLA_KEY_WIDTH), row3(GLA_KEY_WIDTH), row3(GLA_VAL_WIDTH), row3(LANES), row3(GLA_VAL_WIDTH),
            st_spec,
            pl.BlockSpec((LANES, GLA_KEY_WIDTH), lambda b: (0, 0)),
            pl.BlockSpec((1, GLA_KEY_WIDTH), lambda b: (0, 0)),
            pl.BlockSpec((1, GLA_DV), lambda b: (0, 0)),
        ],
        out_specs=(row3(GLA_VAL_WIDTH), st_spec),
        compiler_params=_params(("parallel",)),
        name="gla_sample",
    )(gq, gk, gv, misc, gr, state, w2p, gbias, norm_w.reshape(1, GLA_DV))


def _rope_tables(pos, d):
    inv = ROPE_THETA ** (-jnp.arange(0, d, 2, dtype=F32) / d)
    ang = pos.astype(F32)[:, None] * inv[None, :]
    cos, sin = jnp.cos(ang), jnp.sin(ang)
    reps = LANES // d
    return (jnp.tile(jnp.concatenate([cos, cos], axis=-1), (1, reps)),
            jnp.tile(jnp.concatenate([-sin, sin], axis=-1), (1, reps)))


def kernel(x_prompt, x_sample, cache_k, cache_v, cache_kidx, page_table, state_gla,
           ffn1_pre_w, ffn1_w_gate, ffn1_w_up, ffn1_w_down, ffn1_post_w,
           mix_pre_w, w_in, gla_gate_w2, gla_gate_b, gla_norm_w,
           w_proj_attn, w_proj_gla, w_out, mix_post_w,
           ffn2_pre_w, ffn2_w_gate, ffn2_w_up, ffn2_w_down, ffn2_post_w):
    b, s, d = x_prompt.shape
    nd, td, _ = x_sample.shape
    n_pool, page = cache_k.shape[:2]
    n_pages = page_table.shape[1]
    past = n_pages * page
    assert td == 1 and s % Q_BLOCK == 0 and d % MIX_TN == 0 and C_GA % d == 0

    wpa, wpg, wo = w_proj_attn.astype(BF16), w_proj_gla.astype(BF16), w_out.astype(BF16)
    w2p = jnp.zeros((LANES, GLA_KEY_WIDTH), F32).at[MISC_GLR:MISC_GLR + GLA_GATE_RANK].set(gla_gate_w2)
    gbias = gla_gate_b.reshape(1, GLA_KEY_WIDTH)

    def trunk(x, tm, tm_mix, tabs, tab_rows, mixer, f1, f2, w_mix):
        h = _ffn(x, ffn1_pre_w, *f1, ffn1_post_w, tm)
        h, f1b = h if isinstance(h, tuple) else (h, f1)
        z, k_rows, v_rows, *w_mix_b = _mix_in(h, mix_pre_w, w_mix, tabs, tm_mix, tab_rows // tm_mix)
        o_attn, o_gla, s_fin = mixer(z)
        h = _merge_out(o_attn, o_gla, z, h, wpa, wpg, wo, mix_post_w, min(tm, MERGE_TM))
        y = _ffn(h, ffn2_pre_w, *f2, ffn2_post_w, tm)
        y, f2b = y if isinstance(y, tuple) else (y, f2)
        return y, (k_rows, v_rows, z[:, C_MISC:C_MISC + IDX_DIM]), s_fin, f1b, f2b, (w_mix_b or [w_mix])[0]

    pos_p = jnp.arange(s, dtype=I32)
    tm_p = ROW_TM if s % ROW_TM == 0 else Q_BLOCK
    tabs_p = _rope_tables(pos_p, HEAD_DIM) + _rope_tables(pos_p, IDX_DIM)
    top_p = min(TOPK_MAX, s // 4)

    def mixer_p(z):
        o_attn = _attn_prompt(z, b, s, top_p)
        o_gla, s_fin = _gla_prompt(z, w2p, gbias, gla_norm_w, b, s)
        return o_attn, o_gla, s_fin

    tm_mix = MIX_TM if s % MIX_TM == 0 else tm_p

    pos_s = jnp.full((nd,), past, I32)
    tabs_s = _rope_tables(pos_s, HEAD_DIM) + _rope_tables(pos_s, IDX_DIM)
    top_s = min(TOPK_MAX, (past + td) // 4)
    ck2 = cache_k.reshape(n_pool, page * N_KV_HEADS, HEAD_DIM)
    cv2 = cache_v.reshape(n_pool, page * N_KV_HEADS, HEAD_DIM)
    kidx_t = jnp.swapaxes(cache_kidx, 1, 2)

    def mixer_s(z):
        misc = z[:, C_MISC:C_MISC + LANES]
        qi2 = z[:, C_QI:C_QI + QI_WIDTH]
        s_past = _sample_scores(page_table, qi2.reshape(nd, IDX_HEADS, IDX_DIM),
                                misc[:, MISC_WI:MISC_GLR].reshape(nd, IDX_HEADS, 1), kidx_t)
        bias_past, bias_self = _sample_select(s_past.reshape(nd, past), qi2, misc, top_s)
        bias2 = jnp.repeat(bias_past, N_KV_HEADS, axis=1).reshape(nd, 1, past * N_KV_HEADS)
        o_attn = _sample_attn(page_table, z[:, C_Q:C_Q + ATTN_WIDTH].reshape(nd, N_HEADS, HEAD_DIM),
                              z[:, C_K:C_K + KV_WIDTH].reshape(nd, 1, KV_WIDTH),
                              z[:, C_V:C_V + KV_WIDTH].reshape(nd, 1, KV_WIDTH),
                              bias2, bias_self.reshape(nd, 1, LANES), ck2, cv2)
        o_gla, s_fin = _gla_sample(z[:, C_GQ:C_GQ + GLA_KEY_WIDTH].reshape(nd, 1, GLA_KEY_WIDTH),
                                   z[:, C_GK:C_GK + GLA_KEY_WIDTH].reshape(nd, 1, GLA_KEY_WIDTH),
                                   z[:, C_GV:C_GV + GLA_VAL_WIDTH].reshape(nd, 1, GLA_VAL_WIDTH),
                                   misc.reshape(nd, 1, LANES),
                                   z[:, C_GR:C_GR + GLA_VAL_WIDTH].reshape(nd, 1, GLA_VAL_WIDTH),
                                   state_gla, w2p, gbias, gla_norm_w)
        return o_attn.reshape(nd, ATTN_WIDTH), o_gla.reshape(nd, GLA_VAL_WIDTH), s_fin

    y_s, kv_s, gla_s, f1b, f2b, w_mix_b = trunk(
        x_sample.reshape(nd, d), nd, nd, tabs_s, nd, mixer_s,
        (ffn1_w_gate, ffn1_w_up, ffn1_w_down), (ffn2_w_gate, ffn2_w_up, ffn2_w_down), w_in.T)
    y_p, kv_p, gla_p, _, _, _ = trunk(x_prompt.reshape(b * s, d), tm_p, tm_mix, tabs_p, s, mixer_p, f1b, f2b, w_mix_b)

    def kv_out(kv, n, t):
        k_rows, v_rows, ki = kv
        return (k_rows.reshape(n, t, N_KV_HEADS, HEAD_DIM), v_rows.reshape(n, t, N_KV_HEADS, HEAD_DIM),
                ki.reshape(n, t, IDX_DIM))

    k_p, v_p, ki_p = kv_out(kv_p, b, s)
    k_s, v_s, ki_s = kv_out(kv_s, nd, td)
    return (y_p.reshape(b, s, d), y_s.reshape(nd, td, d), k_p, v_p, ki_p, gla_p, k_s, v_s, ki_s, gla_s)
```
